```python
import jax, jax.numpy as jnp
from jax import lax
import numpy as np

D_MODEL = 2048
BATCH = 4
SEQ = 2048
DEPTH = 2

CHUNK = 64
N_META = 16
FRONT_PAD = CHUNK - N_META
Q_BLOCK = 128
EPS = 1e-6

RET_WIDTH = D_MODEL // 2
RET_DV = 128
RET_HEADS = RET_WIDTH // RET_DV
RET_DK = 128
RET_QK = RET_HEADS * RET_DK

DIFF_WIDTH = D_MODEL // 2
DIFF_DH = 64
DIFF_DV = 2 * DIFF_DH
DIFF_HEADS = DIFF_WIDTH // DIFF_DV
DIFF_QK = DIFF_HEADS * 2 * DIFF_DH

MIX_WIDTH = RET_WIDTH + DIFF_WIDTH
IN_SIZES = [RET_QK, RET_QK, RET_WIDTH, RET_WIDTH, DIFF_QK, DIFF_QK, DIFF_WIDTH, DIFF_WIDTH]
IN_WIDTH = sum(IN_SIZES)

kernel_name = 'hymba_retention_diffattn_chunk_causal'


def rms_norm(x, g):
    xf = x.astype(jnp.float32)
    y = xf * lax.rsqrt(jnp.mean(xf * xf, axis=-1, keepdims=True) + EPS)
    return (y * g).astype(x.dtype)


def head_layer_norm(o, g):
    B, L, H, E = o.shape
    of = o.astype(jnp.float32)
    mu = jnp.mean(of, axis=-1, keepdims=True)
    var = jnp.mean((of - mu) ** 2, axis=-1, keepdims=True)
    y = ((of - mu) * lax.rsqrt(var + EPS)).reshape(B, L, H * E)
    return (y * g).astype(o.dtype)


def head_rms_norm(o, g):
    B, L, H, E = o.shape
    of = o.astype(jnp.float32)
    y = (of * lax.rsqrt(jnp.mean(of * of, axis=-1, keepdims=True) + EPS)).reshape(B, L, H * E)
    return (y * g).astype(o.dtype)


def retention(q, k, v, valid):
    B, L, H, DK = q.shape
    DV = v.shape[-1]
    N = L // CHUNK
    log_g = jnp.log(1.0 - 2.0 ** (-5.0 - jnp.arange(H, dtype=jnp.float32)))
    k = k * (valid[None, :, None, None].astype(k.dtype) * (DK ** -0.5))
    qc = q.reshape(B, N, CHUNK, H, DK)
    kc = k.reshape(B, N, CHUNK, H, DK)
    vc = v.reshape(B, N, CHUNK, H, DV)
    idx = jnp.arange(CHUNK, dtype=jnp.float32)
    dist = jnp.abs(idx[:, None] - idx[None, :])
    dmat = jnp.exp(log_g[:, None, None] * dist).astype(v.dtype)
    s = jnp.einsum('bnthd,bnshd->bnhts', qc, kc) * dmat
    intra = jnp.einsum('bnhts,bnshe->bnthe', s, vc)
    zeta = jnp.exp(log_g[:, None] * (CHUNK - 1 - idx)[None, :]).astype(v.dtype)
    kv = jnp.einsum('bnshd,hs,bnshe->bnhde', kc, zeta, vc)
    chunk_decay = jnp.exp(log_g * CHUNK).astype(v.dtype)[None, :, None, None]

    def step(state, kv_n):
        return chunk_decay * state + kv_n, state

    init = jnp.zeros((B, H, DK, DV), kv.dtype)
    _, prev = lax.scan(step, init, jnp.moveaxis(kv, 1, 0))
    prev = jnp.moveaxis(prev, 0, 1)
    xi = jnp.exp(log_g[:, None] * (idx + 1.0)[None, :]).astype(v.dtype)
    cross = jnp.einsum('bnthd,bnhde,ht->bnthe', qc, prev, xi)
    return (intra + cross).reshape(B, L, H, DV)


def diff_attention(q, k, v, valid, lam):
    B, L, H, _, DH = q.shape
    nb = L // Q_BLOCK
    slopes = 2.0 ** (-8.0 * jnp.arange(1, H + 1, dtype=jnp.float32) / H)
    kpos = jnp.arange(L)
    kchunk = kpos // CHUNK
    qb_all = jnp.moveaxis(q.reshape(B, nb, Q_BLOCK, H, 2, DH), 1, 0)
    starts = jnp.arange(nb, dtype=jnp.int32) * Q_BLOCK
    scale = DH ** -0.5

    def block(args):
        qb, start = args
        qpos = start + jnp.arange(Q_BLOCK, dtype=jnp.int32)
        dist = jnp.abs(qpos[:, None] - kpos[None, :]).astype(jnp.float32)
        bias = -slopes[:, None, None] * dist
        allowed = (kchunk[None, :] <= (qpos // CHUNK)[:, None]) & valid[None, :]
        s = jnp.einsum('bqhcd,bkhcd->bhcqk', qb, k).astype(jnp.float32) * scale
        s = jnp.where(allowed[None, None, None], s + bias[None, :, None], -jnp.inf)
        p = jax.nn.softmax(s, axis=-1)
        a = p[:, :, 0] - lam * p[:, :, 1]
        return jnp.einsum('bhqk,bkhe->bqhe', a.astype(v.dtype), v)

    o = lax.map(block, (qb_all, starts))
    return jnp.moveaxis(o, 0, 1).reshape(B, L, H, v.shape[-1])


def hybrid_layer(h, valid, g_norm, w_in, w_out, g_ret, g_diff, lq1, lk1, lq2, lk2, lam_init):
    B, L, _ = h.shape
    u = rms_norm(h, g_norm)
    proj = u @ w_in
    offs = np.cumsum(IN_SIZES)[:-1].tolist()
    rq, rk, rv, rgate, dq, dk, dv, dgate = jnp.split(proj, offs, axis=-1)
    r = retention(rq.reshape(B, L, RET_HEADS, RET_DK), rk.reshape(B, L, RET_HEADS, RET_DK),
                  rv.reshape(B, L, RET_HEADS, RET_DV), valid)
    r = head_layer_norm(r, g_ret) * jax.nn.silu(rgate)
    lam = (jnp.exp(jnp.sum(lq1.astype(jnp.float32) * lk1.astype(jnp.float32)))
           - jnp.exp(jnp.sum(lq2.astype(jnp.float32) * lk2.astype(jnp.float32))) + lam_init)
    d = diff_attention(dq.reshape(B, L, DIFF_HEADS, 2, DIFF_DH), dk.reshape(B, L, DIFF_HEADS, 2, DIFF_DH),
                       dv.reshape(B, L, DIFF_HEADS, DIFF_DV), valid, lam)
    d = head_rms_norm(d, g_diff) * (1.0 - lam_init) * jax.nn.silu(dgate)
    y = jnp.concatenate([r, d], axis=-1) @ w_out
    return h + y


def setup_inputs(seed: int = 0) -> dict:
    key = jax.random.key(seed)
    ks = jax.random.split(key, 13)
    f32 = jnp.float32
    return {
        'x': jax.random.normal(ks[0], (BATCH, SEQ, D_MODEL), f32),
        'meta_tokens': jax.random.normal(ks[1], (N_META, D_MODEL), f32),
        'norm_g': 1.0 + 0.01 * jax.random.normal(ks[2], (DEPTH, D_MODEL), f32),
        'w_in': jax.random.normal(ks[3], (DEPTH, D_MODEL, IN_WIDTH), f32) * D_MODEL ** -0.5,
        'w_out': jax.random.normal(ks[4], (DEPTH, MIX_WIDTH, D_MODEL), f32) * MIX_WIDTH ** -0.5,
        'ret_norm_g': 1.0 + 0.01 * jax.random.normal(ks[5], (DEPTH, RET_WIDTH), f32),
        'diff_norm_g': 1.0 + 0.01 * jax.random.normal(ks[6], (DEPTH, DIFF_WIDTH), f32),
        'lambda_q1': 0.1 * jax.random.normal(ks[7], (DEPTH, DIFF_DH), f32),
        'lambda_k1': 0.1 * jax.random.normal(ks[8], (DEPTH, DIFF_DH), f32),
        'lambda_q2': 0.1 * jax.random.normal(ks[9], (DEPTH, DIFF_DH), f32),
        'lambda_k2': 0.1 * jax.random.normal(ks[10], (DEPTH, DIFF_DH), f32),
        'final_norm_g': 1.0 + 0.01 * jax.random.normal(ks[11], (D_MODEL,), f32),
    }


def reference(x, meta_tokens, norm_g, w_in, w_out, ret_norm_g, diff_norm_g,
              lambda_q1, lambda_k1, lambda_q2, lambda_k2, final_norm_g):
    B, S, D = x.shape
    total = CHUNK + S
    Lp = -(-total // Q_BLOCK) * Q_BLOCK
    meta = jnp.broadcast_to(meta_tokens[None].astype(x.dtype), (B, N_META, D))
    h = jnp.concatenate([jnp.zeros((B, FRONT_PAD, D), x.dtype), meta, x,
                         jnp.zeros((B, Lp - total, D), x.dtype)], axis=1)
    pos = jnp.arange(Lp)
    valid = (pos >= FRONT_PAD) & (pos < total)
    for i in range(DEPTH):
        lam_init = 0.8 - 0.6 * float(np.exp(-0.3 * i))
        h = hybrid_layer(h, valid, norm_g[i], w_in[i], w_out[i], ret_norm_g[i], diff_norm_g[i],
                         lambda_q1[i], lambda_k1[i], lambda_q2[i], lambda_k2[i], lam_init)
    h = rms_norm(h, final_norm_g)
    return h[:, CHUNK:CHUNK + S]
```

```python
import functools

import numpy as np
import jax
import jax.numpy as jnp
from jax import lax
from jax.experimental import pallas as pl
from jax.experimental.pallas import tpu as pltpu

F32 = jnp.float32
BF16 = jnp.bfloat16

D_MODEL = 2048
CHUNK = 64
N_META = 16
FRONT_PAD = CHUNK - N_META
EPS = 1e-6
HEADS = 8
HEAD_DIM = 128
DIFF_DH = 64
MIX_HALF = HEADS * HEAD_DIM
IN_WIDTH = 8 * MIX_HALF
BLK = 128
NEG = -1e30

VMEM_LIMIT = 56 * 1024 * 1024


def _inproj_kernel(h_ref, g_ref, w_ref, o_ref, u_ref):
    @pl.when(pl.program_id(1) == 0)
    def _():
        x = h_ref[...]
        ms = jnp.mean(x * x, axis=-1, keepdims=True)
        u_ref[...] = (x * lax.rsqrt(ms + EPS) * g_ref[...]).astype(BF16)

    o_ref[...] = jnp.dot(u_ref[...], w_ref[...], preferred_element_type=F32).astype(BF16)


def _inproj(h, g, w, tm, tn):
    m, d = h.shape
    n = w.shape[1]
    return pl.pallas_call(
        _inproj_kernel,
        out_shape=jax.ShapeDtypeStruct((m, n), BF16),
        grid=(m // tm, n // tn),
        in_specs=[
            pl.BlockSpec((tm, d), lambda i, j: (i, 0)),
            pl.BlockSpec((1, d), lambda i, j: (0, 0)),
            pl.BlockSpec((d, tn), lambda i, j: (0, j)),
        ],
        out_specs=pl.BlockSpec((tm, tn), lambda i, j: (i, j)),
        scratch_shapes=[pltpu.VMEM((tm, d), BF16)],
        compiler_params=pltpu.CompilerParams(
            dimension_semantics=("parallel", "arbitrary"),
            vmem_limit_bytes=VMEM_LIMIT),
        name="inproj",
    )(h, g, w)


def _retention_kernel(q_ref, k_ref, v_ref, gate_ref, gn_ref, o_ref,
                      state_ref, dmat_ref, rq_ref, rk_ref, *, total):
    j = pl.program_id(1)
    scale = HEAD_DIM ** -0.5

    @pl.when(j == 0)
    def _():
        state_ref[...] = jnp.zeros_like(state_ref)
        t = lax.broadcasted_iota(jnp.int32, (BLK, BLK), 0)
        s = lax.broadcasted_iota(jnp.int32, (BLK, BLK), 1)
        dist = jnp.abs(t - s).astype(F32)
        allowed = (s // CHUNK) <= (t // CHUNK)
        tf = t.astype(F32)
        for h in range(HEADS):
            log_g = float(np.log(1.0 - 2.0 ** (-5.0 - h)))
            dmat_ref[h] = jnp.where(allowed, jnp.exp(log_g * dist) * scale, 0.0)
            rq_ref[h] = jnp.exp(log_g * (tf + 1.0))
            rk_ref[h] = jnp.exp(log_g * (BLK - 1.0 - tf)) * scale

    pos = j * BLK + lax.broadcasted_iota(jnp.int32, (BLK, HEAD_DIM), 0)
    valid = (pos >= FRONT_PAD) & (pos < total)

    for h in range(HEADS):
        cols = slice(h * HEAD_DIM, (h + 1) * HEAD_DIM)
        decay_blk = float((1.0 - 2.0 ** (-5.0 - h)) ** BLK)
        q = q_ref[:, cols]
        k = jnp.where(valid, k_ref[:, cols], jnp.zeros((), BF16))
        v = v_ref[:, cols]
        s = lax.dot_general(q, k, (((1,), (1,)), ((), ())),
                            preferred_element_type=F32) * dmat_ref[h]
        intra = jnp.dot(s.astype(BF16), v, preferred_element_type=F32)
        state = state_ref[h]
        qx = (q.astype(F32) * rq_ref[h]).astype(BF16)
        cross = jnp.dot(qx, state.astype(BF16), preferred_element_type=F32)
        kz = (k.astype(F32) * rk_ref[h]).astype(BF16)
        kv = lax.dot_general(kz, v, (((0,), (0,)), ((), ())),
                             preferred_element_type=F32)
        state_ref[h] = decay_blk * state + kv

        o = intra + cross
        mu = jnp.mean(o, axis=-1, keepdims=True)
        oc = o - mu
        var = jnp.mean(oc * oc, axis=-1, keepdims=True)
        y = oc * lax.rsqrt(var + EPS) * gn_ref[:, cols]
        gate = gate_ref[:, cols].astype(F32)
        y = y * (gate / (1.0 + jnp.exp(-gate)))
        o_ref[:, cols] = y.astype(BF16)


def _retention(proj, gn, batch, lp, total):
    m = proj.shape[0]
    nblk = lp // BLK
    row = lambda b, j: b * nblk + j
    return pl.pallas_call(
        functools.partial(_retention_kernel, total=total),
        out_shape=jax.ShapeDtypeStruct((m, MIX_HALF), BF16),
        grid=(batch, nblk),
        in_specs=[
            pl.BlockSpec((BLK, MIX_HALF), lambda b, j: (row(b, j), 0)),
            pl.BlockSpec((BLK, MIX_HALF), lambda b, j: (row(b, j), 1)),
            pl.BlockSpec((BLK, MIX_HALF), lambda b, j: (row(b, j), 2)),
            pl.BlockSpec((BLK, MIX_HALF), lambda b, j: (row(b, j), 3)),
            pl.BlockSpec((1, MIX_HALF), lambda b, j: (0, 0)),
        ],
        out_specs=pl.BlockSpec((BLK, MIX_HALF), lambda b, j: (row(b, j), 0)),
        scratch_shapes=[
            pltpu.VMEM((HEADS, HEAD_DIM, HEAD_DIM), F32),
            pltpu.VMEM((HEADS, BLK, BLK), F32),
            pltpu.VMEM((HEADS, BLK, HEAD_DIM), F32),
            pltpu.VMEM((HEADS, BLK, HEAD_DIM), F32),
        ],
        compiler_params=pltpu.CompilerParams(
            dimension_semantics=("parallel", "arbitrary"),
            vmem_limit_bytes=VMEM_LIMIT),
        name="retention",
    )(proj, proj, proj, proj, gn)


def _diffattn_kernel(slopes_ref, q_ref, k_ref, v_ref, gate_ref, gn_ref,
                     lq1_ref, lk1_ref, lq2_ref, lk2_ref, o_ref,
                     m_ref, l_ref, acc_ref, *, total, lam_init):
    h = pl.program_id(1)
    i = pl.program_id(2)
    slope = slopes_ref[h]

    q = q_ref[...]
    lane = lax.broadcasted_iota(jnp.int32, (BLK, HEAD_DIM), 1)
    zero = jnp.zeros((), BF16)
    qscaled = q * jnp.asarray(DIFF_DH ** -0.5, BF16)
    qs = jnp.concatenate([jnp.where(lane < DIFF_DH, qscaled, zero),
                          jnp.where(lane >= DIFF_DH, qscaled, zero)], axis=0)

    m_ref[...] = jnp.full_like(m_ref, -jnp.inf)
    l_ref[...] = jnp.zeros_like(l_ref)
    acc_ref[...] = jnp.zeros_like(acc_ref)

    def step(j, bias):
        start = pl.multiple_of(j * BLK, BLK)
        kt = k_ref[pl.ds(start, BLK), :]
        vt = v_ref[pl.ds(start, BLK), :]
        s = lax.dot_general(qs, kt, (((1,), (1,)), ((), ())),
                            preferred_element_type=F32) + bias
        m_old = m_ref[...]
        m_new = jnp.maximum(m_old, jnp.max(s, axis=-1, keepdims=True))
        alpha = jnp.exp(m_old - m_new)
        p = jnp.exp(s - m_new)
        l_ref[...] = alpha * l_ref[...] + jnp.sum(p, axis=-1, keepdims=True)
        acc_ref[...] = alpha * acc_ref[...] + jnp.dot(
            p.astype(BF16), vt, preferred_element_type=F32)
        m_ref[...] = m_new

    col = lax.broadcasted_iota(jnp.int32, (1, BLK), 1)

    def off_diag(j, carry):
        kpos = j * BLK + col
        kvalid = (kpos >= FRONT_PAD) & (kpos < total)
        bias = jnp.where(kvalid, slope * (kpos - i * BLK).astype(F32), NEG)
        step(j, bias)
        return carry

    lax.fori_loop(0, i, off_diag, 0)

    r = lax.broadcasted_iota(jnp.int32, (2 * BLK, BLK), 0) % BLK
    c = lax.broadcasted_iota(jnp.int32, (2 * BLK, BLK), 1)
    kpos = i * BLK + c
    allowed = ((c // CHUNK) <= (r // CHUNK)) & (kpos >= FRONT_PAD) & (kpos < total)
    bias = jnp.where(allowed, slope * (r - jnp.abs(r - c)).astype(F32), NEG)
    step(i, bias)

    lam = (jnp.exp(jnp.sum(lq1_ref[...] * lk1_ref[...], axis=-1, keepdims=True))
           - jnp.exp(jnp.sum(lq2_ref[...] * lk2_ref[...], axis=-1, keepdims=True))
           + lam_init)
    a = acc_ref[...] / l_ref[...]
    d = a[:BLK] - lam * a[BLK:]
    y = d * lax.rsqrt(jnp.mean(d * d, axis=-1, keepdims=True) + EPS)
    y = y * gn_ref[...] * (1.0 - lam_init)
    gate = gate_ref[...].astype(F32)
    y = y * (gate / (1.0 + jnp.exp(-gate)))
    o_ref[...] = y.astype(BF16)


def _diffattn(proj, gn, lq1, lk1, lq2, lk2, batch, lp, total, lam_init):
    m = proj.shape[0]
    nblk = lp // BLK
    slopes = jnp.asarray(2.0 ** (-8.0 * np.arange(1, HEADS + 1) / HEADS), F32)
    qcol, kcol, vcol, gcol = (4 * HEADS, 5 * HEADS, 6 * HEADS, 7 * HEADS)
    vec = pl.BlockSpec((1, DIFF_DH), lambda b, h, i: (0, 0))
    return pl.pallas_call(
        functools.partial(_diffattn_kernel, total=total, lam_init=lam_init),
        out_shape=jax.ShapeDtypeStruct((m, MIX_HALF), BF16),
        grid=(batch, HEADS, nblk),
        in_specs=[
            pl.BlockSpec(memory_space=pltpu.SMEM),
            pl.BlockSpec((BLK, HEAD_DIM), lambda b, h, i: (b * nblk + i, qcol + h)),
            pl.BlockSpec((lp, HEAD_DIM), lambda b, h, i: (b, kcol + h)),
            pl.BlockSpec((lp, HEAD_DIM), lambda b, h, i: (b, vcol + h)),
            pl.BlockSpec((BLK, HEAD_DIM), lambda b, h, i: (b * nblk + i, gcol + h)),
            pl.BlockSpec((1, HEAD_DIM), lambda b, h, i: (0, h)),
            vec, vec, vec, vec,
        ],
        out_specs=pl.BlockSpec((BLK, HEAD_DIM), lambda b, h, i: (b * nblk + i, h)),
        scratch_shapes=[
            pltpu.VMEM((2 * BLK, 1), F32),
            pltpu.VMEM((2 * BLK, 1), F32),
            pltpu.VMEM((2 * BLK, HEAD_DIM), F32),
        ],
        compiler_params=pltpu.CompilerParams(
            dimension_semantics=("parallel", "parallel", "arbitrary"),
            vmem_limit_bytes=VMEM_LIMIT),
        name="diffattn",
    )(slopes, proj, proj, proj, proj, gn, lq1, lk1, lq2, lk2)


def _outproj_kernel(r_ref, d_ref, wr_ref, wd_ref, h_ref, gf_ref, o_ref, *, final_norm):
    y = jnp.dot(r_ref[...], wr_ref[...], preferred_element_type=F32)
    y = y + jnp.dot(d_ref[...], wd_ref[...], preferred_element_type=F32)
    hn = h_ref[...] + y
    if final_norm:
        ms = jnp.mean(hn * hn, axis=-1, keepdims=True)
        hn = hn * lax.rsqrt(ms + EPS) * gf_ref[...]
    o_ref[...] = hn


def _outproj(r, d, w, h, gf, tm, final_norm):
    m, dm = h.shape
    return pl.pallas_call(
        functools.partial(_outproj_kernel, final_norm=final_norm),
        out_shape=jax.ShapeDtypeStruct((m, dm), F32),
        grid=(m // tm,),
        in_specs=[
            pl.BlockSpec((tm, MIX_HALF), lambda i: (i, 0)),
            pl.BlockSpec((tm, MIX_HALF), lambda i: (i, 0)),
            pl.BlockSpec((MIX_HALF, dm), lambda i: (0, 0)),
            pl.BlockSpec((MIX_HALF, dm), lambda i: (1, 0)),
            pl.BlockSpec((tm, dm), lambda i: (i, 0)),
            pl.BlockSpec((1, dm), lambda i: (0, 0)),
        ],
        out_specs=pl.BlockSpec((tm, dm), lambda i: (i, 0)),
        compiler_params=pltpu.CompilerParams(
            dimension_semantics=("parallel",),
            vmem_limit_bytes=VMEM_LIMIT),
        name="outproj",
    )(r, d, w, w, h, gf)


def kernel(x, meta_tokens, norm_g, w_in, w_out, ret_norm_g, diff_norm_g,
           lambda_q1, lambda_k1, lambda_q2, lambda_k2, final_norm_g):
    batch, seq, dm = x.shape
    depth = w_in.shape[0]
    total = CHUNK + seq
    lp = -(-total // BLK) * BLK
    m = batch * lp

    meta = jnp.broadcast_to(meta_tokens[None].astype(x.dtype), (batch, N_META, dm))
    h = jnp.concatenate([jnp.zeros((batch, FRONT_PAD, dm), x.dtype), meta, x,
                         jnp.zeros((batch, lp - total, dm), x.dtype)], axis=1)
    h = h.reshape(m, dm)
    w_in_b = w_in.astype(BF16)
    w_out_b = w_out.astype(BF16)
    gf = final_norm_g.reshape(1, dm)

    for i in range(depth):
        lam_init = 0.8 - 0.6 * float(np.exp(-0.3 * i))
        proj = _inproj(h, norm_g[i].reshape(1, dm), w_in_b[i], tm=m // 8, tn=1024)
        r = _retention(proj, ret_norm_g[i].reshape(1, MIX_HALF), batch, lp, total)
        d = _diffattn(proj, diff_norm_g[i].reshape(1, MIX_HALF),
                      lambda_q1[i].reshape(1, DIFF_DH), lambda_k1[i].reshape(1, DIFF_DH),
                      lambda_q2[i].reshape(1, DIFF_DH), lambda_k2[i].reshape(1, DIFF_DH),
                      batch, lp, total, lam_init)
        h = _outproj(r, d, w_out_b[i], h, gf, tm=m // 16, final_norm=(i == depth - 1))

    return h.reshape(batch, lp, dm)[:, CHUNK:CHUNK + seq]
```

```python
import functools

import numpy as np
import jax
import jax.numpy as jnp
from jax import lax
from jax.experimental import pallas as pl
from jax.experimental.pallas import tpu as pltpu

F32 = jnp.float32
BF16 = jnp.bfloat16

D_MODEL = 2048
CHUNK = 64
N_META = 16
FRONT_PAD = CHUNK - N_META
EPS = 1e-6
HEADS = 8
HEAD_DIM = 128
DIFF_DH = 64
MIX_HALF = HEADS * HEAD_DIM
IN_WIDTH = 8 * MIX_HALF
BLK = 128
NEG = -1e30

VMEM_LIMIT = 56 * 1024 * 1024


def _inproj_kernel(h_ref, g_ref, w_ref, o_ref, u_ref):
    @pl.when(pl.program_id(1) == 0)
    def _():
        x = h_ref[...]
        ms = jnp.mean(x * x, axis=-1, keepdims=True)
        u_ref[...] = (x * lax.rsqrt(ms + EPS) * g_ref[...]).astype(BF16)

    o_ref[...] = jnp.dot(u_ref[...], w_ref[...], preferred_element_type=F32).astype(BF16)


def _inproj(h, g, w, layer, tm, tn):
    m, d = h.shape
    n = w.shape[-1]
    return pl.pallas_call(
        _inproj_kernel,
        out_shape=jax.ShapeDtypeStruct((m, n), BF16),
        grid=(m // tm, n // tn),
        in_specs=[
            pl.BlockSpec((tm, d), lambda i, j: (i, 0)),
            pl.BlockSpec((None, 1, d), lambda i, j: (layer, 0, 0)),
            pl.BlockSpec((None, d, tn), lambda i, j: (layer, 0, j)),
        ],
        out_specs=pl.BlockSpec((tm, tn), lambda i, j: (i, j)),
        scratch_shapes=[pltpu.VMEM((tm, d), BF16)],
        compiler_params=pltpu.CompilerParams(
            dimension_semantics=("parallel", "arbitrary"),
            vmem_limit_bytes=VMEM_LIMIT),
        name="inproj",
    )(h, g, w)


def _retention_kernel(q_ref, k_ref, v_ref, gate_ref, gn_ref, o_ref,
                      state_ref, dmat_ref, rq_ref, rk_ref, *, total):
    j = pl.program_id(1)
    scale = HEAD_DIM ** -0.5

    @pl.when(j == 0)
    def _():
        state_ref[...] = jnp.zeros_like(state_ref)
        t = lax.broadcasted_iota(jnp.int32, (BLK, BLK), 0)
        s = lax.broadcasted_iota(jnp.int32, (BLK, BLK), 1)
        dist = jnp.abs(t - s).astype(F32)
        allowed = (s // CHUNK) <= (t // CHUNK)
        tf = t.astype(F32)
        for h in range(HEADS):
            log_g = float(np.log(1.0 - 2.0 ** (-5.0 - h)))
            dmat_ref[h] = jnp.where(allowed, jnp.exp(log_g * dist) * scale, 0.0)
            rq_ref[h] = jnp.exp(log_g * (tf + 1.0))
            rk_ref[h] = jnp.exp(log_g * (BLK - 1.0 - tf)) * scale

    pos = j * BLK + lax.broadcasted_iota(jnp.int32, (BLK, HEAD_DIM), 0)
    valid = (pos >= FRONT_PAD) & (pos < total)

    for h in range(HEADS):
        cols = slice(h * HEAD_DIM, (h + 1) * HEAD_DIM)
        decay_blk = float((1.0 - 2.0 ** (-5.0 - h)) ** BLK)
        q = q_ref[:, cols]
        k = jnp.where(valid, k_ref[:, cols], jnp.zeros((), BF16))
        v = v_ref[:, cols]
        s = lax.dot_general(q, k, (((1,), (1,)), ((), ())),
                            preferred_element_type=F32) * dmat_ref[h]
        intra = jnp.dot(s.astype(BF16), v, preferred_element_type=F32)
        state = state_ref[h]
        qx = (q.astype(F32) * rq_ref[h]).astype(BF16)
        cross = jnp.dot(qx, state.astype(BF16), preferred_element_type=F32)
        kz = (k.astype(F32) * rk_ref[h]).astype(BF16)
        kv = lax.dot_general(kz, v, (((0,), (0,)), ((), ())),
                             preferred_element_type=F32)
        state_ref[h] = decay_blk * state + kv

        o = intra + cross
        mu = jnp.mean(o, axis=-1, keepdims=True)
        oc = o - mu
        var = jnp.mean(oc * oc, axis=-1, keepdims=True)
        y = oc * lax.rsqrt(var + EPS) * gn_ref[:, cols]
        gate = gate_ref[:, cols].astype(F32)
        y = y * (gate / (1.0 + jnp.exp(-gate)))
        o_ref[:, cols] = y.astype(BF16)


def _retention(proj, gn, layer, batch, lp, total):
    m = proj.shape[0]
    nblk = lp // BLK
    row = lambda b, j: b * nblk + j
    return pl.pallas_call(
        functools.partial(_retention_kernel, total=total),
        out_shape=jax.ShapeDtypeStruct((m, MIX_HALF), BF16),
        grid=(batch, nblk),
        in_specs=[
            pl.BlockSpec((BLK, MIX_HALF), lambda b, j: (row(b, j), 0)),
            pl.BlockSpec((BLK, MIX_HALF), lambda b, j: (row(b, j), 1)),
            pl.BlockSpec((BLK, MIX_HALF), lambda b, j: (row(b, j), 2)),
            pl.BlockSpec((BLK, MIX_HALF), lambda b, j: (row(b, j), 3)),
            pl.BlockSpec((None, 1, MIX_HALF), lambda b, j: (layer, 0, 0)),
        ],
        out_specs=pl.BlockSpec((BLK, MIX_HALF), lambda b, j: (row(b, j), 0)),
        scratch_shapes=[
            pltpu.VMEM((HEADS, HEAD_DIM, HEAD_DIM), F32),
            pltpu.VMEM((HEADS, BLK, BLK), F32),
            pltpu.VMEM((HEADS, BLK, HEAD_DIM), F32),
            pltpu.VMEM((HEADS, BLK, HEAD_DIM), F32),
        ],
        compiler_params=pltpu.CompilerParams(
            dimension_semantics=("parallel", "arbitrary"),
            vmem_limit_bytes=VMEM_LIMIT),
        name="retention",
    )(proj, proj, proj, proj, gn)


HEADS_PER_STEP = 4


def _diffattn_kernel(slopes_ref, q_ref, k_ref, v_ref, gate_ref, gn_ref,
                     lq1_ref, lk1_ref, lq2_ref, lk2_ref, o_ref,
                     qs_ref, s_ref, mrun_ref, mrow_ref, lrun_ref, acc_ref,
                     *, total, lam_init):
    grp = pl.program_id(1)
    i = pl.program_id(2)
    hb = HEADS_PER_STEP
    nt = (((1,), (1,)), ((), ()))

    lane = lax.broadcasted_iota(jnp.int32, (BLK, HEAD_DIM), 1)
    zero = jnp.zeros((), BF16)
    for hh in range(hb):
        cols = slice(hh * HEAD_DIM, (hh + 1) * HEAD_DIM)
        qscaled = q_ref[:, cols] * jnp.asarray(DIFF_DH ** -0.5, BF16)
        qs_ref[hh, :BLK] = jnp.where(lane < DIFF_DH, qscaled, zero)
        qs_ref[hh, BLK:] = jnp.where(lane >= DIFF_DH, qscaled, zero)
        mrun_ref[hh] = jnp.full((2 * BLK, BLK), -jnp.inf, F32)
        lrun_ref[hh] = jnp.zeros((2 * BLK, BLK), F32)
        acc_ref[hh] = jnp.zeros((2 * BLK, HEAD_DIM), F32)

    def scores(j, hh, bias):
        start = pl.multiple_of(j * BLK, BLK)
        kt = k_ref[pl.ds(start, BLK), hh * HEAD_DIM:(hh + 1) * HEAD_DIM]
        s = lax.dot_general(qs_ref[hh], kt, nt, preferred_element_type=F32) + bias
        s_ref[hh, j] = s
        mrun_ref[hh] = jnp.maximum(mrun_ref[hh], s)

    col = lax.broadcasted_iota(jnp.int32, (1, BLK), 1)

    def off_diag(j, carry):
        kpos = j * BLK + col
        kvalid = (kpos >= FRONT_PAD) & (kpos < total)
        rel = (kpos - i * BLK).astype(F32)
        for hh in range(hb):
            slope = slopes_ref[grp * hb + hh]
            scores(j, hh, jnp.where(kvalid, slope * rel, NEG))
        return carry

    lax.fori_loop(0, i, off_diag, 0)

    r = lax.broadcasted_iota(jnp.int32, (2 * BLK, BLK), 0) % BLK
    c = lax.broadcasted_iota(jnp.int32, (2 * BLK, BLK), 1)
    kpos = i * BLK + c
    allowed = ((c // CHUNK) <= (r // CHUNK)) & (kpos >= FRONT_PAD) & (kpos < total)
    rel = (r - jnp.abs(r - c)).astype(F32)
    for hh in range(hb):
        slope = slopes_ref[grp * hb + hh]
        scores(i, hh, jnp.where(allowed, slope * rel, NEG))
        mrow_ref[hh] = jnp.broadcast_to(
            jnp.max(mrun_ref[hh], axis=-1, keepdims=True), (2 * BLK, BLK))

    def probs(j, carry):
        start = pl.multiple_of(j * BLK, BLK)
        for hh in range(hb):
            p = jnp.exp(s_ref[hh, j] - mrow_ref[hh])
            lrun_ref[hh] = lrun_ref[hh] + p
            vt = v_ref[pl.ds(start, BLK), hh * HEAD_DIM:(hh + 1) * HEAD_DIM]
            acc_ref[hh] = acc_ref[hh] + jnp.dot(p.astype(BF16), vt,
                                                preferred_element_type=F32)
        return carry

    lax.fori_loop(0, i + 1, probs, 0)

    lam = (jnp.exp(jnp.sum(lq1_ref[...] * lk1_ref[...], axis=-1, keepdims=True))
           - jnp.exp(jnp.sum(lq2_ref[...] * lk2_ref[...], axis=-1, keepdims=True))
           + lam_init)
    for hh in range(hb):
        cols = slice(hh * HEAD_DIM, (hh + 1) * HEAD_DIM)
        a = acc_ref[hh] / jnp.sum(lrun_ref[hh], axis=-1, keepdims=True)
        d = a[:BLK] - lam * a[BLK:]
        y = d * lax.rsqrt(jnp.mean(d * d, axis=-1, keepdims=True) + EPS)
        y = y * gn_ref[:, cols] * (1.0 - lam_init)
        gate = gate_ref[:, cols].astype(F32)
        y = y * (gate / (1.0 + jnp.exp(-gate)))
        o_ref[:, cols] = y.astype(BF16)


def _diffattn(proj, gn, lq1, lk1, lq2, lk2, layer, batch, lp, total, lam_init):
    m = proj.shape[0]
    nblk = lp // BLK
    hb = HEADS_PER_STEP
    ngrp = HEADS // hb
    width = hb * HEAD_DIM
    slopes = jnp.asarray(2.0 ** (-8.0 * np.arange(1, HEADS + 1) / HEADS), F32)
    qcol, kcol, vcol, gcol = (4 * ngrp, 5 * ngrp, 6 * ngrp, 7 * ngrp)
    vec = pl.BlockSpec((None, 1, DIFF_DH), lambda b, g, i: (layer, 0, 0))
    return pl.pallas_call(
        functools.partial(_diffattn_kernel, total=total, lam_init=lam_init),
        out_shape=jax.ShapeDtypeStruct((m, MIX_HALF), BF16),
        grid=(batch, ngrp, nblk),
        in_specs=[
            pl.BlockSpec(memory_space=pltpu.SMEM),
            pl.BlockSpec((BLK, width), lambda b, g, i: (b * nblk + i, qcol + g)),
            pl.BlockSpec((lp, width), lambda b, g, i: (b, kcol + g)),
            pl.BlockSpec((lp, width), lambda b, g, i: (b, vcol + g)),
            pl.BlockSpec((BLK, width), lambda b, g, i: (b * nblk + i, gcol + g)),
            pl.BlockSpec((None, 1, width), lambda b, g, i: (layer, 0, g)),
            vec, vec, vec, vec,
        ],
        out_specs=pl.BlockSpec((BLK, width), lambda b, g, i: (b * nblk + i, g)),
        scratch_shapes=[
            pltpu.VMEM((hb, 2 * BLK, HEAD_DIM), BF16),
            pltpu.VMEM((hb, nblk, 2 * BLK, BLK), F32),
            pltpu.VMEM((hb, 2 * BLK, BLK), F32),
            pltpu.VMEM((hb, 2 * BLK, BLK), F32),
            pltpu.VMEM((hb, 2 * BLK, BLK), F32),
            pltpu.VMEM((hb, 2 * BLK, HEAD_DIM), F32),
        ],
        compiler_params=pltpu.CompilerParams(
            dimension_semantics=("parallel", "parallel", "arbitrary"),
            vmem_limit_bytes=VMEM_LIMIT),
        name="diffattn",
    )(slopes, proj, proj, proj, proj, gn, lq1, lk1, lq2, lk2)


def _outproj_kernel(r_ref, d_ref, wr_ref, wd_ref, h_ref, gf_ref, o_ref, *, final_norm):
    y = jnp.dot(r_ref[...], wr_ref[...], preferred_element_type=F32)
    y = y + jnp.dot(d_ref[...], wd_ref[...], preferred_element_type=F32)
    hn = h_ref[...] + y
    if final_norm:
        ms = jnp.mean(hn * hn, axis=-1, keepdims=True)
        hn = hn * lax.rsqrt(ms + EPS) * gf_ref[...]
    o_ref[...] = hn


def _outproj(r, d, w, h, gf, layer, tm, final_norm):
    m, dm = h.shape
    return pl.pallas_call(
        functools.partial(_outproj_kernel, final_norm=final_norm),
        out_shape=jax.ShapeDtypeStruct((m, dm), F32),
        grid=(m // tm,),
        in_specs=[
            pl.BlockSpec((tm, MIX_HALF), lambda i: (i, 0)),
            pl.BlockSpec((tm, MIX_HALF), lambda i: (i, 0)),
            pl.BlockSpec((None, MIX_HALF, dm), lambda i: (layer, 0, 0)),
            pl.BlockSpec((None, MIX_HALF, dm), lambda i: (layer, 1, 0)),
            pl.BlockSpec((tm, dm), lambda i: (i, 0)),
            pl.BlockSpec((1, dm), lambda i: (0, 0)),
        ],
        out_specs=pl.BlockSpec((tm, dm), lambda i: (i, 0)),
        compiler_params=pltpu.CompilerParams(
            dimension_semantics=("parallel",),
            vmem_limit_bytes=VMEM_LIMIT),
        name="outproj",
    )(r, d, w, w, h, gf)


def kernel(x, meta_tokens, norm_g, w_in, w_out, ret_norm_g, diff_norm_g,
           lambda_q1, lambda_k1, lambda_q2, lambda_k2, final_norm_g):
    batch, seq, dm = x.shape
    depth = w_in.shape[0]
    total = CHUNK + seq
    lp = -(-total // BLK) * BLK
    m = batch * lp

    meta = jnp.broadcast_to(meta_tokens[None].astype(x.dtype), (batch, N_META, dm))
    h = jnp.concatenate([jnp.zeros((batch, FRONT_PAD, dm), x.dtype), meta, x,
                         jnp.zeros((batch, lp - total, dm), x.dtype)], axis=1)
    h = h.reshape(m, dm)
    w_in_b = w_in.astype(BF16)
    w_out_b = w_out.astype(BF16)
    gf = final_norm_g.reshape(1, dm)
    per_layer = lambda a: a.reshape(depth, 1, a.shape[-1])
    norm_g, ret_norm_g, diff_norm_g = map(per_layer, (norm_g, ret_norm_g, diff_norm_g))
    lq1, lk1, lq2, lk2 = map(per_layer, (lambda_q1, lambda_k1, lambda_q2, lambda_k2))

    for i in range(depth):
        lam_init = 0.8 - 0.6 * float(np.exp(-0.3 * i))
        proj = _inproj(h, norm_g, w_in_b, i, tm=m // 8, tn=1024)
        r = _retention(proj, ret_norm_g, i, batch, lp, total)
        d = _diffattn(proj, diff_norm_g, lq1, lk1, lq2, lk2,
                      i, batch, lp, total, lam_init)
        h = _outproj(r, d, w_out_b, h, gf, i, tm=m // 16, final_norm=(i == depth - 1))

    return h.reshape(batch, lp, dm)[:, CHUNK:CHUNK + seq]
```

```python
import functools

import numpy as np
import jax
import jax.numpy as jnp
from jax import lax
from jax.experimental import pallas as pl
from jax.experimental.pallas import tpu as pltpu

F32 = jnp.float32
BF16 = jnp.bfloat16

CHUNK = 64
N_META = 16
EPS = 1e-6
HEADS = 8
HEAD_DIM = 128
DIFF_DH = 64
MIX_HALF = HEADS * HEAD_DIM
BLK = 128
QT = 256
NEG = -1e30
HEADS_PER_STEP = 4

VMEM_LIMIT = 56 * 1024 * 1024

NT = (((1,), (1,)), ((), ()))
TN = (((0,), (0,)), ((), ()))


def _log_g(h):
    return float(np.log(1.0 - 2.0 ** (-5.0 - h)))


def _silu(x):
    return x * (1.0 / (1.0 + jnp.exp(-x)))


def _lambda(lq1_ref, lk1_ref, lq2_ref, lk2_ref, lam_init):
    return (jnp.exp(jnp.sum(lq1_ref[...] * lk1_ref[...], axis=-1, keepdims=True))
            - jnp.exp(jnp.sum(lq2_ref[...] * lk2_ref[...], axis=-1, keepdims=True))
            + lam_init)


def _pad_rows(a, rows):
    return jnp.concatenate([a, jnp.zeros((rows - a.shape[0], a.shape[1]), a.dtype)], axis=0)


def _split_components(q):
    lane = lax.broadcasted_iota(jnp.int32, q.shape, 1)
    zero = jnp.zeros((), BF16)
    qscaled = q * jnp.asarray(DIFF_DH ** -0.5, BF16)
    return jnp.concatenate([jnp.where(lane < DIFF_DH, qscaled, zero),
                            jnp.where(lane >= DIFF_DH, qscaled, zero)], axis=0)


def _meta_kernel(*refs, lam_init, with_out):
    if with_out:
        (hm_ref, g_ref, w_ref, wr_ref, wd_ref, gret_ref, gdiff_ref,
         lq1_ref, lk1_ref, lq2_ref, lk2_ref, pm_ref, hm_out_ref, u_ref, sec_ref) = refs
    else:
        hm_ref, g_ref, w_ref, pm_ref, u_ref = refs
    j = pl.program_id(0)

    @pl.when(j == 0)
    def _():
        x = hm_ref[...]
        ms = jnp.mean(x * x, axis=-1, keepdims=True)
        u_ref[...] = (x * lax.rsqrt(ms + EPS) * g_ref[...]).astype(BF16)

    p = jnp.dot(u_ref[...], w_ref[...], preferred_element_type=F32).astype(BF16)
    pm_ref[...] = p
    if not with_out:
        return
    sec_ref[j] = p

    @pl.when(j == pl.num_programs(0) - 1)
    def _():
        rq, rk, rv, rgate, dq, dk, dv, dgate = [sec_ref[s] for s in range(8)]
        t = lax.broadcasted_iota(jnp.int32, (BLK, BLK), 0)
        s_i = lax.broadcasted_iota(jnp.int32, (BLK, BLK), 1)
        dist = jnp.abs(t - s_i).astype(F32)
        r2 = lax.broadcasted_iota(jnp.int32, (2 * BLK, BLK), 0) % BLK
        c2 = lax.broadcasted_iota(jnp.int32, (2 * BLK, BLK), 1)
        dist2 = jnp.abs(r2 - c2).astype(F32)
        lam = _lambda(lq1_ref, lk1_ref, lq2_ref, lk2_ref, lam_init)
        r_parts, d_parts = [], []
        for h in range(HEADS):
            cols = slice(h * HEAD_DIM, (h + 1) * HEAD_DIM)
            q = _pad_rows(rq[:, cols], BLK)
            k = _pad_rows(rk[:, cols], BLK)
            v = _pad_rows(rv[:, cols], BLK)
            dmat = jnp.exp(_log_g(h) * dist) * (HEAD_DIM ** -0.5)
            s = lax.dot_general(q, k, NT, preferred_element_type=F32) * dmat
            o = jnp.dot(s.astype(BF16), v, preferred_element_type=F32)[:N_META]
            mu = jnp.mean(o, axis=-1, keepdims=True)
            oc = o - mu
            var = jnp.mean(oc * oc, axis=-1, keepdims=True)
            y = oc * lax.rsqrt(var + EPS) * gret_ref[:, cols]
            r_parts.append((y * _silu(rgate[:, cols].astype(F32))).astype(BF16))
            slope = 2.0 ** (-(h + 1.0))
            qs = _split_components(_pad_rows(dq[:, cols], BLK))
            k = _pad_rows(dk[:, cols], BLK)
            v = _pad_rows(dv[:, cols], BLK)
            bias = jnp.where(c2 < N_META, -slope * dist2, NEG)
            s = lax.dot_general(qs, k, NT, preferred_element_type=F32) + bias
            pr = jnp.exp(s - jnp.max(s, axis=-1, keepdims=True))
            a = (jnp.dot(pr.astype(BF16), v, preferred_element_type=F32)
                 / jnp.sum(pr, axis=-1, keepdims=True))
            d = a[:N_META] - lam * a[BLK:BLK + N_META]
            y = d * lax.rsqrt(jnp.mean(d * d, axis=-1, keepdims=True) + EPS)
            y = y * gdiff_ref[:, cols] * (1.0 - lam_init)
            d_parts.append((y * _silu(dgate[:, cols].astype(F32))).astype(BF16))
        r = jnp.concatenate(r_parts, axis=1)
        d = jnp.concatenate(d_parts, axis=1)
        y = jnp.dot(r, wr_ref[...], preferred_element_type=F32)
        y = y + jnp.dot(d, wd_ref[...], preferred_element_type=F32)
        hm_out_ref[...] = hm_ref[...] + y


def _meta_layer(hm, norm_g, w_in, w_out, gret, gdiff, lams, layer, lam_init, with_out):
    dm = hm.shape[1]
    n = w_in.shape[-1]
    tn = MIX_HALF
    vec = lambda width: pl.BlockSpec((None, 1, width), lambda j: (layer, 0, 0))
    in_specs = [
        pl.BlockSpec((N_META, dm), lambda j: (0, 0)),
        vec(dm),
        pl.BlockSpec((None, dm, tn), lambda j: (layer, 0, j)),
    ]
    args = [hm, norm_g, w_in]
    out_shape = [jax.ShapeDtypeStruct((N_META, n), BF16)]
    out_specs = [pl.BlockSpec((N_META, tn), lambda j: (0, j))]
    scratch = [pltpu.VMEM((N_META, dm), BF16)]
    if with_out:
        in_specs += [
            pl.BlockSpec((None, MIX_HALF, dm), lambda j: (layer, 0, 0)),
            pl.BlockSpec((None, MIX_HALF, dm), lambda j: (layer, 1, 0)),
            vec(MIX_HALF), vec(MIX_HALF),
            vec(DIFF_DH), vec(DIFF_DH), vec(DIFF_DH), vec(DIFF_DH),
        ]
        args += [w_out, w_out, gret, gdiff, *lams]
        out_shape.append(jax.ShapeDtypeStruct((N_META, dm), F32))
        out_specs.append(pl.BlockSpec((N_META, dm), lambda j: (0, 0)))
        scratch.append(pltpu.VMEM((n // tn, N_META, tn), BF16))
    outs = pl.pallas_call(
        functools.partial(_meta_kernel, lam_init=lam_init, with_out=with_out),
        out_shape=out_shape,
        grid=(n // tn,),
        in_specs=in_specs,
        out_specs=out_specs,
        scratch_shapes=scratch,
        compiler_params=pltpu.CompilerParams(
            dimension_semantics=("arbitrary",),
            vmem_limit_bytes=VMEM_LIMIT),
        name="meta_layer",
    )(*args)
    return outs if with_out else (outs[0], None)


def _inproj_kernel(h_ref, g_ref, w_ref, o_ref, u_ref):
    @pl.when(pl.program_id(1) == 0)
    def _():
        x = h_ref[...]
        ms = jnp.mean(x * x, axis=-1, keepdims=True)
        u_ref[...] = (x * lax.rsqrt(ms + EPS) * g_ref[...]).astype(BF16)

    o_ref[...] = jnp.dot(u_ref[...], w_ref[...], preferred_element_type=F32).astype(BF16)


def _inproj(h, g, w, layer, tm, tn):
    m, d = h.shape
    n = w.shape[-1]
    return pl.pallas_call(
        _inproj_kernel,
        out_shape=jax.ShapeDtypeStruct((m, n), BF16),
        grid=(m // tm, n // tn),
        in_specs=[
            pl.BlockSpec((tm, d), lambda i, j: (i, 0)),
            pl.BlockSpec((None, 1, d), lambda i, j: (layer, 0, 0)),
            pl.BlockSpec((None, d, tn), lambda i, j: (layer, 0, j)),
        ],
        out_specs=pl.BlockSpec((tm, tn), lambda i, j: (i, j)),
        scratch_shapes=[pltpu.VMEM((tm, d), BF16)],
        compiler_params=pltpu.CompilerParams(
            dimension_semantics=("parallel", "arbitrary"),
            vmem_limit_bytes=VMEM_LIMIT),
        name="inproj",
    )(h, g, w)


def _retention_kernel(q_ref, k_ref, v_ref, gate_ref, km_ref, vm_ref, gn_ref, o_ref,
                      state_ref, dmat_ref, rq_ref, rk_ref):
    j = pl.program_id(1)
    scale = HEAD_DIM ** -0.5

    @pl.when(j == 0)
    def _():
        t = lax.broadcasted_iota(jnp.int32, (BLK, BLK), 0)
        s = lax.broadcasted_iota(jnp.int32, (BLK, BLK), 1)
        dist = jnp.abs(t - s).astype(F32)
        allowed = (s // CHUNK) <= (t // CHUNK)
        tf = t.astype(F32)
        mf = lax.broadcasted_iota(jnp.int32, (N_META, HEAD_DIM), 0).astype(F32)
        for h in range(HEADS):
            cols = slice(h * HEAD_DIM, (h + 1) * HEAD_DIM)
            log_g = _log_g(h)
            dmat_ref[h] = jnp.where(allowed, jnp.exp(log_g * dist) * scale, 0.0)
            rq_ref[h] = jnp.exp(log_g * (tf + 1.0))
            rk_ref[h] = jnp.exp(log_g * (BLK - 1.0 - tf)) * scale
            zeta = jnp.exp(log_g * (N_META - 1.0 - mf)) * scale
            kz = (km_ref[:, cols].astype(F32) * zeta).astype(BF16)
            state_ref[h] = lax.dot_general(_pad_rows(kz, BLK), _pad_rows(vm_ref[:, cols], BLK),
                                           TN, preferred_element_type=F32)

    for h in range(HEADS):
        cols = slice(h * HEAD_DIM, (h + 1) * HEAD_DIM)
        decay_blk = float((1.0 - 2.0 ** (-5.0 - h)) ** BLK)
        q = q_ref[:, cols]
        k = k_ref[:, cols]
        v = v_ref[:, cols]
        s = lax.dot_general(q, k, NT, preferred_element_type=F32) * dmat_ref[h]
        intra = jnp.dot(s.astype(BF16), v, preferred_element_type=F32)
        state = state_ref[h]
        qx = (q.astype(F32) * rq_ref[h]).astype(BF16)
        cross = jnp.dot(qx, state.astype(BF16), preferred_element_type=F32)
        kz = (k.astype(F32) * rk_ref[h]).astype(BF16)
        kv = lax.dot_general(kz, v, TN, preferred_element_type=F32)
        state_ref[h] = decay_blk * state + kv

        o = intra + cross
        mu = jnp.mean(o, axis=-1, keepdims=True)
        oc = o - mu
        var = jnp.mean(oc * oc, axis=-1, keepdims=True)
        y = oc * lax.rsqrt(var + EPS) * gn_ref[:, cols]
        o_ref[:, cols] = (y * _silu(gate_ref[:, cols].astype(F32))).astype(BF16)


def _retention(proj, pm, gn, layer, batch, seq):
    m = proj.shape[0]
    nblk = seq // BLK
    row = lambda b, j: b * nblk + j
    return pl.pallas_call(
        _retention_kernel,
        out_shape=jax.ShapeDtypeStruct((m, MIX_HALF), BF16),
        grid=(batch, nblk),
        in_specs=[
            pl.BlockSpec((BLK, MIX_HALF), lambda b, j: (row(b, j), 0)),
            pl.BlockSpec((BLK, MIX_HALF), lambda b, j: (row(b, j), 1)),
            pl.BlockSpec((BLK, MIX_HALF), lambda b, j: (row(b, j), 2)),
            pl.BlockSpec((BLK, MIX_HALF), lambda b, j: (row(b, j), 3)),
            pl.BlockSpec((N_META, MIX_HALF), lambda b, j: (0, 1)),
            pl.BlockSpec((N_META, MIX_HALF), lambda b, j: (0, 2)),
            pl.BlockSpec((None, 1, MIX_HALF), lambda b, j: (layer, 0, 0)),
        ],
        out_specs=pl.BlockSpec((BLK, MIX_HALF), lambda b, j: (row(b, j), 0)),
        scratch_shapes=[
            pltpu.VMEM((HEADS, HEAD_DIM, HEAD_DIM), F32),
            pltpu.VMEM((HEADS, BLK, BLK), F32),
            pltpu.VMEM((HEADS, BLK, HEAD_DIM), F32),
            pltpu.VMEM((HEADS, BLK, HEAD_DIM), F32),
        ],
        compiler_params=pltpu.CompilerParams(
            dimension_semantics=("parallel", "arbitrary"),
            vmem_limit_bytes=VMEM_LIMIT),
        name="retention",
    )(proj, proj, proj, proj, pm, pm, gn)


def _diffattn_kernel(slopes_ref, q_ref, k_ref, v_ref, km_ref, vm_ref, gate_ref, gn_ref,
                     lq1_ref, lk1_ref, lq2_ref, lk2_ref, o_ref,
                     qs_ref, s_ref, sm_ref, mrun_ref, mrow_ref, acc_ref,
                     vext_ref, kmext_ref, vmext_ref, dbias_ref, *, lam_init):
    grp = pl.program_id(1)
    t = pl.program_id(2)
    hb = HEADS_PER_STEP
    seq = k_ref.shape[0]
    head_cols = lambda hh: slice(hh * HEAD_DIM, (hh + 1) * HEAD_DIM)

    @pl.when(t == 0)
    def _():
        r = lax.broadcasted_iota(jnp.int32, (QT, QT), 0)
        c = lax.broadcasted_iota(jnp.int32, (QT, QT), 1)
        allowed = (c // CHUNK) <= (r // CHUNK)
        rel = (r - jnp.abs(r - c)).astype(F32)
        for hh in range(hb):
            cols = head_cols(hh)
            dbias_ref[hh] = jnp.where(allowed, slopes_ref[grp * hb + hh] * rel, NEG)
            vext_ref[hh, :, :HEAD_DIM] = v_ref[:, cols]
            vext_ref[hh, :, HEAD_DIM:] = jnp.ones((seq, HEAD_DIM), BF16)
            kmext_ref[hh] = _pad_rows(km_ref[:, cols], BLK)
            vmext_ref[hh, :, :HEAD_DIM] = _pad_rows(vm_ref[:, cols], BLK)
            vmext_ref[hh, :, HEAD_DIM:] = jnp.ones((BLK, HEAD_DIM), BF16)

    colm = lax.broadcasted_iota(jnp.int32, (1, BLK), 1)
    relm = (colm - N_META - t * QT).astype(F32)
    for hh in range(hb):
        slope = slopes_ref[grp * hb + hh]
        qs_ref[hh] = _split_components(q_ref[:, head_cols(hh)])
        s = lax.dot_general(qs_ref[hh], kmext_ref[hh], NT, preferred_element_type=F32)
        s = s + jnp.where(colm < N_META, slope * relm, NEG)
        sm_ref[hh] = s
        mrun_ref[hh] = s

    def frame_scores(j, hh):
        start = pl.multiple_of(j * QT, QT)
        kt = k_ref[pl.ds(start, QT), head_cols(hh)]
        return lax.dot_general(qs_ref[hh], kt, NT, preferred_element_type=F32)

    def keep(j, hh, s):
        s_ref[hh, j] = s
        mrun_ref[hh] = jnp.maximum(mrun_ref[hh], jnp.maximum(s[:, :BLK], s[:, BLK:]))

    col = lax.broadcasted_iota(jnp.int32, (1, QT), 1)

    def off_diag(j, carry):
        rel = (col + (j - t) * QT).astype(F32)
        for hh in range(hb):
            keep(j, hh, frame_scores(j, hh) + slopes_ref[grp * hb + hh] * rel)
        return carry

    lax.fori_loop(0, t, off_diag, 0)

    for hh in range(hb):
        s = frame_scores(t, hh)
        keep(t, hh, jnp.concatenate([s[:QT] + dbias_ref[hh], s[QT:] + dbias_ref[hh]], axis=0))
        mrow_ref[hh] = jnp.broadcast_to(
            jnp.max(mrun_ref[hh], axis=-1, keepdims=True), (2 * QT, BLK))
        p = jnp.exp(sm_ref[hh] - mrow_ref[hh])
        acc_ref[hh] = jnp.dot(p.astype(BF16), vmext_ref[hh], preferred_element_type=F32)

    def probs(j, carry):
        start = pl.multiple_of(j * QT, QT)
        for hh in range(hb):
            s = s_ref[hh, j]
            m = mrow_ref[hh]
            p = jnp.concatenate([jnp.exp(s[:, :BLK] - m), jnp.exp(s[:, BLK:] - m)], axis=1)
            acc_ref[hh] = acc_ref[hh] + jnp.dot(
                p.astype(BF16), vext_ref[hh, pl.ds(start, QT), :], preferred_element_type=F32)
        return carry

    lax.fori_loop(0, t + 1, probs, 0)

    lam = _lambda(lq1_ref, lk1_ref, lq2_ref, lk2_ref, lam_init)
    for hh in range(hb):
        cols = head_cols(hh)
        acc = acc_ref[hh]
        a = acc[:, :HEAD_DIM] / acc[:, HEAD_DIM:]
        d = a[:QT] - lam * a[QT:]
        y = d * lax.rsqrt(jnp.mean(d * d, axis=-1, keepdims=True) + EPS)
        y = y * gn_ref[:, cols] * (1.0 - lam_init)
        o_ref[:, cols] = (y * _silu(gate_ref[:, cols].astype(F32))).astype(BF16)


def _diffattn(proj, pm, gn, lams, layer, batch, seq, lam_init):
    m = proj.shape[0]
    ntile = seq // QT
    hb = HEADS_PER_STEP
    ngrp = HEADS // hb
    width = hb * HEAD_DIM
    slopes = jnp.asarray(2.0 ** (-8.0 * np.arange(1, HEADS + 1) / HEADS), F32)
    qcol, kcol, vcol, gcol = (4 * ngrp, 5 * ngrp, 6 * ngrp, 7 * ngrp)
    vec = pl.BlockSpec((None, 1, DIFF_DH), lambda b, g, t: (layer, 0, 0))
    return pl.pallas_call(
        functools.partial(_diffattn_kernel, lam_init=lam_init),
        out_shape=jax.ShapeDtypeStruct((m, MIX_HALF), BF16),
        grid=(batch, ngrp, ntile),
        in_specs=[
            pl.BlockSpec(memory_space=pltpu.SMEM),
            pl.BlockSpec((QT, width), lambda b, g, t: (b * ntile + t, qcol + g)),
            pl.BlockSpec((seq, width), lambda b, g, t: (b, kcol + g)),
            pl.BlockSpec((seq, width), lambda b, g, t: (b, vcol + g)),
            pl.BlockSpec((N_META, width), lambda b, g, t: (0, kcol + g)),
            pl.BlockSpec((N_META, width), lambda b, g, t: (0, vcol + g)),
            pl.BlockSpec((QT, width), lambda b, g, t: (b * ntile + t, gcol + g)),
            pl.BlockSpec((None, 1, width), lambda b, g, t: (layer, 0, g)),
            vec, vec, vec, vec,
        ],
        out_specs=pl.BlockSpec((QT, width), lambda b, g, t: (b * ntile + t, g)),
        scratch_shapes=[
            pltpu.VMEM((hb, 2 * QT, HEAD_DIM), BF16),
            pltpu.VMEM((hb, ntile, 2 * QT, QT), F32),
            pltpu.VMEM((hb, 2 * QT, BLK), F32),
            pltpu.VMEM((hb, 2 * QT, BLK), F32),
            pltpu.VMEM((hb, 2 * QT, BLK), F32),
            pltpu.VMEM((hb, 2 * QT, 2 * HEAD_DIM), F32),
            pltpu.VMEM((hb, seq, 2 * HEAD_DIM), BF16),
            pltpu.VMEM((hb, BLK, HEAD_DIM), BF16),
            pltpu.VMEM((hb, BLK, 2 * HEAD_DIM), BF16),
            pltpu.VMEM((hb, QT, QT), F32),
        ],
        compiler_params=pltpu.CompilerParams(
            dimension_semantics=("parallel", "parallel", "arbitrary"),
            vmem_limit_bytes=VMEM_LIMIT),
        name="diffattn",
    )(slopes, proj, proj, proj, pm, pm, proj, gn, *lams)


def _outproj_kernel(r_ref, d_ref, wr_ref, wd_ref, h_ref, gf_ref, o_ref, *, final_norm):
    y = jnp.dot(r_ref[...], wr_ref[...], preferred_element_type=F32)
    y = y + jnp.dot(d_ref[...], wd_ref[...], preferred_element_type=F32)
    hn = h_ref[...] + y
    if final_norm:
        ms = jnp.mean(hn * hn, axis=-1, keepdims=True)
        hn = hn * lax.rsqrt(ms + EPS) * gf_ref[...]
    o_ref[...] = hn


def _outproj(r, d, w, h, gf, layer, tm, final_norm):
    m, dm = h.shape
    return pl.pallas_call(
        functools.partial(_outproj_kernel, final_norm=final_norm),
        out_shape=jax.ShapeDtypeStruct((m, dm), F32),
        grid=(m // tm,),
        in_specs=[
            pl.BlockSpec((tm, MIX_HALF), lambda i: (i, 0)),
            pl.BlockSpec((tm, MIX_HALF), lambda i: (i, 0)),
            pl.BlockSpec((None, MIX_HALF, dm), lambda i: (layer, 0, 0)),
            pl.BlockSpec((None, MIX_HALF, dm), lambda i: (layer, 1, 0)),
            pl.BlockSpec((tm, dm), lambda i: (i, 0)),
            pl.BlockSpec((1, dm), lambda i: (0, 0)),
        ],
        out_specs=pl.BlockSpec((tm, dm), lambda i: (i, 0)),
        compiler_params=pltpu.CompilerParams(
            dimension_semantics=("parallel",),
            vmem_limit_bytes=VMEM_LIMIT),
        name="outproj",
    )(r, d, w, w, h, gf)


def kernel(x, meta_tokens, norm_g, w_in, w_out, ret_norm_g, diff_norm_g,
           lambda_q1, lambda_k1, lambda_q2, lambda_k2, final_norm_g):
    batch, seq, dm = x.shape
    depth = w_in.shape[0]
    m = batch * seq
    assert seq % QT == 0 and meta_tokens.shape[0] == N_META

    h = x.reshape(m, dm)
    hm = meta_tokens.astype(x.dtype)
    w_in_b = w_in.astype(BF16)
    w_out_b = w_out.astype(BF16)
    gf = final_norm_g.reshape(1, dm)
    per_layer = lambda a: a.reshape(depth, 1, a.shape[-1])
    norm_g, ret_norm_g, diff_norm_g = map(per_layer, (norm_g, ret_norm_g, diff_norm_g))
    lams = tuple(map(per_layer, (lambda_q1, lambda_k1, lambda_q2, lambda_k2)))

    for i in range(depth):
        last = i == depth - 1
        lam_init = 0.8 - 0.6 * float(np.exp(-0.3 * i))
        pm, hm = _meta_layer(hm, norm_g, w_in_b, w_out_b, ret_norm_g, diff_norm_g, lams,
                             i, lam_init, with_out=not last)
        proj = _inproj(h, norm_g, w_in_b, i, tm=m // 8, tn=1024)
        r = _retention(proj, pm, ret_norm_g, i, batch, seq)
        d = _diffattn(proj, pm, diff_norm_g, lams, i, batch, seq, lam_init)
        h = _outproj(r, d, w_out_b, h, gf, i, tm=m // 16, final_norm=last)

    return h.reshape(batch, seq, dm)
```

```python
import functools

import numpy as np
import jax
import jax.numpy as jnp
from jax import lax
from jax.experimental import pallas as pl
from jax.experimental.pallas import tpu as pltpu

F32 = jnp.float32
BF16 = jnp.bfloat16

CHUNK = 64
N_META = 16
EPS = 1e-6
HEADS = 8
HEAD_DIM = 128
DIFF_DH = 64
MIX_HALF = HEADS * HEAD_DIM
BLK = 128
QT = 256
NEG = -1e30
LOG2E = 1.4426950408889634
HEADS_PER_STEP = 4

VMEM_LIMIT = 56 * 1024 * 1024

NT = (((1,), (1,)), ((), ()))
TN = (((0,), (0,)), ((), ()))


def _log_g(h):
    return float(np.log(1.0 - 2.0 ** (-5.0 - h)))


def _silu(x):
    return x * (1.0 / (1.0 + jnp.exp(-x)))


def _lambda(lq1_ref, lk1_ref, lq2_ref, lk2_ref, lam_init):
    return (jnp.exp(jnp.sum(lq1_ref[...] * lk1_ref[...], axis=-1, keepdims=True))
            - jnp.exp(jnp.sum(lq2_ref[...] * lk2_ref[...], axis=-1, keepdims=True))
            + lam_init)


def _pad_rows(a, rows):
    return jnp.concatenate([a, jnp.zeros((rows - a.shape[0], a.shape[1]), a.dtype)], axis=0)


def _split_components(q):
    lane = lax.broadcasted_iota(jnp.int32, q.shape, 1)
    zero = jnp.zeros((), BF16)
    qscaled = q * jnp.asarray(DIFF_DH ** -0.5, BF16)
    return jnp.concatenate([jnp.where(lane < DIFF_DH, qscaled, zero),
                            jnp.where(lane >= DIFF_DH, qscaled, zero)], axis=0)


def _meta_kernel(*refs, lam_init, with_out):
    if with_out:
        (hm_ref, g_ref, w_ref, wr_ref, wd_ref, gret_ref, gdiff_ref,
         lq1_ref, lk1_ref, lq2_ref, lk2_ref, pm_ref, hm_out_ref, u_ref, sec_ref) = refs
    else:
        hm_ref, g_ref, w_ref, pm_ref, u_ref = refs
    j = pl.program_id(0)

    @pl.when(j == 0)
    def _():
        x = hm_ref[...]
        ms = jnp.mean(x * x, axis=-1, keepdims=True)
        u_ref[...] = (x * lax.rsqrt(ms + EPS) * g_ref[...]).astype(BF16)

    p = jnp.dot(u_ref[...], w_ref[...].astype(BF16), preferred_element_type=F32).astype(BF16)
    pm_ref[...] = p
    if not with_out:
        return
    sec_ref[j] = p

    @pl.when(j == pl.num_programs(0) - 1)
    def _():
        rq, rk, rv, rgate, dq, dk, dv, dgate = [sec_ref[s] for s in range(8)]
        t = lax.broadcasted_iota(jnp.int32, (BLK, BLK), 0)
        s_i = lax.broadcasted_iota(jnp.int32, (BLK, BLK), 1)
        dist = jnp.abs(t - s_i).astype(F32)
        r2 = lax.broadcasted_iota(jnp.int32, (2 * BLK, BLK), 0) % BLK
        c2 = lax.broadcasted_iota(jnp.int32, (2 * BLK, BLK), 1)
        dist2 = jnp.abs(r2 - c2).astype(F32)
        lam = _lambda(lq1_ref, lk1_ref, lq2_ref, lk2_ref, lam_init)
        r_parts, d_parts = [], []
        for h in range(HEADS):
            cols = slice(h * HEAD_DIM, (h + 1) * HEAD_DIM)
            q = _pad_rows(rq[:, cols], BLK)
            k = _pad_rows(rk[:, cols], BLK)
            v = _pad_rows(rv[:, cols], BLK)
            dmat = jnp.exp(_log_g(h) * dist) * (HEAD_DIM ** -0.5)
            s = lax.dot_general(q, k, NT, preferred_element_type=F32) * dmat
            o = jnp.dot(s.astype(BF16), v, preferred_element_type=F32)[:N_META]
            mu = jnp.mean(o, axis=-1, keepdims=True)
            oc = o - mu
            var = jnp.mean(oc * oc, axis=-1, keepdims=True)
            y = oc * lax.rsqrt(var + EPS) * gret_ref[:, cols]
            r_parts.append((y * _silu(rgate[:, cols].astype(F32))).astype(BF16))
            slope = 2.0 ** (-(h + 1.0))
            qs = _split_components(_pad_rows(dq[:, cols], BLK))
            k = _pad_rows(dk[:, cols], BLK)
            v = _pad_rows(dv[:, cols], BLK)
            bias = jnp.where(c2 < N_META, -slope * dist2, NEG)
            s = lax.dot_general(qs, k, NT, preferred_element_type=F32) + bias
            pr = jnp.exp(s - jnp.max(s, axis=-1, keepdims=True))
            a = (jnp.dot(pr.astype(BF16), v, preferred_element_type=F32)
                 / jnp.sum(pr, axis=-1, keepdims=True))
            d = a[:N_META] - lam * a[BLK:BLK + N_META]
            y = d * lax.rsqrt(jnp.mean(d * d, axis=-1, keepdims=True) + EPS)
            y = y * gdiff_ref[:, cols] * (1.0 - lam_init)
            d_parts.append((y * _silu(dgate[:, cols].astype(F32))).astype(BF16))
        r = jnp.concatenate(r_parts, axis=1)
        d = jnp.concatenate(d_parts, axis=1)
        y = jnp.dot(r, wr_ref[...], preferred_element_type=F32)
        y = y + jnp.dot(d, wd_ref[...], preferred_element_type=F32)
        hm_out_ref[...] = hm_ref[...] + y


def _meta_layer(hm, norm_g, w_in, w_out, gret, gdiff, lams, layer, lam_init, with_out):
    dm = hm.shape[1]
    n = w_in.shape[-1]
    tn = MIX_HALF
    vec = lambda width: pl.BlockSpec((None, 1, width), lambda j: (layer, 0, 0))
    in_specs = [
        pl.BlockSpec((N_META, dm), lambda j: (0, 0)),
        vec(dm),
        pl.BlockSpec((None, dm, tn), lambda j: (layer, 0, j)),
    ]
    args = [hm, norm_g, w_in]
    out_shape = [jax.ShapeDtypeStruct((N_META, n), BF16)]
    out_specs = [pl.BlockSpec((N_META, tn), lambda j: (0, j))]
    scratch = [pltpu.VMEM((N_META, dm), BF16)]
    if with_out:
        in_specs += [
            pl.BlockSpec((None, MIX_HALF, dm), lambda j: (layer, 0, 0)),
            pl.BlockSpec((None, MIX_HALF, dm), lambda j: (layer, 1, 0)),
            vec(MIX_HALF), vec(MIX_HALF),
            vec(DIFF_DH), vec(DIFF_DH), vec(DIFF_DH), vec(DIFF_DH),
        ]
        args += [w_out, w_out, gret, gdiff, *lams]
        out_shape.append(jax.ShapeDtypeStruct((N_META, dm), F32))
        out_specs.append(pl.BlockSpec((N_META, dm), lambda j: (0, 0)))
        scratch.append(pltpu.VMEM((n // tn, N_META, tn), BF16))
    outs = pl.pallas_call(
        functools.partial(_meta_kernel, lam_init=lam_init, with_out=with_out),
        out_shape=out_shape,
        grid=(n // tn,),
        in_specs=in_specs,
        out_specs=out_specs,
        scratch_shapes=scratch,
        compiler_params=pltpu.CompilerParams(
            dimension_semantics=("arbitrary",),
            vmem_limit_bytes=VMEM_LIMIT),
        name="meta_layer",
    )(*args)
    return outs if with_out else (outs[0], None)


def _inproj_kernel(h_ref, g_ref, w_ref, o_ref, u_ref):
    @pl.when(pl.program_id(1) == 0)
    def _():
        x = h_ref[...]
        ms = jnp.mean(x * x, axis=-1, keepdims=True)
        u_ref[...] = (x * lax.rsqrt(ms + EPS) * g_ref[...]).astype(BF16)

    o_ref[...] = jnp.dot(u_ref[...], w_ref[...].astype(BF16),
                         preferred_element_type=F32).astype(BF16)


def _inproj(h, g, w, layer, tm, tn):
    m, d = h.shape
    n = w.shape[-1]
    return pl.pallas_call(
        _inproj_kernel,
        out_shape=jax.ShapeDtypeStruct((m, n), BF16),
        grid=(m // tm, n // tn),
        in_specs=[
            pl.BlockSpec((tm, d), lambda i, j: (i, 0)),
            pl.BlockSpec((None, 1, d), lambda i, j: (layer, 0, 0)),
            pl.BlockSpec((None, d, tn), lambda i, j: (layer, 0, j)),
        ],
        out_specs=pl.BlockSpec((tm, tn), lambda i, j: (i, j)),
        scratch_shapes=[pltpu.VMEM((tm, d), BF16)],
        compiler_params=pltpu.CompilerParams(
            dimension_semantics=("parallel", "arbitrary"),
            vmem_limit_bytes=VMEM_LIMIT),
        name="inproj",
    )(h, g, w)


def _retention_kernel(q_ref, k_ref, v_ref, gate_ref, km_ref, vm_ref, gn_ref, o_ref,
                      state_ref, dmat_ref, rq_ref, rk_ref):
    j = pl.program_id(1)
    scale = HEAD_DIM ** -0.5

    @pl.when(j == 0)
    def _():
        t = lax.broadcasted_iota(jnp.int32, (BLK, BLK), 0)
        s = lax.broadcasted_iota(jnp.int32, (BLK, BLK), 1)
        dist = jnp.abs(t - s).astype(F32)
        allowed = (s // CHUNK) <= (t // CHUNK)
        tf = t.astype(F32)
        mf = lax.broadcasted_iota(jnp.int32, (N_META, HEAD_DIM), 0).astype(F32)
        for h in range(HEADS):
            cols = slice(h * HEAD_DIM, (h + 1) * HEAD_DIM)
            log_g = _log_g(h)
            dmat_ref[h] = jnp.where(allowed, jnp.exp(log_g * dist) * scale, 0.0)
            rq_ref[h] = jnp.exp(log_g * (tf + 1.0))
            rk_ref[h] = jnp.exp(log_g * (BLK - 1.0 - tf)) * scale
            zeta = jnp.exp(log_g * (N_META - 1.0 - mf)) * scale
            kz = (km_ref[:, cols].astype(F32) * zeta).astype(BF16)
            state_ref[h] = lax.dot_general(_pad_rows(kz, BLK), _pad_rows(vm_ref[:, cols], BLK),
                                           TN, preferred_element_type=F32)

    for h in range(HEADS):
        cols = slice(h * HEAD_DIM, (h + 1) * HEAD_DIM)
        decay_blk = float((1.0 - 2.0 ** (-5.0 - h)) ** BLK)
        q = q_ref[:, cols]
        k = k_ref[:, cols]
        v = v_ref[:, cols]
        s = lax.dot_general(q, k, NT, preferred_element_type=F32) * dmat_ref[h]
        intra = jnp.dot(s.astype(BF16), v, preferred_element_type=F32)
        state = state_ref[h]
        qx = (q.astype(F32) * rq_ref[h]).astype(BF16)
        cross = jnp.dot(qx, state.astype(BF16), preferred_element_type=F32)
        kz = (k.astype(F32) * rk_ref[h]).astype(BF16)
        kv = lax.dot_general(kz, v, TN, preferred_element_type=F32)
        state_ref[h] = decay_blk * state + kv

        o = intra + cross
        mu = jnp.mean(o, axis=-1, keepdims=True)
        oc = o - mu
        var = jnp.mean(oc * oc, axis=-1, keepdims=True)
        y = oc * lax.rsqrt(var + EPS) * gn_ref[:, cols]
        o_ref[:, cols] = (y * _silu(gate_ref[:, cols].astype(F32))).astype(BF16)


def _retention(proj, pm, gn, layer, batch, seq):
    m = proj.shape[0]
    nblk = seq // BLK
    row = lambda b, j: b * nblk + j
    return pl.pallas_call(
        _retention_kernel,
        out_shape=jax.ShapeDtypeStruct((m, MIX_HALF), BF16),
        grid=(batch, nblk),
        in_specs=[
            pl.BlockSpec((BLK, MIX_HALF), lambda b, j: (row(b, j), 0)),
            pl.BlockSpec((BLK, MIX_HALF), lambda b, j: (row(b, j), 1)),
            pl.BlockSpec((BLK, MIX_HALF), lambda b, j: (row(b, j), 2)),
            pl.BlockSpec((BLK, MIX_HALF), lambda b, j: (row(b, j), 3)),
            pl.BlockSpec((N_META, MIX_HALF), lambda b, j: (0, 1)),
            pl.BlockSpec((N_META, MIX_HALF), lambda b, j: (0, 2)),
            pl.BlockSpec((None, 1, MIX_HALF), lambda b, j: (layer, 0, 0)),
        ],
        out_specs=pl.BlockSpec((BLK, MIX_HALF), lambda b, j: (row(b, j), 0)),
        scratch_shapes=[
            pltpu.VMEM((HEADS, HEAD_DIM, HEAD_DIM), F32),
            pltpu.VMEM((HEADS, BLK, BLK), F32),
            pltpu.VMEM((HEADS, BLK, HEAD_DIM), F32),
            pltpu.VMEM((HEADS, BLK, HEAD_DIM), F32),
        ],
        compiler_params=pltpu.CompilerParams(
            dimension_semantics=("parallel", "arbitrary"),
            vmem_limit_bytes=VMEM_LIMIT),
        name="retention",
    )(proj, proj, proj, proj, pm, pm, gn)


def _diffattn_kernel(slopes_ref, q_ref, k_ref, v_ref, km_ref, vm_ref, gate_ref, gn_ref,
                     lq1_ref, lk1_ref, lq2_ref, lk2_ref, o_ref,
                     qs_ref, s_ref, sm_ref, mrun_ref, mrow_ref, acc_ref,
                     vext_ref, kmext_ref, vmext_ref, dbias_ref, *, lam_init):
    grp = pl.program_id(1)
    t = pl.program_id(2)
    hb = HEADS_PER_STEP
    seq = k_ref.shape[0]
    head_cols = lambda hh: slice(hh * HEAD_DIM, (hh + 1) * HEAD_DIM)
    slope2 = lambda hh: slopes_ref[grp * hb + hh] * LOG2E

    @pl.when(t == 0)
    def _():
        r = lax.broadcasted_iota(jnp.int32, (QT, QT), 0)
        c = lax.broadcasted_iota(jnp.int32, (QT, QT), 1)
        allowed = (c // CHUNK) <= (r // CHUNK)
        rel = (r - jnp.abs(r - c)).astype(F32)
        for hh in range(hb):
            cols = head_cols(hh)
            dbias_ref[hh] = jnp.where(allowed, slope2(hh) * rel, NEG)
            vext_ref[hh, :, :HEAD_DIM] = v_ref[:, cols]
            vext_ref[hh, :, HEAD_DIM:] = jnp.ones((seq, HEAD_DIM), BF16)
            kmext_ref[hh] = _pad_rows(km_ref[:, cols], BLK)
            vmext_ref[hh, :, :HEAD_DIM] = _pad_rows(vm_ref[:, cols], BLK)
            vmext_ref[hh, :, HEAD_DIM:] = jnp.ones((BLK, HEAD_DIM), BF16)

    colm = lax.broadcasted_iota(jnp.int32, (1, BLK), 1)
    relm = (colm - N_META - t * QT).astype(F32)
    for hh in range(hb):
        qs_ref[hh] = _split_components(q_ref[:, head_cols(hh)])
        s = lax.dot_general(qs_ref[hh], kmext_ref[hh], NT, preferred_element_type=F32)
        s = s * LOG2E + jnp.where(colm < N_META, slope2(hh) * relm, NEG)
        sm_ref[hh] = s
        mrun_ref[hh] = s

    def frame_scores(j, hh):
        start = pl.multiple_of(j * QT, QT)
        kt = k_ref[pl.ds(start, QT), head_cols(hh)]
        return lax.dot_general(qs_ref[hh], kt, NT, preferred_element_type=F32) * LOG2E

    def keep(j, hh, s):
        s_ref[hh, j] = s
        mrun_ref[hh] = jnp.maximum(mrun_ref[hh], jnp.maximum(s[:, :BLK], s[:, BLK:]))

    col = lax.broadcasted_iota(jnp.int32, (1, QT), 1)

    def off_diag(j):
        rel = (col + (j - t) * QT).astype(F32)
        for hh in range(hb):
            keep(j, hh, frame_scores(j, hh) + slope2(hh) * rel)

    def off_diag_pair(jp, carry):
        off_diag(2 * jp)
        off_diag(2 * jp + 1)
        return carry

    lax.fori_loop(0, t // 2, off_diag_pair, 0)

    @pl.when(t % 2 == 1)
    def _():
        off_diag(t - 1)

    for hh in range(hb):
        s = frame_scores(t, hh)
        keep(t, hh, jnp.concatenate([s[:QT] + dbias_ref[hh], s[QT:] + dbias_ref[hh]], axis=0))
        mrow_ref[hh] = jnp.broadcast_to(
            jnp.max(mrun_ref[hh], axis=-1, keepdims=True), (2 * QT, BLK))
        p = jnp.exp2(sm_ref[hh] - mrow_ref[hh])
        acc_ref[hh] = jnp.dot(p.astype(BF16), vmext_ref[hh], preferred_element_type=F32)

    def pv(j, hh):
        start = pl.multiple_of(j * QT, QT)
        s = s_ref[hh, j]
        m = mrow_ref[hh]
        p = jnp.concatenate([jnp.exp2(s[:, :BLK] - m), jnp.exp2(s[:, BLK:] - m)], axis=1)
        return jnp.dot(p.astype(BF16), vext_ref[hh, pl.ds(start, QT), :],
                       preferred_element_type=F32)

    def pv_pair(jp, carry):
        for hh in range(hb):
            acc_ref[hh] = acc_ref[hh] + (pv(2 * jp, hh) + pv(2 * jp + 1, hh))
        return carry

    lax.fori_loop(0, (t + 1) // 2, pv_pair, 0)

    @pl.when(t % 2 == 0)
    def _():
        for hh in range(hb):
            acc_ref[hh] = acc_ref[hh] + pv(t, hh)

    lam = _lambda(lq1_ref, lk1_ref, lq2_ref, lk2_ref, lam_init)
    for hh in range(hb):
        cols = head_cols(hh)
        acc = acc_ref[hh]
        a = acc[:, :HEAD_DIM] / acc[:, HEAD_DIM:]
        d = a[:QT] - lam * a[QT:]
        y = d * lax.rsqrt(jnp.mean(d * d, axis=-1, keepdims=True) + EPS)
        y = y * gn_ref[:, cols] * (1.0 - lam_init)
        o_ref[:, cols] = (y * _silu(gate_ref[:, cols].astype(F32))).astype(BF16)


def _diffattn(proj, pm, gn, lams, layer, batch, seq, lam_init):
    m = proj.shape[0]
    ntile = seq // QT
    hb = HEADS_PER_STEP
    ngrp = HEADS // hb
    width = hb * HEAD_DIM
    slopes = jnp.asarray(2.0 ** (-8.0 * np.arange(1, HEADS + 1) / HEADS), F32)
    qcol, kcol, vcol, gcol = (4 * ngrp, 5 * ngrp, 6 * ngrp, 7 * ngrp)
    vec = pl.BlockSpec((None, 1, DIFF_DH), lambda b, g, t: (layer, 0, 0))
    return pl.pallas_call(
        functools.partial(_diffattn_kernel, lam_init=lam_init),
        out_shape=jax.ShapeDtypeStruct((m, MIX_HALF), BF16),
        grid=(batch, ngrp, ntile),
        in_specs=[
            pl.BlockSpec(memory_space=pltpu.SMEM),
            pl.BlockSpec((QT, width), lambda b, g, t: (b * ntile + t, qcol + g)),
            pl.BlockSpec((seq, width), lambda b, g, t: (b, kcol + g)),
            pl.BlockSpec((seq, width), lambda b, g, t: (b, vcol + g)),
            pl.BlockSpec((N_META, width), lambda b, g, t: (0, kcol + g)),
            pl.BlockSpec((N_META, width), lambda b, g, t: (0, vcol + g)),
            pl.BlockSpec((QT, width), lambda b, g, t: (b * ntile + t, gcol + g)),
            pl.BlockSpec((None, 1, width), lambda b, g, t: (layer, 0, g)),
            vec, vec, vec, vec,
        ],
        out_specs=pl.BlockSpec((QT, width), lambda b, g, t: (b * ntile + t, g)),
        scratch_shapes=[
            pltpu.VMEM((hb, 2 * QT, HEAD_DIM), BF16),
            pltpu.VMEM((hb, ntile, 2 * QT, QT), F32),
            pltpu.VMEM((hb, 2 * QT, BLK), F32),
            pltpu.VMEM((hb, 2 * QT, BLK), F32),
            pltpu.VMEM((hb, 2 * QT, BLK), F32),
            pltpu.VMEM((hb, 2 * QT, 2 * HEAD_DIM), F32),
            pltpu.VMEM((hb, seq, 2 * HEAD_DIM), BF16),
            pltpu.VMEM((hb, BLK, HEAD_DIM), BF16),
            pltpu.VMEM((hb, BLK, 2 * HEAD_DIM), BF16),
            pltpu.VMEM((hb, QT, QT), F32),
        ],
        compiler_params=pltpu.CompilerParams(
            dimension_semantics=("parallel", "parallel", "arbitrary"),
            vmem_limit_bytes=VMEM_LIMIT),
        name="diffattn",
    )(slopes, proj, proj, proj, pm, pm, proj, gn, *lams)


def _outproj_kernel(r_ref, d_ref, wr_ref, wd_ref, h_ref, gf_ref, o_ref, *, final_norm):
    y = jnp.dot(r_ref[...], wr_ref[...], preferred_element_type=F32)
    y = y + jnp.dot(d_ref[...], wd_ref[...], preferred_element_type=F32)
    hn = h_ref[...] + y
    if final_norm:
        ms = jnp.mean(hn * hn, axis=-1, keepdims=True)
        hn = hn * lax.rsqrt(ms + EPS) * gf_ref[...]
    o_ref[...] = hn


def _outproj(r, d, w, h, gf, layer, tm, final_norm):
    m, dm = h.shape
    return pl.pallas_call(
        functools.partial(_outproj_kernel, final_norm=final_norm),
        out_shape=jax.ShapeDtypeStruct((m, dm), F32),
        grid=(m // tm,),
        in_specs=[
            pl.BlockSpec((tm, MIX_HALF), lambda i: (i, 0)),
            pl.BlockSpec((tm, MIX_HALF), lambda i: (i, 0)),
            pl.BlockSpec((None, MIX_HALF, dm), lambda i: (layer, 0, 0)),
            pl.BlockSpec((None, MIX_HALF, dm), lambda i: (layer, 1, 0)),
            pl.BlockSpec((tm, dm), lambda i: (i, 0)),
            pl.BlockSpec((1, dm), lambda i: (0, 0)),
        ],
        out_specs=pl.BlockSpec((tm, dm), lambda i: (i, 0)),
        compiler_params=pltpu.CompilerParams(
            dimension_semantics=("parallel",),
            vmem_limit_bytes=VMEM_LIMIT),
        name="outproj",
    )(r, d, w, w, h, gf)


def kernel(x, meta_tokens, norm_g, w_in, w_out, ret_norm_g, diff_norm_g,
           lambda_q1, lambda_k1, lambda_q2, lambda_k2, final_norm_g):
    batch, seq, dm = x.shape
    depth = w_in.shape[0]
    m = batch * seq
    assert seq % QT == 0 and meta_tokens.shape[0] == N_META

    h = x.reshape(m, dm)
    hm = meta_tokens.astype(x.dtype)
    w_out_b = w_out.astype(BF16)
    gf = final_norm_g.reshape(1, dm)
    per_layer = lambda a: a.reshape(depth, 1, a.shape[-1])
    norm_g, ret_norm_g, diff_norm_g = map(per_layer, (norm_g, ret_norm_g, diff_norm_g))
    lams = tuple(map(per_layer, (lambda_q1, lambda_k1, lambda_q2, lambda_k2)))

    for i in range(depth):
        last = i == depth - 1
        lam_init = 0.8 - 0.6 * float(np.exp(-0.3 * i))
        pm, hm = _meta_layer(hm, norm_g, w_in, w_out_b, ret_norm_g, diff_norm_g, lams,
                             i, lam_init, with_out=not last)
        proj = _inproj(h, norm_g, w_in, i, tm=m // 8, tn=1024)
        r = _retention(proj, pm, ret_norm_g, i, batch, seq)
        d = _diffattn(proj, pm, diff_norm_g, lams, i, batch, seq, lam_init)
        h = _outproj(r, d, w_out_b, h, gf, i, tm=m // 16, final_norm=last)

    return h.reshape(batch, seq, dm)
```

```python
import functools

import numpy as np
import jax
import jax.numpy as jnp
from jax import lax
from jax.experimental import pallas as pl
from jax.experimental.pallas import tpu as pltpu

F32 = jnp.float32
BF16 = jnp.bfloat16

CHUNK = 64
N_META = 16
EPS = 1e-6
HEADS = 8
HEAD_DIM = 128
DIFF_DH = 64
MIX_HALF = HEADS * HEAD_DIM
BLK = 128
QT = 256
NEG = -1e30
LOG2E = 1.4426950408889634
HEADS_PER_STEP = 4

VMEM_LIMIT = 56 * 1024 * 1024

NT = (((1,), (1,)), ((), ()))
TN = (((0,), (0,)), ((), ()))


def _log_g(h):
    return float(np.log(1.0 - 2.0 ** (-5.0 - h)))


def _silu(x):
    return x * (1.0 / (1.0 + jnp.exp(-x)))


def _lambda(lq1_ref, lk1_ref, lq2_ref, lk2_ref, lam_init):
    return (jnp.exp(jnp.sum(lq1_ref[...] * lk1_ref[...], axis=-1, keepdims=True))
            - jnp.exp(jnp.sum(lq2_ref[...] * lk2_ref[...], axis=-1, keepdims=True))
            + lam_init)


def _pad_rows(a, rows):
    return jnp.concatenate([a, jnp.zeros((rows - a.shape[0], a.shape[1]), a.dtype)], axis=0)


def _split_components(q):
    lane = lax.broadcasted_iota(jnp.int32, q.shape, 1)
    zero = jnp.zeros((), BF16)
    qscaled = q * jnp.asarray(DIFF_DH ** -0.5, BF16)
    return jnp.concatenate([jnp.where(lane < DIFF_DH, qscaled, zero),
                            jnp.where(lane >= DIFF_DH, qscaled, zero)], axis=0)


def _meta_kernel(*refs, lam_init, with_out):
    if with_out:
        (hm_ref, g_ref, w_ref, wr_ref, wd_ref, gret_ref, gdiff_ref,
         lq1_ref, lk1_ref, lq2_ref, lk2_ref, pm_ref, hm_out_ref, u_ref, sec_ref) = refs
    else:
        hm_ref, g_ref, w_ref, pm_ref, u_ref = refs
    j = pl.program_id(0)

    @pl.when(j == 0)
    def _():
        x = hm_ref[...]
        ms = jnp.mean(x * x, axis=-1, keepdims=True)
        u_ref[...] = (x * lax.rsqrt(ms + EPS) * g_ref[...]).astype(BF16)

    p = jnp.dot(u_ref[...], w_ref[...].astype(BF16), preferred_element_type=F32).astype(BF16)
    pm_ref[...] = p
    if not with_out:
        return
    sec_ref[j] = p

    @pl.when(j == pl.num_programs(0) - 1)
    def _():
        rq, rk, rv, rgate, dq, dk, dv, dgate = [sec_ref[s] for s in range(8)]
        t = lax.broadcasted_iota(jnp.int32, (BLK, BLK), 0)
        s_i = lax.broadcasted_iota(jnp.int32, (BLK, BLK), 1)
        dist = jnp.abs(t - s_i).astype(F32)
        r2 = lax.broadcasted_iota(jnp.int32, (2 * BLK, BLK), 0) % BLK
        c2 = lax.broadcasted_iota(jnp.int32, (2 * BLK, BLK), 1)
        dist2 = jnp.abs(r2 - c2).astype(F32)
        lam = _lambda(lq1_ref, lk1_ref, lq2_ref, lk2_ref, lam_init)
        r_parts, d_parts = [], []
        for h in range(HEADS):
            cols = slice(h * HEAD_DIM, (h + 1) * HEAD_DIM)
            q = _pad_rows(rq[:, cols], BLK)
            k = _pad_rows(rk[:, cols], BLK)
            v = _pad_rows(rv[:, cols], BLK)
            dmat = jnp.exp(_log_g(h) * dist) * (HEAD_DIM ** -0.5)
            s = lax.dot_general(q, k, NT, preferred_element_type=F32) * dmat
            o = jnp.dot(s.astype(BF16), v, preferred_element_type=F32)[:N_META]
            mu = jnp.mean(o, axis=-1, keepdims=True)
            oc = o - mu
            var = jnp.mean(oc * oc, axis=-1, keepdims=True)
            y = oc * lax.rsqrt(var + EPS) * gret_ref[:, cols]
            r_parts.append((y * _silu(rgate[:, cols].astype(F32))).astype(BF16))
            slope = 2.0 ** (-(h + 1.0))
            qs = _split_components(_pad_rows(dq[:, cols], BLK))
            k = _pad_rows(dk[:, cols], BLK)
            v = _pad_rows(dv[:, cols], BLK)
            bias = jnp.where(c2 < N_META, -slope * dist2, NEG)
            s = lax.dot_general(qs, k, NT, preferred_element_type=F32) + bias
            pr = jnp.exp(s - jnp.max(s, axis=-1, keepdims=True))
            a = (jnp.dot(pr.astype(BF16), v, preferred_element_type=F32)
                 / jnp.sum(pr, axis=-1, keepdims=True))
            d = a[:N_META] - lam * a[BLK:BLK + N_META]
            y = d * lax.rsqrt(jnp.mean(d * d, axis=-1, keepdims=True) + EPS)
            y = y * gdiff_ref[:, cols] * (1.0 - lam_init)
            d_parts.append((y * _silu(dgate[:, cols].astype(F32))).astype(BF16))
        r = jnp.concatenate(r_parts, axis=1)
        d = jnp.concatenate(d_parts, axis=1)
        y = jnp.dot(r, wr_ref[...].astype(BF16), preferred_element_type=F32)
        y = y + jnp.dot(d, wd_ref[...].astype(BF16), preferred_element_type=F32)
        hm_out_ref[...] = hm_ref[...] + y


def _meta_layer(hm, norm_g, w_in, w_out, gret, gdiff, lams, layer, lam_init, with_out):
    dm = hm.shape[1]
    n = w_in.shape[-1]
    tn = MIX_HALF
    vec = lambda width: pl.BlockSpec((None, 1, width), lambda j: (layer, 0, 0))
    in_specs = [
        pl.BlockSpec((N_META, dm), lambda j: (0, 0)),
        vec(dm),
        pl.BlockSpec((None, dm, tn), lambda j: (layer, 0, j)),
    ]
    args = [hm, norm_g, w_in]
    out_shape = [jax.ShapeDtypeStruct((N_META, n), BF16)]
    out_specs = [pl.BlockSpec((N_META, tn), lambda j: (0, j))]
    scratch = [pltpu.VMEM((N_META, dm), BF16)]
    if with_out:
        in_specs += [
            pl.BlockSpec((None, MIX_HALF, dm), lambda j: (layer, 0, 0),
                         pipeline_mode=pl.Buffered(1)),
            pl.BlockSpec((None, MIX_HALF, dm), lambda j: (layer, 1, 0),
                         pipeline_mode=pl.Buffered(1)),
            vec(MIX_HALF), vec(MIX_HALF),
            vec(DIFF_DH), vec(DIFF_DH), vec(DIFF_DH), vec(DIFF_DH),
        ]
        args += [w_out, w_out, gret, gdiff, *lams]
        out_shape.append(jax.ShapeDtypeStruct((N_META, dm), F32))
        out_specs.append(pl.BlockSpec((N_META, dm), lambda j: (0, 0)))
        scratch.append(pltpu.VMEM((n // tn, N_META, tn), BF16))
    outs = pl.pallas_call(
        functools.partial(_meta_kernel, lam_init=lam_init, with_out=with_out),
        out_shape=out_shape,
        grid=(n // tn,),
        in_specs=in_specs,
        out_specs=out_specs,
        scratch_shapes=scratch,
        compiler_params=pltpu.CompilerParams(
            dimension_semantics=("arbitrary",),
            vmem_limit_bytes=VMEM_LIMIT),
        name="meta_layer",
    )(*args)
    return outs if with_out else (outs[0], None)


def _inproj_kernel(h_ref, g_ref, w_ref, o_ref, u_ref):
    @pl.when(pl.program_id(1) == 0)
    def _():
        x = h_ref[...]
        ms = jnp.mean(x * x, axis=-1, keepdims=True)
        u_ref[...] = (x * lax.rsqrt(ms + EPS) * g_ref[...]).astype(BF16)

    o_ref[...] = jnp.dot(u_ref[...], w_ref[...].astype(BF16),
                         preferred_element_type=F32).astype(BF16)


def _inproj(h, g, w, layer, tm, tn):
    m, d = h.shape
    n = w.shape[-1]
    return pl.pallas_call(
        _inproj_kernel,
        out_shape=jax.ShapeDtypeStruct((m, n), BF16),
        grid=(m // tm, n // tn),
        in_specs=[
            pl.BlockSpec((tm, d), lambda i, j: (i, 0)),
            pl.BlockSpec((None, 1, d), lambda i, j: (layer, 0, 0)),
            pl.BlockSpec((None, d, tn), lambda i, j: (layer, 0, j)),
        ],
        out_specs=pl.BlockSpec((tm, tn), lambda i, j: (i, j)),
        scratch_shapes=[pltpu.VMEM((tm, d), BF16)],
        compiler_params=pltpu.CompilerParams(
            dimension_semantics=("parallel", "arbitrary"),
            vmem_limit_bytes=VMEM_LIMIT),
        name="inproj",
    )(h, g, w)


def _retention_kernel(q_ref, k_ref, v_ref, gate_ref, km_ref, vm_ref, gn_ref, o_ref,
                      state_ref, dmat_ref, rq_ref, rk_ref):
    j = pl.program_id(1)
    scale = HEAD_DIM ** -0.5

    @pl.when(j == 0)
    def _():
        t = lax.broadcasted_iota(jnp.int32, (BLK, BLK), 0)
        s = lax.broadcasted_iota(jnp.int32, (BLK, BLK), 1)
        dist = jnp.abs(t - s).astype(F32)
        allowed = (s // CHUNK) <= (t // CHUNK)
        tf = t.astype(F32)
        mf = lax.broadcasted_iota(jnp.int32, (N_META, HEAD_DIM), 0).astype(F32)
        for h in range(HEADS):
            cols = slice(h * HEAD_DIM, (h + 1) * HEAD_DIM)
            log_g = _log_g(h)
            dmat_ref[h] = jnp.where(allowed, jnp.exp(log_g * dist) * scale, 0.0)
            rq_ref[h] = jnp.exp(log_g * (tf + 1.0))
            rk_ref[h] = jnp.exp(log_g * (BLK - 1.0 - tf)) * scale
            zeta = jnp.exp(log_g * (N_META - 1.0 - mf)) * scale
            kz = (km_ref[:, cols].astype(F32) * zeta).astype(BF16)
            state_ref[h] = lax.dot_general(_pad_rows(kz, BLK), _pad_rows(vm_ref[:, cols], BLK),
                                           TN, preferred_element_type=F32)

    def block(i, carry):
        rows = pl.ds(pl.multiple_of(i * BLK, BLK), BLK)
        for h in range(HEADS):
            cols = slice(h * HEAD_DIM, (h + 1) * HEAD_DIM)
            decay_blk = float((1.0 - 2.0 ** (-5.0 - h)) ** BLK)
            q = q_ref[rows, cols]
            k = k_ref[rows, cols]
            v = v_ref[rows, cols]
            s = lax.dot_general(q, k, NT, preferred_element_type=F32) * dmat_ref[h]
            intra = jnp.dot(s.astype(BF16), v, preferred_element_type=F32)
            state = state_ref[h]
            qx = (q.astype(F32) * rq_ref[h]).astype(BF16)
            cross = jnp.dot(qx, state.astype(BF16), preferred_element_type=F32)
            kz = (k.astype(F32) * rk_ref[h]).astype(BF16)
            kv = lax.dot_general(kz, v, TN, preferred_element_type=F32)
            state_ref[h] = decay_blk * state + kv

            o = intra + cross
            mu = jnp.mean(o, axis=-1, keepdims=True)
            oc = o - mu
            var = jnp.mean(oc * oc, axis=-1, keepdims=True)
            y = oc * lax.rsqrt(var + EPS) * gn_ref[:, cols]
            o_ref[rows, cols] = (y * _silu(gate_ref[rows, cols].astype(F32))).astype(BF16)
        return carry

    lax.fori_loop(0, q_ref.shape[0] // BLK, block, 0)


RET_ROWS = 512


def _retention(proj, pm, gn, layer, batch, seq):
    m = proj.shape[0]
    nblk = seq // RET_ROWS
    row = lambda b, j: b * nblk + j
    return pl.pallas_call(
        _retention_kernel,
        out_shape=jax.ShapeDtypeStruct((m, MIX_HALF), BF16),
        grid=(batch, nblk),
        in_specs=[
            pl.BlockSpec((RET_ROWS, MIX_HALF), lambda b, j: (row(b, j), 0)),
            pl.BlockSpec((RET_ROWS, MIX_HALF), lambda b, j: (row(b, j), 1)),
            pl.BlockSpec((RET_ROWS, MIX_HALF), lambda b, j: (row(b, j), 2)),
            pl.BlockSpec((RET_ROWS, MIX_HALF), lambda b, j: (row(b, j), 3)),
            pl.BlockSpec((N_META, MIX_HALF), lambda b, j: (0, 1)),
            pl.BlockSpec((N_META, MIX_HALF), lambda b, j: (0, 2)),
            pl.BlockSpec((None, 1, MIX_HALF), lambda b, j: (layer, 0, 0)),
        ],
        out_specs=pl.BlockSpec((RET_ROWS, MIX_HALF), lambda b, j: (row(b, j), 0)),
        scratch_shapes=[
            pltpu.VMEM((HEADS, HEAD_DIM, HEAD_DIM), F32),
            pltpu.VMEM((HEADS, BLK, BLK), F32),
            pltpu.VMEM((HEADS, BLK, HEAD_DIM), F32),
            pltpu.VMEM((HEADS, BLK, HEAD_DIM), F32),
        ],
        compiler_params=pltpu.CompilerParams(
            dimension_semantics=("parallel", "arbitrary"),
            vmem_limit_bytes=VMEM_LIMIT),
        name="retention",
    )(proj, proj, proj, proj, pm, pm, gn)


def _diffattn_kernel(slopes_ref, q_ref, k_ref, v_ref, km_ref, vm_ref, gate_ref, gn_ref,
                     lq1_ref, lk1_ref, lq2_ref, lk2_ref, o_ref,
                     qs_ref, s_ref, sm_ref, mrun_ref, mrow_ref, acc_ref,
                     vext_ref, kmext_ref, vmext_ref, dbias_ref, *, lam_init):
    grp = pl.program_id(1)
    t = pl.program_id(2)
    hb = HEADS_PER_STEP
    seq = k_ref.shape[0]
    head_cols = lambda hh: slice(hh * HEAD_DIM, (hh + 1) * HEAD_DIM)
    slope2 = lambda hh: slopes_ref[grp * hb + hh] * LOG2E

    @pl.when(t == 0)
    def _():
        r = lax.broadcasted_iota(jnp.int32, (QT, QT), 0)
        c = lax.broadcasted_iota(jnp.int32, (QT, QT), 1)
        allowed = (c // CHUNK) <= (r // CHUNK)
        rel = (r - jnp.abs(r - c)).astype(F32)
        for hh in range(hb):
            cols = head_cols(hh)
            dbias_ref[hh] = jnp.where(allowed, slope2(hh) * rel, NEG)
            vext_ref[hh, :, :HEAD_DIM] = v_ref[:, cols]
            vext_ref[hh, :, HEAD_DIM:] = jnp.ones((seq, HEAD_DIM), BF16)
            kmext_ref[hh] = _pad_rows(km_ref[:, cols], BLK)
            vmext_ref[hh, :, :HEAD_DIM] = _pad_rows(vm_ref[:, cols], BLK)
            vmext_ref[hh, :, HEAD_DIM:] = jnp.ones((BLK, HEAD_DIM), BF16)

    colm = lax.broadcasted_iota(jnp.int32, (1, BLK), 1)
    relm = (colm - N_META - t * QT).astype(F32)
    for hh in range(hb):
        qs_ref[hh] = _split_components(q_ref[:, head_cols(hh)])
        s = lax.dot_general(qs_ref[hh], kmext_ref[hh], NT, preferred_element_type=F32)
        s = s * LOG2E + jnp.where(colm < N_META, slope2(hh) * relm, NEG)
        sm_ref[hh] = s
        mrun_ref[hh] = s

    def frame_scores(j, hh):
        start = pl.multiple_of(j * QT, QT)
        kt = k_ref[pl.ds(start, QT), head_cols(hh)]
        return lax.dot_general(qs_ref[hh], kt, NT, preferred_element_type=F32) * LOG2E

    def keep(j, hh, s):
        s_ref[hh, j] = s
        mrun_ref[hh] = jnp.maximum(mrun_ref[hh], jnp.maximum(s[:, :BLK], s[:, BLK:]))

    col = lax.broadcasted_iota(jnp.int32, (1, QT), 1)

    def off_diag(j):
        rel = (col + (j - t) * QT).astype(F32)
        for hh in range(hb):
            keep(j, hh, frame_scores(j, hh) + slope2(hh) * rel)

    def off_diag_pair(jp, carry):
        off_diag(2 * jp)
        off_diag(2 * jp + 1)
        return carry

    lax.fori_loop(0, t // 2, off_diag_pair, 0)

    @pl.when(t % 2 == 1)
    def _():
        off_diag(t - 1)

    for hh in range(hb):
        s = frame_scores(t, hh)
        keep(t, hh, jnp.concatenate([s[:QT] + dbias_ref[hh], s[QT:] + dbias_ref[hh]], axis=0))
        mrow_ref[hh] = jnp.broadcast_to(
            jnp.max(mrun_ref[hh], axis=-1, keepdims=True), (2 * QT, BLK))
        p = jnp.exp2(sm_ref[hh] - mrow_ref[hh])
        acc_ref[hh] = jnp.dot(p.astype(BF16), vmext_ref[hh], preferred_element_type=F32)

    def pv(j, hh):
        start = pl.multiple_of(j * QT, QT)
        s = s_ref[hh, j]
        m = mrow_ref[hh]
        p = jnp.concatenate([jnp.exp2(s[:, :BLK] - m), jnp.exp2(s[:, BLK:] - m)], axis=1)
        return jnp.dot(p.astype(BF16), vext_ref[hh, pl.ds(start, QT), :],
                       preferred_element_type=F32)

    def pv_pair(jp, carry):
        for hh in range(hb):
            acc_ref[hh] = acc_ref[hh] + (pv(2 * jp, hh) + pv(2 * jp + 1, hh))
        return carry

    lax.fori_loop(0, (t + 1) // 2, pv_pair, 0)

    @pl.when(t % 2 == 0)
    def _():
        for hh in range(hb):
            acc_ref[hh] = acc_ref[hh] + pv(t, hh)

    lam = _lambda(lq1_ref, lk1_ref, lq2_ref, lk2_ref, lam_init)
    for hh in range(hb):
        cols = head_cols(hh)
        acc = acc_ref[hh]
        a = acc[:, :HEAD_DIM] / acc[:, HEAD_DIM:]
        d = a[:QT] - lam * a[QT:]
        y = d * lax.rsqrt(jnp.mean(d * d, axis=-1, keepdims=True) + EPS)
        y = y * gn_ref[:, cols] * (1.0 - lam_init)
        o_ref[:, cols] = (y * _silu(gate_ref[:, cols].astype(F32))).astype(BF16)


def _diffattn(proj, pm, gn, lams, layer, batch, seq, lam_init):
    m = proj.shape[0]
    ntile = seq // QT
    hb = HEADS_PER_STEP
    ngrp = HEADS // hb
    width = hb * HEAD_DIM
    slopes = jnp.asarray(2.0 ** (-8.0 * np.arange(1, HEADS + 1) / HEADS), F32)
    qcol, kcol, vcol, gcol = (4 * ngrp, 5 * ngrp, 6 * ngrp, 7 * ngrp)
    vec = pl.BlockSpec((None, 1, DIFF_DH), lambda b, g, t: (layer, 0, 0))
    return pl.pallas_call(
        functools.partial(_diffattn_kernel, lam_init=lam_init),
        out_shape=jax.ShapeDtypeStruct((m, MIX_HALF), BF16),
        grid=(batch, ngrp, ntile),
        in_specs=[
            pl.BlockSpec(memory_space=pltpu.SMEM),
            pl.BlockSpec((QT, width), lambda b, g, t: (b * ntile + t, qcol + g)),
            pl.BlockSpec((seq, width), lambda b, g, t: (b, kcol + g)),
            pl.BlockSpec((seq, width), lambda b, g, t: (b, vcol + g)),
            pl.BlockSpec((N_META, width), lambda b, g, t: (0, kcol + g)),
            pl.BlockSpec((N_META, width), lambda b, g, t: (0, vcol + g)),
            pl.BlockSpec((QT, width), lambda b, g, t: (b * ntile + t, gcol + g)),
            pl.BlockSpec((None, 1, width), lambda b, g, t: (layer, 0, g)),
            vec, vec, vec, vec,
        ],
        out_specs=pl.BlockSpec((QT, width), lambda b, g, t: (b * ntile + t, g)),
        scratch_shapes=[
            pltpu.VMEM((hb, 2 * QT, HEAD_DIM), BF16),
            pltpu.VMEM((hb, ntile, 2 * QT, QT), F32),
            pltpu.VMEM((hb, 2 * QT, BLK), F32),
            pltpu.VMEM((hb, 2 * QT, BLK), F32),
            pltpu.VMEM((hb, 2 * QT, BLK), F32),
            pltpu.VMEM((hb, 2 * QT, 2 * HEAD_DIM), F32),
            pltpu.VMEM((hb, seq, 2 * HEAD_DIM), BF16),
            pltpu.VMEM((hb, BLK, HEAD_DIM), BF16),
            pltpu.VMEM((hb, BLK, 2 * HEAD_DIM), BF16),
            pltpu.VMEM((hb, QT, QT), F32),
        ],
        compiler_params=pltpu.CompilerParams(
            dimension_semantics=("parallel", "parallel", "arbitrary"),
            vmem_limit_bytes=VMEM_LIMIT),
        name="diffattn",
    )(slopes, proj, proj, proj, pm, pm, proj, gn, *lams)


def _outproj_kernel(r_ref, d_ref, wr_ref, wd_ref, h_ref, gf_ref, o_ref, wb_ref, *, final_norm):
    @pl.when(pl.program_id(0) == 0)
    def _():
        wb_ref[0] = wr_ref[...].astype(BF16)
        wb_ref[1] = wd_ref[...].astype(BF16)

    y = jnp.dot(r_ref[...], wb_ref[0], preferred_element_type=F32)
    y = y + jnp.dot(d_ref[...], wb_ref[1], preferred_element_type=F32)
    hn = h_ref[...] + y
    if final_norm:
        ms = jnp.mean(hn * hn, axis=-1, keepdims=True)
        hn = hn * lax.rsqrt(ms + EPS) * gf_ref[...]
    o_ref[...] = hn


def _outproj(r, d, w, h, gf, layer, tm, final_norm):
    m, dm = h.shape
    return pl.pallas_call(
        functools.partial(_outproj_kernel, final_norm=final_norm),
        out_shape=jax.ShapeDtypeStruct((m, dm), F32),
        grid=(m // tm,),
        in_specs=[
            pl.BlockSpec((tm, MIX_HALF), lambda i: (i, 0)),
            pl.BlockSpec((tm, MIX_HALF), lambda i: (i, 0)),
            pl.BlockSpec((None, MIX_HALF, dm), lambda i: (layer, 0, 0),
                         pipeline_mode=pl.Buffered(1)),
            pl.BlockSpec((None, MIX_HALF, dm), lambda i: (layer, 1, 0),
                         pipeline_mode=pl.Buffered(1)),
            pl.BlockSpec((tm, dm), lambda i: (i, 0)),
            pl.BlockSpec((1, dm), lambda i: (0, 0)),
        ],
        out_specs=pl.BlockSpec((tm, dm), lambda i: (i, 0)),
        scratch_shapes=[pltpu.VMEM((2, MIX_HALF, dm), BF16)],
        compiler_params=pltpu.CompilerParams(
            dimension_semantics=("arbitrary",),
            vmem_limit_bytes=VMEM_LIMIT),
        name="outproj",
    )(r, d, w, w, h, gf)


def kernel(x, meta_tokens, norm_g, w_in, w_out, ret_norm_g, diff_norm_g,
           lambda_q1, lambda_k1, lambda_q2, lambda_k2, final_norm_g):
    batch, seq, dm = x.shape
    depth = w_in.shape[0]
    m = batch * seq
    assert seq % QT == 0 and meta_tokens.shape[0] == N_META

    h = x.reshape(m, dm)
    hm = meta_tokens.astype(x.dtype)
    gf = final_norm_g.reshape(1, dm)
    per_layer = lambda a: a.reshape(depth, 1, a.shape[-1])
    norm_g, ret_norm_g, diff_norm_g = map(per_layer, (norm_g, ret_norm_g, diff_norm_g))
    lams = tuple(map(per_layer, (lambda_q1, lambda_k1, lambda_q2, lambda_k2)))

    for i in range(depth):
        last = i == depth - 1
        lam_init = 0.8 - 0.6 * float(np.exp(-0.3 * i))
        pm, hm = _meta_layer(hm, norm_g, w_in, w_out, ret_norm_g, diff_norm_g, lams,
                             i, lam_init, with_out=not last)
        proj = _inproj(h, norm_g, w_in, i, tm=m // 8, tn=1024)
        r = _retention(proj, pm, ret_norm_g, i, batch, seq)
        d = _diffattn(proj, pm, diff_norm_g, lams, i, batch, seq, lam_init)
        h = _outproj(r, d, w_out, h, gf, i, tm=m // 16, final_norm=last)

    return h.reshape(batch, seq, dm)
```

```python
import functools

import numpy as np
import jax
import jax.numpy as jnp
from jax import lax
from jax.experimental import pallas as pl
from jax.experimental.pallas import tpu as pltpu

F32 = jnp.float32
BF16 = jnp.bfloat16

CHUNK = 64
N_META = 16
EPS = 1e-6
HEADS = 8
HEAD_DIM = 128
DIFF_DH = 64
MIX_HALF = HEADS * HEAD_DIM
BLK = 128
QT = 256
NEG = -1e30
LOG2E = 1.4426950408889634
HEADS_PER_STEP = 4

VMEM_LIMIT = 56 * 1024 * 1024

NT = (((1,), (1,)), ((), ()))
TN = (((0,), (0,)), ((), ()))


def _log_g(h):
    return float(np.log(1.0 - 2.0 ** (-5.0 - h)))


def _silu(x):
    return x * (1.0 / (1.0 + jnp.exp(-x)))


def _lambda(lq1_ref, lk1_ref, lq2_ref, lk2_ref, lam_init):
    return (jnp.exp(jnp.sum(lq1_ref[...] * lk1_ref[...], axis=-1, keepdims=True))
            - jnp.exp(jnp.sum(lq2_ref[...] * lk2_ref[...], axis=-1, keepdims=True))
            + lam_init)


def _pad_rows(a, rows):
    return jnp.concatenate([a, jnp.zeros((rows - a.shape[0], a.shape[1]), a.dtype)], axis=0)


def _split_components(q):
    lane = lax.broadcasted_iota(jnp.int32, q.shape, 1)
    zero = jnp.zeros((), BF16)
    qscaled = q * jnp.asarray(DIFF_DH ** -0.5, BF16)
    return jnp.concatenate([jnp.where(lane < DIFF_DH, qscaled, zero),
                            jnp.where(lane >= DIFF_DH, qscaled, zero)], axis=0)


def _meta_kernel(*refs, lam_init, with_out):
    if with_out:
        (hm_ref, g_ref, w_ref, wr_ref, wd_ref, gret_ref, gdiff_ref,
         lq1_ref, lk1_ref, lq2_ref, lk2_ref, pm_ref, hm_out_ref, u_ref, sec_ref) = refs
    else:
        hm_ref, g_ref, w_ref, pm_ref, u_ref = refs
    j = pl.program_id(0)

    @pl.when(j == 0)
    def _():
        x = hm_ref[...]
        ms = jnp.mean(x * x, axis=-1, keepdims=True)
        u_ref[...] = (x * lax.rsqrt(ms + EPS) * g_ref[...]).astype(BF16)

    p = jnp.dot(u_ref[...], w_ref[...].astype(BF16), preferred_element_type=F32).astype(BF16)
    pm_ref[...] = p
    if not with_out:
        return
    sec_ref[j] = p

    @pl.when(j == pl.num_programs(0) - 1)
    def _():
        rq, rk, rv, rgate, dq, dk, dv, dgate = [sec_ref[s] for s in range(8)]
        t = lax.broadcasted_iota(jnp.int32, (BLK, BLK), 0)
        s_i = lax.broadcasted_iota(jnp.int32, (BLK, BLK), 1)
        dist = jnp.abs(t - s_i).astype(F32)
        r2 = lax.broadcasted_iota(jnp.int32, (2 * BLK, BLK), 0) % BLK
        c2 = lax.broadcasted_iota(jnp.int32, (2 * BLK, BLK), 1)
        dist2 = jnp.abs(r2 - c2).astype(F32)
        lam = _lambda(lq1_ref, lk1_ref, lq2_ref, lk2_ref, lam_init)
        r_parts, d_parts = [], []
        for h in range(HEADS):
            cols = slice(h * HEAD_DIM, (h + 1) * HEAD_DIM)
            q = _pad_rows(rq[:, cols], BLK)
            k = _pad_rows(rk[:, cols], BLK)
            v = _pad_rows(rv[:, cols], BLK)
            dmat = jnp.exp(_log_g(h) * dist) * (HEAD_DIM ** -0.5)
            s = lax.dot_general(q, k, NT, preferred_element_type=F32) * dmat
            o = jnp.dot(s.astype(BF16), v, preferred_element_type=F32)[:N_META]
            mu = jnp.mean(o, axis=-1, keepdims=True)
            oc = o - mu
            var = jnp.mean(oc * oc, axis=-1, keepdims=True)
            y = oc * lax.rsqrt(var + EPS) * gret_ref[:, cols]
            r_parts.append((y * _silu(rgate[:, cols].astype(F32))).astype(BF16))
            slope = 2.0 ** (-(h + 1.0))
            qs = _split_components(_pad_rows(dq[:, cols], BLK))
            k = _pad_rows(dk[:, cols], BLK)
            v = _pad_rows(dv[:, cols], BLK)
            bias = jnp.where(c2 < N_META, -slope * dist2, NEG)
            s = lax.dot_general(qs, k, NT, preferred_element_type=F32) + bias
            pr = jnp.exp(s - jnp.max(s, axis=-1, keepdims=True))
            a = (jnp.dot(pr.astype(BF16), v, preferred_element_type=F32)
                 / jnp.sum(pr, axis=-1, keepdims=True))
            d = a[:N_META] - lam * a[BLK:BLK + N_META]
            y = d * lax.rsqrt(jnp.mean(d * d, axis=-1, keepdims=True) + EPS)
            y = y * gdiff_ref[:, cols] * (1.0 - lam_init)
            d_parts.append((y * _silu(dgate[:, cols].astype(F32))).astype(BF16))
        r = jnp.concatenate(r_parts, axis=1)
        d = jnp.concatenate(d_parts, axis=1)
        y = jnp.dot(r, wr_ref[...].astype(BF16), preferred_element_type=F32)
        y = y + jnp.dot(d, wd_ref[...].astype(BF16), preferred_element_type=F32)
        hm_out_ref[...] = hm_ref[...] + y


def _meta_layer(hm, norm_g, w_in, w_out, gret, gdiff, lams, layer, lam_init, with_out):
    dm = hm.shape[1]
    n = w_in.shape[-1]
    tn = MIX_HALF
    vec = lambda width: pl.BlockSpec((None, 1, width), lambda j: (layer, 0, 0))
    in_specs = [
        pl.BlockSpec((N_META, dm), lambda j: (0, 0)),
        vec(dm),
        pl.BlockSpec((None, dm, tn), lambda j: (layer, 0, j)),
    ]
    args = [hm, norm_g, w_in]
    out_shape = [jax.ShapeDtypeStruct((N_META, n), BF16)]
    out_specs = [pl.BlockSpec((N_META, tn), lambda j: (0, j))]
    scratch = [pltpu.VMEM((N_META, dm), BF16)]
    if with_out:
        in_specs += [
            pl.BlockSpec((None, MIX_HALF, dm), lambda j: (layer, 0, 0),
                         pipeline_mode=pl.Buffered(1)),
            pl.BlockSpec((None, MIX_HALF, dm), lambda j: (layer, 1, 0),
                         pipeline_mode=pl.Buffered(1)),
            vec(MIX_HALF), vec(MIX_HALF),
            vec(DIFF_DH), vec(DIFF_DH), vec(DIFF_DH), vec(DIFF_DH),
        ]
        args += [w_out, w_out, gret, gdiff, *lams]
        out_shape.append(jax.ShapeDtypeStruct((N_META, dm), F32))
        out_specs.append(pl.BlockSpec((N_META, dm), lambda j: (0, 0)))
        scratch.append(pltpu.VMEM((n // tn, N_META, tn), BF16))
    outs = pl.pallas_call(
        functools.partial(_meta_kernel, lam_init=lam_init, with_out=with_out),
        out_shape=out_shape,
        grid=(n // tn,),
        in_specs=in_specs,
        out_specs=out_specs,
        scratch_shapes=scratch,
        compiler_params=pltpu.CompilerParams(
            dimension_semantics=("arbitrary",),
            vmem_limit_bytes=VMEM_LIMIT),
        name="meta_layer",
    )(*args)
    return outs if with_out else (outs[0], None)


def _inproj_kernel(h_ref, g_ref, w_ref, o_ref, u_ref):
    @pl.when(pl.program_id(1) == 0)
    def _():
        x = h_ref[...]
        ms = jnp.mean(x * x, axis=-1, keepdims=True)
        u_ref[...] = (x * lax.rsqrt(ms + EPS) * g_ref[...]).astype(BF16)

    o_ref[...] = jnp.dot(u_ref[...], w_ref[...].astype(BF16),
                         preferred_element_type=F32).astype(BF16)


def _inproj(h, g, w, layer, tm, tn):
    m, d = h.shape
    n = w.shape[-1]
    return pl.pallas_call(
        _inproj_kernel,
        out_shape=jax.ShapeDtypeStruct((m, n), BF16),
        grid=(m // tm, n // tn),
        in_specs=[
            pl.BlockSpec((tm, d), lambda i, j: (i, 0)),
            pl.BlockSpec((None, 1, d), lambda i, j: (layer, 0, 0)),
            pl.BlockSpec((None, d, tn), lambda i, j: (layer, 0, j)),
        ],
        out_specs=pl.BlockSpec((tm, tn), lambda i, j: (i, j)),
        scratch_shapes=[pltpu.VMEM((tm, d), BF16)],
        compiler_params=pltpu.CompilerParams(
            dimension_semantics=("parallel", "arbitrary"),
            vmem_limit_bytes=VMEM_LIMIT),
        name="inproj",
    )(h, g, w)


def _retention_kernel(q_ref, k_ref, v_ref, gate_ref, km_ref, vm_ref, gn_ref, o_ref,
                      state_ref, dmat_ref, rq_ref, rk_ref):
    j = pl.program_id(1)
    scale = HEAD_DIM ** -0.5

    @pl.when(j == 0)
    def _():
        t = lax.broadcasted_iota(jnp.int32, (BLK, BLK), 0)
        s = lax.broadcasted_iota(jnp.int32, (BLK, BLK), 1)
        dist = jnp.abs(t - s).astype(F32)
        allowed = (s // CHUNK) <= (t // CHUNK)
        tf = t.astype(F32)
        mf = lax.broadcasted_iota(jnp.int32, (N_META, HEAD_DIM), 0).astype(F32)
        for h in range(HEADS):
            cols = slice(h * HEAD_DIM, (h + 1) * HEAD_DIM)
            log_g = _log_g(h)
            dmat_ref[h] = jnp.where(allowed, jnp.exp(log_g * dist) * scale, 0.0)
            rq_ref[h] = jnp.exp(log_g * (tf + 1.0))
            rk_ref[h] = jnp.exp(log_g * (BLK - 1.0 - tf)) * scale
            zeta = jnp.exp(log_g * (N_META - 1.0 - mf)) * scale
            kz = (km_ref[:, cols].astype(F32) * zeta).astype(BF16)
            state_ref[h] = lax.dot_general(_pad_rows(kz, BLK), _pad_rows(vm_ref[:, cols], BLK),
                                           TN, preferred_element_type=F32)

    def block(i, carry):
        rows = pl.ds(pl.multiple_of(i * BLK, BLK), BLK)
        for h in range(HEADS):
            cols = slice(h * HEAD_DIM, (h + 1) * HEAD_DIM)
            decay_blk = float((1.0 - 2.0 ** (-5.0 - h)) ** BLK)
            q = q_ref[rows, cols]
            k = k_ref[rows, cols]
            v = v_ref[rows, cols]
            s = lax.dot_general(q, k, NT, preferred_element_type=F32) * dmat_ref[h]
            intra = jnp.dot(s.astype(BF16), v, preferred_element_type=F32)
            state = state_ref[h]
            qx = (q.astype(F32) * rq_ref[h]).astype(BF16)
            cross = jnp.dot(qx, state.astype(BF16), preferred_element_type=F32)
            kz = (k.astype(F32) * rk_ref[h]).astype(BF16)
            kv = lax.dot_general(kz, v, TN, preferred_element_type=F32)
            state_ref[h] = decay_blk * state + kv

            o = intra + cross
            mu = jnp.mean(o, axis=-1, keepdims=True)
            oc = o - mu
            var = jnp.mean(oc * oc, axis=-1, keepdims=True)
            y = oc * lax.rsqrt(var + EPS) * gn_ref[:, cols]
            o_ref[rows, cols] = (y * _silu(gate_ref[rows, cols].astype(F32))).astype(BF16)
        return carry

    lax.fori_loop(0, q_ref.shape[0] // BLK, block, 0)


RET_ROWS = 512


def _retention(proj, pm, gn, layer, batch, seq):
    m = proj.shape[0]
    nblk = seq // RET_ROWS
    row = lambda b, j: b * nblk + j
    return pl.pallas_call(
        _retention_kernel,
        out_shape=jax.ShapeDtypeStruct((m, MIX_HALF), BF16),
        grid=(batch, nblk),
        in_specs=[
            pl.BlockSpec((RET_ROWS, MIX_HALF), lambda b, j: (row(b, j), 0)),
            pl.BlockSpec((RET_ROWS, MIX_HALF), lambda b, j: (row(b, j), 1)),
            pl.BlockSpec((RET_ROWS, MIX_HALF), lambda b, j: (row(b, j), 2)),
            pl.BlockSpec((RET_ROWS, MIX_HALF), lambda b, j: (row(b, j), 3)),
            pl.BlockSpec((N_META, MIX_HALF), lambda b, j: (0, 1)),
            pl.BlockSpec((N_META, MIX_HALF), lambda b, j: (0, 2)),
            pl.BlockSpec((None, 1, MIX_HALF), lambda b, j: (layer, 0, 0)),
        ],
        out_specs=pl.BlockSpec((RET_ROWS, MIX_HALF), lambda b, j: (row(b, j), 0)),
        scratch_shapes=[
            pltpu.VMEM((HEADS, HEAD_DIM, HEAD_DIM), F32),
            pltpu.VMEM((HEADS, BLK, BLK), F32),
            pltpu.VMEM((HEADS, BLK, HEAD_DIM), F32),
            pltpu.VMEM((HEADS, BLK, HEAD_DIM), F32),
        ],
        compiler_params=pltpu.CompilerParams(
            dimension_semantics=("parallel", "arbitrary"),
            vmem_limit_bytes=VMEM_LIMIT),
        name="retention",
    )(proj, proj, proj, proj, pm, pm, gn)


def _diffattn_kernel(slopes_ref, q_ref, k_ref, v_ref, km_ref, vm_ref, gate_ref, gn_ref,
                     lq1_ref, lk1_ref, lq2_ref, lk2_ref, o_ref,
                     qs_ref, s_ref, sm_ref, mrun_ref, mrow_ref, acc_ref,
                     vext_ref, kmext_ref, vmext_ref, dbias_ref, *, lam_init):
    grp = pl.program_id(1)
    t = pl.program_id(2)
    hb = HEADS_PER_STEP
    seq = k_ref.shape[0]
    head_cols = lambda hh: slice(hh * HEAD_DIM, (hh + 1) * HEAD_DIM)
    slope2 = lambda hh: slopes_ref[grp * hb + hh] * LOG2E

    @pl.when(t == 0)
    def _():
        r = lax.broadcasted_iota(jnp.int32, (QT, QT), 0)
        c = lax.broadcasted_iota(jnp.int32, (QT, QT), 1)
        allowed = (c // CHUNK) <= (r // CHUNK)
        rel = (r - jnp.abs(r - c)).astype(F32)
        for hh in range(hb):
            cols = head_cols(hh)
            dbias_ref[hh] = jnp.where(allowed, slope2(hh) * rel, NEG)
            vext_ref[hh, :, :HEAD_DIM] = v_ref[:, cols]
            vext_ref[hh, :, HEAD_DIM:] = jnp.ones((seq, HEAD_DIM), BF16)
            kmext_ref[hh] = _pad_rows(km_ref[:, cols], BLK)
            vmext_ref[hh, :, :HEAD_DIM] = _pad_rows(vm_ref[:, cols], BLK)
            vmext_ref[hh, :, HEAD_DIM:] = jnp.ones((BLK, HEAD_DIM), BF16)

    def frame_scores(j, hh):
        start = pl.multiple_of(j * QT, QT)
        kt = k_ref[pl.ds(start, QT), head_cols(hh)]
        return lax.dot_general(qs_ref[hh], kt, NT, preferred_element_type=F32) * LOG2E

    def keep(j, hh, s):
        s_ref[hh, j] = s
        mrun_ref[hh] = jnp.maximum(mrun_ref[hh], jnp.maximum(s[:, :BLK], s[:, BLK:]))

    colm = lax.broadcasted_iota(jnp.int32, (1, BLK), 1)
    relm = (colm - N_META - t * QT).astype(F32)
    for hh in range(hb):
        qs_ref[hh] = _split_components(q_ref[:, head_cols(hh)])
        s = lax.dot_general(qs_ref[hh], kmext_ref[hh], NT, preferred_element_type=F32)
        s = s * LOG2E + jnp.where(colm < N_META, slope2(hh) * relm, NEG)
        sm_ref[hh] = s
        mrun_ref[hh] = s
        s = frame_scores(t, hh)
        keep(t, hh, jnp.concatenate([s[:QT] + dbias_ref[hh], s[QT:] + dbias_ref[hh]], axis=0))

    col = lax.broadcasted_iota(jnp.int32, (1, QT), 1)

    def off_diag(j):
        rel = (col + (j - t) * QT).astype(F32)
        for hh in range(hb):
            keep(j, hh, frame_scores(j, hh) + slope2(hh) * rel)

    def off_diag_pair(jp, carry):
        off_diag(2 * jp)
        off_diag(2 * jp + 1)
        return carry

    lax.fori_loop(0, t // 2, off_diag_pair, 0)

    @pl.when(t % 2 == 1)
    def _():
        off_diag(t - 1)

    for hh in range(hb):
        mrow_ref[hh] = jnp.broadcast_to(
            jnp.max(mrun_ref[hh], axis=-1, keepdims=True), (2 * QT, BLK))
        p = jnp.exp2(sm_ref[hh] - mrow_ref[hh])
        acc_ref[hh] = jnp.dot(p.astype(BF16), vmext_ref[hh], preferred_element_type=F32)

    def pv(j, hh):
        start = pl.multiple_of(j * QT, QT)
        s = s_ref[hh, j]
        m = mrow_ref[hh]
        p = jnp.concatenate([jnp.exp2(s[:, :BLK] - m), jnp.exp2(s[:, BLK:] - m)], axis=1)
        return jnp.dot(p.astype(BF16), vext_ref[hh, pl.ds(start, QT), :],
                       preferred_element_type=F32)

    def pv_pair(jp, carry):
        for hh in range(hb):
            acc_ref[hh] = acc_ref[hh] + (pv(2 * jp, hh) + pv(2 * jp + 1, hh))
        return carry

    lax.fori_loop(0, (t + 1) // 2, pv_pair, 0)

    @pl.when(t % 2 == 0)
    def _():
        for hh in range(hb):
            acc_ref[hh] = acc_ref[hh] + pv(t, hh)

    lam = _lambda(lq1_ref, lk1_ref, lq2_ref, lk2_ref, lam_init)
    for hh in range(hb):
        cols = head_cols(hh)
        acc = acc_ref[hh]
        a = acc[:, :HEAD_DIM] / acc[:, HEAD_DIM:]
        d = a[:QT] - lam * a[QT:]
        y = d * lax.rsqrt(jnp.mean(d * d, axis=-1, keepdims=True) + EPS)
        y = y * gn_ref[:, cols] * (1.0 - lam_init)
        o_ref[:, cols] = (y * _silu(gate_ref[:, cols].astype(F32))).astype(BF16)


def _diffattn(proj, pm, gn, lams, layer, batch, seq, lam_init):
    m = proj.shape[0]
    ntile = seq // QT
    hb = HEADS_PER_STEP
    ngrp = HEADS // hb
    width = hb * HEAD_DIM
    slopes = jnp.asarray(2.0 ** (-8.0 * np.arange(1, HEADS + 1) / HEADS), F32)
    qcol, kcol, vcol, gcol = (4 * ngrp, 5 * ngrp, 6 * ngrp, 7 * ngrp)
    vec = pl.BlockSpec((None, 1, DIFF_DH), lambda b, g, t: (layer, 0, 0))
    return pl.pallas_call(
        functools.partial(_diffattn_kernel, lam_init=lam_init),
        out_shape=jax.ShapeDtypeStruct((m, MIX_HALF), BF16),
        grid=(batch, ngrp, ntile),
        in_specs=[
            pl.BlockSpec(memory_space=pltpu.SMEM),
            pl.BlockSpec((QT, width), lambda b, g, t: (b * ntile + t, qcol + g)),
            pl.BlockSpec((seq, width), lambda b, g, t: (b, kcol + g)),
            pl.BlockSpec((seq, width), lambda b, g, t: (b, vcol + g)),
            pl.BlockSpec((N_META, width), lambda b, g, t: (0, kcol + g)),
            pl.BlockSpec((N_META, width), lambda b, g, t: (0, vcol + g)),
            pl.BlockSpec((QT, width), lambda b, g, t: (b * ntile + t, gcol + g)),
            pl.BlockSpec((None, 1, width), lambda b, g, t: (layer, 0, g)),
            vec, vec, vec, vec,
        ],
        out_specs=pl.BlockSpec((QT, width), lambda b, g, t: (b * ntile + t, g)),
        scratch_shapes=[
            pltpu.VMEM((hb, 2 * QT, HEAD_DIM), BF16),
            pltpu.VMEM((hb, ntile, 2 * QT, QT), F32),
            pltpu.VMEM((hb, 2 * QT, BLK), F32),
            pltpu.VMEM((hb, 2 * QT, BLK), F32),
            pltpu.VMEM((hb, 2 * QT, BLK), F32),
            pltpu.VMEM((hb, 2 * QT, 2 * HEAD_DIM), F32),
            pltpu.VMEM((hb, seq, 2 * HEAD_DIM), BF16),
            pltpu.VMEM((hb, BLK, HEAD_DIM), BF16),
            pltpu.VMEM((hb, BLK, 2 * HEAD_DIM), BF16),
            pltpu.VMEM((hb, QT, QT), F32),
        ],
        compiler_params=pltpu.CompilerParams(
            dimension_semantics=("parallel", "parallel", "arbitrary"),
            vmem_limit_bytes=VMEM_LIMIT),
        name="diffattn",
    )(slopes, proj, proj, proj, pm, pm, proj, gn, *lams)


def _outproj_kernel(r_ref, d_ref, wr_ref, wd_ref, h_ref, gf_ref, o_ref, wb_ref, *, final_norm):
    @pl.when(pl.program_id(0) == 0)
    def _():
        wb_ref[0] = wr_ref[...].astype(BF16)
        wb_ref[1] = wd_ref[...].astype(BF16)

    y = jnp.dot(r_ref[...], wb_ref[0], preferred_element_type=F32)
    y = y + jnp.dot(d_ref[...], wb_ref[1], preferred_element_type=F32)
    hn = h_ref[...] + y
    if final_norm:
        ms = jnp.mean(hn * hn, axis=-1, keepdims=True)
        hn = hn * lax.rsqrt(ms + EPS) * gf_ref[...]
    o_ref[...] = hn


def _outproj(r, d, w, h, gf, layer, tm, final_norm):
    m, dm = h.shape
    return pl.pallas_call(
        functools.partial(_outproj_kernel, final_norm=final_norm),
        out_shape=jax.ShapeDtypeStruct((m, dm), F32),
        grid=(m // tm,),
        in_specs=[
            pl.BlockSpec((tm, MIX_HALF), lambda i: (i, 0)),
            pl.BlockSpec((tm, MIX_HALF), lambda i: (i, 0)),
            pl.BlockSpec((None, MIX_HALF, dm), lambda i: (layer, 0, 0),
                         pipeline_mode=pl.Buffered(1)),
            pl.BlockSpec((None, MIX_HALF, dm), lambda i: (layer, 1, 0),
                         pipeline_mode=pl.Buffered(1)),
            pl.BlockSpec((tm, dm), lambda i: (i, 0)),
            pl.BlockSpec((1, dm), lambda i: (0, 0)),
        ],
        out_specs=pl.BlockSpec((tm, dm), lambda i: (i, 0)),
        scratch_shapes=[pltpu.VMEM((2, MIX_HALF, dm), BF16)],
        compiler_params=pltpu.CompilerParams(
            dimension_semantics=("arbitrary",),
            vmem_limit_bytes=VMEM_LIMIT),
        name="outproj",
    )(r, d, w, w, h, gf)


def kernel(x, meta_tokens, norm_g, w_in, w_out, ret_norm_g, diff_norm_g,
           lambda_q1, lambda_k1, lambda_q2, lambda_k2, final_norm_g):
    batch, seq, dm = x.shape
    depth = w_in.shape[0]
    m = batch * seq
    assert seq % QT == 0 and meta_tokens.shape[0] == N_META

    h = x.reshape(m, dm)
    hm = meta_tokens.astype(x.dtype)
    gf = final_norm_g.reshape(1, dm)
    per_layer = lambda a: a.reshape(depth, 1, a.shape[-1])
    norm_g, ret_norm_g, diff_norm_g = map(per_layer, (norm_g, ret_norm_g, diff_norm_g))
    lams = tuple(map(per_layer, (lambda_q1, lambda_k1, lambda_q2, lambda_k2)))

    for i in range(depth):
        last = i == depth - 1
        lam_init = 0.8 - 0.6 * float(np.exp(-0.3 * i))
        pm, hm = _meta_layer(hm, norm_g, w_in, w_out, ret_norm_g, diff_norm_g, lams,
                             i, lam_init, with_out=not last)
        proj = _inproj(h, norm_g, w_in, i, tm=m // 8, tn=1024)
        r = _retention(proj, pm, ret_norm_g, i, batch, seq)
        d = _diffattn(proj, pm, diff_norm_g, lams, i, batch, seq, lam_init)
        h = _outproj(r, d, w_out, h, gf, i, tm=m // 16, final_norm=last)

    return h.reshape(batch, seq, dm)
```

```python
import functools

import numpy as np
import jax
import jax.numpy as jnp
from jax import lax
from jax.experimental import pallas as pl
from jax.experimental.pallas import tpu as pltpu

F32 = jnp.float32
BF16 = jnp.bfloat16

CHUNK = 64
N_META = 16
EPS = 1e-6
HEADS = 8
HEAD_DIM = 128
DIFF_DH = 64
MIX_HALF = HEADS * HEAD_DIM
BLK = 128
QT = 256
NEG = -1e30
LOG2E = 1.4426950408889634
HEADS_PER_STEP = 4

VMEM_LIMIT = 56 * 1024 * 1024

NT = (((1,), (1,)), ((), ()))
TN = (((0,), (0,)), ((), ()))


def _log_g(h):
    return float(np.log(1.0 - 2.0 ** (-5.0 - h)))


def _silu(x):
    return x * (1.0 / (1.0 + jnp.exp(-x)))


def _lambda(lq1_ref, lk1_ref, lq2_ref, lk2_ref, lam_init):
    return (jnp.exp(jnp.sum(lq1_ref[...] * lk1_ref[...], axis=-1, keepdims=True))
            - jnp.exp(jnp.sum(lq2_ref[...] * lk2_ref[...], axis=-1, keepdims=True))
            + lam_init)


def _pad_rows(a, rows):
    return jnp.concatenate([a, jnp.zeros((rows - a.shape[0], a.shape[1]), a.dtype)], axis=0)


def _split_components(q):
    lane = lax.broadcasted_iota(jnp.int32, q.shape, 1)
    zero = jnp.zeros((), BF16)
    qscaled = q * jnp.asarray(DIFF_DH ** -0.5, BF16)
    return jnp.concatenate([jnp.where(lane < DIFF_DH, qscaled, zero),
                            jnp.where(lane >= DIFF_DH, qscaled, zero)], axis=0)


def _rms_norm_bf16(x, g):
    ms = jnp.mean(x * x, axis=-1, keepdims=True)
    return (x * lax.rsqrt(ms + EPS) * g).astype(BF16)


def _inproj_kernel(h_ref, hm_ref, g_ref, w_ref, o_ref, pm_ref, u_ref):
    tm = h_ref.shape[0]

    @pl.when(pl.program_id(1) == 0)
    def _():
        u_ref[:tm] = _rms_norm_bf16(h_ref[...], g_ref[...])
        u_ref[tm:] = _rms_norm_bf16(hm_ref[...], g_ref[...])

    out = jnp.dot(u_ref[...], w_ref[...].astype(BF16),
                  preferred_element_type=F32).astype(BF16)
    o_ref[...] = out[:tm]
    pm_ref[...] = out[tm:]


def _inproj(h, hm, g, w, layer, tm, tn):
    m, d = h.shape
    n = w.shape[-1]
    return pl.pallas_call(
        _inproj_kernel,
        out_shape=[jax.ShapeDtypeStruct((m, n), BF16),
                   jax.ShapeDtypeStruct((m // tm, N_META, n), BF16)],
        grid=(m // tm, n // tn),
        in_specs=[
            pl.BlockSpec((tm, d), lambda i, j: (i, 0)),
            pl.BlockSpec((N_META, d), lambda i, j: (0, 0)),
            pl.BlockSpec((None, 1, d), lambda i, j: (layer, 0, 0)),
            pl.BlockSpec((None, d, tn), lambda i, j: (layer, 0, j)),
        ],
        out_specs=[pl.BlockSpec((tm, tn), lambda i, j: (i, j)),
                   pl.BlockSpec((None, N_META, tn), lambda i, j: (i, 0, j))],
        scratch_shapes=[pltpu.VMEM((tm + N_META, d), BF16)],
        compiler_params=pltpu.CompilerParams(
            dimension_semantics=("parallel", "arbitrary"),
            vmem_limit_bytes=VMEM_LIMIT),
        name="inproj",
    )(h, hm, g, w)


def _meta_kernel(pm_ref, hm_ref, wr_ref, wd_ref, gret_ref, gdiff_ref,
                 lq1_ref, lk1_ref, lq2_ref, lk2_ref, hm_out_ref, *, lam_init):
    section = lambda s: pm_ref[:, s * MIX_HALF:(s + 1) * MIX_HALF]
    rq, rk, rv, rgate, dq, dk, dv, dgate = [section(s) for s in range(8)]
    t = lax.broadcasted_iota(jnp.int32, (BLK, BLK), 0)
    s_i = lax.broadcasted_iota(jnp.int32, (BLK, BLK), 1)
    dist = jnp.abs(t - s_i).astype(F32)
    r2 = lax.broadcasted_iota(jnp.int32, (2 * BLK, BLK), 0) % BLK
    c2 = lax.broadcasted_iota(jnp.int32, (2 * BLK, BLK), 1)
    dist2 = jnp.abs(r2 - c2).astype(F32)
    lam = _lambda(lq1_ref, lk1_ref, lq2_ref, lk2_ref, lam_init)
    r_parts, d_parts = [], []
    for h in range(HEADS):
        cols = slice(h * HEAD_DIM, (h + 1) * HEAD_DIM)
        q = _pad_rows(rq[:, cols], BLK)
        k = _pad_rows(rk[:, cols], BLK)
        v = _pad_rows(rv[:, cols], BLK)
        dmat = jnp.exp(_log_g(h) * dist) * (HEAD_DIM ** -0.5)
        s = lax.dot_general(q, k, NT, preferred_element_type=F32) * dmat
        o = jnp.dot(s.astype(BF16), v, preferred_element_type=F32)[:N_META]
        mu = jnp.mean(o, axis=-1, keepdims=True)
        oc = o - mu
        var = jnp.mean(oc * oc, axis=-1, keepdims=True)
        y = oc * lax.rsqrt(var + EPS) * gret_ref[:, cols]
        r_parts.append((y * _silu(rgate[:, cols].astype(F32))).astype(BF16))
        slope = 2.0 ** (-(h + 1.0))
        qs = _split_components(_pad_rows(dq[:, cols], BLK))
        k = _pad_rows(dk[:, cols], BLK)
        v = _pad_rows(dv[:, cols], BLK)
        bias = jnp.where(c2 < N_META, -slope * dist2, NEG)
        s = lax.dot_general(qs, k, NT, preferred_element_type=F32) + bias
        pr = jnp.exp(s - jnp.max(s, axis=-1, keepdims=True))
        a = (jnp.dot(pr.astype(BF16), v, preferred_element_type=F32)
             / jnp.sum(pr, axis=-1, keepdims=True))
        d = a[:N_META] - lam * a[BLK:BLK + N_META]
        y = d * lax.rsqrt(jnp.mean(d * d, axis=-1, keepdims=True) + EPS)
        y = y * gdiff_ref[:, cols] * (1.0 - lam_init)
        d_parts.append((y * _silu(dgate[:, cols].astype(F32))).astype(BF16))
    r = jnp.concatenate(r_parts, axis=1)
    d = jnp.concatenate(d_parts, axis=1)
    y = jnp.dot(r, wr_ref[...].astype(BF16), preferred_element_type=F32)
    y = y + jnp.dot(d, wd_ref[...].astype(BF16), preferred_element_type=F32)
    hm_out_ref[...] = hm_ref[...] + y


def _meta_layer(pm, hm, w_out, gret, gdiff, lams, layer, lam_init):
    dm = hm.shape[1]
    n = pm.shape[-1]
    vec = lambda width: pl.BlockSpec((None, 1, width), lambda i: (layer, 0, 0))
    return pl.pallas_call(
        functools.partial(_meta_kernel, lam_init=lam_init),
        out_shape=jax.ShapeDtypeStruct((N_META, dm), F32),
        grid=(1,),
        in_specs=[
            pl.BlockSpec((None, N_META, n), lambda i: (0, 0, 0)),
            pl.BlockSpec((N_META, dm), lambda i: (0, 0)),
            pl.BlockSpec((None, MIX_HALF, dm), lambda i: (layer, 0, 0),
                         pipeline_mode=pl.Buffered(1)),
            pl.BlockSpec((None, MIX_HALF, dm), lambda i: (layer, 1, 0),
                         pipeline_mode=pl.Buffered(1)),
            vec(MIX_HALF), vec(MIX_HALF),
            vec(DIFF_DH), vec(DIFF_DH), vec(DIFF_DH), vec(DIFF_DH),
        ],
        out_specs=pl.BlockSpec((N_META, dm), lambda i: (0, 0)),
        compiler_params=pltpu.CompilerParams(
            dimension_semantics=("arbitrary",),
            vmem_limit_bytes=VMEM_LIMIT),
        name="meta_layer",
    )(pm, hm, w_out, w_out, gret, gdiff, *lams)


def _retention_kernel(q_ref, k_ref, v_ref, gate_ref, km_ref, vm_ref, gn_ref, o_ref,
                      state_ref, dmat_ref, rq_ref, rk_ref):
    j = pl.program_id(1)
    scale = HEAD_DIM ** -0.5

    @pl.when(j == 0)
    def _():
        t = lax.broadcasted_iota(jnp.int32, (BLK, BLK), 0)
        s = lax.broadcasted_iota(jnp.int32, (BLK, BLK), 1)
        dist = jnp.abs(t - s).astype(F32)
        allowed = (s // CHUNK) <= (t // CHUNK)
        tf = t.astype(F32)
        mf = lax.broadcasted_iota(jnp.int32, (N_META, HEAD_DIM), 0).astype(F32)
        for h in range(HEADS):
            cols = slice(h * HEAD_DIM, (h + 1) * HEAD_DIM)
            log_g = _log_g(h)
            dmat_ref[h] = jnp.where(allowed, jnp.exp(log_g * dist) * scale, 0.0)
            rq_ref[h] = jnp.exp(log_g * (tf + 1.0))
            rk_ref[h] = jnp.exp(log_g * (BLK - 1.0 - tf)) * scale
            zeta = jnp.exp(log_g * (N_META - 1.0 - mf)) * scale
            kz = (km_ref[:, cols].astype(F32) * zeta).astype(BF16)
            state_ref[h] = lax.dot_general(_pad_rows(kz, BLK), _pad_rows(vm_ref[:, cols], BLK),
                                           TN, preferred_element_type=F32)

    def block(i, carry):
        rows = pl.ds(pl.multiple_of(i * BLK, BLK), BLK)
        for h in range(HEADS):
            cols = slice(h * HEAD_DIM, (h + 1) * HEAD_DIM)
            decay_blk = float((1.0 - 2.0 ** (-5.0 - h)) ** BLK)
            q = q_ref[rows, cols]
            k = k_ref[rows, cols]
            v = v_ref[rows, cols]
            s = lax.dot_general(q, k, NT, preferred_element_type=F32) * dmat_ref[h]
            intra = jnp.dot(s.astype(BF16), v, preferred_element_type=F32)
            state = state_ref[h]
            qx = (q.astype(F32) * rq_ref[h]).astype(BF16)
            cross = jnp.dot(qx, state.astype(BF16), preferred_element_type=F32)
            kz = (k.astype(F32) * rk_ref[h]).astype(BF16)
            kv = lax.dot_general(kz, v, TN, preferred_element_type=F32)
            state_ref[h] = decay_blk * state + kv

            o = intra + cross
            mu = jnp.mean(o, axis=-1, keepdims=True)
            oc = o - mu
            var = jnp.mean(oc * oc, axis=-1, keepdims=True)
            y = oc * lax.rsqrt(var + EPS) * gn_ref[:, cols]
            o_ref[rows, cols] = (y * _silu(gate_ref[rows, cols].astype(F32))).astype(BF16)
        return carry

    lax.fori_loop(0, q_ref.shape[0] // BLK, block, 0)


RET_ROWS = 512


def _retention(proj, pm, gn, layer, batch, seq):
    m = proj.shape[0]
    nblk = seq // RET_ROWS
    row = lambda b, j: b * nblk + j
    return pl.pallas_call(
        _retention_kernel,
        out_shape=jax.ShapeDtypeStruct((m, MIX_HALF), BF16),
        grid=(batch, nblk),
        in_specs=[
            pl.BlockSpec((RET_ROWS, MIX_HALF), lambda b, j: (row(b, j), 0)),
            pl.BlockSpec((RET_ROWS, MIX_HALF), lambda b, j: (row(b, j), 1)),
            pl.BlockSpec((RET_ROWS, MIX_HALF), lambda b, j: (row(b, j), 2)),
            pl.BlockSpec((RET_ROWS, MIX_HALF), lambda b, j: (row(b, j), 3)),
            pl.BlockSpec((None, N_META, MIX_HALF), lambda b, j: (0, 0, 1)),
            pl.BlockSpec((None, N_META, MIX_HALF), lambda b, j: (0, 0, 2)),
            pl.BlockSpec((None, 1, MIX_HALF), lambda b, j: (layer, 0, 0)),
        ],
        out_specs=pl.BlockSpec((RET_ROWS, MIX_HALF), lambda b, j: (row(b, j), 0)),
        scratch_shapes=[
            pltpu.VMEM((HEADS, HEAD_DIM, HEAD_DIM), F32),
            pltpu.VMEM((HEADS, BLK, BLK), F32),
            pltpu.VMEM((HEADS, BLK, HEAD_DIM), F32),
            pltpu.VMEM((HEADS, BLK, HEAD_DIM), F32),
        ],
        compiler_params=pltpu.CompilerParams(
            dimension_semantics=("parallel", "arbitrary"),
            vmem_limit_bytes=VMEM_LIMIT),
        name="retention",
    )(proj, proj, proj, proj, pm, pm, gn)


def _diffattn_kernel(slopes_ref, q_ref, k_ref, v_ref, km_ref, vm_ref, gate_ref, gn_ref,
                     lq1_ref, lk1_ref, lq2_ref, lk2_ref, o_ref,
                     qs_ref, s_ref, sm_ref, mrun_ref, mrow_ref, acc_ref,
                     vext_ref, kmext_ref, vmext_ref, dbias_ref, *, lam_init):
    grp = pl.program_id(1)
    t = pl.program_id(2)
    hb = HEADS_PER_STEP
    seq = k_ref.shape[0]
    head_cols = lambda hh: slice(hh * HEAD_DIM, (hh + 1) * HEAD_DIM)
    slope2 = lambda hh: slopes_ref[grp * hb + hh] * LOG2E

    @pl.when(t == 0)
    def _():
        r = lax.broadcasted_iota(jnp.int32, (QT, QT), 0)
        c = lax.broadcasted_iota(jnp.int32, (QT, QT), 1)
        allowed = (c // CHUNK) <= (r // CHUNK)
        rel = (r - jnp.abs(r - c)).astype(F32)
        for hh in range(hb):
            cols = head_cols(hh)
            dbias_ref[hh] = jnp.where(allowed, slope2(hh) * rel, NEG)
            vext_ref[hh, :, :HEAD_DIM] = v_ref[:, cols]
            vext_ref[hh, :, HEAD_DIM:] = jnp.ones((seq, HEAD_DIM), BF16)
            kmext_ref[hh] = _pad_rows(km_ref[:, cols], BLK)
            vmext_ref[hh, :, :HEAD_DIM] = _pad_rows(vm_ref[:, cols], BLK)
            vmext_ref[hh, :, HEAD_DIM:] = jnp.ones((BLK, HEAD_DIM), BF16)

    def frame_scores(j, hh):
        start = pl.multiple_of(j * QT, QT)
        kt = k_ref[pl.ds(start, QT), head_cols(hh)]
        return lax.dot_general(qs_ref[hh], kt, NT, preferred_element_type=F32) * LOG2E

    def keep(j, hh, s):
        s_ref[hh, j] = s
        mrun_ref[hh] = jnp.maximum(mrun_ref[hh], jnp.maximum(s[:, :BLK], s[:, BLK:]))

    colm = lax.broadcasted_iota(jnp.int32, (1, BLK), 1)
    relm = (colm - N_META - t * QT).astype(F32)
    for hh in range(hb):
        qs_ref[hh] = _split_components(q_ref[:, head_cols(hh)])
        s = lax.dot_general(qs_ref[hh], kmext_ref[hh], NT, preferred_element_type=F32)
        s = s * LOG2E + jnp.where(colm < N_META, slope2(hh) * relm, NEG)
        sm_ref[hh] = s
        mrun_ref[hh] = s
        s = frame_scores(t, hh)
        keep(t, hh, jnp.concatenate([s[:QT] + dbias_ref[hh], s[QT:] + dbias_ref[hh]], axis=0))

    col = lax.broadcasted_iota(jnp.int32, (1, QT), 1)

    def off_diag(j):
        rel = (col + (j - t) * QT).astype(F32)
        for hh in range(hb):
            keep(j, hh, frame_scores(j, hh) + slope2(hh) * rel)

    def off_diag_pair(jp, carry):
        off_diag(2 * jp)
        off_diag(2 * jp + 1)
        return carry

    lax.fori_loop(0, t // 2, off_diag_pair, 0)

    @pl.when(t % 2 == 1)
    def _():
        off_diag(t - 1)

    for hh in range(hb):
        mrow_ref[hh] = jnp.broadcast_to(
            jnp.max(mrun_ref[hh], axis=-1, keepdims=True), (2 * QT, BLK))
        p = jnp.exp2(sm_ref[hh] - mrow_ref[hh])
        acc_ref[hh] = jnp.dot(p.astype(BF16), vmext_ref[hh], preferred_element_type=F32)

    def pv(j, hh):
        start = pl.multiple_of(j * QT, QT)
        s = s_ref[hh, j]
        m = mrow_ref[hh]
        p = jnp.concatenate([jnp.exp2(s[:, :BLK] - m), jnp.exp2(s[:, BLK:] - m)], axis=1)
        return jnp.dot(p.astype(BF16), vext_ref[hh, pl.ds(start, QT), :],
                       preferred_element_type=F32)

    def pv_pair(jp, carry):
        for hh in range(hb):
            acc_ref[hh] = acc_ref[hh] + (pv(2 * jp, hh) + pv(2 * jp + 1, hh))
        return carry

    lax.fori_loop(0, (t + 1) // 2, pv_pair, 0)

    @pl.when(t % 2 == 0)
    def _():
        for hh in range(hb):
            acc_ref[hh] = acc_ref[hh] + pv(t, hh)

    lam = _lambda(lq1_ref, lk1_ref, lq2_ref, lk2_ref, lam_init)
    for hh in range(hb):
        cols = head_cols(hh)
        acc = acc_ref[hh]
        a = acc[:, :HEAD_DIM] / acc[:, HEAD_DIM:]
        d = a[:QT] - lam * a[QT:]
        y = d * lax.rsqrt(jnp.mean(d * d, axis=-1, keepdims=True) + EPS)
        y = y * gn_ref[:, cols] * (1.0 - lam_init)
        o_ref[:, cols] = (y * _silu(gate_ref[:, cols].astype(F32))).astype(BF16)


def _diffattn(proj, pm, gn, lams, layer, batch, seq, lam_init):
    m = proj.shape[0]
    ntile = seq // QT
    hb = HEADS_PER_STEP
    ngrp = HEADS // hb
    width = hb * HEAD_DIM
    slopes = jnp.asarray(2.0 ** (-8.0 * np.arange(1, HEADS + 1) / HEADS), F32)
    qcol, kcol, vcol, gcol = (4 * ngrp, 5 * ngrp, 6 * ngrp, 7 * ngrp)
    vec = pl.BlockSpec((None, 1, DIFF_DH), lambda b, g, t: (layer, 0, 0))
    return pl.pallas_call(
        functools.partial(_diffattn_kernel, lam_init=lam_init),
        out_shape=jax.ShapeDtypeStruct((m, MIX_HALF), BF16),
        grid=(batch, ngrp, ntile),
        in_specs=[
            pl.BlockSpec(memory_space=pltpu.SMEM),
            pl.BlockSpec((QT, width), lambda b, g, t: (b * ntile + t, qcol + g)),
            pl.BlockSpec((seq, width), lambda b, g, t: (b, kcol + g)),
            pl.BlockSpec((seq, width), lambda b, g, t: (b, vcol + g)),
            pl.BlockSpec((None, N_META, width), lambda b, g, t: (0, 0, kcol + g)),
            pl.BlockSpec((None, N_META, width), lambda b, g, t: (0, 0, vcol + g)),
            pl.BlockSpec((QT, width), lambda b, g, t: (b * ntile + t, gcol + g)),
            pl.BlockSpec((None, 1, width), lambda b, g, t: (layer, 0, g)),
            vec, vec, vec, vec,
        ],
        out_specs=pl.BlockSpec((QT, width), lambda b, g, t: (b * ntile + t, g)),
        scratch_shapes=[
            pltpu.VMEM((hb, 2 * QT, HEAD_DIM), BF16),
            pltpu.VMEM((hb, ntile, 2 * QT, QT), F32),
            pltpu.VMEM((hb, 2 * QT, BLK), F32),
            pltpu.VMEM((hb, 2 * QT, BLK), F32),
            pltpu.VMEM((hb, 2 * QT, BLK), F32),
            pltpu.VMEM((hb, 2 * QT, 2 * HEAD_DIM), F32),
            pltpu.VMEM((hb, seq, 2 * HEAD_DIM), BF16),
            pltpu.VMEM((hb, BLK, HEAD_DIM), BF16),
            pltpu.VMEM((hb, BLK, 2 * HEAD_DIM), BF16),
            pltpu.VMEM((hb, QT, QT), F32),
        ],
        compiler_params=pltpu.CompilerParams(
            dimension_semantics=("parallel", "parallel", "arbitrary"),
            vmem_limit_bytes=VMEM_LIMIT),
        name="diffattn",
    )(slopes, proj, proj, proj, pm, pm, proj, gn, *lams)


def _outproj_kernel(r_ref, d_ref, wr_ref, wd_ref, h_ref, gf_ref, o_ref, wb_ref, *, final_norm):
    @pl.when(pl.program_id(0) == 0)
    def _():
        wb_ref[0] = wr_ref[...].astype(BF16)
        wb_ref[1] = wd_ref[...].astype(BF16)

    y = jnp.dot(r_ref[...], wb_ref[0], preferred_element_type=F32)
    y = y + jnp.dot(d_ref[...], wb_ref[1], preferred_element_type=F32)
    hn = h_ref[...] + y
    if final_norm:
        ms = jnp.mean(hn * hn, axis=-1, keepdims=True)
        hn = hn * lax.rsqrt(ms + EPS) * gf_ref[...]
    o_ref[...] = hn


def _outproj(r, d, w, h, gf, layer, tm, final_norm):
    m, dm = h.shape
    return pl.pallas_call(
        functools.partial(_outproj_kernel, final_norm=final_norm),
        out_shape=jax.ShapeDtypeStruct((m, dm), F32),
        grid=(m // tm,),
        in_specs=[
            pl.BlockSpec((tm, MIX_HALF), lambda i: (i, 0)),
            pl.BlockSpec((tm, MIX_HALF), lambda i: (i, 0)),
            pl.BlockSpec((None, MIX_HALF, dm), lambda i: (layer, 0, 0),
                         pipeline_mode=pl.Buffered(1)),
            pl.BlockSpec((None, MIX_HALF, dm), lambda i: (layer, 1, 0),
                         pipeline_mode=pl.Buffered(1)),
            pl.BlockSpec((tm, dm), lambda i: (i, 0)),
            pl.BlockSpec((1, dm), lambda i: (0, 0)),
        ],
        out_specs=pl.BlockSpec((tm, dm), lambda i: (i, 0)),
        scratch_shapes=[pltpu.VMEM((2, MIX_HALF, dm), BF16)],
        compiler_params=pltpu.CompilerParams(
            dimension_semantics=("arbitrary",),
            vmem_limit_bytes=VMEM_LIMIT),
        name="outproj",
    )(r, d, w, w, h, gf)


def kernel(x, meta_tokens, norm_g, w_in, w_out, ret_norm_g, diff_norm_g,
           lambda_q1, lambda_k1, lambda_q2, lambda_k2, final_norm_g):
    batch, seq, dm = x.shape
    depth = w_in.shape[0]
    m = batch * seq
    assert seq % QT == 0 and meta_tokens.shape[0] == N_META

    h = x.reshape(m, dm)
    hm = meta_tokens.astype(x.dtype)
    gf = final_norm_g.reshape(1, dm)
    per_layer = lambda a: a.reshape(depth, 1, a.shape[-1])
    norm_g, ret_norm_g, diff_norm_g = map(per_layer, (norm_g, ret_norm_g, diff_norm_g))
    lams = tuple(map(per_layer, (lambda_q1, lambda_k1, lambda_q2, lambda_k2)))

    for i in range(depth):
        last = i == depth - 1
        lam_init = 0.8 - 0.6 * float(np.exp(-0.3 * i))
        proj, pm = _inproj(h, hm, norm_g, w_in, i, tm=m // 8, tn=1024)
        if not last:
            hm = _meta_layer(pm, hm, w_out, ret_norm_g, diff_norm_g, lams, i, lam_init)
        r = _retention(proj, pm, ret_norm_g, i, batch, seq)
        d = _diffattn(proj, pm, diff_norm_g, lams, i, batch, seq, lam_init)
        h = _outproj(r, d, w_out, h, gf, i, tm=m // 16, final_norm=last)

    return h.reshape(batch, seq, dm)
```

```python
import functools

import numpy as np
import jax
import jax.numpy as jnp
from jax import lax
from jax.experimental import pallas as pl
from jax.experimental.pallas import tpu as pltpu

F32 = jnp.float32
BF16 = jnp.bfloat16

CHUNK = 64
N_META = 16
EPS = 1e-6
HEADS = 8
HEAD_DIM = 128
DIFF_DH = 64
MIX_HALF = HEADS * HEAD_DIM
BLK = 128
QT = 256
NEG = -1e30
LOG2E = 1.4426950408889634
HEADS_PER_STEP = 4

VMEM_LIMIT = 56 * 1024 * 1024

NT = (((1,), (1,)), ((), ()))
TN = (((0,), (0,)), ((), ()))


def _log_g(h):
    return float(np.log(1.0 - 2.0 ** (-5.0 - h)))


def _silu(x):
    return x * (1.0 / (1.0 + jnp.exp(-x)))


def _lambda(lq1_ref, lk1_ref, lq2_ref, lk2_ref, lam_init):
    return (jnp.exp(jnp.sum(lq1_ref[...] * lk1_ref[...], axis=-1, keepdims=True))
            - jnp.exp(jnp.sum(lq2_ref[...] * lk2_ref[...], axis=-1, keepdims=True))
            + lam_init)


def _pad_rows(a, rows):
    return jnp.concatenate([a, jnp.zeros((rows - a.shape[0], a.shape[1]), a.dtype)], axis=0)


def _split_components(q):
    lane = lax.broadcasted_iota(jnp.int32, q.shape, 1)
    zero = jnp.zeros((), BF16)
    qscaled = q * jnp.asarray(DIFF_DH ** -0.5, BF16)
    return jnp.concatenate([jnp.where(lane < DIFF_DH, qscaled, zero),
                            jnp.where(lane >= DIFF_DH, qscaled, zero)], axis=0)


def _rms_norm_bf16(x, g):
    ms = jnp.mean(x * x, axis=-1, keepdims=True)
    return (x * lax.rsqrt(ms + EPS) * g).astype(BF16)


def _inproj_kernel(h_ref, hm_ref, g_ref, w_ref, o_ref, pm_ref, u_ref):
    tm = h_ref.shape[0]

    @pl.when(pl.program_id(1) == 0)
    def _():
        u_ref[:tm] = _rms_norm_bf16(h_ref[...], g_ref[...])
        u_ref[tm:] = _rms_norm_bf16(hm_ref[...], g_ref[...])

    out = jnp.dot(u_ref[...], w_ref[...].astype(BF16),
                  preferred_element_type=F32).astype(BF16)
    o_ref[...] = out[:tm]
    pm_ref[...] = out[tm:]


def _inproj(h, hm, g, w, layer, tm, tn):
    m, d = h.shape
    n = w.shape[-1]
    return pl.pallas_call(
        _inproj_kernel,
        out_shape=[jax.ShapeDtypeStruct((m, n), BF16),
                   jax.ShapeDtypeStruct((m // tm, N_META, n), BF16)],
        grid=(m // tm, n // tn),
        in_specs=[
            pl.BlockSpec((tm, d), lambda i, j: (i, 0)),
            pl.BlockSpec((N_META, d), lambda i, j: (0, 0)),
            pl.BlockSpec((None, 1, d), lambda i, j: (layer, 0, 0)),
            pl.BlockSpec((None, d, tn), lambda i, j: (layer, 0, j)),
        ],
        out_specs=[pl.BlockSpec((tm, tn), lambda i, j: (i, j)),
                   pl.BlockSpec((None, N_META, tn), lambda i, j: (i, 0, j))],
        scratch_shapes=[pltpu.VMEM((tm + N_META, d), BF16)],
        compiler_params=pltpu.CompilerParams(
            dimension_semantics=("parallel", "arbitrary"),
            vmem_limit_bytes=VMEM_LIMIT),
        name="inproj",
    )(h, hm, g, w)


def _meta_kernel(pm_ref, hm_ref, wr_ref, wd_ref, gret_ref, gdiff_ref,
                 lq1_ref, lk1_ref, lq2_ref, lk2_ref, hm_out_ref, *, lam_init):
    section = lambda s: pm_ref[:, s * MIX_HALF:(s + 1) * MIX_HALF]
    rq, rk, rv, rgate, dq, dk, dv, dgate = [section(s) for s in range(8)]
    t = lax.broadcasted_iota(jnp.int32, (BLK, BLK), 0)
    s_i = lax.broadcasted_iota(jnp.int32, (BLK, BLK), 1)
    dist = jnp.abs(t - s_i).astype(F32)
    r2 = lax.broadcasted_iota(jnp.int32, (2 * BLK, BLK), 0) % BLK
    c2 = lax.broadcasted_iota(jnp.int32, (2 * BLK, BLK), 1)
    dist2 = jnp.abs(r2 - c2).astype(F32)
    lam = _lambda(lq1_ref, lk1_ref, lq2_ref, lk2_ref, lam_init)
    r_parts, d_parts = [], []
    for h in range(HEADS):
        cols = slice(h * HEAD_DIM, (h + 1) * HEAD_DIM)
        q = _pad_rows(rq[:, cols], BLK)
        k = _pad_rows(rk[:, cols], BLK)
        v = _pad_rows(rv[:, cols], BLK)
        dmat = jnp.exp(_log_g(h) * dist) * (HEAD_DIM ** -0.5)
        s = lax.dot_general(q, k, NT, preferred_element_type=F32) * dmat
        o = jnp.dot(s.astype(BF16), v, preferred_element_type=F32)[:N_META]
        mu = jnp.mean(o, axis=-1, keepdims=True)
        oc = o - mu
        var = jnp.mean(oc * oc, axis=-1, keepdims=True)
        y = oc * lax.rsqrt(var + EPS) * gret_ref[:, cols]
        r_parts.append((y * _silu(rgate[:, cols].astype(F32))).astype(BF16))
        slope = 2.0 ** (-(h + 1.0))
        qs = _split_components(_pad_rows(dq[:, cols], BLK))
        k = _pad_rows(dk[:, cols], BLK)
        v = _pad_rows(dv[:, cols], BLK)
        bias = jnp.where(c2 < N_META, -slope * dist2, NEG)
        s = lax.dot_general(qs, k, NT, preferred_element_type=F32) + bias
        pr = jnp.exp(s - jnp.max(s, axis=-1, keepdims=True))
        a = (jnp.dot(pr.astype(BF16), v, preferred_element_type=F32)
             / jnp.sum(pr, axis=-1, keepdims=True))
        d = a[:N_META] - lam * a[BLK:BLK + N_META]
        y = d * lax.rsqrt(jnp.mean(d * d, axis=-1, keepdims=True) + EPS)
        y = y * gdiff_ref[:, cols] * (1.0 - lam_init)
        d_parts.append((y * _silu(dgate[:, cols].astype(F32))).astype(BF16))
    r = jnp.concatenate(r_parts, axis=1)
    d = jnp.concatenate(d_parts, axis=1)
    y = jnp.dot(r, wr_ref[...].astype(BF16), preferred_element_type=F32)
    y = y + jnp.dot(d, wd_ref[...].astype(BF16), preferred_element_type=F32)
    hm_out_ref[...] = hm_ref[...] + y


def _meta_layer(pm, hm, w_out, gret, gdiff, lams, layer, lam_init):
    dm = hm.shape[1]
    n = pm.shape[-1]
    vec = lambda width: pl.BlockSpec((None, 1, width), lambda i: (layer, 0, 0))
    return pl.pallas_call(
        functools.partial(_meta_kernel, lam_init=lam_init),
        out_shape=jax.ShapeDtypeStruct((N_META, dm), F32),
        grid=(1,),
        in_specs=[
            pl.BlockSpec((None, N_META, n), lambda i: (0, 0, 0)),
            pl.BlockSpec((N_META, dm), lambda i: (0, 0)),
            pl.BlockSpec((None, MIX_HALF, dm), lambda i: (layer, 0, 0),
                         pipeline_mode=pl.Buffered(1)),
            pl.BlockSpec((None, MIX_HALF, dm), lambda i: (layer, 1, 0),
                         pipeline_mode=pl.Buffered(1)),
            vec(MIX_HALF), vec(MIX_HALF),
            vec(DIFF_DH), vec(DIFF_DH), vec(DIFF_DH), vec(DIFF_DH),
        ],
        out_specs=pl.BlockSpec((N_META, dm), lambda i: (0, 0)),
        compiler_params=pltpu.CompilerParams(
            dimension_semantics=("arbitrary",),
            vmem_limit_bytes=VMEM_LIMIT),
        name="meta_layer",
    )(pm, hm, w_out, w_out, gret, gdiff, *lams)


def _retention_kernel(q_ref, k_ref, v_ref, gate_ref, km_ref, vm_ref, gn_ref, o_ref,
                      state_ref, dmat_ref, rq_ref, rk_ref):
    j = pl.program_id(1)
    scale = HEAD_DIM ** -0.5

    @pl.when(j == 0)
    def _():
        t = lax.broadcasted_iota(jnp.int32, (BLK, BLK), 0)
        s = lax.broadcasted_iota(jnp.int32, (BLK, BLK), 1)
        dist = jnp.abs(t - s).astype(F32)
        allowed = (s // CHUNK) <= (t // CHUNK)
        tf = t.astype(F32)
        mf = lax.broadcasted_iota(jnp.int32, (N_META, HEAD_DIM), 0).astype(F32)
        for h in range(HEADS):
            cols = slice(h * HEAD_DIM, (h + 1) * HEAD_DIM)
            log_g = _log_g(h)
            dmat_ref[h] = jnp.where(allowed, jnp.exp(log_g * dist) * scale, 0.0)
            rq_ref[h] = jnp.exp(log_g * (tf + 1.0))
            rk_ref[h] = jnp.exp(log_g * (BLK - 1.0 - tf)) * scale
            zeta = jnp.exp(log_g * (N_META - 1.0 - mf)) * scale
            kz = (km_ref[:, cols].astype(F32) * zeta).astype(BF16)
            state_ref[h] = lax.dot_general(_pad_rows(kz, BLK), _pad_rows(vm_ref[:, cols], BLK),
                                           TN, preferred_element_type=F32)

    def block(i, carry):
        rows = pl.ds(pl.multiple_of(i * BLK, BLK), BLK)
        for h in range(HEADS):
            cols = slice(h * HEAD_DIM, (h + 1) * HEAD_DIM)
            decay_blk = float((1.0 - 2.0 ** (-5.0 - h)) ** BLK)
            q = q_ref[rows, cols]
            k = k_ref[rows, cols]
            v = v_ref[rows, cols]
            s = lax.dot_general(q, k, NT, preferred_element_type=F32) * dmat_ref[h]
            intra = jnp.dot(s.astype(BF16), v, preferred_element_type=F32)
            state = state_ref[h]
            qx = (q.astype(F32) * rq_ref[h]).astype(BF16)
            cross = jnp.dot(qx, state.astype(BF16), preferred_element_type=F32)
            kz = (k.astype(F32) * rk_ref[h]).astype(BF16)
            kv = lax.dot_general(kz, v, TN, preferred_element_type=F32)
            state_ref[h] = decay_blk * state + kv

            o = intra + cross
            mu = jnp.mean(o, axis=-1, keepdims=True)
            oc = o - mu
            var = jnp.mean(oc * oc, axis=-1, keepdims=True)
            y = oc * lax.rsqrt(var + EPS) * gn_ref[:, cols]
            o_ref[rows, cols] = (y * _silu(gate_ref[rows, cols].astype(F32))).astype(BF16)
        return carry

    lax.fori_loop(0, q_ref.shape[0] // BLK, block, 0)


RET_ROWS = 512


def _retention(proj, pm, gn, layer, batch, seq):
    m = proj.shape[0]
    nblk = seq // RET_ROWS
    row = lambda b, j: b * nblk + j
    return pl.pallas_call(
        _retention_kernel,
        out_shape=jax.ShapeDtypeStruct((m, MIX_HALF), BF16),
        grid=(batch, nblk),
        in_specs=[
            pl.BlockSpec((RET_ROWS, MIX_HALF), lambda b, j: (row(b, j), 0)),
            pl.BlockSpec((RET_ROWS, MIX_HALF), lambda b, j: (row(b, j), 1)),
            pl.BlockSpec((RET_ROWS, MIX_HALF), lambda b, j: (row(b, j), 2)),
            pl.BlockSpec((RET_ROWS, MIX_HALF), lambda b, j: (row(b, j), 3)),
            pl.BlockSpec((None, N_META, MIX_HALF), lambda b, j: (0, 0, 1)),
            pl.BlockSpec((None, N_META, MIX_HALF), lambda b, j: (0, 0, 2)),
            pl.BlockSpec((None, 1, MIX_HALF), lambda b, j: (layer, 0, 0)),
        ],
        out_specs=pl.BlockSpec((RET_ROWS, MIX_HALF), lambda b, j: (row(b, j), 0)),
        scratch_shapes=[
            pltpu.VMEM((HEADS, HEAD_DIM, HEAD_DIM), F32),
            pltpu.VMEM((HEADS, BLK, BLK), F32),
            pltpu.VMEM((HEADS, BLK, HEAD_DIM), F32),
            pltpu.VMEM((HEADS, BLK, HEAD_DIM), F32),
        ],
        compiler_params=pltpu.CompilerParams(
            dimension_semantics=("parallel", "arbitrary"),
            vmem_limit_bytes=VMEM_LIMIT),
        name="retention",
    )(proj, proj, proj, proj, pm, pm, gn)


def _diffattn_kernel(slopes_ref, qa_ref, qb_ref, k_ref, v_ref, km_ref, vm_ref,
                     gatea_ref, gateb_ref, gn_ref,
                     lq1_ref, lk1_ref, lq2_ref, lk2_ref, o_ref,
                     qs_ref, s_ref, sm_ref, mrun_ref, mrow_ref, acc_ref,
                     vext_ref, kmext_ref, vmext_ref, dbias_ref, *, lam_init):
    grp = pl.program_id(1)
    step = pl.program_id(2)
    hb = HEADS_PER_STEP
    seq = k_ref.shape[0]
    ntile = seq // QT
    head_cols = lambda hh: slice(hh * HEAD_DIM, (hh + 1) * HEAD_DIM)
    slope2 = lambda hh: slopes_ref[grp * hb + hh] * LOG2E
    tiles = ((step, 0, qa_ref, gatea_ref),
             (ntile - 1 - step, step + 1, qb_ref, gateb_ref))

    @pl.when(step == 0)
    def _():
        r = lax.broadcasted_iota(jnp.int32, (QT, QT), 0)
        c = lax.broadcasted_iota(jnp.int32, (QT, QT), 1)
        allowed = (c // CHUNK) <= (r // CHUNK)
        rel = (r - jnp.abs(r - c)).astype(F32)
        for hh in range(hb):
            cols = head_cols(hh)
            dbias_ref[hh] = jnp.where(allowed, slope2(hh) * rel, NEG)
            vext_ref[hh, :, :HEAD_DIM] = v_ref[:, cols]
            vext_ref[hh, :, HEAD_DIM:] = jnp.ones((seq, HEAD_DIM), BF16)
            kmext_ref[hh] = _pad_rows(km_ref[:, cols], BLK)
            vmext_ref[hh, :, :HEAD_DIM] = _pad_rows(vm_ref[:, cols], BLK)
            vmext_ref[hh, :, HEAD_DIM:] = jnp.ones((BLK, HEAD_DIM), BF16)

    def frame_scores(x, hh, j):
        start = pl.multiple_of(j * QT, QT)
        kt = k_ref[pl.ds(start, QT), head_cols(hh)]
        return lax.dot_general(qs_ref[x, hh], kt, NT, preferred_element_type=F32) * LOG2E

    def keep(x, hh, slot, s):
        s_ref[hh, slot] = s
        mrun_ref[x, hh] = jnp.maximum(mrun_ref[x, hh], jnp.maximum(s[:, :BLK], s[:, BLK:]))

    colm = lax.broadcasted_iota(jnp.int32, (1, BLK), 1)
    for x, (t, base, q_ref, _) in enumerate(tiles):
        relm = (colm - N_META - t * QT).astype(F32)
        for hh in range(hb):
            qs_ref[x, hh] = _split_components(q_ref[:, head_cols(hh)])
            s = lax.dot_general(qs_ref[x, hh], kmext_ref[hh], NT, preferred_element_type=F32)
            s = s * LOG2E + jnp.where(colm < N_META, slope2(hh) * relm, NEG)
            sm_ref[x, hh] = s
            mrun_ref[x, hh] = s
            s = frame_scores(x, hh, t)
            keep(x, hh, base + t,
                 jnp.concatenate([s[:QT] + dbias_ref[hh], s[QT:] + dbias_ref[hh]], axis=0))

    col = lax.broadcasted_iota(jnp.int32, (1, QT), 1)

    def off_diag_loops(x, t, base):
        def off_diag(j):
            rel = (col + (j - t) * QT).astype(F32)
            for hh in range(hb):
                keep(x, hh, base + j, frame_scores(x, hh, j) + slope2(hh) * rel)

        def off_diag_pair(jp, carry):
            off_diag(2 * jp)
            off_diag(2 * jp + 1)
            return carry

        lax.fori_loop(0, t // 2, off_diag_pair, 0)

        @pl.when(t % 2 == 1)
        def _():
            off_diag(t - 1)

    for x, (t, base, _, _) in enumerate(tiles):
        off_diag_loops(x, t, base)

    for x in range(2):
        for hh in range(hb):
            mrow_ref[x, hh] = jnp.broadcast_to(
                jnp.max(mrun_ref[x, hh], axis=-1, keepdims=True), (2 * QT, BLK))
            p = jnp.exp2(sm_ref[x, hh] - mrow_ref[x, hh])
            acc_ref[x, hh] = jnp.dot(p.astype(BF16), vmext_ref[hh], preferred_element_type=F32)

    def pv_loops(x, t, base):
        def pv(j, hh):
            start = pl.multiple_of(j * QT, QT)
            s = s_ref[hh, base + j]
            m = mrow_ref[x, hh]
            p = jnp.concatenate([jnp.exp2(s[:, :BLK] - m), jnp.exp2(s[:, BLK:] - m)], axis=1)
            return jnp.dot(p.astype(BF16), vext_ref[hh, pl.ds(start, QT), :],
                           preferred_element_type=F32)

        def pv_pair(jp, carry):
            for hh in range(hb):
                acc_ref[x, hh] = acc_ref[x, hh] + (pv(2 * jp, hh) + pv(2 * jp + 1, hh))
            return carry

        lax.fori_loop(0, (t + 1) // 2, pv_pair, 0)

        @pl.when(t % 2 == 0)
        def _():
            for hh in range(hb):
                acc_ref[x, hh] = acc_ref[x, hh] + pv(t, hh)

    for x, (t, base, _, _) in enumerate(tiles):
        pv_loops(x, t, base)

    lam = _lambda(lq1_ref, lk1_ref, lq2_ref, lk2_ref, lam_init)
    for x, (_, _, _, gate_ref) in enumerate(tiles):
        for hh in range(hb):
            cols = head_cols(hh)
            acc = acc_ref[x, hh]
            a = acc[:, :HEAD_DIM] / acc[:, HEAD_DIM:]
            d = a[:QT] - lam * a[QT:]
            y = d * lax.rsqrt(jnp.mean(d * d, axis=-1, keepdims=True) + EPS)
            y = y * gn_ref[:, cols] * (1.0 - lam_init)
            o_ref[x * QT:(x + 1) * QT, cols] = (
                y * _silu(gate_ref[:, cols].astype(F32))).astype(BF16)


def _diffattn(proj, pm, gn, lams, layer, batch, seq, lam_init):
    m = proj.shape[0]
    ntile = seq // QT
    hb = HEADS_PER_STEP
    ngrp = HEADS // hb
    width = hb * HEAD_DIM
    slopes = jnp.asarray(2.0 ** (-8.0 * np.arange(1, HEADS + 1) / HEADS), F32)
    qcol, kcol, vcol, gcol = (4 * ngrp, 5 * ngrp, 6 * ngrp, 7 * ngrp)
    vec = pl.BlockSpec((None, 1, DIFF_DH), lambda b, g, t: (layer, 0, 0))
    tile_a = lambda b, t: b * ntile + t
    tile_b = lambda b, t: b * ntile + ntile - 1 - t
    return pl.pallas_call(
        functools.partial(_diffattn_kernel, lam_init=lam_init),
        out_shape=jax.ShapeDtypeStruct((m, MIX_HALF), BF16),
        grid=(batch, ngrp, ntile // 2),
        in_specs=[
            pl.BlockSpec(memory_space=pltpu.SMEM),
            pl.BlockSpec((QT, width), lambda b, g, t: (tile_a(b, t), qcol + g)),
            pl.BlockSpec((QT, width), lambda b, g, t: (tile_b(b, t), qcol + g)),
            pl.BlockSpec((seq, width), lambda b, g, t: (b, kcol + g)),
            pl.BlockSpec((seq, width), lambda b, g, t: (b, vcol + g)),
            pl.BlockSpec((None, N_META, width), lambda b, g, t: (0, 0, kcol + g)),
            pl.BlockSpec((None, N_META, width), lambda b, g, t: (0, 0, vcol + g)),
            pl.BlockSpec((QT, width), lambda b, g, t: (tile_a(b, t), gcol + g)),
            pl.BlockSpec((QT, width), lambda b, g, t: (tile_b(b, t), gcol + g)),
            pl.BlockSpec((None, 1, width), lambda b, g, t: (layer, 0, g)),
            vec, vec, vec, vec,
        ],
        out_specs=pl.BlockSpec((2 * QT, width), lambda b, g, t: (b * (ntile // 2) + t, g)),
        scratch_shapes=[
            pltpu.VMEM((2, hb, 2 * QT, HEAD_DIM), BF16),
            pltpu.VMEM((hb, ntile + 1, 2 * QT, QT), F32),
            pltpu.VMEM((2, hb, 2 * QT, BLK), F32),
            pltpu.VMEM((2, hb, 2 * QT, BLK), F32),
            pltpu.VMEM((2, hb, 2 * QT, BLK), F32),
            pltpu.VMEM((2, hb, 2 * QT, 2 * HEAD_DIM), F32),
            pltpu.VMEM((hb, seq, 2 * HEAD_DIM), BF16),
            pltpu.VMEM((hb, BLK, HEAD_DIM), BF16),
            pltpu.VMEM((hb, BLK, 2 * HEAD_DIM), BF16),
            pltpu.VMEM((hb, QT, QT), F32),
        ],
        compiler_params=pltpu.CompilerParams(
            dimension_semantics=("parallel", "parallel", "arbitrary"),
            vmem_limit_bytes=VMEM_LIMIT),
        name="diffattn",
    )(slopes, proj, proj, proj, proj, pm, pm, proj, proj, gn, *lams)


def _outproj_kernel(r_ref, d_ref, wr_ref, wd_ref, h_ref, gf_ref, o_ref, wb_ref, *, final_norm):
    @pl.when(pl.program_id(0) == 0)
    def _():
        wb_ref[0] = wr_ref[...].astype(BF16)
        wb_ref[1] = wd_ref[...].astype(BF16)

    y = jnp.dot(r_ref[...], wb_ref[0], preferred_element_type=F32)
    y = y + jnp.dot(d_ref[...], wb_ref[1], preferred_element_type=F32)
    hn = h_ref[...] + y
    if final_norm:
        ms = jnp.mean(hn * hn, axis=-1, keepdims=True)
        hn = hn * lax.rsqrt(ms + EPS) * gf_ref[...]
    o_ref[...] = hn


def _outproj(r, d, w, h, gf, layer, seq, final_norm):
    m, dm = h.shape
    tm = QT
    ntile = seq // QT

    def d_block(i):
        b, t = i // ntile, i % ntile
        return b * ntile + jnp.where(t < ntile // 2, 2 * t, 2 * (ntile - 1 - t) + 1), 0

    return pl.pallas_call(
        functools.partial(_outproj_kernel, final_norm=final_norm),
        out_shape=jax.ShapeDtypeStruct((m, dm), F32),
        grid=(m // tm,),
        in_specs=[
            pl.BlockSpec((tm, MIX_HALF), lambda i: (i, 0)),
            pl.BlockSpec((tm, MIX_HALF), d_block),
            pl.BlockSpec((None, MIX_HALF, dm), lambda i: (layer, 0, 0),
                         pipeline_mode=pl.Buffered(1)),
            pl.BlockSpec((None, MIX_HALF, dm), lambda i: (layer, 1, 0),
                         pipeline_mode=pl.Buffered(1)),
            pl.BlockSpec((tm, dm), lambda i: (i, 0)),
            pl.BlockSpec((1, dm), lambda i: (0, 0)),
        ],
        out_specs=pl.BlockSpec((tm, dm), lambda i: (i, 0)),
        scratch_shapes=[pltpu.VMEM((2, MIX_HALF, dm), BF16)],
        compiler_params=pltpu.CompilerParams(
            dimension_semantics=("arbitrary",),
            vmem_limit_bytes=VMEM_LIMIT),
        name="outproj",
    )(r, d, w, w, h, gf)


def kernel(x, meta_tokens, norm_g, w_in, w_out, ret_norm_g, diff_norm_g,
           lambda_q1, lambda_k1, lambda_q2, lambda_k2, final_norm_g):
    batch, seq, dm = x.shape
    depth = w_in.shape[0]
    m = batch * seq
    assert seq % QT == 0 and meta_tokens.shape[0] == N_META

    h = x.reshape(m, dm)
    hm = meta_tokens.astype(x.dtype)
    gf = final_norm_g.reshape(1, dm)
    per_layer = lambda a: a.reshape(depth, 1, a.shape[-1])
    norm_g, ret_norm_g, diff_norm_g = map(per_layer, (norm_g, ret_norm_g, diff_norm_g))
    lams = tuple(map(per_layer, (lambda_q1, lambda_k1, lambda_q2, lambda_k2)))

    for i in range(depth):
        last = i == depth - 1
        lam_init = 0.8 - 0.6 * float(np.exp(-0.3 * i))
        proj, pm = _inproj(h, hm, norm_g, w_in, i, tm=m // 8, tn=1024)
        if not last:
            hm = _meta_layer(pm, hm, w_out, ret_norm_g, diff_norm_g, lams, i, lam_init)
        r = _retention(proj, pm, ret_norm_g, i, batch, seq)
        d = _diffattn(proj, pm, diff_norm_g, lams, i, batch, seq, lam_init)
        h = _outproj(r, d, w_out, h, gf, i, seq, final_norm=last)

    return h.reshape(batch, seq, dm)
```

```python
import functools

import numpy as np
import jax
import jax.numpy as jnp
from jax import lax
from jax.experimental import pallas as pl
from jax.experimental.pallas import tpu as pltpu

F32 = jnp.float32
BF16 = jnp.bfloat16

CHUNK = 64
N_META = 16
EPS = 1e-6
HEADS = 8
HEAD_DIM = 128
DIFF_DH = 64
MIX_HALF = HEADS * HEAD_DIM
BLK = 128
QT = 256
NEG = -1e30
LOG2E = 1.4426950408889634
HEADS_PER_STEP = 4

VMEM_LIMIT = 56 * 1024 * 1024

NT = (((1,), (1,)), ((), ()))
TN = (((0,), (0,)), ((), ()))


def _log_g(h):
    return float(np.log(1.0 - 2.0 ** (-5.0 - h)))


def _silu(x):
    return x * (1.0 / (1.0 + jnp.exp(-x)))


def _lambda(lq1_ref, lk1_ref, lq2_ref, lk2_ref, lam_init):
    return (jnp.exp(jnp.sum(lq1_ref[...] * lk1_ref[...], axis=-1, keepdims=True))
            - jnp.exp(jnp.sum(lq2_ref[...] * lk2_ref[...], axis=-1, keepdims=True))
            + lam_init)


def _pad_rows(a, rows):
    return jnp.concatenate([a, jnp.zeros((rows - a.shape[0], a.shape[1]), a.dtype)], axis=0)


def _split_components(q):
    lane = lax.broadcasted_iota(jnp.int32, q.shape, 1)
    zero = jnp.zeros((), BF16)
    qscaled = q * jnp.asarray(DIFF_DH ** -0.5, BF16)
    return jnp.concatenate([jnp.where(lane < DIFF_DH, qscaled, zero),
                            jnp.where(lane >= DIFF_DH, qscaled, zero)], axis=0)


def _rms_norm_bf16(x, g):
    ms = jnp.mean(x * x, axis=-1, keepdims=True)
    return (x * lax.rsqrt(ms + EPS) * g).astype(BF16)


def _inproj_kernel(h_ref, hm_ref, g_ref, w_ref, o_ref, pm_ref, u_ref):
    tm = h_ref.shape[0]

    @pl.when(pl.program_id(1) == 0)
    def _():
        u_ref[:tm] = _rms_norm_bf16(h_ref[...], g_ref[...])
        u_ref[tm:] = _rms_norm_bf16(hm_ref[...], g_ref[...])

    out = jnp.dot(u_ref[...], w_ref[...].astype(BF16),
                  preferred_element_type=F32).astype(BF16)
    o_ref[...] = out[:tm]
    pm_ref[...] = out[tm:]


def _inproj(h, hm, g, w, layer, tm, tn):
    m, d = h.shape
    n = w.shape[-1]
    return pl.pallas_call(
        _inproj_kernel,
        out_shape=[jax.ShapeDtypeStruct((m, n), BF16),
                   jax.ShapeDtypeStruct((m // tm, N_META, n), BF16)],
        grid=(m // tm, n // tn),
        in_specs=[
            pl.BlockSpec((tm, d), lambda i, j: (i, 0)),
            pl.BlockSpec((N_META, d), lambda i, j: (0, 0)),
            pl.BlockSpec((None, 1, d), lambda i, j: (layer, 0, 0)),
            pl.BlockSpec((None, d, tn), lambda i, j: (layer, 0, j)),
        ],
        out_specs=[pl.BlockSpec((tm, tn), lambda i, j: (i, j)),
                   pl.BlockSpec((None, N_META, tn), lambda i, j: (i, 0, j))],
        scratch_shapes=[pltpu.VMEM((tm + N_META, d), BF16)],
        compiler_params=pltpu.CompilerParams(
            dimension_semantics=("parallel", "arbitrary"),
            vmem_limit_bytes=VMEM_LIMIT),
        name="inproj",
    )(h, hm, g, w)


def _meta_kernel(pm_ref, hm_ref, wr_ref, wd_ref, gret_ref, gdiff_ref,
                 lq1_ref, lk1_ref, lq2_ref, lk2_ref, hm_out_ref, *, lam_init):
    section = lambda s: pm_ref[:, s * MIX_HALF:(s + 1) * MIX_HALF]
    rq, rk, rv, rgate, dq, dk, dv, dgate = [section(s) for s in range(8)]
    t = lax.broadcasted_iota(jnp.int32, (BLK, BLK), 0)
    s_i = lax.broadcasted_iota(jnp.int32, (BLK, BLK), 1)
    dist = jnp.abs(t - s_i).astype(F32)
    r2 = lax.broadcasted_iota(jnp.int32, (2 * BLK, BLK), 0) % BLK
    c2 = lax.broadcasted_iota(jnp.int32, (2 * BLK, BLK), 1)
    dist2 = jnp.abs(r2 - c2).astype(F32)
    lam = _lambda(lq1_ref, lk1_ref, lq2_ref, lk2_ref, lam_init)
    r_parts, d_parts = [], []
    for h in range(HEADS):
        cols = slice(h * HEAD_DIM, (h + 1) * HEAD_DIM)
        q = _pad_rows(rq[:, cols], BLK)
        k = _pad_rows(rk[:, cols], BLK)
        v = _pad_rows(rv[:, cols], BLK)
        dmat = jnp.exp(_log_g(h) * dist) * (HEAD_DIM ** -0.5)
        s = lax.dot_general(q, k, NT, preferred_element_type=F32) * dmat
        o = jnp.dot(s.astype(BF16), v, preferred_element_type=F32)[:N_META]
        mu = jnp.mean(o, axis=-1, keepdims=True)
        oc = o - mu
        var = jnp.mean(oc * oc, axis=-1, keepdims=True)
        y = oc * lax.rsqrt(var + EPS) * gret_ref[:, cols]
        r_parts.append((y * _silu(rgate[:, cols].astype(F32))).astype(BF16))
        slope = 2.0 ** (-(h + 1.0))
        qs = _split_components(_pad_rows(dq[:, cols], BLK))
        k = _pad_rows(dk[:, cols], BLK)
        v = _pad_rows(dv[:, cols], BLK)
        bias = jnp.where(c2 < N_META, -slope * dist2, NEG)
        s = lax.dot_general(qs, k, NT, preferred_element_type=F32) + bias
        pr = jnp.exp(s - jnp.max(s, axis=-1, keepdims=True))
        a = (jnp.dot(pr.astype(BF16), v, preferred_element_type=F32)
             / jnp.sum(pr, axis=-1, keepdims=True))
        d = a[:N_META] - lam * a[BLK:BLK + N_META]
        y = d * lax.rsqrt(jnp.mean(d * d, axis=-1, keepdims=True) + EPS)
        y = y * gdiff_ref[:, cols] * (1.0 - lam_init)
        d_parts.append((y * _silu(dgate[:, cols].astype(F32))).astype(BF16))
    r = jnp.concatenate(r_parts, axis=1)
    d = jnp.concatenate(d_parts, axis=1)
    y = jnp.dot(r, wr_ref[...].astype(BF16), preferred_element_type=F32)
    y = y + jnp.dot(d, wd_ref[...].astype(BF16), preferred_element_type=F32)
    hm_out_ref[...] = hm_ref[...] + y


def _meta_layer(pm, hm, w_out, gret, gdiff, lams, layer, lam_init):
    dm = hm.shape[1]
    n = pm.shape[-1]
    vec = lambda width: pl.BlockSpec((None, 1, width), lambda i: (layer, 0, 0))
    return pl.pallas_call(
        functools.partial(_meta_kernel, lam_init=lam_init),
        out_shape=jax.ShapeDtypeStruct((N_META, dm), F32),
        grid=(1,),
        in_specs=[
            pl.BlockSpec((None, N_META, n), lambda i: (0, 0, 0)),
            pl.BlockSpec((N_META, dm), lambda i: (0, 0)),
            pl.BlockSpec((None, MIX_HALF, dm), lambda i: (layer, 0, 0),
                         pipeline_mode=pl.Buffered(1)),
            pl.BlockSpec((None, MIX_HALF, dm), lambda i: (layer, 1, 0),
                         pipeline_mode=pl.Buffered(1)),
            vec(MIX_HALF), vec(MIX_HALF),
            vec(DIFF_DH), vec(DIFF_DH), vec(DIFF_DH), vec(DIFF_DH),
        ],
        out_specs=pl.BlockSpec((N_META, dm), lambda i: (0, 0)),
        compiler_params=pltpu.CompilerParams(
            dimension_semantics=("arbitrary",),
            vmem_limit_bytes=VMEM_LIMIT),
        name="meta_layer",
    )(pm, hm, w_out, w_out, gret, gdiff, *lams)


def _retention_kernel(q_ref, k_ref, v_ref, gate_ref, km_ref, vm_ref, gn_ref, o_ref,
                      state_ref, dmat_ref, rq_ref, rk_ref):
    j = pl.program_id(1)
    scale = HEAD_DIM ** -0.5

    @pl.when(j == 0)
    def _():
        t = lax.broadcasted_iota(jnp.int32, (BLK, BLK), 0)
        s = lax.broadcasted_iota(jnp.int32, (BLK, BLK), 1)
        dist = jnp.abs(t - s).astype(F32)
        allowed = (s // CHUNK) <= (t // CHUNK)
        tf = t.astype(F32)
        mf = lax.broadcasted_iota(jnp.int32, (N_META, HEAD_DIM), 0).astype(F32)
        for h in range(HEADS):
            cols = slice(h * HEAD_DIM, (h + 1) * HEAD_DIM)
            log_g = _log_g(h)
            dmat_ref[h] = jnp.where(allowed, jnp.exp(log_g * dist) * scale, 0.0)
            rq_ref[h] = jnp.exp(log_g * (tf + 1.0))
            rk_ref[h] = jnp.exp(log_g * (BLK - 1.0 - tf)) * scale
            zeta = jnp.exp(log_g * (N_META - 1.0 - mf)) * scale
            kz = (km_ref[:, cols].astype(F32) * zeta).astype(BF16)
            state_ref[h] = lax.dot_general(_pad_rows(kz, BLK), _pad_rows(vm_ref[:, cols], BLK),
                                           TN, preferred_element_type=F32)

    def block(i, carry):
        rows = pl.ds(pl.multiple_of(i * BLK, BLK), BLK)
        for h in range(HEADS):
            cols = slice(h * HEAD_DIM, (h + 1) * HEAD_DIM)
            decay_blk = float((1.0 - 2.0 ** (-5.0 - h)) ** BLK)
            q = q_ref[rows, cols]
            k = k_ref[rows, cols]
            v = v_ref[rows, cols]
            s = lax.dot_general(q, k, NT, preferred_element_type=F32) * dmat_ref[h]
            intra = jnp.dot(s.astype(BF16), v, preferred_element_type=F32)
            state = state_ref[h]
            qx = (q.astype(F32) * rq_ref[h]).astype(BF16)
            cross = jnp.dot(qx, state.astype(BF16), preferred_element_type=F32)
            kz = (k.astype(F32) * rk_ref[h]).astype(BF16)
            kv = lax.dot_general(kz, v, TN, preferred_element_type=F32)
            state_ref[h] = decay_blk * state + kv

            o = intra + cross
            mu = jnp.mean(o, axis=-1, keepdims=True)
            oc = o - mu
            var = jnp.mean(oc * oc, axis=-1, keepdims=True)
            y = oc * lax.rsqrt(var + EPS) * gn_ref[:, cols]
            o_ref[rows, cols] = (y * _silu(gate_ref[rows, cols].astype(F32))).astype(BF16)
        return carry

    lax.fori_loop(0, q_ref.shape[0] // BLK, block, 0)


RET_ROWS = 512


def _retention(proj, pm, gn, layer, batch, seq):
    m = proj.shape[0]
    nblk = seq // RET_ROWS
    row = lambda b, j: b * nblk + j
    return pl.pallas_call(
        _retention_kernel,
        out_shape=jax.ShapeDtypeStruct((m, MIX_HALF), BF16),
        grid=(batch, nblk),
        in_specs=[
            pl.BlockSpec((RET_ROWS, MIX_HALF), lambda b, j: (row(b, j), 0)),
            pl.BlockSpec((RET_ROWS, MIX_HALF), lambda b, j: (row(b, j), 1)),
            pl.BlockSpec((RET_ROWS, MIX_HALF), lambda b, j: (row(b, j), 2)),
            pl.BlockSpec((RET_ROWS, MIX_HALF), lambda b, j: (row(b, j), 3)),
            pl.BlockSpec((None, N_META, MIX_HALF), lambda b, j: (0, 0, 1)),
            pl.BlockSpec((None, N_META, MIX_HALF), lambda b, j: (0, 0, 2)),
            pl.BlockSpec((None, 1, MIX_HALF), lambda b, j: (layer, 0, 0)),
        ],
        out_specs=pl.BlockSpec((RET_ROWS, MIX_HALF), lambda b, j: (row(b, j), 0)),
        scratch_shapes=[
            pltpu.VMEM((HEADS, HEAD_DIM, HEAD_DIM), F32),
            pltpu.VMEM((HEADS, BLK, BLK), F32),
            pltpu.VMEM((HEADS, BLK, HEAD_DIM), F32),
            pltpu.VMEM((HEADS, BLK, HEAD_DIM), F32),
        ],
        compiler_params=pltpu.CompilerParams(
            dimension_semantics=("parallel", "arbitrary"),
            vmem_limit_bytes=VMEM_LIMIT),
        name="retention",
    )(proj, proj, proj, proj, pm, pm, gn)


def _diffattn_kernel(slopes_ref, qa_ref, qb_ref, k_ref, v_ref, km_ref, vm_ref,
                     gatea_ref, gateb_ref, gn_ref,
                     lq1_ref, lk1_ref, lq2_ref, lk2_ref, o_ref,
                     qs_ref, s_ref, sm_ref, mrun_ref, mrow_ref, acc_ref,
                     vext_ref, kmext_ref, vmext_ref, dbias_ref, *, lam_init):
    grp = pl.program_id(1)
    step = pl.program_id(2)
    hb = HEADS_PER_STEP
    seq = k_ref.shape[0]
    ntile = seq // QT
    head_cols = lambda hh: slice(hh * HEAD_DIM, (hh + 1) * HEAD_DIM)
    slope2 = lambda hh: slopes_ref[grp * hb + hh] * LOG2E
    tiles = ((step, 0, qa_ref, gatea_ref),
             (ntile - 1 - step, step + 1, qb_ref, gateb_ref))

    @pl.when(step == 0)
    def _():
        r = lax.broadcasted_iota(jnp.int32, (QT, QT), 0)
        c = lax.broadcasted_iota(jnp.int32, (QT, QT), 1)
        allowed = (c // CHUNK) <= (r // CHUNK)
        rel = (r - jnp.abs(r - c)).astype(F32)
        for hh in range(hb):
            cols = head_cols(hh)
            dbias_ref[hh] = jnp.where(allowed, slope2(hh) * rel, NEG)
            vext_ref[hh, :, :HEAD_DIM] = v_ref[:, cols]
            vext_ref[hh, :, HEAD_DIM:] = jnp.ones((seq, HEAD_DIM), BF16)
            kmext_ref[hh] = _pad_rows(km_ref[:, cols], BLK)
            vmext_ref[hh, :, :HEAD_DIM] = _pad_rows(vm_ref[:, cols], BLK)
            vmext_ref[hh, :, HEAD_DIM:] = jnp.ones((BLK, HEAD_DIM), BF16)

    def frame_scores(x, hh, j):
        start = pl.multiple_of(j * QT, QT)
        kt = k_ref[pl.ds(start, QT), head_cols(hh)]
        return lax.dot_general(qs_ref[x, hh], kt, NT, preferred_element_type=F32) * LOG2E

    def keep(x, hh, slot, s):
        s_ref[hh, slot] = s
        mrun_ref[x, hh] = jnp.maximum(mrun_ref[x, hh], jnp.maximum(s[:, :BLK], s[:, BLK:]))

    colm = lax.broadcasted_iota(jnp.int32, (1, BLK), 1)
    for x, (t, base, q_ref, _) in enumerate(tiles):
        relm = (colm - N_META - t * QT).astype(F32)
        for hh in range(hb):
            qs_ref[x, hh] = _split_components(q_ref[:, head_cols(hh)])
            s = lax.dot_general(qs_ref[x, hh], kmext_ref[hh], NT, preferred_element_type=F32)
            s = s * LOG2E + jnp.where(colm < N_META, slope2(hh) * relm, NEG)
            sm_ref[x, hh] = s
            mrun_ref[x, hh] = s
            s = frame_scores(x, hh, t)
            keep(x, hh, base + t,
                 jnp.concatenate([s[:QT] + dbias_ref[hh], s[QT:] + dbias_ref[hh]], axis=0))

    col = lax.broadcasted_iota(jnp.int32, (1, QT), 1)

    def off_diag_loops(x, t, base):
        def off_diag(j):
            rel = (col + (j - t) * QT).astype(F32)
            for hh in range(hb):
                keep(x, hh, base + j, frame_scores(x, hh, j) + slope2(hh) * rel)

        def off_diag_pair(jp, carry):
            off_diag(2 * jp)
            off_diag(2 * jp + 1)
            return carry

        lax.fori_loop(0, t // 2, off_diag_pair, 0)

        @pl.when(t % 2 == 1)
        def _():
            off_diag(t - 1)

    for x, (t, base, _, _) in enumerate(tiles):
        off_diag_loops(x, t, base)

    for x in range(2):
        for hh in range(hb):
            mrow_ref[x, hh] = jnp.broadcast_to(
                jnp.max(mrun_ref[x, hh], axis=-1, keepdims=True), (2 * QT, BLK))
            p = jnp.exp2(sm_ref[x, hh] - mrow_ref[x, hh])
            acc_ref[x, hh] = jnp.dot(p.astype(BF16), vmext_ref[hh], preferred_element_type=F32)

    def pv_loops(x, t, base):
        def pv(j, hh):
            start = pl.multiple_of(j * QT, QT)
            s = s_ref[hh, base + j]
            m = mrow_ref[x, hh]
            p = jnp.concatenate([jnp.exp2(s[:, :BLK] - m), jnp.exp2(s[:, BLK:] - m)], axis=1)
            return jnp.dot(p.astype(BF16), vext_ref[hh, pl.ds(start, QT), :],
                           preferred_element_type=F32)

        def pv_pair(jp, carry):
            for hh in range(hb):
                acc_ref[x, hh] = acc_ref[x, hh] + (pv(2 * jp, hh) + pv(2 * jp + 1, hh))
            return carry

        lax.fori_loop(0, (t + 1) // 2, pv_pair, 0)

        @pl.when(t % 2 == 0)
        def _():
            for hh in range(hb):
                acc_ref[x, hh] = acc_ref[x, hh] + pv(t, hh)

    for x, (t, base, _, _) in enumerate(tiles):
        pv_loops(x, t, base)

    lam = _lambda(lq1_ref, lk1_ref, lq2_ref, lk2_ref, lam_init)
    for x, (_, _, _, gate_ref) in enumerate(tiles):
        for hh in range(hb):
            cols = head_cols(hh)
            acc = acc_ref[x, hh]
            a = acc[:, :HEAD_DIM] / acc[:, HEAD_DIM:]
            d = a[:QT] - lam * a[QT:]
            y = d * lax.rsqrt(jnp.mean(d * d, axis=-1, keepdims=True) + EPS)
            y = y * gn_ref[:, cols] * (1.0 - lam_init)
            o_ref[x * QT:(x + 1) * QT, cols] = (
                y * _silu(gate_ref[:, cols].astype(F32))).astype(BF16)


def _diffattn(proj, pm, gn, lams, layer, batch, seq, lam_init):
    m = proj.shape[0]
    ntile = seq // QT
    hb = HEADS_PER_STEP
    ngrp = HEADS // hb
    width = hb * HEAD_DIM
    slopes = jnp.asarray(2.0 ** (-8.0 * np.arange(1, HEADS + 1) / HEADS), F32)
    qcol, kcol, vcol, gcol = (4 * ngrp, 5 * ngrp, 6 * ngrp, 7 * ngrp)
    vec = pl.BlockSpec((None, 1, DIFF_DH), lambda b, g, t: (layer, 0, 0))
    tile_a = lambda b, t: b * ntile + t
    tile_b = lambda b, t: b * ntile + ntile - 1 - t
    return pl.pallas_call(
        functools.partial(_diffattn_kernel, lam_init=lam_init),
        out_shape=jax.ShapeDtypeStruct((m, MIX_HALF), BF16),
        grid=(batch, ngrp, ntile // 2),
        in_specs=[
            pl.BlockSpec(memory_space=pltpu.SMEM),
            pl.BlockSpec((QT, width), lambda b, g, t: (tile_a(b, t), qcol + g)),
            pl.BlockSpec((QT, width), lambda b, g, t: (tile_b(b, t), qcol + g)),
            pl.BlockSpec((seq, width), lambda b, g, t: (b, kcol + g)),
            pl.BlockSpec((seq, width), lambda b, g, t: (b, vcol + g)),
            pl.BlockSpec((None, N_META, width), lambda b, g, t: (0, 0, kcol + g)),
            pl.BlockSpec((None, N_META, width), lambda b, g, t: (0, 0, vcol + g)),
            pl.BlockSpec((QT, width), lambda b, g, t: (tile_a(b, t), gcol + g)),
            pl.BlockSpec((QT, width), lambda b, g, t: (tile_b(b, t), gcol + g)),
            pl.BlockSpec((None, 1, width), lambda b, g, t: (layer, 0, g)),
            vec, vec, vec, vec,
        ],
        out_specs=pl.BlockSpec((2 * QT, width), lambda b, g, t: (b * (ntile // 2) + t, g)),
        scratch_shapes=[
            pltpu.VMEM((2, hb, 2 * QT, HEAD_DIM), BF16),
            pltpu.VMEM((hb, ntile + 1, 2 * QT, QT), F32),
            pltpu.VMEM((2, hb, 2 * QT, BLK), F32),
            pltpu.VMEM((2, hb, 2 * QT, BLK), F32),
            pltpu.VMEM((2, hb, 2 * QT, BLK), F32),
            pltpu.VMEM((2, hb, 2 * QT, 2 * HEAD_DIM), F32),
            pltpu.VMEM((hb, seq, 2 * HEAD_DIM), BF16),
            pltpu.VMEM((hb, BLK, HEAD_DIM), BF16),
            pltpu.VMEM((hb, BLK, 2 * HEAD_DIM), BF16),
            pltpu.VMEM((hb, QT, QT), F32),
        ],
        compiler_params=pltpu.CompilerParams(
            dimension_semantics=("parallel", "parallel", "arbitrary"),
            vmem_limit_bytes=VMEM_LIMIT),
        name="diffattn",
    )(slopes, proj, proj, proj, proj, pm, pm, proj, proj, gn, *lams)


def _outproj_kernel(r_ref, d_ref, wr_ref, wd_ref, h_ref, gf_ref, o_ref, wb_ref, *, final_norm):
    @pl.when(pl.program_id(0) == 0)
    def _():
        wb_ref[0] = wr_ref[...].astype(BF16)
        wb_ref[1] = wd_ref[...].astype(BF16)

    y = jnp.dot(r_ref[...], wb_ref[0], preferred_element_type=F32)
    y = y + jnp.dot(d_ref[...], wb_ref[1], preferred_element_type=F32)
    hn = h_ref[...] + y
    if final_norm:
        ms = jnp.mean(hn * hn, axis=-1, keepdims=True)
        hn = hn * lax.rsqrt(ms + EPS) * gf_ref[...]
    o_ref[...] = hn


def _outproj(r, d, w, h, gf, layer, seq, final_norm):
    m, dm = h.shape
    tm = QT
    ntile = seq // QT

    def d_block(i):
        b, t = i // ntile, i % ntile
        return b * ntile + jnp.where(t < ntile // 2, 2 * t, 2 * (ntile - 1 - t) + 1), 0

    return pl.pallas_call(
        functools.partial(_outproj_kernel, final_norm=final_norm),
        out_shape=jax.ShapeDtypeStruct((m, dm), F32),
        grid=(m // tm,),
        in_specs=[
            pl.BlockSpec((tm, MIX_HALF), lambda i: (i, 0)),
            pl.BlockSpec((tm, MIX_HALF), d_block),
            pl.BlockSpec((None, MIX_HALF, dm), lambda i: (layer, 0, 0),
                         pipeline_mode=pl.Buffered(1)),
            pl.BlockSpec((None, MIX_HALF, dm), lambda i: (layer, 1, 0),
                         pipeline_mode=pl.Buffered(1)),
            pl.BlockSpec((tm, dm), lambda i: (i, 0)),
            pl.BlockSpec((1, dm), lambda i: (0, 0)),
        ],
        out_specs=pl.BlockSpec((tm, dm), lambda i: (i, 0)),
        scratch_shapes=[pltpu.VMEM((2, MIX_HALF, dm), BF16)],
        compiler_params=pltpu.CompilerParams(
            dimension_semantics=("arbitrary",),
            vmem_limit_bytes=VMEM_LIMIT),
        name="outproj",
    )(r, d, w, w, h, gf)


def _mix_outproj_kernel(q_ref, k_ref, v_ref, gate_ref, km_ref, vm_ref, gret_ref,
                        da_ref, db_ref, wr_ref, wd_ref, h_ref, gf_ref, o_ref,
                        r_ref, state_ref, init_ref, dmat_ref, rq_ref, rk_ref,
                        *, final_norm, tiles_per_seq):
    i = pl.program_id(0)
    scale = HEAD_DIM ** -0.5
    tm = q_ref.shape[0]

    @pl.when(i == 0)
    def _():
        r_ref[...] = jnp.zeros_like(r_ref)
        state_ref[...] = jnp.zeros_like(state_ref)
        t = lax.broadcasted_iota(jnp.int32, (BLK, BLK), 0)
        s = lax.broadcasted_iota(jnp.int32, (BLK, BLK), 1)
        dist = jnp.abs(t - s).astype(F32)
        allowed = (s // CHUNK) <= (t // CHUNK)
        tf = t.astype(F32)
        mf = lax.broadcasted_iota(jnp.int32, (N_META, HEAD_DIM), 0).astype(F32)
        for h in range(HEADS):
            cols = slice(h * HEAD_DIM, (h + 1) * HEAD_DIM)
            log_g = _log_g(h)
            dmat_ref[h] = jnp.where(allowed, jnp.exp(log_g * dist) * scale, 0.0)
            rq_ref[h] = jnp.exp(log_g * (tf + 1.0))
            rk_ref[h] = jnp.exp(log_g * (BLK - 1.0 - tf)) * scale
            zeta = jnp.exp(log_g * (N_META - 1.0 - mf)) * scale
            kz = (km_ref[:, cols].astype(F32) * zeta).astype(BF16)
            init_ref[h] = lax.dot_general(_pad_rows(kz, BLK), _pad_rows(vm_ref[:, cols], BLK),
                                          TN, preferred_element_type=F32)

    r = r_ref[...]
    y = jnp.dot(r, wr_ref[...], preferred_element_type=F32)
    y = y + jnp.concatenate([jnp.dot(da_ref[...], wd_ref[...], preferred_element_type=F32),
                             jnp.dot(db_ref[...], wd_ref[...], preferred_element_type=F32)],
                            axis=0)
    hn = h_ref[...] + y
    if final_norm:
        ms = jnp.mean(hn * hn, axis=-1, keepdims=True)
        hn = hn * lax.rsqrt(ms + EPS) * gf_ref[...]
    o_ref[...] = hn

    first = (i % tiles_per_seq) == 0
    for blk in range(tm // BLK):
        rows = slice(blk * BLK, (blk + 1) * BLK)
        for h in range(HEADS):
            cols = slice(h * HEAD_DIM, (h + 1) * HEAD_DIM)
            decay_blk = float((1.0 - 2.0 ** (-5.0 - h)) ** BLK)
            q = q_ref[rows, cols]
            k = k_ref[rows, cols]
            v = v_ref[rows, cols]
            s = lax.dot_general(q, k, NT, preferred_element_type=F32) * dmat_ref[h]
            intra = jnp.dot(s.astype(BF16), v, preferred_element_type=F32)
            state = state_ref[h]
            if blk == 0:
                state = jnp.where(first, init_ref[h], state)
            qx = (q.astype(F32) * rq_ref[h]).astype(BF16)
            cross = jnp.dot(qx, state.astype(BF16), preferred_element_type=F32)
            kz = (k.astype(F32) * rk_ref[h]).astype(BF16)
            kv = lax.dot_general(kz, v, TN, preferred_element_type=F32)
            state_ref[h] = decay_blk * state + kv

            o = intra + cross
            mu = jnp.mean(o, axis=-1, keepdims=True)
            oc = o - mu
            var = jnp.mean(oc * oc, axis=-1, keepdims=True)
            y = oc * lax.rsqrt(var + EPS) * gret_ref[:, cols]
            r_ref[rows, cols] = (y * _silu(gate_ref[rows, cols].astype(F32))).astype(BF16)


def _mix_outproj(proj, pm, d, w, h, gret, gf, layer, seq, final_norm):
    m, dm = h.shape
    tm = 2 * QT
    ntiles = m // tm
    ntile_q = seq // QT
    cur = lambda i: jnp.minimum(i, ntiles - 1)
    prev = lambda i: jnp.maximum(i - 1, 0)

    def d_block(which):
        def index(i):
            tile = 2 * prev(i) + which
            b, t = tile // ntile_q, tile % ntile_q
            pos = jnp.where(t < ntile_q // 2, 2 * t, 2 * (ntile_q - 1 - t) + 1)
            return b * ntile_q + pos, 0
        return index

    return pl.pallas_call(
        functools.partial(_mix_outproj_kernel, final_norm=final_norm,
                          tiles_per_seq=seq // tm),
        out_shape=jax.ShapeDtypeStruct((m, dm), F32),
        grid=(ntiles + 1,),
        in_specs=[
            pl.BlockSpec((tm, MIX_HALF), lambda i: (cur(i), 0)),
            pl.BlockSpec((tm, MIX_HALF), lambda i: (cur(i), 1)),
            pl.BlockSpec((tm, MIX_HALF), lambda i: (cur(i), 2)),
            pl.BlockSpec((tm, MIX_HALF), lambda i: (cur(i), 3)),
            pl.BlockSpec((None, N_META, MIX_HALF), lambda i: (0, 0, 1)),
            pl.BlockSpec((None, N_META, MIX_HALF), lambda i: (0, 0, 2)),
            pl.BlockSpec((None, 1, MIX_HALF), lambda i: (layer, 0, 0)),
            pl.BlockSpec((QT, MIX_HALF), d_block(0)),
            pl.BlockSpec((QT, MIX_HALF), d_block(1)),
            pl.BlockSpec((None, MIX_HALF, dm), lambda i: (layer, 0, 0),
                         pipeline_mode=pl.Buffered(1)),
            pl.BlockSpec((None, MIX_HALF, dm), lambda i: (layer, 1, 0),
                         pipeline_mode=pl.Buffered(1)),
            pl.BlockSpec((tm, dm), lambda i: (prev(i), 0)),
            pl.BlockSpec((1, dm), lambda i: (0, 0)),
        ],
        out_specs=pl.BlockSpec((tm, dm), lambda i: (prev(i), 0)),
        scratch_shapes=[
            pltpu.VMEM((tm, MIX_HALF), BF16),
            pltpu.VMEM((HEADS, HEAD_DIM, HEAD_DIM), F32),
            pltpu.VMEM((HEADS, HEAD_DIM, HEAD_DIM), F32),
            pltpu.VMEM((HEADS, BLK, BLK), F32),
            pltpu.VMEM((HEADS, BLK, HEAD_DIM), F32),
            pltpu.VMEM((HEADS, BLK, HEAD_DIM), F32),
        ],
        compiler_params=pltpu.CompilerParams(
            dimension_semantics=("arbitrary",),
            vmem_limit_bytes=VMEM_LIMIT),
        name="mix_outproj",
    )(proj, proj, proj, proj, pm, pm, gret, d, d, w, w, h, gf)


def kernel(x, meta_tokens, norm_g, w_in, w_out, ret_norm_g, diff_norm_g,
           lambda_q1, lambda_k1, lambda_q2, lambda_k2, final_norm_g):
    batch, seq, dm = x.shape
    depth = w_in.shape[0]
    m = batch * seq
    assert seq % QT == 0 and meta_tokens.shape[0] == N_META

    h = x.reshape(m, dm)
    hm = meta_tokens.astype(x.dtype)
    gf = final_norm_g.reshape(1, dm)
    w_out_b = w_out.astype(BF16)
    per_layer = lambda a: a.reshape(depth, 1, a.shape[-1])
    norm_g, ret_norm_g, diff_norm_g = map(per_layer, (norm_g, ret_norm_g, diff_norm_g))
    lams = tuple(map(per_layer, (lambda_q1, lambda_k1, lambda_q2, lambda_k2)))

    for i in range(depth):
        last = i == depth - 1
        lam_init = 0.8 - 0.6 * float(np.exp(-0.3 * i))
        proj, pm = _inproj(h, hm, norm_g, w_in, i, tm=m // 8, tn=1024)
        if not last:
            hm = _meta_layer(pm, hm, w_out, ret_norm_g, diff_norm_g, lams, i, lam_init)
        d = _diffattn(proj, pm, diff_norm_g, lams, i, batch, seq, lam_init)
        h = _mix_outproj(proj, pm, d, w_out_b, h, ret_norm_g, gf, i, seq, final_norm=last)

    return h.reshape(batch, seq, dm)
```

```python
import functools

import numpy as np
import jax
import jax.numpy as jnp
from jax import lax
from jax.experimental import pallas as pl
from jax.experimental.pallas import tpu as pltpu

F32 = jnp.float32
BF16 = jnp.bfloat16

CHUNK = 64
N_META = 16
EPS = 1e-6
HEADS = 8
HEAD_DIM = 128
DIFF_DH = 64
MIX_HALF = HEADS * HEAD_DIM
BLK = 128
QT = 256
NEG = -1e30
LOG2E = 1.4426950408889634
HEADS_PER_STEP = 4

VMEM_LIMIT = 56 * 1024 * 1024

NT = (((1,), (1,)), ((), ()))
TN = (((0,), (0,)), ((), ()))


def _log_g(h):
    return float(np.log(1.0 - 2.0 ** (-5.0 - h)))


def _silu(x):
    return x * (1.0 / (1.0 + jnp.exp(-x)))


def _lambda(lq1_ref, lk1_ref, lq2_ref, lk2_ref, lam_init):
    return (jnp.exp(jnp.sum(lq1_ref[...] * lk1_ref[...], axis=-1, keepdims=True))
            - jnp.exp(jnp.sum(lq2_ref[...] * lk2_ref[...], axis=-1, keepdims=True))
            + lam_init)


def _pad_rows(a, rows):
    return jnp.concatenate([a, jnp.zeros((rows - a.shape[0], a.shape[1]), a.dtype)], axis=0)


def _split_components(q):
    lane = lax.broadcasted_iota(jnp.int32, q.shape, 1)
    zero = jnp.zeros((), BF16)
    qscaled = q * jnp.asarray(DIFF_DH ** -0.5, BF16)
    return jnp.concatenate([jnp.where(lane < DIFF_DH, qscaled, zero),
                            jnp.where(lane >= DIFF_DH, qscaled, zero)], axis=0)


def _rms_norm_bf16(x, g):
    ms = jnp.mean(x * x, axis=-1, keepdims=True)
    return (x * lax.rsqrt(ms + EPS) * g).astype(BF16)


def _inproj_kernel(h_ref, hm_ref, g_ref, w_ref, o_ref, pm_ref, u_ref):
    tm = h_ref.shape[0]

    @pl.when(pl.program_id(1) == 0)
    def _():
        u_ref[:tm] = _rms_norm_bf16(h_ref[...], g_ref[...])
        u_ref[tm:] = _rms_norm_bf16(hm_ref[...], g_ref[...])

    out = jnp.dot(u_ref[...], w_ref[...].astype(BF16),
                  preferred_element_type=F32).astype(BF16)
    o_ref[...] = out[:tm]
    pm_ref[...] = out[tm:]


def _inproj(h, hm, g, w, layer, tm, tn):
    m, d = h.shape
    n = w.shape[-1]
    if w.ndim == 3:
        w_spec = pl.BlockSpec((None, d, tn), lambda i, j: (layer, 0, j))
    else:
        w_spec = pl.BlockSpec((d, tn), lambda i, j: (0, j))
    return pl.pallas_call(
        _inproj_kernel,
        out_shape=[jax.ShapeDtypeStruct((m, n), BF16),
                   jax.ShapeDtypeStruct((m // tm, N_META, n), BF16)],
        grid=(m // tm, n // tn),
        in_specs=[
            pl.BlockSpec((tm, d), lambda i, j: (i, 0)),
            pl.BlockSpec((N_META, d), lambda i, j: (0, 0)),
            pl.BlockSpec((None, 1, d), lambda i, j: (layer, 0, 0)),
            w_spec,
        ],
        out_specs=[pl.BlockSpec((tm, tn), lambda i, j: (i, j)),
                   pl.BlockSpec((None, N_META, tn), lambda i, j: (i, 0, j))],
        scratch_shapes=[pltpu.VMEM((tm + N_META, d), BF16)],
        compiler_params=pltpu.CompilerParams(
            dimension_semantics=("parallel", "arbitrary"),
            vmem_limit_bytes=VMEM_LIMIT),
        name="inproj",
    )(h, hm, g, w)


def _meta_kernel(pm_ref, hm_ref, wr_ref, wd_ref, gret_ref, gdiff_ref,
                 lq1_ref, lk1_ref, lq2_ref, lk2_ref, hm_out_ref, *, lam_init):
    section = lambda s: pm_ref[:, s * MIX_HALF:(s + 1) * MIX_HALF]
    rq, rk, rv, rgate, dq, dk, dv, dgate = [section(s) for s in range(8)]
    t = lax.broadcasted_iota(jnp.int32, (BLK, BLK), 0)
    s_i = lax.broadcasted_iota(jnp.int32, (BLK, BLK), 1)
    dist = jnp.abs(t - s_i).astype(F32)
    r2 = lax.broadcasted_iota(jnp.int32, (2 * BLK, BLK), 0) % BLK
    c2 = lax.broadcasted_iota(jnp.int32, (2 * BLK, BLK), 1)
    dist2 = jnp.abs(r2 - c2).astype(F32)
    lam = _lambda(lq1_ref, lk1_ref, lq2_ref, lk2_ref, lam_init)
    r_parts, d_parts = [], []
    for h in range(HEADS):
        cols = slice(h * HEAD_DIM, (h + 1) * HEAD_DIM)
        q = _pad_rows(rq[:, cols], BLK)
        k = _pad_rows(rk[:, cols], BLK)
        v = _pad_rows(rv[:, cols], BLK)
        dmat = jnp.exp(_log_g(h) * dist) * (HEAD_DIM ** -0.5)
        s = lax.dot_general(q, k, NT, preferred_element_type=F32) * dmat
        o = jnp.dot(s.astype(BF16), v, preferred_element_type=F32)[:N_META]
        mu = jnp.mean(o, axis=-1, keepdims=True)
        oc = o - mu
        var = jnp.mean(oc * oc, axis=-1, keepdims=True)
        y = oc * lax.rsqrt(var + EPS) * gret_ref[:, cols]
        r_parts.append((y * _silu(rgate[:, cols].astype(F32))).astype(BF16))
        slope = 2.0 ** (-(h + 1.0))
        qs = _split_components(_pad_rows(dq[:, cols], BLK))
        k = _pad_rows(dk[:, cols], BLK)
        v = _pad_rows(dv[:, cols], BLK)
        bias = jnp.where(c2 < N_META, -slope * dist2, NEG)
        s = lax.dot_general(qs, k, NT, preferred_element_type=F32) + bias
        pr = jnp.exp(s - jnp.max(s, axis=-1, keepdims=True))
        a = (jnp.dot(pr.astype(BF16), v, preferred_element_type=F32)
             / jnp.sum(pr, axis=-1, keepdims=True))
        d = a[:N_META] - lam * a[BLK:BLK + N_META]
        y = d * lax.rsqrt(jnp.mean(d * d, axis=-1, keepdims=True) + EPS)
        y = y * gdiff_ref[:, cols] * (1.0 - lam_init)
        d_parts.append((y * _silu(dgate[:, cols].astype(F32))).astype(BF16))
    r = jnp.concatenate(r_parts, axis=1)
    d = jnp.concatenate(d_parts, axis=1)
    y = jnp.dot(r, wr_ref[...].astype(BF16), preferred_element_type=F32)
    y = y + jnp.dot(d, wd_ref[...].astype(BF16), preferred_element_type=F32)
    hm_out_ref[...] = hm_ref[...] + y


def _meta_layer(pm, hm, w_out, gret, gdiff, lams, layer, lam_init):
    dm = hm.shape[1]
    n = pm.shape[-1]
    vec = lambda width: pl.BlockSpec((None, 1, width), lambda i: (layer, 0, 0))
    return pl.pallas_call(
        functools.partial(_meta_kernel, lam_init=lam_init),
        out_shape=jax.ShapeDtypeStruct((N_META, dm), F32),
        grid=(1,),
        in_specs=[
            pl.BlockSpec((None, N_META, n), lambda i: (0, 0, 0)),
            pl.BlockSpec((N_META, dm), lambda i: (0, 0)),
            pl.BlockSpec((None, MIX_HALF, dm), lambda i: (layer, 0, 0),
                         pipeline_mode=pl.Buffered(1)),
            pl.BlockSpec((None, MIX_HALF, dm), lambda i: (layer, 1, 0),
                         pipeline_mode=pl.Buffered(1)),
            vec(MIX_HALF), vec(MIX_HALF),
            vec(DIFF_DH), vec(DIFF_DH), vec(DIFF_DH), vec(DIFF_DH),
        ],
        out_specs=pl.BlockSpec((N_META, dm), lambda i: (0, 0)),
        compiler_params=pltpu.CompilerParams(
            dimension_semantics=("arbitrary",),
            vmem_limit_bytes=VMEM_LIMIT),
        name="meta_layer",
    )(pm, hm, w_out, w_out, gret, gdiff, *lams)


def _retention_kernel(q_ref, k_ref, v_ref, gate_ref, km_ref, vm_ref, gn_ref, o_ref,
                      state_ref, dmat_ref, rq_ref, rk_ref):
    j = pl.program_id(1)
    scale = HEAD_DIM ** -0.5

    @pl.when(j == 0)
    def _():
        t = lax.broadcasted_iota(jnp.int32, (BLK, BLK), 0)
        s = lax.broadcasted_iota(jnp.int32, (BLK, BLK), 1)
        dist = jnp.abs(t - s).astype(F32)
        allowed = (s // CHUNK) <= (t // CHUNK)
        tf = t.astype(F32)
        mf = lax.broadcasted_iota(jnp.int32, (N_META, HEAD_DIM), 0).astype(F32)
        for h in range(HEADS):
            cols = slice(h * HEAD_DIM, (h + 1) * HEAD_DIM)
            log_g = _log_g(h)
            dmat_ref[h] = jnp.where(allowed, jnp.exp(log_g * dist) * scale, 0.0)
            rq_ref[h] = jnp.exp(log_g * (tf + 1.0))
            rk_ref[h] = jnp.exp(log_g * (BLK - 1.0 - tf)) * scale
            zeta = jnp.exp(log_g * (N_META - 1.0 - mf)) * scale
            kz = (km_ref[:, cols].astype(F32) * zeta).astype(BF16)
            state_ref[h] = lax.dot_general(_pad_rows(kz, BLK), _pad_rows(vm_ref[:, cols], BLK),
                                           TN, preferred_element_type=F32)

    def block(i, carry):
        rows = pl.ds(pl.multiple_of(i * BLK, BLK), BLK)
        for h in range(HEADS):
            cols = slice(h * HEAD_DIM, (h + 1) * HEAD_DIM)
            decay_blk = float((1.0 - 2.0 ** (-5.0 - h)) ** BLK)
            q = q_ref[rows, cols]
            k = k_ref[rows, cols]
            v = v_ref[rows, cols]
            s = lax.dot_general(q, k, NT, preferred_element_type=F32) * dmat_ref[h]
            intra = jnp.dot(s.astype(BF16), v, preferred_element_type=F32)
            state = state_ref[h]
            qx = (q.astype(F32) * rq_ref[h]).astype(BF16)
            cross = jnp.dot(qx, state.astype(BF16), preferred_element_type=F32)
            kz = (k.astype(F32) * rk_ref[h]).astype(BF16)
            kv = lax.dot_general(kz, v, TN, preferred_element_type=F32)
            state_ref[h] = decay_blk * state + kv

            o = intra + cross
            mu = jnp.mean(o, axis=-1, keepdims=True)
            oc = o - mu
            var = jnp.mean(oc * oc, axis=-1, keepdims=True)
            y = oc * lax.rsqrt(var + EPS) * gn_ref[:, cols]
            o_ref[rows, cols] = (y * _silu(gate_ref[rows, cols].astype(F32))).astype(BF16)
        return carry

    lax.fori_loop(0, q_ref.shape[0] // BLK, block, 0)


RET_ROWS = 512


def _retention(proj, pm, gn, layer, batch, seq):
    m = proj.shape[0]
    nblk = seq // RET_ROWS
    row = lambda b, j: b * nblk + j
    return pl.pallas_call(
        _retention_kernel,
        out_shape=jax.ShapeDtypeStruct((m, MIX_HALF), BF16),
        grid=(batch, nblk),
        in_specs=[
            pl.BlockSpec((RET_ROWS, MIX_HALF), lambda b, j: (row(b, j), 0)),
            pl.BlockSpec((RET_ROWS, MIX_HALF), lambda b, j: (row(b, j), 1)),
            pl.BlockSpec((RET_ROWS, MIX_HALF), lambda b, j: (row(b, j), 2)),
            pl.BlockSpec((RET_ROWS, MIX_HALF), lambda b, j: (row(b, j), 3)),
            pl.BlockSpec((None, N_META, MIX_HALF), lambda b, j: (0, 0, 1)),
            pl.BlockSpec((None, N_META, MIX_HALF), lambda b, j: (0, 0, 2)),
            pl.BlockSpec((None, 1, MIX_HALF), lambda b, j: (layer, 0, 0)),
        ],
        out_specs=pl.BlockSpec((RET_ROWS, MIX_HALF), lambda b, j: (row(b, j), 0)),
        scratch_shapes=[
            pltpu.VMEM((HEADS, HEAD_DIM, HEAD_DIM), F32),
            pltpu.VMEM((HEADS, BLK, BLK), F32),
            pltpu.VMEM((HEADS, BLK, HEAD_DIM), F32),
            pltpu.VMEM((HEADS, BLK, HEAD_DIM), F32),
        ],
        compiler_params=pltpu.CompilerParams(
            dimension_semantics=("parallel", "arbitrary"),
            vmem_limit_bytes=VMEM_LIMIT),
        name="retention",
    )(proj, proj, proj, proj, pm, pm, gn)


def _diffattn_kernel(*refs, lam_init, ncast):
    (slopes_ref, qa_ref, qb_ref, k_ref, v_ref, km_ref, vm_ref, gatea_ref, gateb_ref, gn_ref,
     lq1_ref, lk1_ref, lq2_ref, lk2_ref) = refs[:14]
    cast_src = refs[14:14 + ncast]
    o_ref = refs[14 + ncast]
    cast_dst = refs[15 + ncast:15 + 2 * ncast]
    (qs_ref, s_ref, sm_ref, mrun_ref, mrow_ref, acc_ref,
     vext_ref, kmext_ref, vmext_ref, dbias_ref) = refs[15 + 2 * ncast:]

    for src, dst in zip(cast_src, cast_dst):
        dst[...] = src[...].astype(BF16)

    grp = pl.program_id(1)
    step = pl.program_id(2)
    hb = HEADS_PER_STEP
    seq = k_ref.shape[0]
    ntile = seq // QT
    head_cols = lambda hh: slice(hh * HEAD_DIM, (hh + 1) * HEAD_DIM)
    slope2 = lambda hh: slopes_ref[grp * hb + hh] * LOG2E
    tiles = ((step, 0, qa_ref, gatea_ref),
             (ntile - 1 - step, step + 1, qb_ref, gateb_ref))

    @pl.when(step == 0)
    def _():
        r = lax.broadcasted_iota(jnp.int32, (QT, QT), 0)
        c = lax.broadcasted_iota(jnp.int32, (QT, QT), 1)
        allowed = (c // CHUNK) <= (r // CHUNK)
        rel = (r - jnp.abs(r - c)).astype(F32)
        for hh in range(hb):
            cols = head_cols(hh)
            dbias_ref[hh] = jnp.where(allowed, slope2(hh) * rel, NEG)
            vext_ref[hh, :, :HEAD_DIM] = v_ref[:, cols]
            vext_ref[hh, :, HEAD_DIM:] = jnp.ones((seq, HEAD_DIM), BF16)
            kmext_ref[hh] = _pad_rows(km_ref[:, cols], BLK)
            vmext_ref[hh, :, :HEAD_DIM] = _pad_rows(vm_ref[:, cols], BLK)
            vmext_ref[hh, :, HEAD_DIM:] = jnp.ones((BLK, HEAD_DIM), BF16)

    def frame_scores(x, hh, j):
        start = pl.multiple_of(j * QT, QT)
        kt = k_ref[pl.ds(start, QT), head_cols(hh)]
        return lax.dot_general(qs_ref[x, hh], kt, NT, preferred_element_type=F32) * LOG2E

    def keep(x, hh, slot, s):
        s_ref[hh, slot] = s
        mrun_ref[x, hh] = jnp.maximum(mrun_ref[x, hh], jnp.maximum(s[:, :BLK], s[:, BLK:]))

    colm = lax.broadcasted_iota(jnp.int32, (1, BLK), 1)
    for x, (t, base, q_ref, _) in enumerate(tiles):
        relm = (colm - N_META - t * QT).astype(F32)
        for hh in range(hb):
            qs_ref[x, hh] = _split_components(q_ref[:, head_cols(hh)])
            s = lax.dot_general(qs_ref[x, hh], kmext_ref[hh], NT, preferred_element_type=F32)
            s = s * LOG2E + jnp.where(colm < N_META, slope2(hh) * relm, NEG)
            sm_ref[x, hh] = s
            mrun_ref[x, hh] = s
            s = frame_scores(x, hh, t)
            keep(x, hh, base + t,
                 jnp.concatenate([s[:QT] + dbias_ref[hh], s[QT:] + dbias_ref[hh]], axis=0))

    col = lax.broadcasted_iota(jnp.int32, (1, QT), 1)

    def off_diag_loops(x, t, base):
        def off_diag(j):
            rel = (col + (j - t) * QT).astype(F32)
            for hh in range(hb):
                keep(x, hh, base + j, frame_scores(x, hh, j) + slope2(hh) * rel)

        def off_diag_pair(jp, carry):
            off_diag(2 * jp)
            off_diag(2 * jp + 1)
            return carry

        lax.fori_loop(0, t // 2, off_diag_pair, 0)

        @pl.when(t % 2 == 1)
        def _():
            off_diag(t - 1)

    for x, (t, base, _, _) in enumerate(tiles):
        off_diag_loops(x, t, base)

    for x in range(2):
        for hh in range(hb):
            mrow_ref[x, hh] = jnp.broadcast_to(
                jnp.max(mrun_ref[x, hh], axis=-1, keepdims=True), (2 * QT, BLK))
            p = jnp.exp2(sm_ref[x, hh] - mrow_ref[x, hh])
            acc_ref[x, hh] = jnp.dot(p.astype(BF16), vmext_ref[hh], preferred_element_type=F32)

    def pv_loops(x, t, base):
        def pv(j, hh):
            start = pl.multiple_of(j * QT, QT)
            s = s_ref[hh, base + j]
            m = mrow_ref[x, hh]
            p = jnp.concatenate([jnp.exp2(s[:, :BLK] - m), jnp.exp2(s[:, BLK:] - m)], axis=1)
            return jnp.dot(p.astype(BF16), vext_ref[hh, pl.ds(start, QT), :],
                           preferred_element_type=F32)

        def pv_pair(jp, carry):
            for hh in range(hb):
                acc_ref[x, hh] = acc_ref[x, hh] + (pv(2 * jp, hh) + pv(2 * jp + 1, hh))
            return carry

        lax.fori_loop(0, (t + 1) // 2, pv_pair, 0)

        @pl.when(t % 2 == 0)
        def _():
            for hh in range(hb):
                acc_ref[x, hh] = acc_ref[x, hh] + pv(t, hh)

    for x, (t, base, _, _) in enumerate(tiles):
        pv_loops(x, t, base)

    lam = _lambda(lq1_ref, lk1_ref, lq2_ref, lk2_ref, lam_init)
    for x, (_, _, _, gate_ref) in enumerate(tiles):
        for hh in range(hb):
            cols = head_cols(hh)
            acc = acc_ref[x, hh]
            a = acc[:, :HEAD_DIM] / acc[:, HEAD_DIM:]
            d = a[:QT] - lam * a[QT:]
            y = d * lax.rsqrt(jnp.mean(d * d, axis=-1, keepdims=True) + EPS)
            y = y * gn_ref[:, cols] * (1.0 - lam_init)
            o_ref[x * QT:(x + 1) * QT, cols] = (
                y * _silu(gate_ref[:, cols].astype(F32))).astype(BF16)


def _diffattn(proj, pm, gn, lams, layer, batch, seq, lam_init, casts):
    m = proj.shape[0]
    ntile = seq // QT
    hb = HEADS_PER_STEP
    ngrp = HEADS // hb
    width = hb * HEAD_DIM
    slopes = jnp.asarray(2.0 ** (-8.0 * np.arange(1, HEADS + 1) / HEADS), F32)
    qcol, kcol, vcol, gcol = (4 * ngrp, 5 * ngrp, 6 * ngrp, 7 * ngrp)
    vec = pl.BlockSpec((None, 1, DIFF_DH), lambda b, g, t: (layer, 0, 0))
    tile_a = lambda b, t: b * ntile + t
    tile_b = lambda b, t: b * ntile + ntile - 1 - t
    nstep = batch * ngrp * (ntile // 2)
    linear = lambda b, g, t: (b * ngrp + g) * (ntile // 2) + t
    cast_in, cast_out_specs, cast_out_shapes, cast_args = [], [], [], []
    for w, wl in casts:
        rows, ncol = w.shape[1:]
        slab = rows // nstep
        cast_in.append(pl.BlockSpec((None, slab, ncol),
                                    lambda b, g, t, wl=wl: (wl, linear(b, g, t), 0)))
        cast_out_specs.append(pl.BlockSpec((slab, ncol), lambda b, g, t: (linear(b, g, t), 0)))
        cast_out_shapes.append(jax.ShapeDtypeStruct((rows, ncol), BF16))
        cast_args.append(w)
    outs = pl.pallas_call(
        functools.partial(_diffattn_kernel, lam_init=lam_init, ncast=len(casts)),
        out_shape=[jax.ShapeDtypeStruct((m, MIX_HALF), BF16)] + cast_out_shapes,
        grid=(batch, ngrp, ntile // 2),
        in_specs=[
            pl.BlockSpec(memory_space=pltpu.SMEM),
            pl.BlockSpec((QT, width), lambda b, g, t: (tile_a(b, t), qcol + g)),
            pl.BlockSpec((QT, width), lambda b, g, t: (tile_b(b, t), qcol + g)),
            pl.BlockSpec((seq, width), lambda b, g, t: (b, kcol + g)),
            pl.BlockSpec((seq, width), lambda b, g, t: (b, vcol + g)),
            pl.BlockSpec((None, N_META, width), lambda b, g, t: (0, 0, kcol + g)),
            pl.BlockSpec((None, N_META, width), lambda b, g, t: (0, 0, vcol + g)),
            pl.BlockSpec((QT, width), lambda b, g, t: (tile_a(b, t), gcol + g)),
            pl.BlockSpec((QT, width), lambda b, g, t: (tile_b(b, t), gcol + g)),
            pl.BlockSpec((None, 1, width), lambda b, g, t: (layer, 0, g)),
            vec, vec, vec, vec,
        ] + cast_in,
        out_specs=[pl.BlockSpec((2 * QT, width), lambda b, g, t: (b * (ntile // 2) + t, g))]
        + cast_out_specs,
        scratch_shapes=[
            pltpu.VMEM((2, hb, 2 * QT, HEAD_DIM), BF16),
            pltpu.VMEM((hb, ntile + 1, 2 * QT, QT), F32),
            pltpu.VMEM((2, hb, 2 * QT, BLK), F32),
            pltpu.VMEM((2, hb, 2 * QT, BLK), F32),
            pltpu.VMEM((2, hb, 2 * QT, BLK), F32),
            pltpu.VMEM((2, hb, 2 * QT, 2 * HEAD_DIM), F32),
            pltpu.VMEM((hb, seq, 2 * HEAD_DIM), BF16),
            pltpu.VMEM((hb, BLK, HEAD_DIM), BF16),
            pltpu.VMEM((hb, BLK, 2 * HEAD_DIM), BF16),
            pltpu.VMEM((hb, QT, QT), F32),
        ],
        compiler_params=pltpu.CompilerParams(
            dimension_semantics=("parallel", "parallel", "arbitrary"),
            vmem_limit_bytes=VMEM_LIMIT),
        name="diffattn",
    )(slopes, proj, proj, proj, proj, pm, pm, proj, proj, gn, *lams, *cast_args)
    return outs[0], outs[1:]


def _outproj_kernel(r_ref, d_ref, wr_ref, wd_ref, h_ref, gf_ref, o_ref, wb_ref, *, final_norm):
    @pl.when(pl.program_id(0) == 0)
    def _():
        wb_ref[0] = wr_ref[...].astype(BF16)
        wb_ref[1] = wd_ref[...].astype(BF16)

    y = jnp.dot(r_ref[...], wb_ref[0], preferred_element_type=F32)
    y = y + jnp.dot(d_ref[...], wb_ref[1], preferred_element_type=F32)
    hn = h_ref[...] + y
    if final_norm:
        ms = jnp.mean(hn * hn, axis=-1, keepdims=True)
        hn = hn * lax.rsqrt(ms + EPS) * gf_ref[...]
    o_ref[...] = hn


def _outproj(r, d, w, h, gf, layer, seq, final_norm):
    m, dm = h.shape
    tm = QT
    ntile = seq // QT

    def d_block(i):
        b, t = i // ntile, i % ntile
        return b * ntile + jnp.where(t < ntile // 2, 2 * t, 2 * (ntile - 1 - t) + 1), 0

    return pl.pallas_call(
        functools.partial(_outproj_kernel, final_norm=final_norm),
        out_shape=jax.ShapeDtypeStruct((m, dm), F32),
        grid=(m // tm,),
        in_specs=[
            pl.BlockSpec((tm, MIX_HALF), lambda i: (i, 0)),
            pl.BlockSpec((tm, MIX_HALF), d_block),
            pl.BlockSpec((None, MIX_HALF, dm), lambda i: (layer, 0, 0),
                         pipeline_mode=pl.Buffered(1)),
            pl.BlockSpec((None, MIX_HALF, dm), lambda i: (layer, 1, 0),
                         pipeline_mode=pl.Buffered(1)),
            pl.BlockSpec((tm, dm), lambda i: (i, 0)),
            pl.BlockSpec((1, dm), lambda i: (0, 0)),
        ],
        out_specs=pl.BlockSpec((tm, dm), lambda i: (i, 0)),
        scratch_shapes=[pltpu.VMEM((2, MIX_HALF, dm), BF16)],
        compiler_params=pltpu.CompilerParams(
            dimension_semantics=("arbitrary",),
            vmem_limit_bytes=VMEM_LIMIT),
        name="outproj",
    )(r, d, w, w, h, gf)


def _mix_outproj_kernel(q_ref, k_ref, v_ref, gate_ref, km_ref, vm_ref, gret_ref,
                        da_ref, db_ref, wr_ref, wd_ref, h_ref, gf_ref, o_ref,
                        r_ref, state_ref, init_ref, dmat_ref, rq_ref, rk_ref,
                        *, final_norm, tiles_per_seq):
    i = pl.program_id(0)
    scale = HEAD_DIM ** -0.5
    tm = q_ref.shape[0]

    @pl.when(i == 0)
    def _():
        r_ref[...] = jnp.zeros_like(r_ref)
        state_ref[...] = jnp.zeros_like(state_ref)
        t = lax.broadcasted_iota(jnp.int32, (BLK, BLK), 0)
        s = lax.broadcasted_iota(jnp.int32, (BLK, BLK), 1)
        dist = jnp.abs(t - s).astype(F32)
        allowed = (s // CHUNK) <= (t // CHUNK)
        tf = t.astype(F32)
        mf = lax.broadcasted_iota(jnp.int32, (N_META, HEAD_DIM), 0).astype(F32)
        for h in range(HEADS):
            cols = slice(h * HEAD_DIM, (h + 1) * HEAD_DIM)
            log_g = _log_g(h)
            dmat_ref[h] = jnp.where(allowed, jnp.exp(log_g * dist) * scale, 0.0)
            rq_ref[h] = jnp.exp(log_g * (tf + 1.0))
            rk_ref[h] = jnp.exp(log_g * (BLK - 1.0 - tf)) * scale
            zeta = jnp.exp(log_g * (N_META - 1.0 - mf)) * scale
            kz = (km_ref[:, cols].astype(F32) * zeta).astype(BF16)
            init_ref[h] = lax.dot_general(_pad_rows(kz, BLK), _pad_rows(vm_ref[:, cols], BLK),
                                          TN, preferred_element_type=F32)

    r = r_ref[...]
    y = jnp.dot(r, wr_ref[...], preferred_element_type=F32)
    y = y + jnp.concatenate([jnp.dot(da_ref[...], wd_ref[...], preferred_element_type=F32),
                             jnp.dot(db_ref[...], wd_ref[...], preferred_element_type=F32)],
                            axis=0)
    hn = h_ref[...] + y
    if final_norm:
        ms = jnp.mean(hn * hn, axis=-1, keepdims=True)
        hn = hn * lax.rsqrt(ms + EPS) * gf_ref[...]
    o_ref[...] = hn

    first = (i % tiles_per_seq) == 0
    for blk in range(tm // BLK):
        rows = slice(blk * BLK, (blk + 1) * BLK)
        for h in range(HEADS):
            cols = slice(h * HEAD_DIM, (h + 1) * HEAD_DIM)
            decay_blk = float((1.0 - 2.0 ** (-5.0 - h)) ** BLK)
            q = q_ref[rows, cols]
            k = k_ref[rows, cols]
            v = v_ref[rows, cols]
            s = lax.dot_general(q, k, NT, preferred_element_type=F32) * dmat_ref[h]
            intra = jnp.dot(s.astype(BF16), v, preferred_element_type=F32)
            state = state_ref[h]
            if blk == 0:
                state = jnp.where(first, init_ref[h], state)
            qx = (q.astype(F32) * rq_ref[h]).astype(BF16)
            cross = jnp.dot(qx, state.astype(BF16), preferred_element_type=F32)
            kz = (k.astype(F32) * rk_ref[h]).astype(BF16)
            kv = lax.dot_general(kz, v, TN, preferred_element_type=F32)
            state_ref[h] = decay_blk * state + kv

            o = intra + cross
            mu = jnp.mean(o, axis=-1, keepdims=True)
            oc = o - mu
            var = jnp.mean(oc * oc, axis=-1, keepdims=True)
            y = oc * lax.rsqrt(var + EPS) * gret_ref[:, cols]
            r_ref[rows, cols] = (y * _silu(gate_ref[rows, cols].astype(F32))).astype(BF16)


def _mix_outproj(proj, pm, d, w, h, gret, gf, layer, seq, final_norm):
    m, dm = h.shape
    tm = 2 * QT
    ntiles = m // tm
    ntile_q = seq // QT
    cur = lambda i: jnp.minimum(i, ntiles - 1)
    prev = lambda i: jnp.maximum(i - 1, 0)

    def d_block(which):
        def index(i):
            tile = 2 * prev(i) + which
            b, t = tile // ntile_q, tile % ntile_q
            pos = jnp.where(t < ntile_q // 2, 2 * t, 2 * (ntile_q - 1 - t) + 1)
            return b * ntile_q + pos, 0
        return index

    return pl.pallas_call(
        functools.partial(_mix_outproj_kernel, final_norm=final_norm,
                          tiles_per_seq=seq // tm),
        out_shape=jax.ShapeDtypeStruct((m, dm), F32),
        grid=(ntiles + 1,),
        in_specs=[
            pl.BlockSpec((tm, MIX_HALF), lambda i: (cur(i), 0)),
            pl.BlockSpec((tm, MIX_HALF), lambda i: (cur(i), 1)),
            pl.BlockSpec((tm, MIX_HALF), lambda i: (cur(i), 2)),
            pl.BlockSpec((tm, MIX_HALF), lambda i: (cur(i), 3)),
            pl.BlockSpec((None, N_META, MIX_HALF), lambda i: (0, 0, 1)),
            pl.BlockSpec((None, N_META, MIX_HALF), lambda i: (0, 0, 2)),
            pl.BlockSpec((None, 1, MIX_HALF), lambda i: (layer, 0, 0)),
            pl.BlockSpec((QT, MIX_HALF), d_block(0)),
            pl.BlockSpec((QT, MIX_HALF), d_block(1)),
            pl.BlockSpec((MIX_HALF, dm), lambda i: (0, 0), pipeline_mode=pl.Buffered(1)),
            pl.BlockSpec((MIX_HALF, dm), lambda i: (1, 0), pipeline_mode=pl.Buffered(1)),
            pl.BlockSpec((tm, dm), lambda i: (prev(i), 0)),
            pl.BlockSpec((1, dm), lambda i: (0, 0)),
        ],
        out_specs=pl.BlockSpec((tm, dm), lambda i: (prev(i), 0)),
        scratch_shapes=[
            pltpu.VMEM((tm, MIX_HALF), BF16),
            pltpu.VMEM((HEADS, HEAD_DIM, HEAD_DIM), F32),
            pltpu.VMEM((HEADS, HEAD_DIM, HEAD_DIM), F32),
            pltpu.VMEM((HEADS, BLK, BLK), F32),
            pltpu.VMEM((HEADS, BLK, HEAD_DIM), F32),
            pltpu.VMEM((HEADS, BLK, HEAD_DIM), F32),
        ],
        compiler_params=pltpu.CompilerParams(
            dimension_semantics=("arbitrary",),
            vmem_limit_bytes=VMEM_LIMIT),
        name="mix_outproj",
    )(proj, proj, proj, proj, pm, pm, gret, d, d, w, w, h, gf)


def kernel(x, meta_tokens, norm_g, w_in, w_out, ret_norm_g, diff_norm_g,
           lambda_q1, lambda_k1, lambda_q2, lambda_k2, final_norm_g):
    batch, seq, dm = x.shape
    depth = w_in.shape[0]
    m = batch * seq
    assert seq % QT == 0 and meta_tokens.shape[0] == N_META

    h = x.reshape(m, dm)
    hm = meta_tokens.astype(x.dtype)
    gf = final_norm_g.reshape(1, dm)
    per_layer = lambda a: a.reshape(depth, 1, a.shape[-1])
    norm_g, ret_norm_g, diff_norm_g = map(per_layer, (norm_g, ret_norm_g, diff_norm_g))
    lams = tuple(map(per_layer, (lambda_q1, lambda_k1, lambda_q2, lambda_k2)))

    w_in_cur = w_in
    for i in range(depth):
        last = i == depth - 1
        lam_init = 0.8 - 0.6 * float(np.exp(-0.3 * i))
        proj, pm = _inproj(h, hm, norm_g, w_in_cur, i, tm=m // 8, tn=1024)
        if not last:
            hm = _meta_layer(pm, hm, w_out, ret_norm_g, diff_norm_g, lams, i, lam_init)
        casts = [(w_out, i)] + ([] if last else [(w_in, i + 1)])
        d, cast = _diffattn(proj, pm, diff_norm_g, lams, i, batch, seq, lam_init, casts)
        if not last:
            w_in_cur = cast[1]
        h = _mix_outproj(proj, pm, d, cast[0], h, ret_norm_g, gf, i, seq, final_norm=last)

    return h.reshape(batch, seq, dm)
```

```python
import functools

import numpy as np
import jax
import jax.numpy as jnp
from jax import lax
from jax.experimental import pallas as pl
from jax.experimental.pallas import tpu as pltpu

F32 = jnp.float32
BF16 = jnp.bfloat16

CHUNK = 64
N_META = 16
EPS = 1e-6
HEADS = 8
HEAD_DIM = 128
DIFF_DH = 64
MIX_HALF = HEADS * HEAD_DIM
BLK = 128
QT = 256
NEG = -1e30
LOG2E = 1.4426950408889634
HEADS_PER_STEP = 4

VMEM_LIMIT = 56 * 1024 * 1024

NT = (((1,), (1,)), ((), ()))
TN = (((0,), (0,)), ((), ()))


def _log_g(h):
    return float(np.log(1.0 - 2.0 ** (-5.0 - h)))


def _silu(x):
    return x * (1.0 / (1.0 + jnp.exp(-x)))


def _lambda(lq1_ref, lk1_ref, lq2_ref, lk2_ref, lam_init):
    return (jnp.exp(jnp.sum(lq1_ref[...] * lk1_ref[...], axis=-1, keepdims=True))
            - jnp.exp(jnp.sum(lq2_ref[...] * lk2_ref[...], axis=-1, keepdims=True))
            + lam_init)


def _pad_rows(a, rows):
    return jnp.concatenate([a, jnp.zeros((rows - a.shape[0], a.shape[1]), a.dtype)], axis=0)


def _split_components(q):
    lane = lax.broadcasted_iota(jnp.int32, q.shape, 1)
    zero = jnp.zeros((), BF16)
    qscaled = q * jnp.asarray(DIFF_DH ** -0.5, BF16)
    return jnp.concatenate([jnp.where(lane < DIFF_DH, qscaled, zero),
                            jnp.where(lane >= DIFF_DH, qscaled, zero)], axis=0)


def _rms_norm_bf16(x, g):
    ms = jnp.mean(x * x, axis=-1, keepdims=True)
    return (x * lax.rsqrt(ms + EPS) * g).astype(BF16)


def _inproj_kernel(h_ref, hm_ref, g_ref, w_ref, o_ref, pm_ref, u_ref):
    tm = h_ref.shape[0]

    @pl.when(pl.program_id(1) == 0)
    def _():
        u_ref[:tm] = _rms_norm_bf16(h_ref[...], g_ref[...])
        u_ref[tm:] = _rms_norm_bf16(hm_ref[...], g_ref[...])

    out = jnp.dot(u_ref[...], w_ref[...].astype(BF16),
                  preferred_element_type=F32).astype(BF16)
    o_ref[...] = out[:tm]
    pm_ref[...] = out[tm:]


def _inproj(h, hm, g, w, layer, tm, tn):
    m, d = h.shape
    n = w.shape[-1]
    if w.ndim == 3:
        w_spec = pl.BlockSpec((None, d, tn), lambda i, j: (layer, 0, j))
    else:
        w_spec = pl.BlockSpec((d, tn), lambda i, j: (0, j))
    return pl.pallas_call(
        _inproj_kernel,
        out_shape=[jax.ShapeDtypeStruct((m, n), BF16),
                   jax.ShapeDtypeStruct((m // tm, N_META, n), BF16)],
        grid=(m // tm, n // tn),
        in_specs=[
            pl.BlockSpec((tm, d), lambda i, j: (i, 0)),
            pl.BlockSpec((N_META, d), lambda i, j: (0, 0)),
            pl.BlockSpec((None, 1, d), lambda i, j: (layer, 0, 0)),
            w_spec,
        ],
        out_specs=[pl.BlockSpec((tm, tn), lambda i, j: (i, j)),
                   pl.BlockSpec((None, N_META, tn), lambda i, j: (i, 0, j))],
        scratch_shapes=[pltpu.VMEM((tm + N_META, d), BF16)],
        compiler_params=pltpu.CompilerParams(
            dimension_semantics=("parallel", "arbitrary"),
            vmem_limit_bytes=VMEM_LIMIT),
        name="inproj",
    )(h, hm, g, w)


def _meta_kernel(pm_ref, hm_ref, wr_ref, wd_ref, gret_ref, gdiff_ref,
                 lq1_ref, lk1_ref, lq2_ref, lk2_ref, hm_out_ref, *, lam_init):
    section = lambda s: pm_ref[:, s * MIX_HALF:(s + 1) * MIX_HALF]
    rq, rk, rv, rgate, dq, dk, dv, dgate = [section(s) for s in range(8)]
    t = lax.broadcasted_iota(jnp.int32, (BLK, BLK), 0)
    s_i = lax.broadcasted_iota(jnp.int32, (BLK, BLK), 1)
    dist = jnp.abs(t - s_i).astype(F32)
    r2 = lax.broadcasted_iota(jnp.int32, (2 * BLK, BLK), 0) % BLK
    c2 = lax.broadcasted_iota(jnp.int32, (2 * BLK, BLK), 1)
    dist2 = jnp.abs(r2 - c2).astype(F32)
    lam = _lambda(lq1_ref, lk1_ref, lq2_ref, lk2_ref, lam_init)
    r_parts, d_parts = [], []
    for h in range(HEADS):
        cols = slice(h * HEAD_DIM, (h + 1) * HEAD_DIM)
        q = _pad_rows(rq[:, cols], BLK)
        k = _pad_rows(rk[:, cols], BLK)
        v = _pad_rows(rv[:, cols], BLK)
        dmat = jnp.exp(_log_g(h) * dist) * (HEAD_DIM ** -0.5)
        s = lax.dot_general(q, k, NT, preferred_element_type=F32) * dmat
        o = jnp.dot(s.astype(BF16), v, preferred_element_type=F32)[:N_META]
        mu = jnp.mean(o, axis=-1, keepdims=True)
        oc = o - mu
        var = jnp.mean(oc * oc, axis=-1, keepdims=True)
        y = oc * lax.rsqrt(var + EPS) * gret_ref[:, cols]
        r_parts.append((y * _silu(rgate[:, cols].astype(F32))).astype(BF16))
        slope = 2.0 ** (-(h + 1.0))
        qs = _split_components(_pad_rows(dq[:, cols], BLK))
        k = _pad_rows(dk[:, cols], BLK)
        v = _pad_rows(dv[:, cols], BLK)
        bias = jnp.where(c2 < N_META, -slope * dist2, NEG)
        s = lax.dot_general(qs, k, NT, preferred_element_type=F32) + bias
        pr = jnp.exp(s - jnp.max(s, axis=-1, keepdims=True))
        a = (jnp.dot(pr.astype(BF16), v, preferred_element_type=F32)
             / jnp.sum(pr, axis=-1, keepdims=True))
        d = a[:N_META] - lam * a[BLK:BLK + N_META]
        y = d * lax.rsqrt(jnp.mean(d * d, axis=-1, keepdims=True) + EPS)
        y = y * gdiff_ref[:, cols] * (1.0 - lam_init)
        d_parts.append((y * _silu(dgate[:, cols].astype(F32))).astype(BF16))
    r = jnp.concatenate(r_parts, axis=1)
    d = jnp.concatenate(d_parts, axis=1)
    y = jnp.dot(r, wr_ref[...].astype(BF16), preferred_element_type=F32)
    y = y + jnp.dot(d, wd_ref[...].astype(BF16), preferred_element_type=F32)
    hm_out_ref[...] = hm_ref[...] + y


def _meta_layer(pm, hm, w_out, gret, gdiff, lams, layer, lam_init):
    dm = hm.shape[1]
    n = pm.shape[-1]
    vec = lambda width: pl.BlockSpec((None, 1, width), lambda i: (layer, 0, 0))
    return pl.pallas_call(
        functools.partial(_meta_kernel, lam_init=lam_init),
        out_shape=jax.ShapeDtypeStruct((N_META, dm), F32),
        grid=(1,),
        in_specs=[
            pl.BlockSpec((None, N_META, n), lambda i: (0, 0, 0)),
            pl.BlockSpec((N_META, dm), lambda i: (0, 0)),
            pl.BlockSpec((None, MIX_HALF, dm), lambda i: (layer, 0, 0),
                         pipeline_mode=pl.Buffered(1)),
            pl.BlockSpec((None, MIX_HALF, dm), lambda i: (layer, 1, 0),
                         pipeline_mode=pl.Buffered(1)),
            vec(MIX_HALF), vec(MIX_HALF),
            vec(DIFF_DH), vec(DIFF_DH), vec(DIFF_DH), vec(DIFF_DH),
        ],
        out_specs=pl.BlockSpec((N_META, dm), lambda i: (0, 0)),
        compiler_params=pltpu.CompilerParams(
            dimension_semantics=("arbitrary",),
            vmem_limit_bytes=VMEM_LIMIT),
        name="meta_layer",
    )(pm, hm, w_out, w_out, gret, gdiff, *lams)


def _diffattn_kernel(*refs, lam_init, ncast):
    (slopes_ref, qa_ref, qb_ref, k_ref, v_ref, km_ref, vm_ref, gatea_ref, gateb_ref, gn_ref,
     lq1_ref, lk1_ref, lq2_ref, lk2_ref) = refs[:14]
    cast_src = refs[14:14 + ncast]
    o_ref = refs[14 + ncast]
    cast_dst = refs[15 + ncast:15 + 2 * ncast]
    (qs_ref, s_ref, sm_ref, mrun_ref, mrow_ref, acc_ref,
     vext_ref, kmext_ref, vmext_ref, dbias_ref) = refs[15 + 2 * ncast:]

    for src, dst in zip(cast_src, cast_dst):
        dst[...] = src[...].astype(BF16)

    grp = pl.program_id(1)
    step = pl.program_id(2)
    hb = HEADS_PER_STEP
    seq = k_ref.shape[0]
    ntile = seq // QT
    head_cols = lambda hh: slice(hh * HEAD_DIM, (hh + 1) * HEAD_DIM)
    slope2 = lambda hh: slopes_ref[grp * hb + hh] * LOG2E
    tiles = ((step, 0, qa_ref, gatea_ref),
             (ntile - 1 - step, step + 1, qb_ref, gateb_ref))

    @pl.when(step == 0)
    def _():
        r = lax.broadcasted_iota(jnp.int32, (QT, QT), 0)
        c = lax.broadcasted_iota(jnp.int32, (QT, QT), 1)
        allowed = (c // CHUNK) <= (r // CHUNK)
        rel = (r - jnp.abs(r - c)).astype(F32)
        for hh in range(hb):
            cols = head_cols(hh)
            dbias_ref[hh] = jnp.where(allowed, slope2(hh) * rel, NEG)
            vext_ref[hh, :, :HEAD_DIM] = v_ref[:, cols]
            vext_ref[hh, :, HEAD_DIM:] = jnp.ones((seq, HEAD_DIM), BF16)
            kmext_ref[hh] = _pad_rows(km_ref[:, cols], BLK)
            vmext_ref[hh, :, :HEAD_DIM] = _pad_rows(vm_ref[:, cols], BLK)
            vmext_ref[hh, :, HEAD_DIM:] = jnp.ones((BLK, HEAD_DIM), BF16)

    def frame_scores(x, hh, j):
        start = pl.multiple_of(j * QT, QT)
        kt = k_ref[pl.ds(start, QT), head_cols(hh)]
        return lax.dot_general(qs_ref[x, hh], kt, NT, preferred_element_type=F32) * LOG2E

    def keep(x, hh, slot, s):
        s_ref[hh, slot] = s
        mrun_ref[x, hh] = jnp.maximum(mrun_ref[x, hh], jnp.maximum(s[:, :BLK], s[:, BLK:]))

    colm = lax.broadcasted_iota(jnp.int32, (1, BLK), 1)
    for x, (t, base, q_ref, _) in enumerate(tiles):
        relm = (colm - N_META - t * QT).astype(F32)
        for hh in range(hb):
            qs_ref[x, hh] = _split_components(q_ref[:, head_cols(hh)])
            s = lax.dot_general(qs_ref[x, hh], kmext_ref[hh], NT, preferred_element_type=F32)
            s = s * LOG2E + jnp.where(colm < N_META, slope2(hh) * relm, NEG)
            sm_ref[x, hh] = s
            mrun_ref[x, hh] = s
            s = frame_scores(x, hh, t)
            keep(x, hh, base + t,
                 jnp.concatenate([s[:QT] + dbias_ref[hh], s[QT:] + dbias_ref[hh]], axis=0))

    col = lax.broadcasted_iota(jnp.int32, (1, QT), 1)

    def off_diag_loops(x, t, base):
        def off_diag(j):
            rel = (col + (j - t) * QT).astype(F32)
            for hh in range(hb):
                keep(x, hh, base + j, frame_scores(x, hh, j) + slope2(hh) * rel)

        def off_diag_pair(jp, carry):
            off_diag(2 * jp)
            off_diag(2 * jp + 1)
            return carry

        lax.fori_loop(0, t // 2, off_diag_pair, 0)

        @pl.when(t % 2 == 1)
        def _():
            off_diag(t - 1)

    for x, (t, base, _, _) in enumerate(tiles):
        off_diag_loops(x, t, base)

    for x in range(2):
        for hh in range(hb):
            mrow_ref[x, hh] = jnp.broadcast_to(
                jnp.max(mrun_ref[x, hh], axis=-1, keepdims=True), (2 * QT, BLK))
            p = jnp.exp2(sm_ref[x, hh] - mrow_ref[x, hh])
            acc_ref[x, hh] = jnp.dot(p.astype(BF16), vmext_ref[hh], preferred_element_type=F32)

    def pv_loops(x, t, base):
        def pv(j, hh):
            start = pl.multiple_of(j * QT, QT)
            s = s_ref[hh, base + j]
            m = mrow_ref[x, hh]
            p = jnp.concatenate([jnp.exp2(s[:, :BLK] - m), jnp.exp2(s[:, BLK:] - m)], axis=1)
            return jnp.dot(p.astype(BF16), vext_ref[hh, pl.ds(start, QT), :],
                           preferred_element_type=F32)

        def pv_pair(jp, carry):
            for hh in range(hb):
                acc_ref[x, hh] = acc_ref[x, hh] + (pv(2 * jp, hh) + pv(2 * jp + 1, hh))
            return carry

        lax.fori_loop(0, (t + 1) // 2, pv_pair, 0)

        @pl.when(t % 2 == 0)
        def _():
            for hh in range(hb):
                acc_ref[x, hh] = acc_ref[x, hh] + pv(t, hh)

    for x, (t, base, _, _) in enumerate(tiles):
        pv_loops(x, t, base)

    lam = _lambda(lq1_ref, lk1_ref, lq2_ref, lk2_ref, lam_init)
    for x, (_, _, _, gate_ref) in enumerate(tiles):
        for hh in range(hb):
            cols = head_cols(hh)
            acc = acc_ref[x, hh]
            a = acc[:, :HEAD_DIM] / acc[:, HEAD_DIM:]
            d = a[:QT] - lam * a[QT:]
            y = d * lax.rsqrt(jnp.mean(d * d, axis=-1, keepdims=True) + EPS)
            y = y * gn_ref[:, cols] * (1.0 - lam_init)
            o_ref[x * QT:(x + 1) * QT, cols] = (
                y * _silu(gate_ref[:, cols].astype(F32))).astype(BF16)


def _diffattn(proj, pm, gn, lams, layer, batch, seq, lam_init, casts):
    m = proj.shape[0]
    ntile = seq // QT
    hb = HEADS_PER_STEP
    ngrp = HEADS // hb
    width = hb * HEAD_DIM
    slopes = jnp.asarray(2.0 ** (-8.0 * np.arange(1, HEADS + 1) / HEADS), F32)
    qcol, kcol, vcol, gcol = (4 * ngrp, 5 * ngrp, 6 * ngrp, 7 * ngrp)
    vec = pl.BlockSpec((None, 1, DIFF_DH), lambda b, g, t: (layer, 0, 0))
    tile_a = lambda b, t: b * ntile + t
    tile_b = lambda b, t: b * ntile + ntile - 1 - t
    nstep = batch * ngrp * (ntile // 2)
    linear = lambda b, g, t: (b * ngrp + g) * (ntile // 2) + t
    cast_in, cast_out_specs, cast_out_shapes, cast_args = [], [], [], []
    for w, wl in casts:
        rows, ncol = w.shape[1:]
        slab = rows // nstep
        cast_in.append(pl.BlockSpec((None, slab, ncol),
                                    lambda b, g, t, wl=wl: (wl, linear(b, g, t), 0)))
        cast_out_specs.append(pl.BlockSpec((slab, ncol), lambda b, g, t: (linear(b, g, t), 0)))
        cast_out_shapes.append(jax.ShapeDtypeStruct((rows, ncol), BF16))
        cast_args.append(w)
    outs = pl.pallas_call(
        functools.partial(_diffattn_kernel, lam_init=lam_init, ncast=len(casts)),
        out_shape=[jax.ShapeDtypeStruct((m, MIX_HALF), BF16)] + cast_out_shapes,
        grid=(batch, ngrp, ntile // 2),
        in_specs=[
            pl.BlockSpec(memory_space=pltpu.SMEM),
            pl.BlockSpec((QT, width), lambda b, g, t: (tile_a(b, t), qcol + g)),
            pl.BlockSpec((QT, width), lambda b, g, t: (tile_b(b, t), qcol + g)),
            pl.BlockSpec((seq, width), lambda b, g, t: (b, kcol + g)),
            pl.BlockSpec((seq, width), lambda b, g, t: (b, vcol + g)),
            pl.BlockSpec((None, N_META, width), lambda b, g, t: (0, 0, kcol + g)),
            pl.BlockSpec((None, N_META, width), lambda b, g, t: (0, 0, vcol + g)),
            pl.BlockSpec((QT, width), lambda b, g, t: (tile_a(b, t), gcol + g)),
            pl.BlockSpec((QT, width), lambda b, g, t: (tile_b(b, t), gcol + g)),
            pl.BlockSpec((None, 1, width), lambda b, g, t: (layer, 0, g)),
            vec, vec, vec, vec,
        ] + cast_in,
        out_specs=[pl.BlockSpec((2 * QT, width), lambda b, g, t: (b * (ntile // 2) + t, g))]
        + cast_out_specs,
        scratch_shapes=[
            pltpu.VMEM((2, hb, 2 * QT, HEAD_DIM), BF16),
            pltpu.VMEM((hb, ntile + 1, 2 * QT, QT), F32),
            pltpu.VMEM((2, hb, 2 * QT, BLK), F32),
            pltpu.VMEM((2, hb, 2 * QT, BLK), F32),
            pltpu.VMEM((2, hb, 2 * QT, BLK), F32),
            pltpu.VMEM((2, hb, 2 * QT, 2 * HEAD_DIM), F32),
            pltpu.VMEM((hb, seq, 2 * HEAD_DIM), BF16),
            pltpu.VMEM((hb, BLK, HEAD_DIM), BF16),
            pltpu.VMEM((hb, BLK, 2 * HEAD_DIM), BF16),
            pltpu.VMEM((hb, QT, QT), F32),
        ],
        compiler_params=pltpu.CompilerParams(
            dimension_semantics=("parallel", "parallel", "arbitrary"),
            vmem_limit_bytes=VMEM_LIMIT),
        name="diffattn",
    )(slopes, proj, proj, proj, proj, pm, pm, proj, proj, gn, *lams, *cast_args)
    return outs[0], outs[1:]


def _mix_outproj_kernel(q_ref, k_ref, v_ref, gate_ref, km_ref, vm_ref, gret_ref,
                        da_ref, db_ref, wr_ref, wd_ref, h_ref, gf_ref, o_ref,
                        r_ref, state_ref, init_ref, dmat_ref, rq_ref, rk_ref,
                        *, final_norm, tiles_per_seq):
    i = pl.program_id(0)
    scale = HEAD_DIM ** -0.5
    tm = q_ref.shape[0]

    @pl.when(i == 0)
    def _():
        r_ref[...] = jnp.zeros_like(r_ref)
        state_ref[...] = jnp.zeros_like(state_ref)
        t = lax.broadcasted_iota(jnp.int32, (BLK, BLK), 0)
        s = lax.broadcasted_iota(jnp.int32, (BLK, BLK), 1)
        dist = jnp.abs(t - s).astype(F32)
        allowed = (s // CHUNK) <= (t // CHUNK)
        tf = t.astype(F32)
        mf = lax.broadcasted_iota(jnp.int32, (N_META, HEAD_DIM), 0).astype(F32)
        for h in range(HEADS):
            cols = slice(h * HEAD_DIM, (h + 1) * HEAD_DIM)
            log_g = _log_g(h)
            dmat_ref[h] = jnp.where(allowed, jnp.exp(log_g * dist) * scale, 0.0)
            rq_ref[h] = jnp.exp(log_g * (tf + 1.0))
            rk_ref[h] = jnp.exp(log_g * (BLK - 1.0 - tf)) * scale
            zeta = jnp.exp(log_g * (N_META - 1.0 - mf)) * scale
            kz = (km_ref[:, cols].astype(F32) * zeta).astype(BF16)
            init_ref[h] = lax.dot_general(_pad_rows(kz, BLK), _pad_rows(vm_ref[:, cols], BLK),
                                          TN, preferred_element_type=F32)

    r = r_ref[...]
    for half, d_ref in enumerate((da_ref, db_ref)):
        rows = slice(half * QT, (half + 1) * QT)
        y = jnp.dot(r[rows], wr_ref[...], preferred_element_type=F32)
        y = y + jnp.dot(d_ref[...], wd_ref[...], preferred_element_type=F32)
        hn = h_ref[rows] + y
        if final_norm:
            ms = jnp.mean(hn * hn, axis=-1, keepdims=True)
            hn = hn * lax.rsqrt(ms + EPS) * gf_ref[...]
        o_ref[rows] = hn

    first = (i % tiles_per_seq) == 0
    for blk in range(tm // BLK):
        rows = slice(blk * BLK, (blk + 1) * BLK)
        for h in range(HEADS):
            cols = slice(h * HEAD_DIM, (h + 1) * HEAD_DIM)
            decay_blk = float((1.0 - 2.0 ** (-5.0 - h)) ** BLK)
            q = q_ref[rows, cols]
            k = k_ref[rows, cols]
            v = v_ref[rows, cols]
            s = lax.dot_general(q, k, NT, preferred_element_type=F32) * dmat_ref[h]
            intra = jnp.dot(s.astype(BF16), v, preferred_element_type=F32)
            state = state_ref[h]
            if blk == 0:
                state = jnp.where(first, init_ref[h], state)
            qx = (q.astype(F32) * rq_ref[h]).astype(BF16)
            cross = jnp.dot(qx, state.astype(BF16), preferred_element_type=F32)
            kz = (k.astype(F32) * rk_ref[h]).astype(BF16)
            kv = lax.dot_general(kz, v, TN, preferred_element_type=F32)
            state_ref[h] = decay_blk * state + kv

            o = intra + cross
            mu = jnp.mean(o, axis=-1, keepdims=True)
            oc = o - mu
            var = jnp.mean(oc * oc, axis=-1, keepdims=True)
            y = oc * lax.rsqrt(var + EPS) * gret_ref[:, cols]
            r_ref[rows, cols] = (y * _silu(gate_ref[rows, cols].astype(F32))).astype(BF16)


def _mix_outproj(proj, pm, d, w, h, gret, gf, layer, seq, final_norm):
    m, dm = h.shape
    tm = 2 * QT
    ntiles = m // tm
    ntile_q = seq // QT
    cur = lambda i: jnp.minimum(i, ntiles - 1)
    prev = lambda i: jnp.maximum(i - 1, 0)

    def d_block(which):
        def index(i):
            tile = 2 * prev(i) + which
            b, t = tile // ntile_q, tile % ntile_q
            pos = jnp.where(t < ntile_q // 2, 2 * t, 2 * (ntile_q - 1 - t) + 1)
            return b * ntile_q + pos, 0
        return index

    return pl.pallas_call(
        functools.partial(_mix_outproj_kernel, final_norm=final_norm,
                          tiles_per_seq=seq // tm),
        out_shape=jax.ShapeDtypeStruct((m, dm), F32),
        grid=(ntiles + 1,),
        in_specs=[
            pl.BlockSpec((tm, MIX_HALF), lambda i: (cur(i), 0)),
            pl.BlockSpec((tm, MIX_HALF), lambda i: (cur(i), 1)),
            pl.BlockSpec((tm, MIX_HALF), lambda i: (cur(i), 2)),
            pl.BlockSpec((tm, MIX_HALF), lambda i: (cur(i), 3)),
            pl.BlockSpec((None, N_META, MIX_HALF), lambda i: (0, 0, 1)),
            pl.BlockSpec((None, N_META, MIX_HALF), lambda i: (0, 0, 2)),
            pl.BlockSpec((None, 1, MIX_HALF), lambda i: (layer, 0, 0)),
            pl.BlockSpec((QT, MIX_HALF), d_block(0)),
            pl.BlockSpec((QT, MIX_HALF), d_block(1)),
            pl.BlockSpec((MIX_HALF, dm), lambda i: (0, 0), pipeline_mode=pl.Buffered(1)),
            pl.BlockSpec((MIX_HALF, dm), lambda i: (1, 0), pipeline_mode=pl.Buffered(1)),
            pl.BlockSpec((tm, dm), lambda i: (prev(i), 0)),
            pl.BlockSpec((1, dm), lambda i: (0, 0)),
        ],
        out_specs=pl.BlockSpec((tm, dm), lambda i: (prev(i), 0)),
        scratch_shapes=[
            pltpu.VMEM((tm, MIX_HALF), BF16),
            pltpu.VMEM((HEADS, HEAD_DIM, HEAD_DIM), F32),
            pltpu.VMEM((HEADS, HEAD_DIM, HEAD_DIM), F32),
            pltpu.VMEM((HEADS, BLK, BLK), F32),
            pltpu.VMEM((HEADS, BLK, HEAD_DIM), F32),
            pltpu.VMEM((HEADS, BLK, HEAD_DIM), F32),
        ],
        compiler_params=pltpu.CompilerParams(
            dimension_semantics=("arbitrary",),
            vmem_limit_bytes=VMEM_LIMIT),
        name="mix_outproj",
    )(proj, proj, proj, proj, pm, pm, gret, d, d, w, w, h, gf)


def kernel(x, meta_tokens, norm_g, w_in, w_out, ret_norm_g, diff_norm_g,
           lambda_q1, lambda_k1, lambda_q2, lambda_k2, final_norm_g):
    batch, seq, dm = x.shape
    depth = w_in.shape[0]
    m = batch * seq
    assert seq % QT == 0 and meta_tokens.shape[0] == N_META

    h = x.reshape(m, dm)
    hm = meta_tokens.astype(x.dtype)
    gf = final_norm_g.reshape(1, dm)
    per_layer = lambda a: a.reshape(depth, 1, a.shape[-1])
    norm_g, ret_norm_g, diff_norm_g = map(per_layer, (norm_g, ret_norm_g, diff_norm_g))
    lams = tuple(map(per_layer, (lambda_q1, lambda_k1, lambda_q2, lambda_k2)))

    w_in_cur = w_in
    for i in range(depth):
        last = i == depth - 1
        lam_init = 0.8 - 0.6 * float(np.exp(-0.3 * i))
        tn = 1024 if w_in_cur.dtype == F32 else 2048
        proj, pm = _inproj(h, hm, norm_g, w_in_cur, i, tm=m // 8, tn=tn)
        if not last:
            hm = _meta_layer(pm, hm, w_out, ret_norm_g, diff_norm_g, lams, i, lam_init)
        casts = [(w_out, i)] + ([] if last else [(w_in, i + 1)])
        d, cast = _diffattn(proj, pm, diff_norm_g, lams, i, batch, seq, lam_init, casts)
        if not last:
            w_in_cur = cast[1]
        h = _mix_outproj(proj, pm, d, cast[0], h, ret_norm_g, gf, i, seq, final_norm=last)

    return h.reshape(batch, seq, dm)
```

```python
import functools

import numpy as np
import jax
import jax.numpy as jnp
from jax import lax
from jax.experimental import pallas as pl
from jax.experimental.pallas import tpu as pltpu

F32 = jnp.float32
BF16 = jnp.bfloat16

CHUNK = 64
N_META = 16
EPS = 1e-6
HEADS = 8
HEAD_DIM = 128
DIFF_DH = 64
MIX_HALF = HEADS * HEAD_DIM
BLK = 128
QT = 256
NEG = -1e30
LOG2E = 1.4426950408889634
HEADS_PER_STEP = 4
TILE_UNROLL = 4

VMEM_LIMIT = 56 * 1024 * 1024

NT = (((1,), (1,)), ((), ()))
TN = (((0,), (0,)), ((), ()))


def _log_g(h):
    return float(np.log(1.0 - 2.0 ** (-5.0 - h)))


def _silu(x):
    return x * (1.0 / (1.0 + jnp.exp(-x)))


def _lambda(lq1_ref, lk1_ref, lq2_ref, lk2_ref, lam_init):
    return (jnp.exp(jnp.sum(lq1_ref[...] * lk1_ref[...], axis=-1, keepdims=True))
            - jnp.exp(jnp.sum(lq2_ref[...] * lk2_ref[...], axis=-1, keepdims=True))
            + lam_init)


def _pad_rows(a, rows):
    return jnp.concatenate([a, jnp.zeros((rows - a.shape[0], a.shape[1]), a.dtype)], axis=0)


def _split_components(q):
    lane = lax.broadcasted_iota(jnp.int32, q.shape, 1)
    zero = jnp.zeros((), BF16)
    qscaled = q * jnp.asarray(DIFF_DH ** -0.5, BF16)
    return jnp.concatenate([jnp.where(lane < DIFF_DH, qscaled, zero),
                            jnp.where(lane >= DIFF_DH, qscaled, zero)], axis=0)


def _rms_norm_bf16(x, g):
    ms = jnp.mean(x * x, axis=-1, keepdims=True)
    return (x * lax.rsqrt(ms + EPS) * g).astype(BF16)


def _inproj_kernel(h_ref, hm_ref, g_ref, w_ref, o_ref, pm_ref, u_ref):
    tm = h_ref.shape[0]

    @pl.when(pl.program_id(1) == 0)
    def _():
        u_ref[:tm] = _rms_norm_bf16(h_ref[...], g_ref[...])
        u_ref[tm:] = _rms_norm_bf16(hm_ref[...], g_ref[...])

    out = jnp.dot(u_ref[...], w_ref[...].astype(BF16),
                  preferred_element_type=F32).astype(BF16)
    o_ref[...] = out[:tm]
    pm_ref[...] = out[tm:]


def _inproj(h, hm, g, w, layer, tm, tn):
    m, d = h.shape
    n = w.shape[-1]
    if w.ndim == 3:
        w_spec = pl.BlockSpec((None, d, tn), lambda i, j: (layer, 0, j))
    else:
        w_spec = pl.BlockSpec((d, tn), lambda i, j: (0, j))
    return pl.pallas_call(
        _inproj_kernel,
        out_shape=[jax.ShapeDtypeStruct((m, n), BF16),
                   jax.ShapeDtypeStruct((m // tm, N_META, n), BF16)],
        grid=(m // tm, n // tn),
        in_specs=[
            pl.BlockSpec((tm, d), lambda i, j: (i, 0)),
            pl.BlockSpec((N_META, d), lambda i, j: (0, 0)),
            pl.BlockSpec((None, 1, d), lambda i, j: (layer, 0, 0)),
            w_spec,
        ],
        out_specs=[pl.BlockSpec((tm, tn), lambda i, j: (i, j)),
                   pl.BlockSpec((None, N_META, tn), lambda i, j: (i, 0, j))],
        scratch_shapes=[pltpu.VMEM((tm + N_META, d), BF16)],
        compiler_params=pltpu.CompilerParams(
            dimension_semantics=("parallel", "arbitrary"),
            vmem_limit_bytes=VMEM_LIMIT),
        name="inproj",
    )(h, hm, g, w)


def _meta_kernel(pm_ref, hm_ref, wr_ref, wd_ref, gret_ref, gdiff_ref,
                 lq1_ref, lk1_ref, lq2_ref, lk2_ref, hm_out_ref, *, lam_init):
    section = lambda s: pm_ref[:, s * MIX_HALF:(s + 1) * MIX_HALF]
    rq, rk, rv, rgate, dq, dk, dv, dgate = [section(s) for s in range(8)]
    t = lax.broadcasted_iota(jnp.int32, (BLK, BLK), 0)
    s_i = lax.broadcasted_iota(jnp.int32, (BLK, BLK), 1)
    dist = jnp.abs(t - s_i).astype(F32)
    r2 = lax.broadcasted_iota(jnp.int32, (2 * BLK, BLK), 0) % BLK
    c2 = lax.broadcasted_iota(jnp.int32, (2 * BLK, BLK), 1)
    dist2 = jnp.abs(r2 - c2).astype(F32)
    lam = _lambda(lq1_ref, lk1_ref, lq2_ref, lk2_ref, lam_init)
    r_parts, d_parts = [], []
    for h in range(HEADS):
        cols = slice(h * HEAD_DIM, (h + 1) * HEAD_DIM)
        q = _pad_rows(rq[:, cols], BLK)
        k = _pad_rows(rk[:, cols], BLK)
        v = _pad_rows(rv[:, cols], BLK)
        dmat = jnp.exp(_log_g(h) * dist) * (HEAD_DIM ** -0.5)
        s = lax.dot_general(q, k, NT, preferred_element_type=F32) * dmat
        o = jnp.dot(s.astype(BF16), v, preferred_element_type=F32)[:N_META]
        mu = jnp.mean(o, axis=-1, keepdims=True)
        oc = o - mu
        var = jnp.mean(oc * oc, axis=-1, keepdims=True)
        y = oc * lax.rsqrt(var + EPS) * gret_ref[:, cols]
        r_parts.append((y * _silu(rgate[:, cols].astype(F32))).astype(BF16))
        slope = 2.0 ** (-(h + 1.0))
        qs = _split_components(_pad_rows(dq[:, cols], BLK))
        k = _pad_rows(dk[:, cols], BLK)
        v = _pad_rows(dv[:, cols], BLK)
        bias = jnp.where(c2 < N_META, -slope * dist2, NEG)
        s = lax.dot_general(qs, k, NT, preferred_element_type=F32) + bias
        pr = jnp.exp(s - jnp.max(s, axis=-1, keepdims=True))
        a = (jnp.dot(pr.astype(BF16), v, preferred_element_type=F32)
             / jnp.sum(pr, axis=-1, keepdims=True))
        d = a[:N_META] - lam * a[BLK:BLK + N_META]
        y = d * lax.rsqrt(jnp.mean(d * d, axis=-1, keepdims=True) + EPS)
        y = y * gdiff_ref[:, cols] * (1.0 - lam_init)
        d_parts.append((y * _silu(dgate[:, cols].astype(F32))).astype(BF16))
    r = jnp.concatenate(r_parts, axis=1)
    d = jnp.concatenate(d_parts, axis=1)
    y = jnp.dot(r, wr_ref[...].astype(BF16), preferred_element_type=F32)
    y = y + jnp.dot(d, wd_ref[...].astype(BF16), preferred_element_type=F32)
    hm_out_ref[...] = hm_ref[...] + y


def _meta_layer(pm, hm, w_out, gret, gdiff, lams, layer, lam_init):
    dm = hm.shape[1]
    n = pm.shape[-1]
    vec = lambda width: pl.BlockSpec((None, 1, width), lambda i: (layer, 0, 0))
    return pl.pallas_call(
        functools.partial(_meta_kernel, lam_init=lam_init),
        out_shape=jax.ShapeDtypeStruct((N_META, dm), F32),
        grid=(1,),
        in_specs=[
            pl.BlockSpec((None, N_META, n), lambda i: (0, 0, 0)),
            pl.BlockSpec((N_META, dm), lambda i: (0, 0)),
            pl.BlockSpec((None, MIX_HALF, dm), lambda i: (layer, 0, 0),
                         pipeline_mode=pl.Buffered(1)),
            pl.BlockSpec((None, MIX_HALF, dm), lambda i: (layer, 1, 0),
                         pipeline_mode=pl.Buffered(1)),
            vec(MIX_HALF), vec(MIX_HALF),
            vec(DIFF_DH), vec(DIFF_DH), vec(DIFF_DH), vec(DIFF_DH),
        ],
        out_specs=pl.BlockSpec((N_META, dm), lambda i: (0, 0)),
        compiler_params=pltpu.CompilerParams(
            dimension_semantics=("arbitrary",),
            vmem_limit_bytes=VMEM_LIMIT),
        name="meta_layer",
    )(pm, hm, w_out, w_out, gret, gdiff, *lams)


def _diffattn_kernel(*refs, lam_init, ncast):
    (slopes_ref, qa_ref, qb_ref, k_ref, v_ref, km_ref, vm_ref, gatea_ref, gateb_ref, gn_ref,
     lq1_ref, lk1_ref, lq2_ref, lk2_ref) = refs[:14]
    cast_src = refs[14:14 + ncast]
    o_ref = refs[14 + ncast]
    cast_dst = refs[15 + ncast:15 + 2 * ncast]
    (qs_ref, s_ref, sm_ref, mrun_ref, mrow_ref, acc_ref,
     vext_ref, kmext_ref, vmext_ref, dbias_ref) = refs[15 + 2 * ncast:]

    for src, dst in zip(cast_src, cast_dst):
        dst[...] = src[...].astype(BF16)

    grp = pl.program_id(1)
    step = pl.program_id(2)
    hb = HEADS_PER_STEP
    seq = k_ref.shape[0]
    ntile = seq // QT
    head_cols = lambda hh: slice(hh * HEAD_DIM, (hh + 1) * HEAD_DIM)
    slope2 = lambda hh: slopes_ref[grp * hb + hh] * LOG2E
    tiles = ((step, 0, qa_ref, gatea_ref),
             (ntile - 1 - step, step + 1, qb_ref, gateb_ref))

    @pl.when(step == 0)
    def _():
        r = lax.broadcasted_iota(jnp.int32, (QT, QT), 0)
        c = lax.broadcasted_iota(jnp.int32, (QT, QT), 1)
        allowed = (c // CHUNK) <= (r // CHUNK)
        rel = (r - jnp.abs(r - c)).astype(F32)
        for hh in range(hb):
            cols = head_cols(hh)
            dbias_ref[hh] = jnp.where(allowed, slope2(hh) * rel, NEG)
            vext_ref[hh, :, :HEAD_DIM] = v_ref[:, cols]
            vext_ref[hh, :, HEAD_DIM:] = jnp.ones((seq, HEAD_DIM), BF16)
            kmext_ref[hh] = _pad_rows(km_ref[:, cols], BLK)
            vmext_ref[hh, :, :HEAD_DIM] = _pad_rows(vm_ref[:, cols], BLK)
            vmext_ref[hh, :, HEAD_DIM:] = jnp.ones((BLK, HEAD_DIM), BF16)

    def frame_scores(x, hh, j):
        start = pl.multiple_of(j * QT, QT)
        kt = k_ref[pl.ds(start, QT), head_cols(hh)]
        return lax.dot_general(qs_ref[x, hh], kt, NT, preferred_element_type=F32) * LOG2E

    def keep(x, hh, slot, s):
        s_ref[hh, slot] = s
        mrun_ref[x, hh] = jnp.maximum(mrun_ref[x, hh], jnp.maximum(s[:, :BLK], s[:, BLK:]))

    colm = lax.broadcasted_iota(jnp.int32, (1, BLK), 1)
    for x, (t, base, q_ref, _) in enumerate(tiles):
        relm = (colm - N_META - t * QT).astype(F32)
        for hh in range(hb):
            qs_ref[x, hh] = _split_components(q_ref[:, head_cols(hh)])
            s = lax.dot_general(qs_ref[x, hh], kmext_ref[hh], NT, preferred_element_type=F32)
            s = s * LOG2E + jnp.where(colm < N_META, slope2(hh) * relm, NEG)
            sm_ref[x, hh] = s
            mrun_ref[x, hh] = s
            s = frame_scores(x, hh, t)
            keep(x, hh, base + t,
                 jnp.concatenate([s[:QT] + dbias_ref[hh], s[QT:] + dbias_ref[hh]], axis=0))

    col = lax.broadcasted_iota(jnp.int32, (1, QT), 1)

    def for_tile_groups(n, group):
        def trip(i, carry):
            group([TILE_UNROLL * i + u for u in range(TILE_UNROLL)])
            return carry

        lax.fori_loop(0, n // TILE_UNROLL, trip, 0)
        done = (n // TILE_UNROLL) * TILE_UNROLL
        size = TILE_UNROLL // 2
        while size >= 1:
            @pl.when((n & size) != 0)
            def _(done=done, size=size):
                group([done + u for u in range(size)])
            done = done + (n & size)
            size //= 2

    def off_diag_loops(x, t, base):
        def off_diag(js):
            for j in js:
                rel = (col + (j - t) * QT).astype(F32)
                for hh in range(hb):
                    keep(x, hh, base + j, frame_scores(x, hh, j) + slope2(hh) * rel)

        for_tile_groups(t, off_diag)

    for x, (t, base, _, _) in enumerate(tiles):
        off_diag_loops(x, t, base)

    for x in range(2):
        for hh in range(hb):
            mrow_ref[x, hh] = jnp.broadcast_to(
                jnp.max(mrun_ref[x, hh], axis=-1, keepdims=True), (2 * QT, BLK))
            p = jnp.exp2(sm_ref[x, hh] - mrow_ref[x, hh])
            acc_ref[x, hh] = jnp.dot(p.astype(BF16), vmext_ref[hh], preferred_element_type=F32)

    def pv_loops(x, t, base):
        def pv(j, hh):
            start = pl.multiple_of(j * QT, QT)
            s = s_ref[hh, base + j]
            m = mrow_ref[x, hh]
            p = jnp.concatenate([jnp.exp2(s[:, :BLK] - m), jnp.exp2(s[:, BLK:] - m)], axis=1)
            return jnp.dot(p.astype(BF16), vext_ref[hh, pl.ds(start, QT), :],
                           preferred_element_type=F32)

        def pv_group(js):
            for hh in range(hb):
                acc_ref[x, hh] = acc_ref[x, hh] + functools.reduce(
                    lambda a, b: a + b, [pv(j, hh) for j in js])

        for_tile_groups(t + 1, pv_group)

    for x, (t, base, _, _) in enumerate(tiles):
        pv_loops(x, t, base)

    lam = _lambda(lq1_ref, lk1_ref, lq2_ref, lk2_ref, lam_init)
    for x, (_, _, _, gate_ref) in enumerate(tiles):
        for hh in range(hb):
            cols = head_cols(hh)
            acc = acc_ref[x, hh]
            a = acc[:, :HEAD_DIM] / acc[:, HEAD_DIM:]
            d = a[:QT] - lam * a[QT:]
            y = d * lax.rsqrt(jnp.mean(d * d, axis=-1, keepdims=True) + EPS)
            y = y * gn_ref[:, cols] * (1.0 - lam_init)
            o_ref[x * QT:(x + 1) * QT, cols] = (
                y * _silu(gate_ref[:, cols].astype(F32))).astype(BF16)


def _diffattn(proj, pm, gn, lams, layer, batch, seq, lam_init, casts):
    m = proj.shape[0]
    ntile = seq // QT
    hb = HEADS_PER_STEP
    ngrp = HEADS // hb
    width = hb * HEAD_DIM
    slopes = jnp.asarray(2.0 ** (-8.0 * np.arange(1, HEADS + 1) / HEADS), F32)
    qcol, kcol, vcol, gcol = (4 * ngrp, 5 * ngrp, 6 * ngrp, 7 * ngrp)
    vec = pl.BlockSpec((None, 1, DIFF_DH), lambda b, g, t: (layer, 0, 0))
    tile_a = lambda b, t: b * ntile + t
    tile_b = lambda b, t: b * ntile + ntile - 1 - t
    nstep = batch * ngrp * (ntile // 2)
    linear = lambda b, g, t: (b * ngrp + g) * (ntile // 2) + t
    cast_in, cast_out_specs, cast_out_shapes, cast_args = [], [], [], []
    for w, wl in casts:
        rows, ncol = w.shape[1:]
        slab = rows // nstep
        cast_in.append(pl.BlockSpec((None, slab, ncol),
                                    lambda b, g, t, wl=wl: (wl, linear(b, g, t), 0)))
        cast_out_specs.append(pl.BlockSpec((slab, ncol), lambda b, g, t: (linear(b, g, t), 0)))
        cast_out_shapes.append(jax.ShapeDtypeStruct((rows, ncol), BF16))
        cast_args.append(w)
    outs = pl.pallas_call(
        functools.partial(_diffattn_kernel, lam_init=lam_init, ncast=len(casts)),
        out_shape=[jax.ShapeDtypeStruct((m, MIX_HALF), BF16)] + cast_out_shapes,
        grid=(batch, ngrp, ntile // 2),
        in_specs=[
            pl.BlockSpec(memory_space=pltpu.SMEM),
            pl.BlockSpec((QT, width), lambda b, g, t: (tile_a(b, t), qcol + g)),
            pl.BlockSpec((QT, width), lambda b, g, t: (tile_b(b, t), qcol + g)),
            pl.BlockSpec((seq, width), lambda b, g, t: (b, kcol + g)),
            pl.BlockSpec((seq, width), lambda b, g, t: (b, vcol + g)),
            pl.BlockSpec((None, N_META, width), lambda b, g, t: (0, 0, kcol + g)),
            pl.BlockSpec((None, N_META, width), lambda b, g, t: (0, 0, vcol + g)),
            pl.BlockSpec((QT, width), lambda b, g, t: (tile_a(b, t), gcol + g)),
            pl.BlockSpec((QT, width), lambda b, g, t: (tile_b(b, t), gcol + g)),
            pl.BlockSpec((None, 1, width), lambda b, g, t: (layer, 0, g)),
            vec, vec, vec, vec,
        ] + cast_in,
        out_specs=[pl.BlockSpec((2 * QT, width), lambda b, g, t: (b * (ntile // 2) + t, g))]
        + cast_out_specs,
        scratch_shapes=[
            pltpu.VMEM((2, hb, 2 * QT, HEAD_DIM), BF16),
            pltpu.VMEM((hb, ntile + 1, 2 * QT, QT), F32),
            pltpu.VMEM((2, hb, 2 * QT, BLK), F32),
            pltpu.VMEM((2, hb, 2 * QT, BLK), F32),
            pltpu.VMEM((2, hb, 2 * QT, BLK), F32),
            pltpu.VMEM((2, hb, 2 * QT, 2 * HEAD_DIM), F32),
            pltpu.VMEM((hb, seq, 2 * HEAD_DIM), BF16),
            pltpu.VMEM((hb, BLK, HEAD_DIM), BF16),
            pltpu.VMEM((hb, BLK, 2 * HEAD_DIM), BF16),
            pltpu.VMEM((hb, QT, QT), F32),
        ],
        compiler_params=pltpu.CompilerParams(
            dimension_semantics=("parallel", "parallel", "arbitrary"),
            vmem_limit_bytes=VMEM_LIMIT),
        name="diffattn",
    )(slopes, proj, proj, proj, proj, pm, pm, proj, proj, gn, *lams, *cast_args)
    return outs[0], outs[1:]


def _mix_outproj_kernel(q_ref, k_ref, v_ref, gate_ref, km_ref, vm_ref, gret_ref,
                        da_ref, db_ref, wr_ref, wd_ref, h_ref, gf_ref, o_ref,
                        r_ref, state_ref, init_ref, dmat_ref, rq_ref, rk_ref,
                        *, final_norm, tiles_per_seq):
    i = pl.program_id(0)
    scale = HEAD_DIM ** -0.5
    tm = q_ref.shape[0]

    @pl.when(i == 0)
    def _():
        r_ref[...] = jnp.zeros_like(r_ref)
        state_ref[...] = jnp.zeros_like(state_ref)
        t = lax.broadcasted_iota(jnp.int32, (BLK, BLK), 0)
        s = lax.broadcasted_iota(jnp.int32, (BLK, BLK), 1)
        dist = jnp.abs(t - s).astype(F32)
        allowed = (s // CHUNK) <= (t // CHUNK)
        tf = t.astype(F32)
        mf = lax.broadcasted_iota(jnp.int32, (N_META, HEAD_DIM), 0).astype(F32)
        for h in range(HEADS):
            cols = slice(h * HEAD_DIM, (h + 1) * HEAD_DIM)
            log_g = _log_g(h)
            dmat_ref[h] = jnp.where(allowed, jnp.exp(log_g * dist) * scale, 0.0)
            rq_ref[h] = jnp.exp(log_g * (tf + 1.0))
            rk_ref[h] = jnp.exp(log_g * (BLK - 1.0 - tf)) * scale
            zeta = jnp.exp(log_g * (N_META - 1.0 - mf)) * scale
            kz = (km_ref[:, cols].astype(F32) * zeta).astype(BF16)
            init_ref[h] = lax.dot_general(_pad_rows(kz, BLK), _pad_rows(vm_ref[:, cols], BLK),
                                          TN, preferred_element_type=F32)

    r = r_ref[...]
    for half, d_ref in enumerate((da_ref, db_ref)):
        rows = slice(half * QT, (half + 1) * QT)
        y = jnp.dot(r[rows], wr_ref[...], preferred_element_type=F32)
        y = y + jnp.dot(d_ref[...], wd_ref[...], preferred_element_type=F32)
        hn = h_ref[rows] + y
        if final_norm:
            ms = jnp.mean(hn * hn, axis=-1, keepdims=True)
            hn = hn * lax.rsqrt(ms + EPS) * gf_ref[...]
        o_ref[rows] = hn

    first = (i % tiles_per_seq) == 0
    for blk in range(tm // BLK):
        rows = slice(blk * BLK, (blk + 1) * BLK)
        for h in range(HEADS):
            cols = slice(h * HEAD_DIM, (h + 1) * HEAD_DIM)
            decay_blk = float((1.0 - 2.0 ** (-5.0 - h)) ** BLK)
            q = q_ref[rows, cols]
            k = k_ref[rows, cols]
            v = v_ref[rows, cols]
            s = lax.dot_general(q, k, NT, preferred_element_type=F32) * dmat_ref[h]
            intra = jnp.dot(s.astype(BF16), v, preferred_element_type=F32)
            state = state_ref[h]
            if blk == 0:
                state = jnp.where(first, init_ref[h], state)
            qx = (q.astype(F32) * rq_ref[h]).astype(BF16)
            cross = jnp.dot(qx, state.astype(BF16), preferred_element_type=F32)
            kz = (k.astype(F32) * rk_ref[h]).astype(BF16)
            kv = lax.dot_general(kz, v, TN, preferred_element_type=F32)
            state_ref[h] = decay_blk * state + kv

            o = intra + cross
            mu = jnp.mean(o, axis=-1, keepdims=True)
            oc = o - mu
            var = jnp.mean(oc * oc, axis=-1, keepdims=True)
            y = oc * lax.rsqrt(var + EPS) * gret_ref[:, cols]
            r_ref[rows, cols] = (y * _silu(gate_ref[rows, cols].astype(F32))).astype(BF16)


def _mix_outproj(proj, pm, d, w, h, gret, gf, layer, seq, final_norm):
    m, dm = h.shape
    tm = 2 * QT
    ntiles = m // tm
    ntile_q = seq // QT
    cur = lambda i: jnp.minimum(i, ntiles - 1)
    prev = lambda i: jnp.maximum(i - 1, 0)

    def d_block(which):
        def index(i):
            tile = 2 * prev(i) + which
            b, t = tile // ntile_q, tile % ntile_q
            pos = jnp.where(t < ntile_q // 2, 2 * t, 2 * (ntile_q - 1 - t) + 1)
            return b * ntile_q + pos, 0
        return index

    return pl.pallas_call(
        functools.partial(_mix_outproj_kernel, final_norm=final_norm,
                          tiles_per_seq=seq // tm),
        out_shape=jax.ShapeDtypeStruct((m, dm), F32),
        grid=(ntiles + 1,),
        in_specs=[
            pl.BlockSpec((tm, MIX_HALF), lambda i: (cur(i), 0)),
            pl.BlockSpec((tm, MIX_HALF), lambda i: (cur(i), 1)),
            pl.BlockSpec((tm, MIX_HALF), lambda i: (cur(i), 2)),
            pl.BlockSpec((tm, MIX_HALF), lambda i: (cur(i), 3)),
            pl.BlockSpec((None, N_META, MIX_HALF), lambda i: (0, 0, 1)),
            pl.BlockSpec((None, N_META, MIX_HALF), lambda i: (0, 0, 2)),
            pl.BlockSpec((None, 1, MIX_HALF), lambda i: (layer, 0, 0)),
            pl.BlockSpec((QT, MIX_HALF), d_block(0)),
            pl.BlockSpec((QT, MIX_HALF), d_block(1)),
            pl.BlockSpec((MIX_HALF, dm), lambda i: (0, 0), pipeline_mode=pl.Buffered(1)),
            pl.BlockSpec((MIX_HALF, dm), lambda i: (1, 0), pipeline_mode=pl.Buffered(1)),
            pl.BlockSpec((tm, dm), lambda i: (prev(i), 0)),
            pl.BlockSpec((1, dm), lambda i: (0, 0)),
        ],
        out_specs=pl.BlockSpec((tm, dm), lambda i: (prev(i), 0)),
        scratch_shapes=[
            pltpu.VMEM((tm, MIX_HALF), BF16),
            pltpu.VMEM((HEADS, HEAD_DIM, HEAD_DIM), F32),
            pltpu.VMEM((HEADS, HEAD_DIM, HEAD_DIM), F32),
            pltpu.VMEM((HEADS, BLK, BLK), F32),
            pltpu.VMEM((HEADS, BLK, HEAD_DIM), F32),
            pltpu.VMEM((HEADS, BLK, HEAD_DIM), F32),
        ],
        compiler_params=pltpu.CompilerParams(
            dimension_semantics=("arbitrary",),
            vmem_limit_bytes=VMEM_LIMIT),
        name="mix_outproj",
    )(proj, proj, proj, proj, pm, pm, gret, d, d, w, w, h, gf)


def kernel(x, meta_tokens, norm_g, w_in, w_out, ret_norm_g, diff_norm_g,
           lambda_q1, lambda_k1, lambda_q2, lambda_k2, final_norm_g):
    batch, seq, dm = x.shape
    depth = w_in.shape[0]
    m = batch * seq
    assert seq % QT == 0 and meta_tokens.shape[0] == N_META

    h = x.reshape(m, dm)
    hm = meta_tokens.astype(x.dtype)
    gf = final_norm_g.reshape(1, dm)
    per_layer = lambda a: a.reshape(depth, 1, a.shape[-1])
    norm_g, ret_norm_g, diff_norm_g = map(per_layer, (norm_g, ret_norm_g, diff_norm_g))
    lams = tuple(map(per_layer, (lambda_q1, lambda_k1, lambda_q2, lambda_k2)))

    w_in_cur = w_in
    for i in range(depth):
        last = i == depth - 1
        lam_init = 0.8 - 0.6 * float(np.exp(-0.3 * i))
        tn = 1024 if w_in_cur.dtype == F32 else 2048
        proj, pm = _inproj(h, hm, norm_g, w_in_cur, i, tm=m // 8, tn=tn)
        if not last:
            hm = _meta_layer(pm, hm, w_out, ret_norm_g, diff_norm_g, lams, i, lam_init)
        casts = [(w_out, i)] + ([] if last else [(w_in, i + 1)])
        d, cast = _diffattn(proj, pm, diff_norm_g, lams, i, batch, seq, lam_init, casts)
        if not last:
            w_in_cur = cast[1]
        h = _mix_outproj(proj, pm, d, cast[0], h, ret_norm_g, gf, i, seq, final_norm=last)

    return h.reshape(batch, seq, dm)
```

```python
import functools

import numpy as np
import jax
import jax.numpy as jnp
from jax import lax
from jax.experimental import pallas as pl
from jax.experimental.pallas import tpu as pltpu

F32 = jnp.float32
BF16 = jnp.bfloat16

CHUNK = 64
N_META = 16
EPS = 1e-6
HEADS = 8
HEAD_DIM = 128
DIFF_DH = 64
MIX_HALF = HEADS * HEAD_DIM
BLK = 128
QT = 256
NEG = -1e30
LOG2E = 1.4426950408889634
HEADS_PER_STEP = 4
TILE_UNROLL = 4

VMEM_LIMIT = 56 * 1024 * 1024

NT = (((1,), (1,)), ((), ()))
TN = (((0,), (0,)), ((), ()))


def _log_g(h):
    return float(np.log(1.0 - 2.0 ** (-5.0 - h)))


def _silu(x):
    return x * (1.0 / (1.0 + jnp.exp(-x)))


def _lambda(lq1_ref, lk1_ref, lq2_ref, lk2_ref, lam_init):
    return (jnp.exp(jnp.sum(lq1_ref[...] * lk1_ref[...], axis=-1, keepdims=True))
            - jnp.exp(jnp.sum(lq2_ref[...] * lk2_ref[...], axis=-1, keepdims=True))
            + lam_init)


def _pad_rows(a, rows):
    return jnp.concatenate([a, jnp.zeros((rows - a.shape[0], a.shape[1]), a.dtype)], axis=0)


def _split_components(q):
    lane = lax.broadcasted_iota(jnp.int32, q.shape, 1)
    zero = jnp.zeros((), BF16)
    qscaled = q * jnp.asarray(DIFF_DH ** -0.5, BF16)
    return jnp.concatenate([jnp.where(lane < DIFF_DH, qscaled, zero),
                            jnp.where(lane >= DIFF_DH, qscaled, zero)], axis=0)


def _rms_norm_bf16(x, g):
    ms = jnp.mean(x * x, axis=-1, keepdims=True)
    return (x * lax.rsqrt(ms + EPS) * g).astype(BF16)


def _inproj_kernel(h_ref, hm_ref, g_ref, w_ref, o_ref, pm_ref, u_ref):
    tm = h_ref.shape[0]

    @pl.when(pl.program_id(1) == 0)
    def _():
        u_ref[:tm] = _rms_norm_bf16(h_ref[...], g_ref[...])
        u_ref[tm:] = _rms_norm_bf16(hm_ref[...], g_ref[...])

    out = jnp.dot(u_ref[...], w_ref[...].astype(BF16),
                  preferred_element_type=F32).astype(BF16)
    o_ref[...] = out[:tm]
    pm_ref[...] = out[tm:]


def _inproj(h, hm, g, w, layer, tm, tn):
    m, d = h.shape
    n = w.shape[-1]
    if w.ndim == 3:
        w_spec = pl.BlockSpec((None, d, tn), lambda i, j: (layer, 0, j))
    else:
        w_spec = pl.BlockSpec((d, tn), lambda i, j: (0, j))
    return pl.pallas_call(
        _inproj_kernel,
        out_shape=[jax.ShapeDtypeStruct((m, n), BF16),
                   jax.ShapeDtypeStruct((m // tm, N_META, n), BF16)],
        grid=(m // tm, n // tn),
        in_specs=[
            pl.BlockSpec((tm, d), lambda i, j: (i, 0)),
            pl.BlockSpec((N_META, d), lambda i, j: (0, 0)),
            pl.BlockSpec((None, 1, d), lambda i, j: (layer, 0, 0)),
            w_spec,
        ],
        out_specs=[pl.BlockSpec((tm, tn), lambda i, j: (i, j)),
                   pl.BlockSpec((None, N_META, tn), lambda i, j: (i, 0, j))],
        scratch_shapes=[pltpu.VMEM((tm + N_META, d), BF16)],
        compiler_params=pltpu.CompilerParams(
            dimension_semantics=("parallel", "arbitrary"),
            vmem_limit_bytes=VMEM_LIMIT),
        name="inproj",
    )(h, hm, g, w)


def _meta_kernel(pm_ref, hm_ref, wr_ref, wd_ref, gret_ref, gdiff_ref,
                 lq1_ref, lk1_ref, lq2_ref, lk2_ref, hm_out_ref, *, lam_init):
    section = lambda s: pm_ref[:, s * MIX_HALF:(s + 1) * MIX_HALF]
    rq, rk, rv, rgate, dq, dk, dv, dgate = [section(s) for s in range(8)]
    t = lax.broadcasted_iota(jnp.int32, (BLK, BLK), 0)
    s_i = lax.broadcasted_iota(jnp.int32, (BLK, BLK), 1)
    dist = jnp.abs(t - s_i).astype(F32)
    r2 = lax.broadcasted_iota(jnp.int32, (2 * BLK, BLK), 0) % BLK
    c2 = lax.broadcasted_iota(jnp.int32, (2 * BLK, BLK), 1)
    dist2 = jnp.abs(r2 - c2).astype(F32)
    lam = _lambda(lq1_ref, lk1_ref, lq2_ref, lk2_ref, lam_init)
    r_parts, d_parts = [], []
    for h in range(HEADS):
        cols = slice(h * HEAD_DIM, (h + 1) * HEAD_DIM)
        q = _pad_rows(rq[:, cols], BLK)
        k = _pad_rows(rk[:, cols], BLK)
        v = _pad_rows(rv[:, cols], BLK)
        dmat = jnp.exp(_log_g(h) * dist) * (HEAD_DIM ** -0.5)
        s = lax.dot_general(q, k, NT, preferred_element_type=F32) * dmat
        o = jnp.dot(s.astype(BF16), v, preferred_element_type=F32)[:N_META]
        mu = jnp.mean(o, axis=-1, keepdims=True)
        oc = o - mu
        var = jnp.mean(oc * oc, axis=-1, keepdims=True)
        y = oc * lax.rsqrt(var + EPS) * gret_ref[:, cols]
        r_parts.append((y * _silu(rgate[:, cols].astype(F32))).astype(BF16))
        slope = 2.0 ** (-(h + 1.0))
        qs = _split_components(_pad_rows(dq[:, cols], BLK))
        k = _pad_rows(dk[:, cols], BLK)
        v = _pad_rows(dv[:, cols], BLK)
        bias = jnp.where(c2 < N_META, -slope * dist2, NEG)
        s = lax.dot_general(qs, k, NT, preferred_element_type=F32) + bias
        pr = jnp.exp(s - jnp.max(s, axis=-1, keepdims=True))
        a = (jnp.dot(pr.astype(BF16), v, preferred_element_type=F32)
             / jnp.sum(pr, axis=-1, keepdims=True))
        d = a[:N_META] - lam * a[BLK:BLK + N_META]
        y = d * lax.rsqrt(jnp.mean(d * d, axis=-1, keepdims=True) + EPS)
        y = y * gdiff_ref[:, cols] * (1.0 - lam_init)
        d_parts.append((y * _silu(dgate[:, cols].astype(F32))).astype(BF16))
    r = jnp.concatenate(r_parts, axis=1)
    d = jnp.concatenate(d_parts, axis=1)
    y = jnp.dot(r, wr_ref[...].astype(BF16), preferred_element_type=F32)
    y = y + jnp.dot(d, wd_ref[...].astype(BF16), preferred_element_type=F32)
    hm_out_ref[...] = hm_ref[...] + y


def _meta_layer(pm, hm, w_out, gret, gdiff, lams, layer, lam_init):
    dm = hm.shape[1]
    n = pm.shape[-1]
    vec = lambda width: pl.BlockSpec((None, 1, width), lambda i: (layer, 0, 0))
    return pl.pallas_call(
        functools.partial(_meta_kernel, lam_init=lam_init),
        out_shape=jax.ShapeDtypeStruct((N_META, dm), F32),
        grid=(1,),
        in_specs=[
            pl.BlockSpec((None, N_META, n), lambda i: (0, 0, 0)),
            pl.BlockSpec((N_META, dm), lambda i: (0, 0)),
            pl.BlockSpec((None, MIX_HALF, dm), lambda i: (layer, 0, 0),
                         pipeline_mode=pl.Buffered(1)),
            pl.BlockSpec((None, MIX_HALF, dm), lambda i: (layer, 1, 0),
                         pipeline_mode=pl.Buffered(1)),
            vec(MIX_HALF), vec(MIX_HALF),
            vec(DIFF_DH), vec(DIFF_DH), vec(DIFF_DH), vec(DIFF_DH),
        ],
        out_specs=pl.BlockSpec((N_META, dm), lambda i: (0, 0)),
        compiler_params=pltpu.CompilerParams(
            dimension_semantics=("arbitrary",),
            vmem_limit_bytes=VMEM_LIMIT),
        name="meta_layer",
    )(pm, hm, w_out, w_out, gret, gdiff, *lams)


def _diffattn_kernel(*refs, lam_init, ncast):
    (slopes_ref, qa_ref, qb_ref, k_ref, v_ref, km_ref, vm_ref, gatea_ref, gateb_ref, gn_ref,
     lq1_ref, lk1_ref, lq2_ref, lk2_ref) = refs[:14]
    cast_src = refs[14:14 + ncast]
    o_ref = refs[14 + ncast]
    cast_dst = refs[15 + ncast:15 + 2 * ncast]
    (qs_ref, s_ref, sm_ref, mrun_ref, mrow_ref, acc_ref,
     vext_ref, kmext_ref, vmext_ref, dbias_ref) = refs[15 + 2 * ncast:]

    for src, dst in zip(cast_src, cast_dst):
        dst[...] = src[...].astype(BF16)

    grp = pl.program_id(1)
    step = pl.program_id(2)
    hb = HEADS_PER_STEP
    seq = k_ref.shape[0]
    ntile = seq // QT
    head_cols = lambda hh: slice(hh * HEAD_DIM, (hh + 1) * HEAD_DIM)
    slope2 = lambda hh: slopes_ref[grp * hb + hh] * LOG2E
    tiles = ((step, 0, qa_ref, gatea_ref),
             (ntile - 1 - step, step + 1, qb_ref, gateb_ref))

    @pl.when(step == 0)
    def _():
        r = lax.broadcasted_iota(jnp.int32, (QT, QT), 0)
        c = lax.broadcasted_iota(jnp.int32, (QT, QT), 1)
        allowed = (c // CHUNK) <= (r // CHUNK)
        rel = (r - jnp.abs(r - c)).astype(F32)
        for hh in range(hb):
            cols = head_cols(hh)
            dbias_ref[hh] = jnp.where(allowed, slope2(hh) * rel, NEG)
            vext_ref[hh, :, :HEAD_DIM] = v_ref[:, cols]
            vext_ref[hh, :, HEAD_DIM:] = jnp.ones((seq, HEAD_DIM), BF16)
            kmext_ref[hh] = _pad_rows(km_ref[:, cols], BLK)
            vmext_ref[hh, :, :HEAD_DIM] = _pad_rows(vm_ref[:, cols], BLK)
            vmext_ref[hh, :, HEAD_DIM:] = jnp.ones((BLK, HEAD_DIM), BF16)

    def frame_scores(x, hh, j):
        start = pl.multiple_of(j * QT, QT)
        kt = k_ref[pl.ds(start, QT), head_cols(hh)]
        return lax.dot_general(qs_ref[x, hh], kt, NT, preferred_element_type=F32) * LOG2E

    def keep(x, hh, slot, s):
        s_ref[hh, slot] = s
        mrun_ref[x, hh] = jnp.maximum(mrun_ref[x, hh], jnp.maximum(s[:, :BLK], s[:, BLK:]))

    colm = lax.broadcasted_iota(jnp.int32, (1, BLK), 1)
    for x, (t, base, q_ref, _) in enumerate(tiles):
        relm = (colm - N_META - t * QT).astype(F32)
        for hh in range(hb):
            qs_ref[x, hh] = _split_components(q_ref[:, head_cols(hh)])
            s = lax.dot_general(qs_ref[x, hh], kmext_ref[hh], NT, preferred_element_type=F32)
            s = s * LOG2E + jnp.where(colm < N_META, slope2(hh) * relm, NEG)
            sm_ref[x, hh] = s
            mrun_ref[x, hh] = s
            s = frame_scores(x, hh, t)
            keep(x, hh, base + t,
                 jnp.concatenate([s[:QT] + dbias_ref[hh], s[QT:] + dbias_ref[hh]], axis=0))

    col = lax.broadcasted_iota(jnp.int32, (1, QT), 1)

    def for_tile_groups(n, group):
        def trip(i, carry):
            group([TILE_UNROLL * i + u for u in range(TILE_UNROLL)])
            return carry

        lax.fori_loop(0, n // TILE_UNROLL, trip, 0)
        done = (n // TILE_UNROLL) * TILE_UNROLL
        size = TILE_UNROLL // 2
        while size >= 1:
            @pl.when((n & size) != 0)
            def _(done=done, size=size):
                group([done + u for u in range(size)])
            done = done + (n & size)
            size //= 2

    def off_diag_loops(x, t, base):
        def off_diag(js):
            for j in js:
                rel = (col + (j - t) * QT).astype(F32)
                for hh in range(hb):
                    keep(x, hh, base + j, frame_scores(x, hh, j) + slope2(hh) * rel)

        for_tile_groups(t, off_diag)

    for x, (t, base, _, _) in enumerate(tiles):
        off_diag_loops(x, t, base)

    for x in range(2):
        for hh in range(hb):
            mrow_ref[x, hh] = jnp.broadcast_to(
                jnp.max(mrun_ref[x, hh], axis=-1, keepdims=True), (2 * QT, BLK))
            p = jnp.exp2(sm_ref[x, hh] - mrow_ref[x, hh])
            acc_ref[x, hh] = jnp.dot(p.astype(BF16), vmext_ref[hh], preferred_element_type=F32)

    def pv_loops(x, t, base):
        def pv(j, hh):
            start = pl.multiple_of(j * QT, QT)
            s = s_ref[hh, base + j]
            m = mrow_ref[x, hh]
            p = jnp.concatenate([jnp.exp2(s[:, :BLK] - m), jnp.exp2(s[:, BLK:] - m)], axis=1)
            return jnp.dot(p.astype(BF16), vext_ref[hh, pl.ds(start, QT), :],
                           preferred_element_type=F32)

        def pv_group(js):
            for hh in range(hb):
                acc_ref[x, hh] = acc_ref[x, hh] + functools.reduce(
                    lambda a, b: a + b, [pv(j, hh) for j in js])

        for_tile_groups(t + 1, pv_group)

    for x, (t, base, _, _) in enumerate(tiles):
        pv_loops(x, t, base)

    lam = _lambda(lq1_ref, lk1_ref, lq2_ref, lk2_ref, lam_init)
    for x, (_, _, _, gate_ref) in enumerate(tiles):
        for hh in range(hb):
            cols = head_cols(hh)
            acc = acc_ref[x, hh]
            a = acc[:, :HEAD_DIM] / acc[:, HEAD_DIM:]
            d = a[:QT] - lam * a[QT:]
            y = d * lax.rsqrt(jnp.mean(d * d, axis=-1, keepdims=True) + EPS)
            y = y * gn_ref[:, cols] * (1.0 - lam_init)
            o_ref[x * QT:(x + 1) * QT, cols] = (
                y * _silu(gate_ref[:, cols].astype(F32))).astype(BF16)


def _diffattn(proj, pm, gn, lams, layer, batch, seq, lam_init, casts):
    m = proj.shape[0]
    ntile = seq // QT
    hb = HEADS_PER_STEP
    ngrp = HEADS // hb
    width = hb * HEAD_DIM
    slopes = jnp.asarray(2.0 ** (-8.0 * np.arange(1, HEADS + 1) / HEADS), F32)
    qcol, kcol, vcol, gcol = (4 * ngrp, 5 * ngrp, 6 * ngrp, 7 * ngrp)
    vec = pl.BlockSpec((None, 1, DIFF_DH), lambda b, g, t: (layer, 0, 0))
    tile_a = lambda b, t: b * ntile + t
    tile_b = lambda b, t: b * ntile + ntile - 1 - t
    nstep = batch * ngrp * (ntile // 2)
    linear = lambda b, g, t: (b * ngrp + g) * (ntile // 2) + t
    cast_in, cast_out_specs, cast_out_shapes, cast_args = [], [], [], []
    for w, wl in casts:
        rows, ncol = w.shape[1:]
        slab = rows // nstep
        cast_in.append(pl.BlockSpec((None, slab, ncol),
                                    lambda b, g, t, wl=wl: (wl, linear(b, g, t), 0)))
        cast_out_specs.append(pl.BlockSpec((slab, ncol), lambda b, g, t: (linear(b, g, t), 0)))
        cast_out_shapes.append(jax.ShapeDtypeStruct((rows, ncol), BF16))
        cast_args.append(w)
    outs = pl.pallas_call(
        functools.partial(_diffattn_kernel, lam_init=lam_init, ncast=len(casts)),
        out_shape=[jax.ShapeDtypeStruct((m, MIX_HALF), BF16)] + cast_out_shapes,
        grid=(batch, ngrp, ntile // 2),
        in_specs=[
            pl.BlockSpec(memory_space=pltpu.SMEM),
            pl.BlockSpec((QT, width), lambda b, g, t: (tile_a(b, t), qcol + g)),
            pl.BlockSpec((QT, width), lambda b, g, t: (tile_b(b, t), qcol + g)),
            pl.BlockSpec((seq, width), lambda b, g, t: (b, kcol + g)),
            pl.BlockSpec((seq, width), lambda b, g, t: (b, vcol + g)),
            pl.BlockSpec((None, N_META, width), lambda b, g, t: (0, 0, kcol + g)),
            pl.BlockSpec((None, N_META, width), lambda b, g, t: (0, 0, vcol + g)),
            pl.BlockSpec((QT, width), lambda b, g, t: (tile_a(b, t), gcol + g)),
            pl.BlockSpec((QT, width), lambda b, g, t: (tile_b(b, t), gcol + g)),
            pl.BlockSpec((None, 1, width), lambda b, g, t: (layer, 0, g)),
            vec, vec, vec, vec,
        ] + cast_in,
        out_specs=[pl.BlockSpec((2 * QT, width), lambda b, g, t: (b * (ntile // 2) + t, g))]
        + cast_out_specs,
        scratch_shapes=[
            pltpu.VMEM((2, hb, 2 * QT, HEAD_DIM), BF16),
            pltpu.VMEM((hb, ntile + 1, 2 * QT, QT), F32),
            pltpu.VMEM((2, hb, 2 * QT, BLK), F32),
            pltpu.VMEM((2, hb, 2 * QT, BLK), F32),
            pltpu.VMEM((2, hb, 2 * QT, BLK), F32),
            pltpu.VMEM((2, hb, 2 * QT, 2 * HEAD_DIM), F32),
            pltpu.VMEM((hb, seq, 2 * HEAD_DIM), BF16),
            pltpu.VMEM((hb, BLK, HEAD_DIM), BF16),
            pltpu.VMEM((hb, BLK, 2 * HEAD_DIM), BF16),
            pltpu.VMEM((hb, QT, QT), F32),
        ],
        compiler_params=pltpu.CompilerParams(
            dimension_semantics=("parallel", "parallel", "arbitrary"),
            vmem_limit_bytes=VMEM_LIMIT),
        name="diffattn",
    )(slopes, proj, proj, proj, proj, pm, pm, proj, proj, gn, *lams, *cast_args)
    return outs[0], outs[1:]


def _mix_outproj_kernel(q_ref, k_ref, v_ref, gate_ref, km_ref, vm_ref, gret_ref,
                        da_ref, db_ref, wr_ref, wd_ref, h_ref, gf_ref, o_ref,
                        r_ref, state_ref, init_ref, dmat_ref, rq_ref, rk_ref,
                        *, final_norm, tiles_per_seq):
    i = pl.program_id(0)
    scale = HEAD_DIM ** -0.5
    tm = q_ref.shape[0]

    @pl.when(i == 0)
    def _():
        r_ref[...] = jnp.zeros_like(r_ref)
        state_ref[...] = jnp.zeros_like(state_ref)
        t = lax.broadcasted_iota(jnp.int32, (BLK, BLK), 0)
        s = lax.broadcasted_iota(jnp.int32, (BLK, BLK), 1)
        dist = jnp.abs(t - s).astype(F32)
        allowed = (s // CHUNK) <= (t // CHUNK)
        tf = t.astype(F32)
        mf = lax.broadcasted_iota(jnp.int32, (N_META, HEAD_DIM), 0).astype(F32)
        for h in range(HEADS):
            cols = slice(h * HEAD_DIM, (h + 1) * HEAD_DIM)
            log_g = _log_g(h)
            dmat_ref[h] = jnp.where(allowed, jnp.exp(log_g * dist) * scale, 0.0)
            rq_ref[h] = jnp.exp(log_g * (tf + 1.0))
            rk_ref[h] = jnp.exp(log_g * (BLK - 1.0 - tf)) * scale
            zeta = jnp.exp(log_g * (N_META - 1.0 - mf)) * scale
            kz = (km_ref[:, cols].astype(F32) * zeta).astype(BF16)
            init_ref[h] = lax.dot_general(_pad_rows(kz, BLK), _pad_rows(vm_ref[:, cols], BLK),
                                          TN, preferred_element_type=F32)

    first = (i % tiles_per_seq) == 0

    def half_tile(half, carry):
        hrows = pl.ds(pl.multiple_of(half * QT, QT), QT)
        r = r_ref[hrows, :]
        d = jnp.where(half == 0, da_ref[...], db_ref[...])
        y = jnp.dot(r, wr_ref[...], preferred_element_type=F32)
        y = y + jnp.dot(d, wd_ref[...], preferred_element_type=F32)
        hn = h_ref[hrows, :] + y
        if final_norm:
            ms = jnp.mean(hn * hn, axis=-1, keepdims=True)
            hn = hn * lax.rsqrt(ms + EPS) * gf_ref[...]
        o_ref[hrows, :] = hn
        for blk in range(QT // BLK):
            retention_block(pl.ds(pl.multiple_of(half * QT + blk * BLK, BLK), BLK),
                            first & (half == 0) if blk == 0 else None)
        return carry

    def retention_block(rows, restart):
        for h in range(HEADS):
            cols = slice(h * HEAD_DIM, (h + 1) * HEAD_DIM)
            decay_blk = float((1.0 - 2.0 ** (-5.0 - h)) ** BLK)
            q = q_ref[rows, cols]
            k = k_ref[rows, cols]
            v = v_ref[rows, cols]
            s = lax.dot_general(q, k, NT, preferred_element_type=F32) * dmat_ref[h]
            intra = jnp.dot(s.astype(BF16), v, preferred_element_type=F32)
            state = state_ref[h]
            if restart is not None:
                state = jnp.where(restart, init_ref[h], state)
            qx = (q.astype(F32) * rq_ref[h]).astype(BF16)
            cross = jnp.dot(qx, state.astype(BF16), preferred_element_type=F32)
            kz = (k.astype(F32) * rk_ref[h]).astype(BF16)
            kv = lax.dot_general(kz, v, TN, preferred_element_type=F32)
            state_ref[h] = decay_blk * state + kv

            o = intra + cross
            mu = jnp.mean(o, axis=-1, keepdims=True)
            oc = o - mu
            var = jnp.mean(oc * oc, axis=-1, keepdims=True)
            y = oc * lax.rsqrt(var + EPS) * gret_ref[:, cols]
            r_ref[rows, cols] = (y * _silu(gate_ref[rows, cols].astype(F32))).astype(BF16)

    lax.fori_loop(0, tm // QT, half_tile, 0)


def _mix_outproj(proj, pm, d, w, h, gret, gf, layer, seq, final_norm):
    m, dm = h.shape
    tm = 2 * QT
    ntiles = m // tm
    ntile_q = seq // QT
    cur = lambda i: jnp.minimum(i, ntiles - 1)
    prev = lambda i: jnp.maximum(i - 1, 0)

    def d_block(which):
        def index(i):
            tile = 2 * prev(i) + which
            b, t = tile // ntile_q, tile % ntile_q
            pos = jnp.where(t < ntile_q // 2, 2 * t, 2 * (ntile_q - 1 - t) + 1)
            return b * ntile_q + pos, 0
        return index

    return pl.pallas_call(
        functools.partial(_mix_outproj_kernel, final_norm=final_norm,
                          tiles_per_seq=seq // tm),
        out_shape=jax.ShapeDtypeStruct((m, dm), F32),
        grid=(ntiles + 1,),
        in_specs=[
            pl.BlockSpec((tm, MIX_HALF), lambda i: (cur(i), 0)),
            pl.BlockSpec((tm, MIX_HALF), lambda i: (cur(i), 1)),
            pl.BlockSpec((tm, MIX_HALF), lambda i: (cur(i), 2)),
            pl.BlockSpec((tm, MIX_HALF), lambda i: (cur(i), 3)),
            pl.BlockSpec((None, N_META, MIX_HALF), lambda i: (0, 0, 1)),
            pl.BlockSpec((None, N_META, MIX_HALF), lambda i: (0, 0, 2)),
            pl.BlockSpec((None, 1, MIX_HALF), lambda i: (layer, 0, 0)),
            pl.BlockSpec((QT, MIX_HALF), d_block(0)),
            pl.BlockSpec((QT, MIX_HALF), d_block(1)),
            pl.BlockSpec((MIX_HALF, dm), lambda i: (0, 0), pipeline_mode=pl.Buffered(1)),
            pl.BlockSpec((MIX_HALF, dm), lambda i: (1, 0), pipeline_mode=pl.Buffered(1)),
            pl.BlockSpec((tm, dm), lambda i: (prev(i), 0)),
            pl.BlockSpec((1, dm), lambda i: (0, 0)),
        ],
        out_specs=pl.BlockSpec((tm, dm), lambda i: (prev(i), 0)),
        scratch_shapes=[
            pltpu.VMEM((tm, MIX_HALF), BF16),
            pltpu.VMEM((HEADS, HEAD_DIM, HEAD_DIM), F32),
            pltpu.VMEM((HEADS, HEAD_DIM, HEAD_DIM), F32),
            pltpu.VMEM((HEADS, BLK, BLK), F32),
            pltpu.VMEM((HEADS, BLK, HEAD_DIM), F32),
            pltpu.VMEM((HEADS, BLK, HEAD_DIM), F32),
        ],
        compiler_params=pltpu.CompilerParams(
            dimension_semantics=("arbitrary",),
            vmem_limit_bytes=VMEM_LIMIT),
        name="mix_outproj",
    )(proj, proj, proj, proj, pm, pm, gret, d, d, w, w, h, gf)


def kernel(x, meta_tokens, norm_g, w_in, w_out, ret_norm_g, diff_norm_g,
           lambda_q1, lambda_k1, lambda_q2, lambda_k2, final_norm_g):
    batch, seq, dm = x.shape
    depth = w_in.shape[0]
    m = batch * seq
    assert seq % QT == 0 and meta_tokens.shape[0] == N_META

    h = x.reshape(m, dm)
    hm = meta_tokens.astype(x.dtype)
    gf = final_norm_g.reshape(1, dm)
    per_layer = lambda a: a.reshape(depth, 1, a.shape[-1])
    norm_g, ret_norm_g, diff_norm_g = map(per_layer, (norm_g, ret_norm_g, diff_norm_g))
    lams = tuple(map(per_layer, (lambda_q1, lambda_k1, lambda_q2, lambda_k2)))

    w_in_cur = w_in
    for i in range(depth):
        last = i == depth - 1
        lam_init = 0.8 - 0.6 * float(np.exp(-0.3 * i))
        tn = 1024 if w_in_cur.dtype == F32 else 2048
        proj, pm = _inproj(h, hm, norm_g, w_in_cur, i, tm=m // 8, tn=tn)
        if not last:
            hm = _meta_layer(pm, hm, w_out, ret_norm_g, diff_norm_g, lams, i, lam_init)
        casts = [(w_out, i)] + ([] if last else [(w_in, i + 1)])
        d, cast = _diffattn(proj, pm, diff_norm_g, lams, i, batch, seq, lam_init, casts)
        if not last:
            w_in_cur = cast[1]
        h = _mix_outproj(proj, pm, d, cast[0], h, ret_norm_g, gf, i, seq, final_norm=last)

    return h.reshape(batch, seq, dm)
```

```python
import functools

import numpy as np
import jax
import jax.numpy as jnp
from jax import lax
from jax.experimental import pallas as pl
from jax.experimental.pallas import tpu as pltpu

F32 = jnp.float32
BF16 = jnp.bfloat16

CHUNK = 64
N_META = 16
EPS = 1e-6
HEADS = 8
HEAD_DIM = 128
DIFF_DH = 64
MIX_HALF = HEADS * HEAD_DIM
BLK = 128
QT = 256
NEG = -1e30
LOG2E = 1.4426950408889634
HEADS_PER_STEP = 4
TILE_UNROLL = 4

VMEM_LIMIT = 56 * 1024 * 1024

NT = (((1,), (1,)), ((), ()))
TN = (((0,), (0,)), ((), ()))


def _log_g(h):
    return float(np.log(1.0 - 2.0 ** (-5.0 - h)))


def _silu(x):
    return x * (1.0 / (1.0 + jnp.exp(-x)))


def _lambda(lq1_ref, lk1_ref, lq2_ref, lk2_ref, lam_init):
    return (jnp.exp(jnp.sum(lq1_ref[...] * lk1_ref[...], axis=-1, keepdims=True))
            - jnp.exp(jnp.sum(lq2_ref[...] * lk2_ref[...], axis=-1, keepdims=True))
            + lam_init)


def _pad_rows(a, rows):
    return jnp.concatenate([a, jnp.zeros((rows - a.shape[0], a.shape[1]), a.dtype)], axis=0)


def _split_components(q):
    lane = lax.broadcasted_iota(jnp.int32, q.shape, 1)
    zero = jnp.zeros((), BF16)
    qscaled = q * jnp.asarray(DIFF_DH ** -0.5, BF16)
    return jnp.concatenate([jnp.where(lane < DIFF_DH, qscaled, zero),
                            jnp.where(lane >= DIFF_DH, qscaled, zero)], axis=0)


def _rms_norm_bf16(x, g):
    ms = jnp.mean(x * x, axis=-1, keepdims=True)
    return (x * lax.rsqrt(ms + EPS) * g).astype(BF16)


def _inproj_kernel(*refs, with_meta, emit_w, fills):
    refs = list(refs)
    h_ref = refs.pop(0)
    hm_ref = refs.pop(0) if with_meta else None
    g_ref, w_ref = refs.pop(0), refs.pop(0)
    if fills:
        refs.pop(0)
    o_ref = refs.pop(0)
    pm_ref = refs.pop(0) if with_meta else None
    wb_ref = refs.pop(0) if emit_w else None
    (u_ref,) = refs
    tm = h_ref.shape[0]

    @pl.when(pl.program_id(1) == 0)
    def _():
        u_ref[:tm] = _rms_norm_bf16(h_ref[...], g_ref[...])
        if with_meta:
            u_ref[tm:] = _rms_norm_bf16(hm_ref[...], g_ref[...])

    wb = w_ref[...].astype(BF16)
    if emit_w:
        wb_ref[...] = wb
    out = jnp.dot(u_ref[...], wb, preferred_element_type=F32).astype(BF16)
    o_ref[...] = out[:tm]
    if with_meta:
        pm_ref[...] = out[tm:]


def _inproj(h, hm, g, w, layer, tm, tn, tiles=None, emit_w=False, into=None):
    m, d = h.shape
    n = w.shape[-1]
    first, count = tiles if tiles is not None else (0, m // tm)
    with_meta = hm is not None
    assert not (emit_w and count != 1)
    if w.ndim == 3:
        w_spec = pl.BlockSpec((None, d, tn), lambda i, j: (layer, 0, j))
    else:
        w_spec = pl.BlockSpec((d, tn), lambda i, j: (0, j))
    in_specs = [pl.BlockSpec((tm, d), lambda i, j: (i + first, 0))]
    args = [h]
    if with_meta:
        in_specs.append(pl.BlockSpec((N_META, d), lambda i, j: (0, 0)))
        args.append(hm)
    in_specs += [pl.BlockSpec((None, 1, d), lambda i, j: (layer, 0, 0)), w_spec]
    args += [g, w]
    aliases = {}
    if into is not None:
        aliases = {len(args): 0}
        in_specs.append(pl.BlockSpec(memory_space=pl.ANY))
        args.append(into)
    out_shape = [jax.ShapeDtypeStruct((m, n), BF16)]
    out_specs = [pl.BlockSpec((tm, tn), lambda i, j: (i + first, j))]
    if with_meta:
        out_shape.append(jax.ShapeDtypeStruct((count, N_META, n), BF16))
        out_specs.append(pl.BlockSpec((None, N_META, tn), lambda i, j: (i, 0, j)))
    if emit_w:
        out_shape.append(jax.ShapeDtypeStruct((d, n), BF16))
        out_specs.append(pl.BlockSpec((d, tn), lambda i, j: (0, j)))
    return pl.pallas_call(
        functools.partial(_inproj_kernel, with_meta=with_meta, emit_w=emit_w,
                          fills=into is not None),
        out_shape=out_shape,
        grid=(count, n // tn),
        in_specs=in_specs,
        out_specs=out_specs,
        scratch_shapes=[pltpu.VMEM((tm + (N_META if with_meta else 0), d), BF16)],
        input_output_aliases=aliases,
        compiler_params=pltpu.CompilerParams(
            dimension_semantics=("parallel", "arbitrary"),
            vmem_limit_bytes=VMEM_LIMIT),
        name="inproj",
    )(*args)


def _meta_kernel(pm_ref, hm_ref, wr_ref, wd_ref, gret_ref, gdiff_ref,
                 lq1_ref, lk1_ref, lq2_ref, lk2_ref, hm_out_ref, *, lam_init):
    section = lambda s: pm_ref[:, s * MIX_HALF:(s + 1) * MIX_HALF]
    rq, rk, rv, rgate, dq, dk, dv, dgate = [section(s) for s in range(8)]
    t = lax.broadcasted_iota(jnp.int32, (BLK, BLK), 0)
    s_i = lax.broadcasted_iota(jnp.int32, (BLK, BLK), 1)
    dist = jnp.abs(t - s_i).astype(F32)
    r2 = lax.broadcasted_iota(jnp.int32, (2 * BLK, BLK), 0) % BLK
    c2 = lax.broadcasted_iota(jnp.int32, (2 * BLK, BLK), 1)
    dist2 = jnp.abs(r2 - c2).astype(F32)
    lam = _lambda(lq1_ref, lk1_ref, lq2_ref, lk2_ref, lam_init)
    r_parts, d_parts = [], []
    for h in range(HEADS):
        cols = slice(h * HEAD_DIM, (h + 1) * HEAD_DIM)
        q = _pad_rows(rq[:, cols], BLK)
        k = _pad_rows(rk[:, cols], BLK)
        v = _pad_rows(rv[:, cols], BLK)
        dmat = jnp.exp(_log_g(h) * dist) * (HEAD_DIM ** -0.5)
        s = lax.dot_general(q, k, NT, preferred_element_type=F32) * dmat
        o = jnp.dot(s.astype(BF16), v, preferred_element_type=F32)[:N_META]
        mu = jnp.mean(o, axis=-1, keepdims=True)
        oc = o - mu
        var = jnp.mean(oc * oc, axis=-1, keepdims=True)
        y = oc * lax.rsqrt(var + EPS) * gret_ref[:, cols]
        r_parts.append((y * _silu(rgate[:, cols].astype(F32))).astype(BF16))
        slope = 2.0 ** (-(h + 1.0))
        qs = _split_components(_pad_rows(dq[:, cols], BLK))
        k = _pad_rows(dk[:, cols], BLK)
        v = _pad_rows(dv[:, cols], BLK)
        bias = jnp.where(c2 < N_META, -slope * dist2, NEG)
        s = lax.dot_general(qs, k, NT, preferred_element_type=F32) + bias
        pr = jnp.exp(s - jnp.max(s, axis=-1, keepdims=True))
        a = (jnp.dot(pr.astype(BF16), v, preferred_element_type=F32)
             / jnp.sum(pr, axis=-1, keepdims=True))
        d = a[:N_META] - lam * a[BLK:BLK + N_META]
        y = d * lax.rsqrt(jnp.mean(d * d, axis=-1, keepdims=True) + EPS)
        y = y * gdiff_ref[:, cols] * (1.0 - lam_init)
        d_parts.append((y * _silu(dgate[:, cols].astype(F32))).astype(BF16))
    r = jnp.concatenate(r_parts, axis=1)
    d = jnp.concatenate(d_parts, axis=1)
    y = jnp.dot(r, wr_ref[...].astype(BF16), preferred_element_type=F32)
    y = y + jnp.dot(d, wd_ref[...].astype(BF16), preferred_element_type=F32)
    hm_out_ref[...] = hm_ref[...] + y


def _meta_layer(pm, hm, w_out, gret, gdiff, lams, layer, lam_init):
    dm = hm.shape[1]
    n = pm.shape[-1]
    vec = lambda width: pl.BlockSpec((None, 1, width), lambda i: (layer, 0, 0))
    return pl.pallas_call(
        functools.partial(_meta_kernel, lam_init=lam_init),
        out_shape=jax.ShapeDtypeStruct((N_META, dm), F32),
        grid=(1,),
        in_specs=[
            pl.BlockSpec((None, N_META, n), lambda i: (0, 0, 0)),
            pl.BlockSpec((N_META, dm), lambda i: (0, 0)),
            pl.BlockSpec((None, MIX_HALF, dm), lambda i: (layer, 0, 0),
                         pipeline_mode=pl.Buffered(1)),
            pl.BlockSpec((None, MIX_HALF, dm), lambda i: (layer, 1, 0),
                         pipeline_mode=pl.Buffered(1)),
            vec(MIX_HALF), vec(MIX_HALF),
            vec(DIFF_DH), vec(DIFF_DH), vec(DIFF_DH), vec(DIFF_DH),
        ],
        out_specs=pl.BlockSpec((N_META, dm), lambda i: (0, 0)),
        compiler_params=pltpu.CompilerParams(
            dimension_semantics=("arbitrary",),
            vmem_limit_bytes=VMEM_LIMIT),
        name="meta_layer",
    )(pm, hm, w_out, w_out, gret, gdiff, *lams)


def _diffattn_kernel(*refs, lam_init, ncast):
    (slopes_ref, qa_ref, qb_ref, k_ref, v_ref, km_ref, vm_ref, gatea_ref, gateb_ref, gn_ref,
     lq1_ref, lk1_ref, lq2_ref, lk2_ref) = refs[:14]
    cast_src = refs[14:14 + ncast]
    o_ref = refs[14 + ncast]
    cast_dst = refs[15 + ncast:15 + 2 * ncast]
    (qs_ref, s_ref, sm_ref, mrun_ref, mrow_ref, acc_ref,
     vext_ref, kmext_ref, vmext_ref, dbias_ref) = refs[15 + 2 * ncast:]

    for src, dst in zip(cast_src, cast_dst):
        dst[...] = src[...].astype(BF16)

    grp = pl.program_id(1)
    step = pl.program_id(2)
    hb = HEADS_PER_STEP
    seq = k_ref.shape[0]
    ntile = seq // QT
    head_cols = lambda hh: slice(hh * HEAD_DIM, (hh + 1) * HEAD_DIM)
    slope2 = lambda hh: slopes_ref[grp * hb + hh] * LOG2E
    tiles = ((step, 0, qa_ref, gatea_ref),
             (ntile - 1 - step, step + 1, qb_ref, gateb_ref))

    @pl.when(step == 0)
    def _():
        r = lax.broadcasted_iota(jnp.int32, (QT, QT), 0)
        c = lax.broadcasted_iota(jnp.int32, (QT, QT), 1)
        allowed = (c // CHUNK) <= (r // CHUNK)
        rel = (r - jnp.abs(r - c)).astype(F32)
        for hh in range(hb):
            cols = head_cols(hh)
            dbias_ref[hh] = jnp.where(allowed, slope2(hh) * rel, NEG)
            vext_ref[hh, :, :HEAD_DIM] = v_ref[:, cols]
            vext_ref[hh, :, HEAD_DIM:] = jnp.ones((seq, HEAD_DIM), BF16)
            kmext_ref[hh] = _pad_rows(km_ref[:, cols], BLK)
            vmext_ref[hh, :, :HEAD_DIM] = _pad_rows(vm_ref[:, cols], BLK)
            vmext_ref[hh, :, HEAD_DIM:] = jnp.ones((BLK, HEAD_DIM), BF16)

    def frame_scores(x, hh, j):
        start = pl.multiple_of(j * QT, QT)
        kt = k_ref[pl.ds(start, QT), head_cols(hh)]
        return lax.dot_general(qs_ref[x, hh], kt, NT, preferred_element_type=F32) * LOG2E

    def keep(x, hh, slot, s):
        s_ref[hh, slot] = s
        mrun_ref[x, hh] = jnp.maximum(mrun_ref[x, hh], jnp.maximum(s[:, :BLK], s[:, BLK:]))

    colm = lax.broadcasted_iota(jnp.int32, (1, BLK), 1)
    for x, (t, base, q_ref, _) in enumerate(tiles):
        relm = (colm - N_META - t * QT).astype(F32)
        for hh in range(hb):
            qs_ref[x, hh] = _split_components(q_ref[:, head_cols(hh)])
            s = lax.dot_general(qs_ref[x, hh], kmext_ref[hh], NT, preferred_element_type=F32)
            s = s * LOG2E + jnp.where(colm < N_META, slope2(hh) * relm, NEG)
            sm_ref[x, hh] = s
            mrun_ref[x, hh] = s
            s = frame_scores(x, hh, t)
            keep(x, hh, base + t,
                 jnp.concatenate([s[:QT] + dbias_ref[hh], s[QT:] + dbias_ref[hh]], axis=0))

    col = lax.broadcasted_iota(jnp.int32, (1, QT), 1)

    def for_tile_groups(n, group):
        def trip(i, carry):
            group([TILE_UNROLL * i + u for u in range(TILE_UNROLL)])
            return carry

        lax.fori_loop(0, n // TILE_UNROLL, trip, 0)
        done = (n // TILE_UNROLL) * TILE_UNROLL
        size = TILE_UNROLL // 2
        while size >= 1:
            @pl.when((n & size) != 0)
            def _(done=done, size=size):
                group([done + u for u in range(size)])
            done = done + (n & size)
            size //= 2

    def off_diag_loops(x, t, base):
        def off_diag(js):
            for j in js:
                rel = (col + (j - t) * QT).astype(F32)
                for hh in range(hb):
                    keep(x, hh, base + j, frame_scores(x, hh, j) + slope2(hh) * rel)

        for_tile_groups(t, off_diag)

    for x, (t, base, _, _) in enumerate(tiles):
        off_diag_loops(x, t, base)

    for x in range(2):
        for hh in range(hb):
            mrow_ref[x, hh] = jnp.broadcast_to(
                jnp.max(mrun_ref[x, hh], axis=-1, keepdims=True), (2 * QT, BLK))
            p = jnp.exp2(sm_ref[x, hh] - mrow_ref[x, hh])
            acc_ref[x, hh] = jnp.dot(p.astype(BF16), vmext_ref[hh], preferred_element_type=F32)

    def pv_loops(x, t, base):
        def pv(j, hh):
            start = pl.multiple_of(j * QT, QT)
            s = s_ref[hh, base + j]
            m = mrow_ref[x, hh]
            p = jnp.concatenate([jnp.exp2(s[:, :BLK] - m), jnp.exp2(s[:, BLK:] - m)], axis=1)
            return jnp.dot(p.astype(BF16), vext_ref[hh, pl.ds(start, QT), :],
                           preferred_element_type=F32)

        def pv_group(js):
            for hh in range(hb):
                acc_ref[x, hh] = acc_ref[x, hh] + functools.reduce(
                    lambda a, b: a + b, [pv(j, hh) for j in js])

        for_tile_groups(t + 1, pv_group)

    for x, (t, base, _, _) in enumerate(tiles):
        pv_loops(x, t, base)

    lam = _lambda(lq1_ref, lk1_ref, lq2_ref, lk2_ref, lam_init)
    for x, (_, _, _, gate_ref) in enumerate(tiles):
        for hh in range(hb):
            cols = head_cols(hh)
            acc = acc_ref[x, hh]
            a = acc[:, :HEAD_DIM] / acc[:, HEAD_DIM:]
            d = a[:QT] - lam * a[QT:]
            y = d * lax.rsqrt(jnp.mean(d * d, axis=-1, keepdims=True) + EPS)
            y = y * gn_ref[:, cols] * (1.0 - lam_init)
            o_ref[x * QT:(x + 1) * QT, cols] = (
                y * _silu(gate_ref[:, cols].astype(F32))).astype(BF16)


def _diffattn(proj, pm, gn, lams, layer, batch, seq, lam_init, casts):
    m = proj.shape[0]
    ntile = seq // QT
    hb = HEADS_PER_STEP
    ngrp = HEADS // hb
    width = hb * HEAD_DIM
    slopes = jnp.asarray(2.0 ** (-8.0 * np.arange(1, HEADS + 1) / HEADS), F32)
    qcol, kcol, vcol, gcol = (4 * ngrp, 5 * ngrp, 6 * ngrp, 7 * ngrp)
    vec = pl.BlockSpec((None, 1, DIFF_DH), lambda b, g, t: (layer, 0, 0))
    tile_a = lambda b, t: b * ntile + t
    tile_b = lambda b, t: b * ntile + ntile - 1 - t
    nstep = batch * ngrp * (ntile // 2)
    linear = lambda b, g, t: (b * ngrp + g) * (ntile // 2) + t
    cast_in, cast_out_specs, cast_out_shapes, cast_args = [], [], [], []
    for w, wl in casts:
        rows, ncol = w.shape[1:]
        slab = rows // nstep
        cast_in.append(pl.BlockSpec((None, slab, ncol),
                                    lambda b, g, t, wl=wl: (wl, linear(b, g, t), 0)))
        cast_out_specs.append(pl.BlockSpec((slab, ncol), lambda b, g, t: (linear(b, g, t), 0)))
        cast_out_shapes.append(jax.ShapeDtypeStruct((rows, ncol), BF16))
        cast_args.append(w)
    outs = pl.pallas_call(
        functools.partial(_diffattn_kernel, lam_init=lam_init, ncast=len(casts)),
        out_shape=[jax.ShapeDtypeStruct((m, MIX_HALF), BF16)] + cast_out_shapes,
        grid=(batch, ngrp, ntile // 2),
        in_specs=[
            pl.BlockSpec(memory_space=pltpu.SMEM),
            pl.BlockSpec((QT, width), lambda b, g, t: (tile_a(b, t), qcol + g)),
            pl.BlockSpec((QT, width), lambda b, g, t: (tile_b(b, t), qcol + g)),
            pl.BlockSpec((seq, width), lambda b, g, t: (b, kcol + g)),
            pl.BlockSpec((seq, width), lambda b, g, t: (b, vcol + g)),
            pl.BlockSpec((None, N_META, width), lambda b, g, t: (0, 0, kcol + g)),
            pl.BlockSpec((None, N_META, width), lambda b, g, t: (0, 0, vcol + g)),
            pl.BlockSpec((QT, width), lambda b, g, t: (tile_a(b, t), gcol + g)),
            pl.BlockSpec((QT, width), lambda b, g, t: (tile_b(b, t), gcol + g)),
            pl.BlockSpec((None, 1, width), lambda b, g, t: (layer, 0, g)),
            vec, vec, vec, vec,
        ] + cast_in,
        out_specs=[pl.BlockSpec((2 * QT, width), lambda b, g, t: (b * (ntile // 2) + t, g))]
        + cast_out_specs,
        scratch_shapes=[
            pltpu.VMEM((2, hb, 2 * QT, HEAD_DIM), BF16),
            pltpu.VMEM((hb, ntile + 1, 2 * QT, QT), F32),
            pltpu.VMEM((2, hb, 2 * QT, BLK), F32),
            pltpu.VMEM((2, hb, 2 * QT, BLK), F32),
            pltpu.VMEM((2, hb, 2 * QT, BLK), F32),
            pltpu.VMEM((2, hb, 2 * QT, 2 * HEAD_DIM), F32),
            pltpu.VMEM((hb, seq, 2 * HEAD_DIM), BF16),
            pltpu.VMEM((hb, BLK, HEAD_DIM), BF16),
            pltpu.VMEM((hb, BLK, 2 * HEAD_DIM), BF16),
            pltpu.VMEM((hb, QT, QT), F32),
        ],
        compiler_params=pltpu.CompilerParams(
            dimension_semantics=("parallel", "parallel", "arbitrary"),
            vmem_limit_bytes=VMEM_LIMIT),
        name="diffattn",
    )(slopes, proj, proj, proj, proj, pm, pm, proj, proj, gn, *lams, *cast_args)
    return outs[0], outs[1:]


def _mix_outproj_kernel(q_ref, k_ref, v_ref, gate_ref, km_ref, vm_ref, gret_ref,
                        da_ref, db_ref, wr_ref, wd_ref, h_ref, gf_ref, o_ref,
                        r_ref, state_ref, init_ref, dmat_ref, rq_ref, rk_ref,
                        *, final_norm, tiles_per_seq):
    i = pl.program_id(0)
    scale = HEAD_DIM ** -0.5
    tm = q_ref.shape[0]

    @pl.when(i == 0)
    def _():
        r_ref[...] = jnp.zeros_like(r_ref)
        state_ref[...] = jnp.zeros_like(state_ref)
        t = lax.broadcasted_iota(jnp.int32, (BLK, BLK), 0)
        s = lax.broadcasted_iota(jnp.int32, (BLK, BLK), 1)
        dist = jnp.abs(t - s).astype(F32)
        allowed = (s // CHUNK) <= (t // CHUNK)
        tf = t.astype(F32)
        mf = lax.broadcasted_iota(jnp.int32, (N_META, HEAD_DIM), 0).astype(F32)
        for h in range(HEADS):
            cols = slice(h * HEAD_DIM, (h + 1) * HEAD_DIM)
            log_g = _log_g(h)
            dmat_ref[h] = jnp.where(allowed, jnp.exp(log_g * dist) * scale, 0.0)
            rq_ref[h] = jnp.exp(log_g * (tf + 1.0))
            rk_ref[h] = jnp.exp(log_g * (BLK - 1.0 - tf)) * scale
            zeta = jnp.exp(log_g * (N_META - 1.0 - mf)) * scale
            kz = (km_ref[:, cols].astype(F32) * zeta).astype(BF16)
            init_ref[h] = lax.dot_general(_pad_rows(kz, BLK), _pad_rows(vm_ref[:, cols], BLK),
                                          TN, preferred_element_type=F32)

    r = r_ref[...]
    for half, d_ref in enumerate((da_ref, db_ref)):
        rows = slice(half * QT, (half + 1) * QT)
        y = jnp.dot(r[rows], wr_ref[...], preferred_element_type=F32)
        y = y + jnp.dot(d_ref[...], wd_ref[...], preferred_element_type=F32)
        hn = h_ref[rows] + y
        if final_norm:
            ms = jnp.mean(hn * hn, axis=-1, keepdims=True)
            hn = hn * lax.rsqrt(ms + EPS) * gf_ref[...]
        o_ref[rows] = hn

    first = (i % tiles_per_seq) == 0
    for blk in range(tm // BLK):
        rows = slice(blk * BLK, (blk + 1) * BLK)
        for h in range(HEADS):
            cols = slice(h * HEAD_DIM, (h + 1) * HEAD_DIM)
            decay_blk = float((1.0 - 2.0 ** (-5.0 - h)) ** BLK)
            q = q_ref[rows, cols]
            k = k_ref[rows, cols]
            v = v_ref[rows, cols]
            s = lax.dot_general(q, k, NT, preferred_element_type=F32) * dmat_ref[h]
            intra = jnp.dot(s.astype(BF16), v, preferred_element_type=F32)
            state = state_ref[h]
            if blk == 0:
                state = jnp.where(first, init_ref[h], state)
            qx = (q.astype(F32) * rq_ref[h]).astype(BF16)
            cross = jnp.dot(qx, state.astype(BF16), preferred_element_type=F32)
            kz = (k.astype(F32) * rk_ref[h]).astype(BF16)
            kv = lax.dot_general(kz, v, TN, preferred_element_type=F32)
            state_ref[h] = decay_blk * state + kv

            o = intra + cross
            mu = jnp.mean(o, axis=-1, keepdims=True)
            oc = o - mu
            var = jnp.mean(oc * oc, axis=-1, keepdims=True)
            y = oc * lax.rsqrt(var + EPS) * gret_ref[:, cols]
            r_ref[rows, cols] = (y * _silu(gate_ref[rows, cols].astype(F32))).astype(BF16)


def _mix_outproj(proj, pm, d, w, h, gret, gf, layer, seq, final_norm):
    m, dm = h.shape
    tm = 2 * QT
    ntiles = m // tm
    ntile_q = seq // QT
    cur = lambda i: jnp.minimum(i, ntiles - 1)
    prev = lambda i: jnp.maximum(i - 1, 0)

    def d_block(which):
        def index(i):
            tile = 2 * prev(i) + which
            b, t = tile // ntile_q, tile % ntile_q
            pos = jnp.where(t < ntile_q // 2, 2 * t, 2 * (ntile_q - 1 - t) + 1)
            return b * ntile_q + pos, 0
        return index

    return pl.pallas_call(
        functools.partial(_mix_outproj_kernel, final_norm=final_norm,
                          tiles_per_seq=seq // tm),
        out_shape=jax.ShapeDtypeStruct((m, dm), F32),
        grid=(ntiles + 1,),
        in_specs=[
            pl.BlockSpec((tm, MIX_HALF), lambda i: (cur(i), 0)),
            pl.BlockSpec((tm, MIX_HALF), lambda i: (cur(i), 1)),
            pl.BlockSpec((tm, MIX_HALF), lambda i: (cur(i), 2)),
            pl.BlockSpec((tm, MIX_HALF), lambda i: (cur(i), 3)),
            pl.BlockSpec((None, N_META, MIX_HALF), lambda i: (0, 0, 1)),
            pl.BlockSpec((None, N_META, MIX_HALF), lambda i: (0, 0, 2)),
            pl.BlockSpec((None, 1, MIX_HALF), lambda i: (layer, 0, 0)),
            pl.BlockSpec((QT, MIX_HALF), d_block(0)),
            pl.BlockSpec((QT, MIX_HALF), d_block(1)),
            pl.BlockSpec((MIX_HALF, dm), lambda i: (0, 0), pipeline_mode=pl.Buffered(1)),
            pl.BlockSpec((MIX_HALF, dm), lambda i: (1, 0), pipeline_mode=pl.Buffered(1)),
            pl.BlockSpec((tm, dm), lambda i: (prev(i), 0)),
            pl.BlockSpec((1, dm), lambda i: (0, 0)),
        ],
        out_specs=pl.BlockSpec((tm, dm), lambda i: (prev(i), 0)),
        scratch_shapes=[
            pltpu.VMEM((tm, MIX_HALF), BF16),
            pltpu.VMEM((HEADS, HEAD_DIM, HEAD_DIM), F32),
            pltpu.VMEM((HEADS, HEAD_DIM, HEAD_DIM), F32),
            pltpu.VMEM((HEADS, BLK, BLK), F32),
            pltpu.VMEM((HEADS, BLK, HEAD_DIM), F32),
            pltpu.VMEM((HEADS, BLK, HEAD_DIM), F32),
        ],
        compiler_params=pltpu.CompilerParams(
            dimension_semantics=("arbitrary",),
            vmem_limit_bytes=VMEM_LIMIT),
        name="mix_outproj",
    )(proj, proj, proj, proj, pm, pm, gret, d, d, w, w, h, gf)


def kernel(x, meta_tokens, norm_g, w_in, w_out, ret_norm_g, diff_norm_g,
           lambda_q1, lambda_k1, lambda_q2, lambda_k2, final_norm_g):
    batch, seq, dm = x.shape
    depth = w_in.shape[0]
    m = batch * seq
    assert seq % QT == 0 and meta_tokens.shape[0] == N_META

    h = x.reshape(m, dm)
    hm = meta_tokens.astype(x.dtype)
    gf = final_norm_g.reshape(1, dm)
    per_layer = lambda a: a.reshape(depth, 1, a.shape[-1])
    norm_g, ret_norm_g, diff_norm_g = map(per_layer, (norm_g, ret_norm_g, diff_norm_g))
    lams = tuple(map(per_layer, (lambda_q1, lambda_k1, lambda_q2, lambda_k2)))

    tm = m // 8
    w_in_b = None
    for i in range(depth):
        last = i == depth - 1
        lam_init = 0.8 - 0.6 * float(np.exp(-0.3 * i))
        if w_in_b is None:
            proj, pm, w_in_b = _inproj(h, hm, norm_g, w_in, i, tm, 1024, tiles=(0, 1),
                                       emit_w=True)
            (proj,) = _inproj(h, None, norm_g, w_in_b, i, tm, 2048, tiles=(1, m // tm - 1),
                              into=proj)
        else:
            proj, pm = _inproj(h, hm, norm_g, w_in_b, i, tm, 2048)
        if not last:
            hm = _meta_layer(pm, hm, w_out, ret_norm_g, diff_norm_g, lams, i, lam_init)
        casts = [(w_out, i)] + ([] if last else [(w_in, i + 1)])
        d, cast = _diffattn(proj, pm, diff_norm_g, lams, i, batch, seq, lam_init, casts)
        w_in_b = None if last else cast[1]
        h = _mix_outproj(proj, pm, d, cast[0], h, ret_norm_g, gf, i, seq, final_norm=last)

    return h.reshape(batch, seq, dm)
```

```python
import functools

import numpy as np
import jax
import jax.numpy as jnp
from jax import lax
from jax.experimental import pallas as pl
from jax.experimental.pallas import tpu as pltpu

F32 = jnp.float32
BF16 = jnp.bfloat16

CHUNK = 64
N_META = 16
EPS = 1e-6
HEADS = 8
HEAD_DIM = 128
DIFF_DH = 64
MIX_HALF = HEADS * HEAD_DIM
BLK = 128
QT = 256
NEG = -1e30
LOG2E = 1.4426950408889634
HEADS_PER_STEP = 4
TILE_UNROLL = 4

VMEM_LIMIT = 56 * 1024 * 1024

NT = (((1,), (1,)), ((), ()))
TN = (((0,), (0,)), ((), ()))


def _log_g(h):
    return float(np.log(1.0 - 2.0 ** (-5.0 - h)))


def _silu(x):
    return x * (1.0 / (1.0 + jnp.exp(-x)))


def _lambda(lq1_ref, lk1_ref, lq2_ref, lk2_ref, lam_init):
    return (jnp.exp(jnp.sum(lq1_ref[...] * lk1_ref[...], axis=-1, keepdims=True))
            - jnp.exp(jnp.sum(lq2_ref[...] * lk2_ref[...], axis=-1, keepdims=True))
            + lam_init)


def _pad_rows(a, rows):
    return jnp.concatenate([a, jnp.zeros((rows - a.shape[0], a.shape[1]), a.dtype)], axis=0)


def _split_components(q):
    lane = lax.broadcasted_iota(jnp.int32, q.shape, 1)
    zero = jnp.zeros((), BF16)
    qscaled = q * jnp.asarray(DIFF_DH ** -0.5, BF16)
    return jnp.concatenate([jnp.where(lane < DIFF_DH, qscaled, zero),
                            jnp.where(lane >= DIFF_DH, qscaled, zero)], axis=0)


def _rms_norm_bf16(x, g):
    ms = jnp.mean(x * x, axis=-1, keepdims=True)
    return (x * lax.rsqrt(ms + EPS) * g).astype(BF16)


def _inproj_kernel(*refs, with_meta, emit_w, fills):
    refs = list(refs)
    h_ref = refs.pop(0)
    hm_ref = refs.pop(0) if with_meta else None
    g_ref, w_ref = refs.pop(0), refs.pop(0)
    if fills:
        refs.pop(0)
    o_ref = refs.pop(0)
    pm_ref = refs.pop(0) if with_meta else None
    wb_ref = refs.pop(0) if emit_w else None
    (u_ref,) = refs
    tm = h_ref.shape[0]

    @pl.when(pl.program_id(1) == 0)
    def _():
        u_ref[:tm] = _rms_norm_bf16(h_ref[...], g_ref[...])
        if with_meta:
            u_ref[tm:] = _rms_norm_bf16(hm_ref[...], g_ref[...])

    wb = w_ref[...].astype(BF16)
    if emit_w:
        wb_ref[...] = wb
    out = jnp.dot(u_ref[...], wb, preferred_element_type=F32).astype(BF16)
    o_ref[...] = out[:tm]
    if with_meta:
        pm_ref[...] = out[tm:]


def _inproj(h, hm, g, w, layer, tm, tn, tiles=None, emit_w=False, into=None):
    m, d = h.shape
    n = w.shape[-1]
    first, count = tiles if tiles is not None else (0, m // tm)
    with_meta = hm is not None
    assert not (emit_w and count != 1)
    if w.ndim == 3:
        w_spec = pl.BlockSpec((None, d, tn), lambda i, j: (layer, 0, j))
    else:
        w_spec = pl.BlockSpec((d, tn), lambda i, j: (0, j))
    in_specs = [pl.BlockSpec((tm, d), lambda i, j: (i + first, 0))]
    args = [h]
    if with_meta:
        in_specs.append(pl.BlockSpec((N_META, d), lambda i, j: (0, 0)))
        args.append(hm)
    in_specs += [pl.BlockSpec((None, 1, d), lambda i, j: (layer, 0, 0)), w_spec]
    args += [g, w]
    aliases = {}
    if into is not None:
        aliases = {len(args): 0}
        in_specs.append(pl.BlockSpec(memory_space=pl.ANY))
        args.append(into)
    out_shape = [jax.ShapeDtypeStruct((m, n), BF16)]
    out_specs = [pl.BlockSpec((tm, tn), lambda i, j: (i + first, j))]
    if with_meta:
        out_shape.append(jax.ShapeDtypeStruct((count, N_META, n), BF16))
        out_specs.append(pl.BlockSpec((None, N_META, tn), lambda i, j: (i, 0, j)))
    if emit_w:
        out_shape.append(jax.ShapeDtypeStruct((d, n), BF16))
        out_specs.append(pl.BlockSpec((d, tn), lambda i, j: (0, j)))
    return pl.pallas_call(
        functools.partial(_inproj_kernel, with_meta=with_meta, emit_w=emit_w,
                          fills=into is not None),
        out_shape=out_shape,
        grid=(count, n // tn),
        in_specs=in_specs,
        out_specs=out_specs,
        scratch_shapes=[pltpu.VMEM((tm + (N_META if with_meta else 0), d), BF16)],
        input_output_aliases=aliases,
        compiler_params=pltpu.CompilerParams(
            dimension_semantics=("parallel", "arbitrary"),
            vmem_limit_bytes=VMEM_LIMIT),
        name="inproj",
    )(*args)


def _meta_kernel(pm_ref, hm_ref, wr_ref, wd_ref, gret_ref, gdiff_ref,
                 lq1_ref, lk1_ref, lq2_ref, lk2_ref, hm_out_ref, *, lam_init):
    section = lambda s: pm_ref[:, s * MIX_HALF:(s + 1) * MIX_HALF]
    rq, rk, rv, rgate, dq, dk, dv, dgate = [section(s) for s in range(8)]
    t = lax.broadcasted_iota(jnp.int32, (BLK, BLK), 0)
    s_i = lax.broadcasted_iota(jnp.int32, (BLK, BLK), 1)
    dist = jnp.abs(t - s_i).astype(F32)
    r2 = lax.broadcasted_iota(jnp.int32, (2 * BLK, BLK), 0) % BLK
    c2 = lax.broadcasted_iota(jnp.int32, (2 * BLK, BLK), 1)
    dist2 = jnp.abs(r2 - c2).astype(F32)
    lam = _lambda(lq1_ref, lk1_ref, lq2_ref, lk2_ref, lam_init)
    r_parts, d_parts = [], []
    for h in range(HEADS):
        cols = slice(h * HEAD_DIM, (h + 1) * HEAD_DIM)
        q = _pad_rows(rq[:, cols], BLK)
        k = _pad_rows(rk[:, cols], BLK)
        v = _pad_rows(rv[:, cols], BLK)
        dmat = jnp.exp(_log_g(h) * dist) * (HEAD_DIM ** -0.5)
        s = lax.dot_general(q, k, NT, preferred_element_type=F32) * dmat
        o = jnp.dot(s.astype(BF16), v, preferred_element_type=F32)[:N_META]
        mu = jnp.mean(o, axis=-1, keepdims=True)
        oc = o - mu
        var = jnp.mean(oc * oc, axis=-1, keepdims=True)
        y = oc * lax.rsqrt(var + EPS) * gret_ref[:, cols]
        r_parts.append((y * _silu(rgate[:, cols].astype(F32))).astype(BF16))
        slope = 2.0 ** (-(h + 1.0))
        qs = _split_components(_pad_rows(dq[:, cols], BLK))
        k = _pad_rows(dk[:, cols], BLK)
        v = _pad_rows(dv[:, cols], BLK)
        bias = jnp.where(c2 < N_META, -slope * dist2, NEG)
        s = lax.dot_general(qs, k, NT, preferred_element_type=F32) + bias
        pr = jnp.exp(s - jnp.max(s, axis=-1, keepdims=True))
        a = (jnp.dot(pr.astype(BF16), v, preferred_element_type=F32)
             / jnp.sum(pr, axis=-1, keepdims=True))
        d = a[:N_META] - lam * a[BLK:BLK + N_META]
        y = d * lax.rsqrt(jnp.mean(d * d, axis=-1, keepdims=True) + EPS)
        y = y * gdiff_ref[:, cols] * (1.0 - lam_init)
        d_parts.append((y * _silu(dgate[:, cols].astype(F32))).astype(BF16))
    r = jnp.concatenate(r_parts, axis=1)
    d = jnp.concatenate(d_parts, axis=1)
    y = jnp.dot(r, wr_ref[...].astype(BF16), preferred_element_type=F32)
    y = y + jnp.dot(d, wd_ref[...].astype(BF16), preferred_element_type=F32)
    hm_out_ref[...] = hm_ref[...] + y


def _meta_layer(pm, hm, w_out, gret, gdiff, lams, layer, lam_init):
    dm = hm.shape[1]
    n = pm.shape[-1]
    vec = lambda width: pl.BlockSpec((None, 1, width), lambda i: (layer, 0, 0))
    return pl.pallas_call(
        functools.partial(_meta_kernel, lam_init=lam_init),
        out_shape=jax.ShapeDtypeStruct((N_META, dm), F32),
        grid=(1,),
        in_specs=[
            pl.BlockSpec((None, N_META, n), lambda i: (0, 0, 0)),
            pl.BlockSpec((N_META, dm), lambda i: (0, 0)),
            pl.BlockSpec((None, MIX_HALF, dm), lambda i: (layer, 0, 0),
                         pipeline_mode=pl.Buffered(1)),
            pl.BlockSpec((None, MIX_HALF, dm), lambda i: (layer, 1, 0),
                         pipeline_mode=pl.Buffered(1)),
            vec(MIX_HALF), vec(MIX_HALF),
            vec(DIFF_DH), vec(DIFF_DH), vec(DIFF_DH), vec(DIFF_DH),
        ],
        out_specs=pl.BlockSpec((N_META, dm), lambda i: (0, 0)),
        compiler_params=pltpu.CompilerParams(
            dimension_semantics=("arbitrary",),
            vmem_limit_bytes=VMEM_LIMIT),
        name="meta_layer",
    )(pm, hm, w_out, w_out, gret, gdiff, *lams)


def _diffattn_kernel(*refs, lam_init, ncast):
    (slopes_ref, qa_ref, qb_ref, k_ref, v_ref, km_ref, vm_ref, gatea_ref, gateb_ref, gn_ref,
     lq1_ref, lk1_ref, lq2_ref, lk2_ref) = refs[:14]
    cast_src = refs[14:14 + ncast]
    o_ref = refs[14 + ncast]
    cast_dst = refs[15 + ncast:15 + 2 * ncast]
    (qs_ref, s_ref, sm_ref, mrun_ref, mrow_ref, acc_ref,
     vext_ref, kmext_ref, vmext_ref, dbias_ref) = refs[15 + 2 * ncast:]

    for src, dst in zip(cast_src, cast_dst):
        dst[...] = src[...].astype(BF16)

    grp = pl.program_id(1)
    step = pl.program_id(2)
    hb = HEADS_PER_STEP
    seq = k_ref.shape[0]
    ntile = seq // QT
    head_cols = lambda hh: slice(hh * HEAD_DIM, (hh + 1) * HEAD_DIM)
    slope2 = lambda hh: slopes_ref[grp * hb + hh] * LOG2E
    tiles = ((step, 0, qa_ref, gatea_ref),
             (ntile - 1 - step, step + 1, qb_ref, gateb_ref))

    @pl.when(step == 0)
    def _():
        r = lax.broadcasted_iota(jnp.int32, (QT, QT), 0)
        c = lax.broadcasted_iota(jnp.int32, (QT, QT), 1)
        allowed = (c // CHUNK) <= (r // CHUNK)
        rel = (r - jnp.abs(r - c)).astype(F32)
        for hh in range(hb):
            cols = head_cols(hh)
            dbias_ref[hh] = jnp.where(allowed, slope2(hh) * rel, NEG)
            vext_ref[hh, :, :HEAD_DIM] = v_ref[:, cols]
            vext_ref[hh, :, HEAD_DIM:] = jnp.ones((seq, HEAD_DIM), BF16)
            kmext_ref[hh] = _pad_rows(km_ref[:, cols], BLK)
            vmext_ref[hh, :, :HEAD_DIM] = _pad_rows(vm_ref[:, cols], BLK)
            vmext_ref[hh, :, HEAD_DIM:] = jnp.ones((BLK, HEAD_DIM), BF16)

    def frame_scores(x, hh, j):
        start = pl.multiple_of(j * QT, QT)
        kt = k_ref[pl.ds(start, QT), head_cols(hh)]
        return lax.dot_general(qs_ref[x, hh], kt, NT, preferred_element_type=F32) * LOG2E

    def keep(x, hh, slot, s):
        s_ref[hh, slot] = s
        mrun_ref[x, hh] = jnp.maximum(mrun_ref[x, hh], jnp.maximum(s[:, :BLK], s[:, BLK:]))

    colm = lax.broadcasted_iota(jnp.int32, (1, BLK), 1)

    def first_scores(x, hh):
        t, base, q_ref, _ = tiles[x]
        relm = (colm - N_META - t * QT).astype(F32)
        qs_ref[x, hh] = _split_components(q_ref[:, head_cols(hh)])
        s = lax.dot_general(qs_ref[x, hh], kmext_ref[hh], NT, preferred_element_type=F32)
        s = s * LOG2E + jnp.where(colm < N_META, slope2(hh) * relm, NEG)
        sm_ref[x, hh] = s
        mrun_ref[x, hh] = s
        s = frame_scores(x, hh, t)
        keep(x, hh, base + t,
             jnp.concatenate([s[:QT] + dbias_ref[hh], s[QT:] + dbias_ref[hh]], axis=0))

    col = lax.broadcasted_iota(jnp.int32, (1, QT), 1)

    def for_tile_groups(n, group):
        def trip(i, carry):
            group([TILE_UNROLL * i + u for u in range(TILE_UNROLL)])
            return carry

        lax.fori_loop(0, n // TILE_UNROLL, trip, 0)
        done = (n // TILE_UNROLL) * TILE_UNROLL
        size = TILE_UNROLL // 2
        while size >= 1:
            @pl.when((n & size) != 0)
            def _(done=done, size=size):
                group([done + u for u in range(size)])
            done = done + (n & size)
            size //= 2

    def off_diag_loops(x, t, base):
        def off_diag(js):
            for j in js:
                rel = (col + (j - t) * QT).astype(F32)
                for hh in range(hb):
                    keep(x, hh, base + j, frame_scores(x, hh, j) + slope2(hh) * rel)

        for_tile_groups(t, off_diag)

    def row_max(x, hh):
        mrow_ref[x, hh] = jnp.broadcast_to(
            jnp.max(mrun_ref[x, hh], axis=-1, keepdims=True), (2 * QT, BLK))
        p = jnp.exp2(sm_ref[x, hh] - mrow_ref[x, hh])
        acc_ref[x, hh] = jnp.dot(p.astype(BF16), vmext_ref[hh], preferred_element_type=F32)

    def pv_loops(x, t, base):
        def pv(j, hh):
            start = pl.multiple_of(j * QT, QT)
            s = s_ref[hh, base + j]
            m = mrow_ref[x, hh]
            p = jnp.concatenate([jnp.exp2(s[:, :BLK] - m), jnp.exp2(s[:, BLK:] - m)], axis=1)
            return jnp.dot(p.astype(BF16), vext_ref[hh, pl.ds(start, QT), :],
                           preferred_element_type=F32)

        def pv_group(js):
            for hh in range(hb):
                acc_ref[x, hh] = acc_ref[x, hh] + functools.reduce(
                    lambda a, b: a + b, [pv(j, hh) for j in js])

        for_tile_groups(t + 1, pv_group)

    lam = _lambda(lq1_ref, lk1_ref, lq2_ref, lk2_ref, lam_init)

    def finish(x, hh):
        gate_ref = tiles[x][3]
        cols = head_cols(hh)
        acc = acc_ref[x, hh]
        a = acc[:, :HEAD_DIM] / acc[:, HEAD_DIM:]
        d = a[:QT] - lam * a[QT:]
        y = d * lax.rsqrt(jnp.mean(d * d, axis=-1, keepdims=True) + EPS)
        y = y * (gn_ref[:, cols] * (1.0 - lam_init))
        o_ref[x * QT:(x + 1) * QT, cols] = (
            y * _silu(gate_ref[:, cols].astype(F32))).astype(BF16)

    (ta, base_a, _, _), (tb, base_b, _, _) = tiles
    for hh in range(hb):
        first_scores(0, hh)
    off_diag_loops(0, ta, base_a)
    for hh in range(hb):
        first_scores(1, hh)
        row_max(0, hh)
    pv_loops(0, ta, base_a)
    off_diag_loops(1, tb, base_b)
    for hh in range(hb):
        row_max(1, hh)
        finish(0, hh)
    pv_loops(1, tb, base_b)
    for hh in range(hb):
        finish(1, hh)


def _diffattn(proj, pm, gn, lams, layer, batch, seq, lam_init, casts):
    m = proj.shape[0]
    ntile = seq // QT
    hb = HEADS_PER_STEP
    ngrp = HEADS // hb
    width = hb * HEAD_DIM
    slopes = jnp.asarray(2.0 ** (-8.0 * np.arange(1, HEADS + 1) / HEADS), F32)
    qcol, kcol, vcol, gcol = (4 * ngrp, 5 * ngrp, 6 * ngrp, 7 * ngrp)
    vec = pl.BlockSpec((None, 1, DIFF_DH), lambda b, g, t: (layer, 0, 0))
    tile_a = lambda b, t: b * ntile + t
    tile_b = lambda b, t: b * ntile + ntile - 1 - t
    nstep = batch * ngrp * (ntile // 2)
    linear = lambda b, g, t: (b * ngrp + g) * (ntile // 2) + t
    cast_in, cast_out_specs, cast_out_shapes, cast_args = [], [], [], []
    for w, wl in casts:
        rows, ncol = w.shape[1:]
        slab = rows // nstep
        cast_in.append(pl.BlockSpec((None, slab, ncol),
                                    lambda b, g, t, wl=wl: (wl, linear(b, g, t), 0)))
        cast_out_specs.append(pl.BlockSpec((slab, ncol), lambda b, g, t: (linear(b, g, t), 0)))
        cast_out_shapes.append(jax.ShapeDtypeStruct((rows, ncol), BF16))
        cast_args.append(w)
    outs = pl.pallas_call(
        functools.partial(_diffattn_kernel, lam_init=lam_init, ncast=len(casts)),
        out_shape=[jax.ShapeDtypeStruct((m, MIX_HALF), BF16)] + cast_out_shapes,
        grid=(batch, ngrp, ntile // 2),
        in_specs=[
            pl.BlockSpec(memory_space=pltpu.SMEM),
            pl.BlockSpec((QT, width), lambda b, g, t: (tile_a(b, t), qcol + g)),
            pl.BlockSpec((QT, width), lambda b, g, t: (tile_b(b, t), qcol + g)),
            pl.BlockSpec((seq, width), lambda b, g, t: (b, kcol + g)),
            pl.BlockSpec((seq, width), lambda b, g, t: (b, vcol + g)),
            pl.BlockSpec((None, N_META, width), lambda b, g, t: (0, 0, kcol + g)),
            pl.BlockSpec((None, N_META, width), lambda b, g, t: (0, 0, vcol + g)),
            pl.BlockSpec((QT, width), lambda b, g, t: (tile_a(b, t), gcol + g)),
            pl.BlockSpec((QT, width), lambda b, g, t: (tile_b(b, t), gcol + g)),
            pl.BlockSpec((None, 1, width), lambda b, g, t: (layer, 0, g)),
            vec, vec, vec, vec,
        ] + cast_in,
        out_specs=[pl.BlockSpec((2 * QT, width), lambda b, g, t: (b * (ntile // 2) + t, g))]
        + cast_out_specs,
        scratch_shapes=[
            pltpu.VMEM((2, hb, 2 * QT, HEAD_DIM), BF16),
            pltpu.VMEM((hb, ntile + 1, 2 * QT, QT), F32),
            pltpu.VMEM((2, hb, 2 * QT, BLK), F32),
            pltpu.VMEM((2, hb, 2 * QT, BLK), F32),
            pltpu.VMEM((2, hb, 2 * QT, BLK), F32),
            pltpu.VMEM((2, hb, 2 * QT, 2 * HEAD_DIM), F32),
            pltpu.VMEM((hb, seq, 2 * HEAD_DIM), BF16),
            pltpu.VMEM((hb, BLK, HEAD_DIM), BF16),
            pltpu.VMEM((hb, BLK, 2 * HEAD_DIM), BF16),
            pltpu.VMEM((hb, QT, QT), F32),
        ],
        compiler_params=pltpu.CompilerParams(
            dimension_semantics=("parallel", "parallel", "arbitrary"),
            vmem_limit_bytes=VMEM_LIMIT),
        name="diffattn",
    )(slopes, proj, proj, proj, proj, pm, pm, proj, proj, gn, *lams, *cast_args)
    return outs[0], outs[1:]


def _mix_outproj_kernel(q_ref, k_ref, v_ref, gate_ref, km_ref, vm_ref, gret_ref,
                        da_ref, db_ref, wr_ref, wd_ref, h_ref, gf_ref, o_ref,
                        r_ref, state_ref, init_ref, dmat_ref, rq_ref, rk_ref,
                        *, final_norm, tiles_per_seq):
    i = pl.program_id(0)
    scale = HEAD_DIM ** -0.5
    tm = q_ref.shape[0]

    @pl.when(i == 0)
    def _():
        r_ref[...] = jnp.zeros_like(r_ref)
        state_ref[...] = jnp.zeros_like(state_ref)
        t = lax.broadcasted_iota(jnp.int32, (BLK, BLK), 0)
        s = lax.broadcasted_iota(jnp.int32, (BLK, BLK), 1)
        dist = jnp.abs(t - s).astype(F32)
        allowed = (s // CHUNK) <= (t // CHUNK)
        tf = t.astype(F32)
        mf = lax.broadcasted_iota(jnp.int32, (N_META, HEAD_DIM), 0).astype(F32)
        for h in range(HEADS):
            cols = slice(h * HEAD_DIM, (h + 1) * HEAD_DIM)
            log_g = _log_g(h)
            dmat_ref[h] = jnp.where(allowed, jnp.exp(log_g * dist) * scale, 0.0)
            rq_ref[h] = jnp.exp(log_g * (tf + 1.0))
            rk_ref[h] = jnp.exp(log_g * (BLK - 1.0 - tf)) * scale
            zeta = jnp.exp(log_g * (N_META - 1.0 - mf)) * scale
            kz = (km_ref[:, cols].astype(F32) * zeta).astype(BF16)
            init_ref[h] = lax.dot_general(_pad_rows(kz, BLK), _pad_rows(vm_ref[:, cols], BLK),
                                          TN, preferred_element_type=F32)

    r = r_ref[...]
    for half, d_ref in enumerate((da_ref, db_ref)):
        rows = slice(half * QT, (half + 1) * QT)
        y = jnp.dot(r[rows], wr_ref[...], preferred_element_type=F32)
        y = y + jnp.dot(d_ref[...], wd_ref[...], preferred_element_type=F32)
        hn = h_ref[rows] + y
        if final_norm:
            ms = jnp.mean(hn * hn, axis=-1, keepdims=True)
            hn = hn * lax.rsqrt(ms + EPS) * gf_ref[...]
        o_ref[rows] = hn

    first = (i % tiles_per_seq) == 0
    for blk in range(tm // BLK):
        rows = slice(blk * BLK, (blk + 1) * BLK)
        for h in range(HEADS):
            cols = slice(h * HEAD_DIM, (h + 1) * HEAD_DIM)
            decay_blk = float((1.0 - 2.0 ** (-5.0 - h)) ** BLK)
            q = q_ref[rows, cols]
            k = k_ref[rows, cols]
            v = v_ref[rows, cols]
            s = lax.dot_general(q, k, NT, preferred_element_type=F32) * dmat_ref[h]
            intra = jnp.dot(s.astype(BF16), v, preferred_element_type=F32)
            state = state_ref[h]
            if blk == 0:
                state = jnp.where(first, init_ref[h], state)
            qx = (q.astype(F32) * rq_ref[h]).astype(BF16)
            cross = jnp.dot(qx, state.astype(BF16), preferred_element_type=F32)
            kz = (k.astype(F32) * rk_ref[h]).astype(BF16)
            kv = lax.dot_general(kz, v, TN, preferred_element_type=F32)
            state_ref[h] = decay_blk * state + kv

            o = intra + cross
            mu = jnp.mean(o, axis=-1, keepdims=True)
            oc = o - mu
            var = jnp.mean(oc * oc, axis=-1, keepdims=True)
            y = oc * lax.rsqrt(var + EPS) * gret_ref[:, cols]
            r_ref[rows, cols] = (y * _silu(gate_ref[rows, cols].astype(F32))).astype(BF16)


def _mix_outproj(proj, pm, d, w, h, gret, gf, layer, seq, final_norm):
    m, dm = h.shape
    tm = 2 * QT
    ntiles = m // tm
    ntile_q = seq // QT
    cur = lambda i: jnp.minimum(i, ntiles - 1)
    prev = lambda i: jnp.maximum(i - 1, 0)

    def d_block(which):
        def index(i):
            tile = 2 * prev(i) + which
            b, t = tile // ntile_q, tile % ntile_q
            pos = jnp.where(t < ntile_q // 2, 2 * t, 2 * (ntile_q - 1 - t) + 1)
            return b * ntile_q + pos, 0
        return index

    return pl.pallas_call(
        functools.partial(_mix_outproj_kernel, final_norm=final_norm,
                          tiles_per_seq=seq // tm),
        out_shape=jax.ShapeDtypeStruct((m, dm), F32),
        grid=(ntiles + 1,),
        in_specs=[
            pl.BlockSpec((tm, MIX_HALF), lambda i: (cur(i), 0)),
            pl.BlockSpec((tm, MIX_HALF), lambda i: (cur(i), 1)),
            pl.BlockSpec((tm, MIX_HALF), lambda i: (cur(i), 2)),
            pl.BlockSpec((tm, MIX_HALF), lambda i: (cur(i), 3)),
            pl.BlockSpec((None, N_META, MIX_HALF), lambda i: (0, 0, 1)),
            pl.BlockSpec((None, N_META, MIX_HALF), lambda i: (0, 0, 2)),
            pl.BlockSpec((None, 1, MIX_HALF), lambda i: (layer, 0, 0)),
            pl.BlockSpec((QT, MIX_HALF), d_block(0)),
            pl.BlockSpec((QT, MIX_HALF), d_block(1)),
            pl.BlockSpec((MIX_HALF, dm), lambda i: (0, 0), pipeline_mode=pl.Buffered(1)),
            pl.BlockSpec((MIX_HALF, dm), lambda i: (1, 0), pipeline_mode=pl.Buffered(1)),
            pl.BlockSpec((tm, dm), lambda i: (prev(i), 0)),
            pl.BlockSpec((1, dm), lambda i: (0, 0)),
        ],
        out_specs=pl.BlockSpec((tm, dm), lambda i: (prev(i), 0)),
        scratch_shapes=[
            pltpu.VMEM((tm, MIX_HALF), BF16),
            pltpu.VMEM((HEADS, HEAD_DIM, HEAD_DIM), F32),
            pltpu.VMEM((HEADS, HEAD_DIM, HEAD_DIM), F32),
            pltpu.VMEM((HEADS, BLK, BLK), F32),
            pltpu.VMEM((HEADS, BLK, HEAD_DIM), F32),
            pltpu.VMEM((HEADS, BLK, HEAD_DIM), F32),
        ],
        compiler_params=pltpu.CompilerParams(
            dimension_semantics=("arbitrary",),
            vmem_limit_bytes=VMEM_LIMIT),
        name="mix_outproj",
    )(proj, proj, proj, proj, pm, pm, gret, d, d, w, w, h, gf)


def kernel(x, meta_tokens, norm_g, w_in, w_out, ret_norm_g, diff_norm_g,
           lambda_q1, lambda_k1, lambda_q2, lambda_k2, final_norm_g):
    batch, seq, dm = x.shape
    depth = w_in.shape[0]
    m = batch * seq
    assert seq % QT == 0 and meta_tokens.shape[0] == N_META

    h = x.reshape(m, dm)
    hm = meta_tokens.astype(x.dtype)
    gf = final_norm_g.reshape(1, dm)
    per_layer = lambda a: a.reshape(depth, 1, a.shape[-1])
    norm_g, ret_norm_g, diff_norm_g = map(per_layer, (norm_g, ret_norm_g, diff_norm_g))
    lams = tuple(map(per_layer, (lambda_q1, lambda_k1, lambda_q2, lambda_k2)))

    tm = m // 8
    w_in_b = None
    for i in range(depth):
        last = i == depth - 1
        lam_init = 0.8 - 0.6 * float(np.exp(-0.3 * i))
        if w_in_b is None:
            proj, pm, w_in_b = _inproj(h, hm, norm_g, w_in, i, tm, 1024, tiles=(0, 1),
                                       emit_w=True)
            (proj,) = _inproj(h, None, norm_g, w_in_b, i, tm, 2048, tiles=(1, m // tm - 1),
                              into=proj)
        else:
            proj, pm = _inproj(h, hm, norm_g, w_in_b, i, tm, 2048)
        if not last:
            hm = _meta_layer(pm, hm, w_out, ret_norm_g, diff_norm_g, lams, i, lam_init)
        casts = [(w_out, i)] + ([] if last else [(w_in, i + 1)])
        d, cast = _diffattn(proj, pm, diff_norm_g, lams, i, batch, seq, lam_init, casts)
        w_in_b = None if last else cast[1]
        h = _mix_outproj(proj, pm, d, cast[0], h, ret_norm_g, gf, i, seq, final_norm=last)

    return h.reshape(batch, seq, dm)
```

```python
import functools

import numpy as np
import jax
import jax.numpy as jnp
from jax import lax
from jax.experimental import pallas as pl
from jax.experimental.pallas import tpu as pltpu

F32 = jnp.float32
BF16 = jnp.bfloat16

CHUNK = 64
N_META = 16
EPS = 1e-6
HEADS = 8
HEAD_DIM = 128
DIFF_DH = 64
MIX_HALF = HEADS * HEAD_DIM
BLK = 128
QT = 256
NEG = -1e30
LOG2E = 1.4426950408889634
HEADS_PER_STEP = 4
TILE_UNROLL = 4

VMEM_LIMIT = 56 * 1024 * 1024

NT = (((1,), (1,)), ((), ()))
TN = (((0,), (0,)), ((), ()))


def _log_g(h):
    return float(np.log(1.0 - 2.0 ** (-5.0 - h)))


def _silu(x):
    return x * (1.0 / (1.0 + jnp.exp(-x)))


def _lambda(lq1_ref, lk1_ref, lq2_ref, lk2_ref, lam_init):
    return (jnp.exp(jnp.sum(lq1_ref[...] * lk1_ref[...], axis=-1, keepdims=True))
            - jnp.exp(jnp.sum(lq2_ref[...] * lk2_ref[...], axis=-1, keepdims=True))
            + lam_init)


def _pad_rows(a, rows):
    return jnp.concatenate([a, jnp.zeros((rows - a.shape[0], a.shape[1]), a.dtype)], axis=0)


def _split_components(q):
    lane = lax.broadcasted_iota(jnp.int32, q.shape, 1)
    zero = jnp.zeros((), BF16)
    qscaled = q * jnp.asarray(DIFF_DH ** -0.5, BF16)
    return jnp.concatenate([jnp.where(lane < DIFF_DH, qscaled, zero),
                            jnp.where(lane >= DIFF_DH, qscaled, zero)], axis=0)


def _rms_norm_bf16(x, g):
    ms = jnp.mean(x * x, axis=-1, keepdims=True)
    return (x * lax.rsqrt(ms + EPS) * g).astype(BF16)


def _inproj_kernel(h_ref, hm_ref, g_ref, w_ref, o_ref, pm_ref, u_ref):
    tm = h_ref.shape[0]

    @pl.when(pl.program_id(1) == 0)
    def _():
        u_ref[:tm] = _rms_norm_bf16(h_ref[...], g_ref[...])
        u_ref[tm:] = _rms_norm_bf16(hm_ref[...], g_ref[...])

    out = jnp.dot(u_ref[...], w_ref[...].astype(BF16),
                  preferred_element_type=F32).astype(BF16)
    o_ref[...] = out[:tm]
    pm_ref[...] = out[tm:]


def _inproj(h, hm, g, w, layer, tm, tn):
    m, d = h.shape
    n = w.shape[-1]
    if w.ndim == 3:
        w_spec = pl.BlockSpec((None, d, tn), lambda i, j: (layer, 0, j))
    else:
        w_spec = pl.BlockSpec((d, tn), lambda i, j: (0, j))
    return pl.pallas_call(
        _inproj_kernel,
        out_shape=[jax.ShapeDtypeStruct((m, n), BF16),
                   jax.ShapeDtypeStruct((m // tm, N_META, n), BF16)],
        grid=(m // tm, n // tn),
        in_specs=[
            pl.BlockSpec((tm, d), lambda i, j: (i, 0)),
            pl.BlockSpec((N_META, d), lambda i, j: (0, 0)),
            pl.BlockSpec((None, 1, d), lambda i, j: (layer, 0, 0)),
            w_spec,
        ],
        out_specs=[pl.BlockSpec((tm, tn), lambda i, j: (i, j)),
                   pl.BlockSpec((None, N_META, tn), lambda i, j: (i, 0, j))],
        scratch_shapes=[pltpu.VMEM((tm + N_META, d), BF16)],
        compiler_params=pltpu.CompilerParams(
            dimension_semantics=("parallel", "arbitrary"),
            vmem_limit_bytes=VMEM_LIMIT),
        name="inproj",
    )(h, hm, g, w)


def _meta_kernel(pm_ref, hm_ref, wr_ref, wd_ref, gret_ref, gdiff_ref,
                 lq1_ref, lk1_ref, lq2_ref, lk2_ref, hm_out_ref, *, lam_init):
    section = lambda s: pm_ref[:, s * MIX_HALF:(s + 1) * MIX_HALF]
    rq, rk, rv, rgate, dq, dk, dv, dgate = [section(s) for s in range(8)]
    t = lax.broadcasted_iota(jnp.int32, (BLK, BLK), 0)
    s_i = lax.broadcasted_iota(jnp.int32, (BLK, BLK), 1)
    dist = jnp.abs(t - s_i).astype(F32)
    r2 = lax.broadcasted_iota(jnp.int32, (2 * BLK, BLK), 0) % BLK
    c2 = lax.broadcasted_iota(jnp.int32, (2 * BLK, BLK), 1)
    dist2 = jnp.abs(r2 - c2).astype(F32)
    lam = _lambda(lq1_ref, lk1_ref, lq2_ref, lk2_ref, lam_init)
    r_parts, d_parts = [], []
    for h in range(HEADS):
        cols = slice(h * HEAD_DIM, (h + 1) * HEAD_DIM)
        q = _pad_rows(rq[:, cols], BLK)
        k = _pad_rows(rk[:, cols], BLK)
        v = _pad_rows(rv[:, cols], BLK)
        dmat = jnp.exp(_log_g(h) * dist) * (HEAD_DIM ** -0.5)
        s = lax.dot_general(q, k, NT, preferred_element_type=F32) * dmat
        o = jnp.dot(s.astype(BF16), v, preferred_element_type=F32)[:N_META]
        mu = jnp.mean(o, axis=-1, keepdims=True)
        oc = o - mu
        var = jnp.mean(oc * oc, axis=-1, keepdims=True)
        y = oc * lax.rsqrt(var + EPS) * gret_ref[:, cols]
        r_parts.append((y * _silu(rgate[:, cols].astype(F32))).astype(BF16))
        slope = 2.0 ** (-(h + 1.0))
        qs = _split_components(_pad_rows(dq[:, cols], BLK))
        k = _pad_rows(dk[:, cols], BLK)
        v = _pad_rows(dv[:, cols], BLK)
        bias = jnp.where(c2 < N_META, -slope * dist2, NEG)
        s = lax.dot_general(qs, k, NT, preferred_element_type=F32) + bias
        pr = jnp.exp(s - jnp.max(s, axis=-1, keepdims=True))
        a = (jnp.dot(pr.astype(BF16), v, preferred_element_type=F32)
             / jnp.sum(pr, axis=-1, keepdims=True))
        d = a[:N_META] - lam * a[BLK:BLK + N_META]
        y = d * lax.rsqrt(jnp.mean(d * d, axis=-1, keepdims=True) + EPS)
        y = y * gdiff_ref[:, cols] * (1.0 - lam_init)
        d_parts.append((y * _silu(dgate[:, cols].astype(F32))).astype(BF16))
    r = jnp.concatenate(r_parts, axis=1)
    d = jnp.concatenate(d_parts, axis=1)
    y = jnp.dot(r, wr_ref[...].astype(BF16), preferred_element_type=F32)
    y = y + jnp.dot(d, wd_ref[...].astype(BF16), preferred_element_type=F32)
    hm_out_ref[...] = hm_ref[...] + y


def _meta_layer(pm, hm, w_out, gret, gdiff, lams, layer, lam_init):
    dm = hm.shape[1]
    n = pm.shape[-1]
    vec = lambda width: pl.BlockSpec((None, 1, width), lambda i: (layer, 0, 0))
    return pl.pallas_call(
        functools.partial(_meta_kernel, lam_init=lam_init),
        out_shape=jax.ShapeDtypeStruct((N_META, dm), F32),
        grid=(1,),
        in_specs=[
            pl.BlockSpec((None, N_META, n), lambda i: (0, 0, 0)),
            pl.BlockSpec((N_META, dm), lambda i: (0, 0)),
            pl.BlockSpec((None, MIX_HALF, dm), lambda i: (layer, 0, 0),
                         pipeline_mode=pl.Buffered(1)),
            pl.BlockSpec((None, MIX_HALF, dm), lambda i: (layer, 1, 0),
                         pipeline_mode=pl.Buffered(1)),
            vec(MIX_HALF), vec(MIX_HALF),
            vec(DIFF_DH), vec(DIFF_DH), vec(DIFF_DH), vec(DIFF_DH),
        ],
        out_specs=pl.BlockSpec((N_META, dm), lambda i: (0, 0)),
        compiler_params=pltpu.CompilerParams(
            dimension_semantics=("arbitrary",),
            vmem_limit_bytes=VMEM_LIMIT),
        name="meta_layer",
    )(pm, hm, w_out, w_out, gret, gdiff, *lams)


def _diffattn_kernel(*refs, lam_init, ncast):
    (slopes_ref, qa_ref, qb_ref, k_ref, v_ref, km_ref, vm_ref, gatea_ref, gateb_ref, gn_ref,
     lq1_ref, lk1_ref, lq2_ref, lk2_ref) = refs[:14]
    cast_src = refs[14:14 + ncast]
    o_ref = refs[14 + ncast]
    cast_dst = refs[15 + ncast:15 + 2 * ncast]
    (qs_ref, s_ref, sm_ref, mrun_ref, mrow_ref, acc_ref,
     vext_ref, kmext_ref, vmext_ref, dbias_ref) = refs[15 + 2 * ncast:]

    for src, dst in zip(cast_src, cast_dst):
        dst[...] = src[...].astype(BF16)

    grp = pl.program_id(1)
    step = pl.program_id(2)
    hb = HEADS_PER_STEP
    seq = k_ref.shape[0]
    ntile = seq // QT
    head_cols = lambda hh: slice(hh * HEAD_DIM, (hh + 1) * HEAD_DIM)
    slope2 = lambda hh: slopes_ref[grp * hb + hh] * LOG2E
    tiles = ((step, 0, qa_ref, gatea_ref),
             (ntile - 1 - step, step + 1, qb_ref, gateb_ref))

    @pl.when(step == 0)
    def _():
        r = lax.broadcasted_iota(jnp.int32, (QT, QT), 0)
        c = lax.broadcasted_iota(jnp.int32, (QT, QT), 1)
        allowed = (c // CHUNK) <= (r // CHUNK)
        rel = (r - jnp.abs(r - c)).astype(F32)
        for hh in range(hb):
            cols = head_cols(hh)
            dbias_ref[hh] = jnp.where(allowed, slope2(hh) * rel, NEG)
            vext_ref[hh, :, :HEAD_DIM] = v_ref[:, cols]
            vext_ref[hh, :, HEAD_DIM:] = jnp.ones((seq, HEAD_DIM), BF16)
            kmext_ref[hh] = _pad_rows(km_ref[:, cols], BLK)
            vmext_ref[hh, :, :HEAD_DIM] = _pad_rows(vm_ref[:, cols], BLK)
            vmext_ref[hh, :, HEAD_DIM:] = jnp.ones((BLK, HEAD_DIM), BF16)

    def frame_scores(x, hh, j):
        start = pl.multiple_of(j * QT, QT)
        kt = k_ref[pl.ds(start, QT), head_cols(hh)]
        return lax.dot_general(qs_ref[x, hh], kt, NT, preferred_element_type=F32) * LOG2E

    def keep(x, hh, slot, s):
        s_ref[hh, slot] = s
        mrun_ref[x, hh] = jnp.maximum(mrun_ref[x, hh], jnp.maximum(s[:, :BLK], s[:, BLK:]))

    colm = lax.broadcasted_iota(jnp.int32, (1, BLK), 1)

    def first_scores(x, hh):
        t, base, q_ref, _ = tiles[x]
        relm = (colm - N_META - t * QT).astype(F32)
        qs_ref[x, hh] = _split_components(q_ref[:, head_cols(hh)])
        s = lax.dot_general(qs_ref[x, hh], kmext_ref[hh], NT, preferred_element_type=F32)
        s = s * LOG2E + jnp.where(colm < N_META, slope2(hh) * relm, NEG)
        sm_ref[x, hh] = s
        mrun_ref[x, hh] = s
        s = frame_scores(x, hh, t)
        keep(x, hh, base + t,
             jnp.concatenate([s[:QT] + dbias_ref[hh], s[QT:] + dbias_ref[hh]], axis=0))

    col = lax.broadcasted_iota(jnp.int32, (1, QT), 1)

    def for_tile_groups(n, group):
        def trip(i, carry):
            group([TILE_UNROLL * i + u for u in range(TILE_UNROLL)])
            return carry

        lax.fori_loop(0, n // TILE_UNROLL, trip, 0)
        done = (n // TILE_UNROLL) * TILE_UNROLL
        size = TILE_UNROLL // 2
        while size >= 1:
            @pl.when((n & size) != 0)
            def _(done=done, size=size):
                group([done + u for u in range(size)])
            done = done + (n & size)
            size //= 2

    def off_diag_loops(x, t, base):
        def off_diag(js):
            for hh in range(hb):
                part = None
                for j in js:
                    rel = (col + (j - t) * QT).astype(F32)
                    s = frame_scores(x, hh, j) + slope2(hh) * rel
                    s_ref[hh, base + j] = s
                    m = jnp.maximum(s[:, :BLK], s[:, BLK:])
                    part = m if part is None else jnp.maximum(part, m)
                mrun_ref[x, hh] = jnp.maximum(mrun_ref[x, hh], part)

        for_tile_groups(t, off_diag)

    def row_max(x, hh):
        mrow_ref[x, hh] = jnp.broadcast_to(
            jnp.max(mrun_ref[x, hh], axis=-1, keepdims=True), (2 * QT, BLK))
        p = jnp.exp2(sm_ref[x, hh] - mrow_ref[x, hh])
        acc_ref[x, hh] = jnp.dot(p.astype(BF16), vmext_ref[hh], preferred_element_type=F32)

    def pv_loops(x, t, base):
        def pv(j, hh):
            start = pl.multiple_of(j * QT, QT)
            s = s_ref[hh, base + j]
            m = mrow_ref[x, hh]
            p = jnp.concatenate([jnp.exp2(s[:, :BLK] - m), jnp.exp2(s[:, BLK:] - m)], axis=1)
            return jnp.dot(p.astype(BF16), vext_ref[hh, pl.ds(start, QT), :],
                           preferred_element_type=F32)

        def pv_group(js):
            for hh in range(hb):
                acc_ref[x, hh] = acc_ref[x, hh] + functools.reduce(
                    lambda a, b: a + b, [pv(j, hh) for j in js])

        for_tile_groups(t + 1, pv_group)

    lam = _lambda(lq1_ref, lk1_ref, lq2_ref, lk2_ref, lam_init)

    def finish(x, hh):
        gate_ref = tiles[x][3]
        cols = head_cols(hh)
        acc = acc_ref[x, hh]
        a = acc[:, :HEAD_DIM] / acc[:, HEAD_DIM:]
        d = a[:QT] - lam * a[QT:]
        y = d * lax.rsqrt(jnp.mean(d * d, axis=-1, keepdims=True) + EPS)
        y = y * (gn_ref[:, cols] * (1.0 - lam_init))
        o_ref[x * QT:(x + 1) * QT, cols] = (
            y * _silu(gate_ref[:, cols].astype(F32))).astype(BF16)

    (ta, base_a, _, _), (tb, base_b, _, _) = tiles
    for hh in range(hb):
        first_scores(0, hh)
    off_diag_loops(0, ta, base_a)
    for hh in range(hb):
        first_scores(1, hh)
        row_max(0, hh)
    pv_loops(0, ta, base_a)
    off_diag_loops(1, tb, base_b)
    for hh in range(hb):
        row_max(1, hh)
        finish(0, hh)
    pv_loops(1, tb, base_b)
    for hh in range(hb):
        finish(1, hh)


def _diffattn(proj, pm, gn, lams, layer, batch, seq, lam_init, casts):
    m = proj.shape[0]
    ntile = seq // QT
    hb = HEADS_PER_STEP
    ngrp = HEADS // hb
    width = hb * HEAD_DIM
    slopes = jnp.asarray(2.0 ** (-8.0 * np.arange(1, HEADS + 1) / HEADS), F32)
    qcol, kcol, vcol, gcol = (4 * ngrp, 5 * ngrp, 6 * ngrp, 7 * ngrp)
    vec = pl.BlockSpec((None, 1, DIFF_DH), lambda b, g, t: (layer, 0, 0))
    tile_a = lambda b, t: b * ntile + t
    tile_b = lambda b, t: b * ntile + ntile - 1 - t
    nstep = batch * ngrp * (ntile // 2)
    linear = lambda b, g, t: (b * ngrp + g) * (ntile // 2) + t
    cast_in, cast_out_specs, cast_out_shapes, cast_args = [], [], [], []
    for w, wl in casts:
        rows, ncol = w.shape[1:]
        slab = rows // nstep
        cast_in.append(pl.BlockSpec((None, slab, ncol),
                                    lambda b, g, t, wl=wl: (wl, linear(b, g, t), 0)))
        cast_out_specs.append(pl.BlockSpec((slab, ncol), lambda b, g, t: (linear(b, g, t), 0)))
        cast_out_shapes.append(jax.ShapeDtypeStruct((rows, ncol), BF16))
        cast_args.append(w)
    outs = pl.pallas_call(
        functools.partial(_diffattn_kernel, lam_init=lam_init, ncast=len(casts)),
        out_shape=[jax.ShapeDtypeStruct((m, MIX_HALF), BF16)] + cast_out_shapes,
        grid=(batch, ngrp, ntile // 2),
        in_specs=[
            pl.BlockSpec(memory_space=pltpu.SMEM),
            pl.BlockSpec((QT, width), lambda b, g, t: (tile_a(b, t), qcol + g)),
            pl.BlockSpec((QT, width), lambda b, g, t: (tile_b(b, t), qcol + g)),
            pl.BlockSpec((seq, width), lambda b, g, t: (b, kcol + g)),
            pl.BlockSpec((seq, width), lambda b, g, t: (b, vcol + g)),
            pl.BlockSpec((None, N_META, width), lambda b, g, t: (0, 0, kcol + g)),
            pl.BlockSpec((None, N_META, width), lambda b, g, t: (0, 0, vcol + g)),
            pl.BlockSpec((QT, width), lambda b, g, t: (tile_a(b, t), gcol + g)),
            pl.BlockSpec((QT, width), lambda b, g, t: (tile_b(b, t), gcol + g)),
            pl.BlockSpec((None, 1, width), lambda b, g, t: (layer, 0, g)),
            vec, vec, vec, vec,
        ] + cast_in,
        out_specs=[pl.BlockSpec((2 * QT, width), lambda b, g, t: (b * (ntile // 2) + t, g))]
        + cast_out_specs,
        scratch_shapes=[
            pltpu.VMEM((2, hb, 2 * QT, HEAD_DIM), BF16),
            pltpu.VMEM((hb, ntile + 1, 2 * QT, QT), F32),
            pltpu.VMEM((2, hb, 2 * QT, BLK), F32),
            pltpu.VMEM((2, hb, 2 * QT, BLK), F32),
            pltpu.VMEM((2, hb, 2 * QT, BLK), F32),
            pltpu.VMEM((2, hb, 2 * QT, 2 * HEAD_DIM), F32),
            pltpu.VMEM((hb, seq, 2 * HEAD_DIM), BF16),
            pltpu.VMEM((hb, BLK, HEAD_DIM), BF16),
            pltpu.VMEM((hb, BLK, 2 * HEAD_DIM), BF16),
            pltpu.VMEM((hb, QT, QT), F32),
        ],
        compiler_params=pltpu.CompilerParams(
            dimension_semantics=("parallel", "parallel", "arbitrary"),
            vmem_limit_bytes=VMEM_LIMIT),
        name="diffattn",
    )(slopes, proj, proj, proj, proj, pm, pm, proj, proj, gn, *lams, *cast_args)
    return outs[0], outs[1:]


def _mix_outproj_kernel(q_ref, k_ref, v_ref, gate_ref, km_ref, vm_ref, gret_ref,
                        da_ref, db_ref, wr_ref, wd_ref, h_ref, gf_ref, o_ref,
                        r_ref, state_ref, init_ref, dmat_ref, rq_ref, rk_ref,
                        *, final_norm, tiles_per_seq):
    i = pl.program_id(0)
    scale = HEAD_DIM ** -0.5
    tm = q_ref.shape[0]

    @pl.when(i == 0)
    def _():
        r_ref[...] = jnp.zeros_like(r_ref)
        state_ref[...] = jnp.zeros_like(state_ref)
        t = lax.broadcasted_iota(jnp.int32, (BLK, BLK), 0)
        s = lax.broadcasted_iota(jnp.int32, (BLK, BLK), 1)
        dist = jnp.abs(t - s).astype(F32)
        allowed = (s // CHUNK) <= (t // CHUNK)
        tf = t.astype(F32)
        mf = lax.broadcasted_iota(jnp.int32, (N_META, HEAD_DIM), 0).astype(F32)
        for h in range(HEADS):
            cols = slice(h * HEAD_DIM, (h + 1) * HEAD_DIM)
            log_g = _log_g(h)
            dmat_ref[h] = jnp.where(allowed, jnp.exp(log_g * dist) * scale, 0.0)
            rq_ref[h] = jnp.exp(log_g * (tf + 1.0))
            rk_ref[h] = jnp.exp(log_g * (BLK - 1.0 - tf)) * scale
            zeta = jnp.exp(log_g * (N_META - 1.0 - mf)) * scale
            kz = (km_ref[:, cols].astype(F32) * zeta).astype(BF16)
            init_ref[h] = lax.dot_general(_pad_rows(kz, BLK), _pad_rows(vm_ref[:, cols], BLK),
                                          TN, preferred_element_type=F32)

    r = r_ref[...]
    for half, d_ref in enumerate((da_ref, db_ref)):
        rows = slice(half * QT, (half + 1) * QT)
        y = jnp.dot(r[rows], wr_ref[...], preferred_element_type=F32)
        y = y + jnp.dot(d_ref[...], wd_ref[...], preferred_element_type=F32)
        hn = h_ref[rows] + y
        if final_norm:
            ms = jnp.mean(hn * hn, axis=-1, keepdims=True)
            hn = hn * lax.rsqrt(ms + EPS) * gf_ref[...]
        o_ref[rows] = hn

    first = (i % tiles_per_seq) == 0
    for blk in range(tm // BLK):
        rows = slice(blk * BLK, (blk + 1) * BLK)
        for h in range(HEADS):
            cols = slice(h * HEAD_DIM, (h + 1) * HEAD_DIM)
            decay_blk = float((1.0 - 2.0 ** (-5.0 - h)) ** BLK)
            q = q_ref[rows, cols]
            k = k_ref[rows, cols]
            v = v_ref[rows, cols]
            s = lax.dot_general(q, k, NT, preferred_element_type=F32) * dmat_ref[h]
            intra = jnp.dot(s.astype(BF16), v, preferred_element_type=F32)
            state = state_ref[h]
            if blk == 0:
                state = jnp.where(first, init_ref[h], state)
            qx = (q.astype(F32) * rq_ref[h]).astype(BF16)
            cross = jnp.dot(qx, state.astype(BF16), preferred_element_type=F32)
            kz = (k.astype(F32) * rk_ref[h]).astype(BF16)
            kv = lax.dot_general(kz, v, TN, preferred_element_type=F32)
            state_ref[h] = decay_blk * state + kv

            o = intra + cross
            mu = jnp.mean(o, axis=-1, keepdims=True)
            oc = o - mu
            var = jnp.mean(oc * oc, axis=-1, keepdims=True)
            y = oc * lax.rsqrt(var + EPS) * gret_ref[:, cols]
            r_ref[rows, cols] = (y * _silu(gate_ref[rows, cols].astype(F32))).astype(BF16)


def _mix_outproj(proj, pm, d, w, h, gret, gf, layer, seq, final_norm):
    m, dm = h.shape
    tm = 2 * QT
    ntiles = m // tm
    ntile_q = seq // QT
    cur = lambda i: jnp.minimum(i, ntiles - 1)
    prev = lambda i: jnp.maximum(i - 1, 0)

    def d_block(which):
        def index(i):
            tile = 2 * prev(i) + which
            b, t = tile // ntile_q, tile % ntile_q
            pos = jnp.where(t < ntile_q // 2, 2 * t, 2 * (ntile_q - 1 - t) + 1)
            return b * ntile_q + pos, 0
        return index

    return pl.pallas_call(
        functools.partial(_mix_outproj_kernel, final_norm=final_norm,
                          tiles_per_seq=seq // tm),
        out_shape=jax.ShapeDtypeStruct((m, dm), F32),
        grid=(ntiles + 1,),
        in_specs=[
            pl.BlockSpec((tm, MIX_HALF), lambda i: (cur(i), 0)),
            pl.BlockSpec((tm, MIX_HALF), lambda i: (cur(i), 1)),
            pl.BlockSpec((tm, MIX_HALF), lambda i: (cur(i), 2)),
            pl.BlockSpec((tm, MIX_HALF), lambda i: (cur(i), 3)),
            pl.BlockSpec((None, N_META, MIX_HALF), lambda i: (0, 0, 1)),
            pl.BlockSpec((None, N_META, MIX_HALF), lambda i: (0, 0, 2)),
            pl.BlockSpec((None, 1, MIX_HALF), lambda i: (layer, 0, 0)),
            pl.BlockSpec((QT, MIX_HALF), d_block(0)),
            pl.BlockSpec((QT, MIX_HALF), d_block(1)),
            pl.BlockSpec((MIX_HALF, dm), lambda i: (0, 0), pipeline_mode=pl.Buffered(1)),
            pl.BlockSpec((MIX_HALF, dm), lambda i: (1, 0), pipeline_mode=pl.Buffered(1)),
            pl.BlockSpec((tm, dm), lambda i: (prev(i), 0)),
            pl.BlockSpec((1, dm), lambda i: (0, 0)),
        ],
        out_specs=pl.BlockSpec((tm, dm), lambda i: (prev(i), 0)),
        scratch_shapes=[
            pltpu.VMEM((tm, MIX_HALF), BF16),
            pltpu.VMEM((HEADS, HEAD_DIM, HEAD_DIM), F32),
            pltpu.VMEM((HEADS, HEAD_DIM, HEAD_DIM), F32),
            pltpu.VMEM((HEADS, BLK, BLK), F32),
            pltpu.VMEM((HEADS, BLK, HEAD_DIM), F32),
            pltpu.VMEM((HEADS, BLK, HEAD_DIM), F32),
        ],
        compiler_params=pltpu.CompilerParams(
            dimension_semantics=("arbitrary",),
            vmem_limit_bytes=VMEM_LIMIT),
        name="mix_outproj",
    )(proj, proj, proj, proj, pm, pm, gret, d, d, w, w, h, gf)


def kernel(x, meta_tokens, norm_g, w_in, w_out, ret_norm_g, diff_norm_g,
           lambda_q1, lambda_k1, lambda_q2, lambda_k2, final_norm_g):
    batch, seq, dm = x.shape
    depth = w_in.shape[0]
    m = batch * seq
    assert seq % QT == 0 and meta_tokens.shape[0] == N_META

    h = x.reshape(m, dm)
    hm = meta_tokens.astype(x.dtype)
    gf = final_norm_g.reshape(1, dm)
    per_layer = lambda a: a.reshape(depth, 1, a.shape[-1])
    norm_g, ret_norm_g, diff_norm_g = map(per_layer, (norm_g, ret_norm_g, diff_norm_g))
    lams = tuple(map(per_layer, (lambda_q1, lambda_k1, lambda_q2, lambda_k2)))

    tm = m // 8
    w_in_b = None
    for i in range(depth):
        last = i == depth - 1
        lam_init = 0.8 - 0.6 * float(np.exp(-0.3 * i))
        if w_in_b is None:
            proj, pm = _inproj(h, hm, norm_g, w_in, i, tm, 1024)
        else:
            proj, pm = _inproj(h, hm, norm_g, w_in_b, i, tm, 2048)
        if not last:
            hm = _meta_layer(pm, hm, w_out, ret_norm_g, diff_norm_g, lams, i, lam_init)
        casts = [(w_out, i)] + ([] if last else [(w_in, i + 1)])
        d, cast = _diffattn(proj, pm, diff_norm_g, lams, i, batch, seq, lam_init, casts)
        w_in_b = None if last else cast[1]
        h = _mix_outproj(proj, pm, d, cast[0], h, ret_norm_g, gf, i, seq, final_norm=last)

    return h.reshape(batch, seq, dm)
```

```python
import functools

import numpy as np
import jax
import jax.numpy as jnp
from jax import lax
from jax.experimental import pallas as pl
from jax.experimental.pallas import tpu as pltpu

F32 = jnp.float32
BF16 = jnp.bfloat16

CHUNK = 64
N_META = 16
EPS = 1e-6
HEADS = 8
HEAD_DIM = 128
DIFF_DH = 64
MIX_HALF = HEADS * HEAD_DIM
BLK = 128
QT = 256
NEG = -1e30
LOG2E = 1.4426950408889634
HEADS_PER_STEP = 4
TILE_UNROLL = 4

VMEM_LIMIT = 56 * 1024 * 1024

NT = (((1,), (1,)), ((), ()))
TN = (((0,), (0,)), ((), ()))


def _log_g(h):
    return float(np.log(1.0 - 2.0 ** (-5.0 - h)))


def _silu(x):
    return x * (1.0 / (1.0 + jnp.exp(-x)))


def _lambda(lq1_ref, lk1_ref, lq2_ref, lk2_ref, lam_init):
    return (jnp.exp(jnp.sum(lq1_ref[...] * lk1_ref[...], axis=-1, keepdims=True))
            - jnp.exp(jnp.sum(lq2_ref[...] * lk2_ref[...], axis=-1, keepdims=True))
            + lam_init)


def _pad_rows(a, rows):
    return jnp.concatenate([a, jnp.zeros((rows - a.shape[0], a.shape[1]), a.dtype)], axis=0)


def _split_components(q):
    lane = lax.broadcasted_iota(jnp.int32, q.shape, 1)
    zero = jnp.zeros((), BF16)
    qscaled = q * jnp.asarray(DIFF_DH ** -0.5, BF16)
    return jnp.concatenate([jnp.where(lane < DIFF_DH, qscaled, zero),
                            jnp.where(lane >= DIFF_DH, qscaled, zero)], axis=0)


def _rms_norm_bf16(x, g):
    ms = jnp.mean(x * x, axis=-1, keepdims=True)
    return (x * lax.rsqrt(ms + EPS) * g).astype(BF16)


def _inproj_kernel(h_ref, hm_ref, g_ref, w_ref, o_ref, pm_ref, u_ref):
    tm = h_ref.shape[0]

    @pl.when(pl.program_id(1) == 0)
    def _():
        u_ref[:tm] = _rms_norm_bf16(h_ref[...], g_ref[...])
        u_ref[tm:] = _rms_norm_bf16(hm_ref[...], g_ref[...])

    out = jnp.dot(u_ref[...], w_ref[...].astype(BF16),
                  preferred_element_type=F32).astype(BF16)
    o_ref[...] = out[:tm]
    pm_ref[...] = out[tm:]


def _inproj(h, hm, g, w, layer, tm, tn):
    m, d = h.shape
    n = w.shape[-1]
    if w.ndim == 3:
        w_spec = pl.BlockSpec((None, d, tn), lambda i, j: (layer, 0, j))
    else:
        w_spec = pl.BlockSpec((d, tn), lambda i, j: (0, j))
    return pl.pallas_call(
        _inproj_kernel,
        out_shape=[jax.ShapeDtypeStruct((m, n), BF16),
                   jax.ShapeDtypeStruct((m // tm, N_META, n), BF16)],
        grid=(m // tm, n // tn),
        in_specs=[
            pl.BlockSpec((tm, d), lambda i, j: (i, 0)),
            pl.BlockSpec((N_META, d), lambda i, j: (0, 0)),
            pl.BlockSpec((None, 1, d), lambda i, j: (layer, 0, 0)),
            w_spec,
        ],
        out_specs=[pl.BlockSpec((tm, tn), lambda i, j: (i, j)),
                   pl.BlockSpec((None, N_META, tn), lambda i, j: (i, 0, j))],
        scratch_shapes=[pltpu.VMEM((tm + N_META, d), BF16)],
        compiler_params=pltpu.CompilerParams(
            dimension_semantics=("parallel", "arbitrary"),
            vmem_limit_bytes=VMEM_LIMIT),
        name="inproj",
    )(h, hm, g, w)


def _meta_kernel(pm_ref, hm_ref, wr_ref, wd_ref, gret_ref, gdiff_ref,
                 lq1_ref, lk1_ref, lq2_ref, lk2_ref, hm_out_ref, *, lam_init):
    section = lambda s: pm_ref[:, s * MIX_HALF:(s + 1) * MIX_HALF]
    rq, rk, rv, rgate, dq, dk, dv, dgate = [section(s) for s in range(8)]
    t = lax.broadcasted_iota(jnp.int32, (BLK, BLK), 0)
    s_i = lax.broadcasted_iota(jnp.int32, (BLK, BLK), 1)
    dist = jnp.abs(t - s_i).astype(F32)
    r2 = lax.broadcasted_iota(jnp.int32, (2 * BLK, BLK), 0) % BLK
    c2 = lax.broadcasted_iota(jnp.int32, (2 * BLK, BLK), 1)
    dist2 = jnp.abs(r2 - c2).astype(F32)
    lam = _lambda(lq1_ref, lk1_ref, lq2_ref, lk2_ref, lam_init)
    r_parts, d_parts = [], []
    for h in range(HEADS):
        cols = slice(h * HEAD_DIM, (h + 1) * HEAD_DIM)
        q = _pad_rows(rq[:, cols], BLK)
        k = _pad_rows(rk[:, cols], BLK)
        v = _pad_rows(rv[:, cols], BLK)
        dmat = jnp.exp(_log_g(h) * dist) * (HEAD_DIM ** -0.5)
        s = lax.dot_general(q, k, NT, preferred_element_type=F32) * dmat
        o = jnp.dot(s.astype(BF16), v, preferred_element_type=F32)[:N_META]
        mu = jnp.mean(o, axis=-1, keepdims=True)
        oc = o - mu
        var = jnp.mean(oc * oc, axis=-1, keepdims=True)
        y = oc * lax.rsqrt(var + EPS) * gret_ref[:, cols]
        r_parts.append((y * _silu(rgate[:, cols].astype(F32))).astype(BF16))
        slope = 2.0 ** (-(h + 1.0))
        qs = _split_components(_pad_rows(dq[:, cols], BLK))
        k = _pad_rows(dk[:, cols], BLK)
        v = _pad_rows(dv[:, cols], BLK)
        bias = jnp.where(c2 < N_META, -slope * dist2, NEG)
        s = lax.dot_general(qs, k, NT, preferred_element_type=F32) + bias
        pr = jnp.exp(s - jnp.max(s, axis=-1, keepdims=True))
        a = (jnp.dot(pr.astype(BF16), v, preferred_element_type=F32)
             / jnp.sum(pr, axis=-1, keepdims=True))
        d = a[:N_META] - lam * a[BLK:BLK + N_META]
        y = d * lax.rsqrt(jnp.mean(d * d, axis=-1, keepdims=True) + EPS)
        y = y * gdiff_ref[:, cols] * (1.0 - lam_init)
        d_parts.append((y * _silu(dgate[:, cols].astype(F32))).astype(BF16))
    r = jnp.concatenate(r_parts, axis=1)
    d = jnp.concatenate(d_parts, axis=1)
    y = jnp.dot(r, wr_ref[...].astype(BF16), preferred_element_type=F32)
    y = y + jnp.dot(d, wd_ref[...].astype(BF16), preferred_element_type=F32)
    hm_out_ref[...] = hm_ref[...] + y


def _meta_layer(pm, hm, w_out, gret, gdiff, lams, layer, lam_init):
    dm = hm.shape[1]
    n = pm.shape[-1]
    vec = lambda width: pl.BlockSpec((None, 1, width), lambda i: (layer, 0, 0))
    return pl.pallas_call(
        functools.partial(_meta_kernel, lam_init=lam_init),
        out_shape=jax.ShapeDtypeStruct((N_META, dm), F32),
        grid=(1,),
        in_specs=[
            pl.BlockSpec((None, N_META, n), lambda i: (0, 0, 0)),
            pl.BlockSpec((N_META, dm), lambda i: (0, 0)),
            pl.BlockSpec((None, MIX_HALF, dm), lambda i: (layer, 0, 0),
                         pipeline_mode=pl.Buffered(1)),
            pl.BlockSpec((None, MIX_HALF, dm), lambda i: (layer, 1, 0),
                         pipeline_mode=pl.Buffered(1)),
            vec(MIX_HALF), vec(MIX_HALF),
            vec(DIFF_DH), vec(DIFF_DH), vec(DIFF_DH), vec(DIFF_DH),
        ],
        out_specs=pl.BlockSpec((N_META, dm), lambda i: (0, 0)),
        compiler_params=pltpu.CompilerParams(
            dimension_semantics=("arbitrary",),
            vmem_limit_bytes=VMEM_LIMIT),
        name="meta_layer",
    )(pm, hm, w_out, w_out, gret, gdiff, *lams)


def _diffattn_kernel(*refs, lam_init, ncast):
    (slopes_ref, qa_ref, qb_ref, k_ref, v_ref, km_ref, vm_ref, gatea_ref, gateb_ref, gn_ref,
     lq1_ref, lk1_ref, lq2_ref, lk2_ref) = refs[:14]
    cast_src = refs[14:14 + ncast]
    o_ref = refs[14 + ncast]
    cast_dst = refs[15 + ncast:15 + 2 * ncast]
    (qs_ref, s_ref, sm_ref, mrun_ref, mrow_ref, acc_ref,
     vext_ref, kmext_ref, vmext_ref, dbias_ref) = refs[15 + 2 * ncast:]

    for src, dst in zip(cast_src, cast_dst):
        dst[...] = src[...].astype(BF16)

    grp = pl.program_id(1)
    step = pl.program_id(2)
    hb = HEADS_PER_STEP
    seq = k_ref.shape[0]
    ntile = seq // QT
    head_cols = lambda hh: slice(hh * HEAD_DIM, (hh + 1) * HEAD_DIM)
    slope2 = lambda hh: slopes_ref[grp * hb + hh] * LOG2E
    tiles = ((step, 0, qa_ref, gatea_ref),
             (ntile - 1 - step, step + 1, qb_ref, gateb_ref))

    @pl.when(step == 0)
    def _():
        r = lax.broadcasted_iota(jnp.int32, (QT, QT), 0)
        c = lax.broadcasted_iota(jnp.int32, (QT, QT), 1)
        allowed = (c // CHUNK) <= (r // CHUNK)
        rel = (r - jnp.abs(r - c)).astype(F32)
        for hh in range(hb):
            cols = head_cols(hh)
            dbias_ref[hh] = jnp.where(allowed, slope2(hh) * rel, NEG)
            vext_ref[hh, :, :HEAD_DIM] = v_ref[:, cols]
            vext_ref[hh, :, HEAD_DIM:] = jnp.ones((seq, HEAD_DIM), BF16)
            kmext_ref[hh] = _pad_rows(km_ref[:, cols], BLK)
            vmext_ref[hh, :, :HEAD_DIM] = _pad_rows(vm_ref[:, cols], BLK)
            vmext_ref[hh, :, HEAD_DIM:] = jnp.ones((BLK, HEAD_DIM), BF16)

    def frame_scores(x, hh, j):
        start = pl.multiple_of(j * QT, QT)
        kt = k_ref[pl.ds(start, QT), head_cols(hh)]
        return lax.dot_general(qs_ref[x, hh], kt, NT, preferred_element_type=F32) * LOG2E

    def keep(x, hh, slot, s):
        s_ref[hh, slot] = s
        mrun_ref[x, hh] = jnp.maximum(mrun_ref[x, hh], jnp.maximum(s[:, :BLK], s[:, BLK:]))

    colm = lax.broadcasted_iota(jnp.int32, (1, BLK), 1)

    def first_scores(x, hh):
        t, base, q_ref, _ = tiles[x]
        relm = (colm - N_META - t * QT).astype(F32)
        qs_ref[x, hh] = _split_components(q_ref[:, head_cols(hh)])
        s = lax.dot_general(qs_ref[x, hh], kmext_ref[hh], NT, preferred_element_type=F32)
        s = s * LOG2E + jnp.where(colm < N_META, slope2(hh) * relm, NEG)
        sm_ref[x, hh] = s
        mrun_ref[x, hh] = s
        s = frame_scores(x, hh, t)
        keep(x, hh, base + t,
             jnp.concatenate([s[:QT] + dbias_ref[hh], s[QT:] + dbias_ref[hh]], axis=0))

    col = lax.broadcasted_iota(jnp.int32, (1, QT), 1)

    def for_tile_groups(n, group):
        def trip(i, carry):
            group([TILE_UNROLL * i + u for u in range(TILE_UNROLL)])
            return carry

        lax.fori_loop(0, n // TILE_UNROLL, trip, 0)
        done = (n // TILE_UNROLL) * TILE_UNROLL
        for rest in range(1, TILE_UNROLL):
            @pl.when(n - done == rest)
            def _(rest=rest):
                group([done + u for u in range(rest)])

    def off_diag_loops(x, t, base):
        def off_diag(js):
            for j in js:
                rel = (col + (j - t) * QT).astype(F32)
                for hh in range(hb):
                    keep(x, hh, base + j, frame_scores(x, hh, j) + slope2(hh) * rel)

        for_tile_groups(t, off_diag)

    def row_max(x, hh):
        mrow_ref[x, hh] = jnp.broadcast_to(
            jnp.max(mrun_ref[x, hh], axis=-1, keepdims=True), (2 * QT, BLK))
        p = jnp.exp2(sm_ref[x, hh] - mrow_ref[x, hh])
        acc_ref[x, hh] = jnp.dot(p.astype(BF16), vmext_ref[hh], preferred_element_type=F32)

    def pv_loops(x, t, base):
        def pv(j, hh):
            start = pl.multiple_of(j * QT, QT)
            s = s_ref[hh, base + j]
            m = mrow_ref[x, hh]
            p = jnp.concatenate([jnp.exp2(s[:, :BLK] - m), jnp.exp2(s[:, BLK:] - m)], axis=1)
            return jnp.dot(p.astype(BF16), vext_ref[hh, pl.ds(start, QT), :],
                           preferred_element_type=F32)

        def pv_group(js):
            for hh in range(hb):
                acc_ref[x, hh] = acc_ref[x, hh] + functools.reduce(
                    lambda a, b: a + b, [pv(j, hh) for j in js])

        for_tile_groups(t + 1, pv_group)

    lam = _lambda(lq1_ref, lk1_ref, lq2_ref, lk2_ref, lam_init)

    def finish(x, hh):
        gate_ref = tiles[x][3]
        cols = head_cols(hh)
        acc = acc_ref[x, hh]
        a = acc[:, :HEAD_DIM] / acc[:, HEAD_DIM:]
        d = a[:QT] - lam * a[QT:]
        y = d * lax.rsqrt(jnp.mean(d * d, axis=-1, keepdims=True) + EPS)
        y = y * (gn_ref[:, cols] * (1.0 - lam_init))
        o_ref[x * QT:(x + 1) * QT, cols] = (
            y * _silu(gate_ref[:, cols].astype(F32))).astype(BF16)

    (ta, base_a, _, _), (tb, base_b, _, _) = tiles
    for hh in range(hb):
        first_scores(0, hh)
    off_diag_loops(0, ta, base_a)
    for hh in range(hb):
        first_scores(1, hh)
        row_max(0, hh)
    pv_loops(0, ta, base_a)
    off_diag_loops(1, tb, base_b)
    for hh in range(hb):
        row_max(1, hh)
        finish(0, hh)
    pv_loops(1, tb, base_b)
    for hh in range(hb):
        finish(1, hh)


def _diffattn(proj, pm, gn, lams, layer, batch, seq, lam_init, casts):
    m = proj.shape[0]
    ntile = seq // QT
    hb = HEADS_PER_STEP
    ngrp = HEADS // hb
    width = hb * HEAD_DIM
    slopes = jnp.asarray(2.0 ** (-8.0 * np.arange(1, HEADS + 1) / HEADS), F32)
    qcol, kcol, vcol, gcol = (4 * ngrp, 5 * ngrp, 6 * ngrp, 7 * ngrp)
    vec = pl.BlockSpec((None, 1, DIFF_DH), lambda b, g, t: (layer, 0, 0))
    tile_a = lambda b, t: b * ntile + t
    tile_b = lambda b, t: b * ntile + ntile - 1 - t
    nstep = batch * ngrp * (ntile // 2)
    linear = lambda b, g, t: (b * ngrp + g) * (ntile // 2) + t
    cast_in, cast_out_specs, cast_out_shapes, cast_args = [], [], [], []
    for w, wl in casts:
        rows, ncol = w.shape[1:]
        slab = rows // nstep
        cast_in.append(pl.BlockSpec((None, slab, ncol),
                                    lambda b, g, t, wl=wl: (wl, linear(b, g, t), 0)))
        cast_out_specs.append(pl.BlockSpec((slab, ncol), lambda b, g, t: (linear(b, g, t), 0)))
        cast_out_shapes.append(jax.ShapeDtypeStruct((rows, ncol), BF16))
        cast_args.append(w)
    outs = pl.pallas_call(
        functools.partial(_diffattn_kernel, lam_init=lam_init, ncast=len(casts)),
        out_shape=[jax.ShapeDtypeStruct((m, MIX_HALF), BF16)] + cast_out_shapes,
        grid=(batch, ngrp, ntile // 2),
        in_specs=[
            pl.BlockSpec(memory_space=pltpu.SMEM),
            pl.BlockSpec((QT, width), lambda b, g, t: (tile_a(b, t), qcol + g)),
            pl.BlockSpec((QT, width), lambda b, g, t: (tile_b(b, t), qcol + g)),
            pl.BlockSpec((seq, width), lambda b, g, t: (b, kcol + g)),
            pl.BlockSpec((seq, width), lambda b, g, t: (b, vcol + g)),
            pl.BlockSpec((None, N_META, width), lambda b, g, t: (0, 0, kcol + g)),
            pl.BlockSpec((None, N_META, width), lambda b, g, t: (0, 0, vcol + g)),
            pl.BlockSpec((QT, width), lambda b, g, t: (tile_a(b, t), gcol + g)),
            pl.BlockSpec((QT, width), lambda b, g, t: (tile_b(b, t), gcol + g)),
            pl.BlockSpec((None, 1, width), lambda b, g, t: (layer, 0, g)),
            vec, vec, vec, vec,
        ] + cast_in,
        out_specs=[pl.BlockSpec((2 * QT, width), lambda b, g, t: (b * (ntile // 2) + t, g))]
        + cast_out_specs,
        scratch_shapes=[
            pltpu.VMEM((2, hb, 2 * QT, HEAD_DIM), BF16),
            pltpu.VMEM((hb, ntile + 1, 2 * QT, QT), F32),
            pltpu.VMEM((2, hb, 2 * QT, BLK), F32),
            pltpu.VMEM((2, hb, 2 * QT, BLK), F32),
            pltpu.VMEM((2, hb, 2 * QT, BLK), F32),
            pltpu.VMEM((2, hb, 2 * QT, 2 * HEAD_DIM), F32),
            pltpu.VMEM((hb, seq, 2 * HEAD_DIM), BF16),
            pltpu.VMEM((hb, BLK, HEAD_DIM), BF16),
            pltpu.VMEM((hb, BLK, 2 * HEAD_DIM), BF16),
            pltpu.VMEM((hb, QT, QT), F32),
        ],
        compiler_params=pltpu.CompilerParams(
            dimension_semantics=("parallel", "parallel", "arbitrary"),
            vmem_limit_bytes=VMEM_LIMIT),
        name="diffattn",
    )(slopes, proj, proj, proj, proj, pm, pm, proj, proj, gn, *lams, *cast_args)
    return outs[0], outs[1:]


def _mix_outproj_kernel(q_ref, k_ref, v_ref, gate_ref, km_ref, vm_ref, gret_ref,
                        da_ref, db_ref, wr_ref, wd_ref, h_ref, gf_ref, o_ref,
                        r_ref, state_ref, init_ref, dmat_ref, rq_ref, rk_ref,
                        *, final_norm, tiles_per_seq):
    i = pl.program_id(0)
    scale = HEAD_DIM ** -0.5
    tm = q_ref.shape[0]

    @pl.when(i == 0)
    def _():
        r_ref[...] = jnp.zeros_like(r_ref)
        state_ref[...] = jnp.zeros_like(state_ref)
        t = lax.broadcasted_iota(jnp.int32, (BLK, BLK), 0)
        s = lax.broadcasted_iota(jnp.int32, (BLK, BLK), 1)
        dist = jnp.abs(t - s).astype(F32)
        allowed = (s // CHUNK) <= (t // CHUNK)
        tf = t.astype(F32)
        mf = lax.broadcasted_iota(jnp.int32, (N_META, HEAD_DIM), 0).astype(F32)
        for h in range(HEADS):
            cols = slice(h * HEAD_DIM, (h + 1) * HEAD_DIM)
            log_g = _log_g(h)
            dmat_ref[h] = jnp.where(allowed, jnp.exp(log_g * dist) * scale, 0.0)
            rq_ref[h] = jnp.exp(log_g * (tf + 1.0))
            rk_ref[h] = jnp.exp(log_g * (BLK - 1.0 - tf)) * scale
            zeta = jnp.exp(log_g * (N_META - 1.0 - mf)) * scale
            kz = (km_ref[:, cols].astype(F32) * zeta).astype(BF16)
            init_ref[h] = lax.dot_general(_pad_rows(kz, BLK), _pad_rows(vm_ref[:, cols], BLK),
                                          TN, preferred_element_type=F32)

    r = r_ref[...]
    for half, d_ref in enumerate((da_ref, db_ref)):
        rows = slice(half * QT, (half + 1) * QT)
        y = jnp.dot(r[rows], wr_ref[...], preferred_element_type=F32)
        y = y + jnp.dot(d_ref[...], wd_ref[...], preferred_element_type=F32)
        hn = h_ref[rows] + y
        if final_norm:
            ms = jnp.mean(hn * hn, axis=-1, keepdims=True)
            hn = hn * lax.rsqrt(ms + EPS) * gf_ref[...]
        o_ref[rows] = hn

    first = (i % tiles_per_seq) == 0
    for blk in range(tm // BLK):
        rows = slice(blk * BLK, (blk + 1) * BLK)
        for h in range(HEADS):
            cols = slice(h * HEAD_DIM, (h + 1) * HEAD_DIM)
            decay_blk = float((1.0 - 2.0 ** (-5.0 - h)) ** BLK)
            q = q_ref[rows, cols]
            k = k_ref[rows, cols]
            v = v_ref[rows, cols]
            s = lax.dot_general(q, k, NT, preferred_element_type=F32) * dmat_ref[h]
            intra = jnp.dot(s.astype(BF16), v, preferred_element_type=F32)
            state = state_ref[h]
            if blk == 0:
                state = jnp.where(first, init_ref[h], state)
            qx = (q.astype(F32) * rq_ref[h]).astype(BF16)
            cross = jnp.dot(qx, state.astype(BF16), preferred_element_type=F32)
            kz = (k.astype(F32) * rk_ref[h]).astype(BF16)
            kv = lax.dot_general(kz, v, TN, preferred_element_type=F32)
            state_ref[h] = decay_blk * state + kv

            o = intra + cross
            mu = jnp.mean(o, axis=-1, keepdims=True)
            oc = o - mu
            var = jnp.mean(oc * oc, axis=-1, keepdims=True)
            y = oc * lax.rsqrt(var + EPS) * gret_ref[:, cols]
            r_ref[rows, cols] = (y * _silu(gate_ref[rows, cols].astype(F32))).astype(BF16)


def _mix_outproj(proj, pm, d, w, h, gret, gf, layer, seq, final_norm):
    m, dm = h.shape
    tm = 2 * QT
    ntiles = m // tm
    ntile_q = seq // QT
    cur = lambda i: jnp.minimum(i, ntiles - 1)
    prev = lambda i: jnp.maximum(i - 1, 0)

    def d_block(which):
        def index(i):
            tile = 2 * prev(i) + which
            b, t = tile // ntile_q, tile % ntile_q
            pos = jnp.where(t < ntile_q // 2, 2 * t, 2 * (ntile_q - 1 - t) + 1)
            return b * ntile_q + pos, 0
        return index

    return pl.pallas_call(
        functools.partial(_mix_outproj_kernel, final_norm=final_norm,
                          tiles_per_seq=seq // tm),
        out_shape=jax.ShapeDtypeStruct((m, dm), F32),
        grid=(ntiles + 1,),
        in_specs=[
            pl.BlockSpec((tm, MIX_HALF), lambda i: (cur(i), 0)),
            pl.BlockSpec((tm, MIX_HALF), lambda i: (cur(i), 1)),
            pl.BlockSpec((tm, MIX_HALF), lambda i: (cur(i), 2)),
            pl.BlockSpec((tm, MIX_HALF), lambda i: (cur(i), 3)),
            pl.BlockSpec((None, N_META, MIX_HALF), lambda i: (0, 0, 1)),
            pl.BlockSpec((None, N_META, MIX_HALF), lambda i: (0, 0, 2)),
            pl.BlockSpec((None, 1, MIX_HALF), lambda i: (layer, 0, 0)),
            pl.BlockSpec((QT, MIX_HALF), d_block(0)),
            pl.BlockSpec((QT, MIX_HALF), d_block(1)),
            pl.BlockSpec((MIX_HALF, dm), lambda i: (0, 0), pipeline_mode=pl.Buffered(1)),
            pl.BlockSpec((MIX_HALF, dm), lambda i: (1, 0), pipeline_mode=pl.Buffered(1)),
            pl.BlockSpec((tm, dm), lambda i: (prev(i), 0)),
            pl.BlockSpec((1, dm), lambda i: (0, 0)),
        ],
        out_specs=pl.BlockSpec((tm, dm), lambda i: (prev(i), 0)),
        scratch_shapes=[
            pltpu.VMEM((tm, MIX_HALF), BF16),
            pltpu.VMEM((HEADS, HEAD_DIM, HEAD_DIM), F32),
            pltpu.VMEM((HEADS, HEAD_DIM, HEAD_DIM), F32),
            pltpu.VMEM((HEADS, BLK, BLK), F32),
            pltpu.VMEM((HEADS, BLK, HEAD_DIM), F32),
            pltpu.VMEM((HEADS, BLK, HEAD_DIM), F32),
        ],
        compiler_params=pltpu.CompilerParams(
            dimension_semantics=("arbitrary",),
            vmem_limit_bytes=VMEM_LIMIT),
        name="mix_outproj",
    )(proj, proj, proj, proj, pm, pm, gret, d, d, w, w, h, gf)


def kernel(x, meta_tokens, norm_g, w_in, w_out, ret_norm_g, diff_norm_g,
           lambda_q1, lambda_k1, lambda_q2, lambda_k2, final_norm_g):
    batch, seq, dm = x.shape
    depth = w_in.shape[0]
    m = batch * seq
    assert seq % QT == 0 and meta_tokens.shape[0] == N_META

    h = x.reshape(m, dm)
    hm = meta_tokens.astype(x.dtype)
    gf = final_norm_g.reshape(1, dm)
    per_layer = lambda a: a.reshape(depth, 1, a.shape[-1])
    norm_g, ret_norm_g, diff_norm_g = map(per_layer, (norm_g, ret_norm_g, diff_norm_g))
    lams = tuple(map(per_layer, (lambda_q1, lambda_k1, lambda_q2, lambda_k2)))

    tm = m // 8
    w_in_b = None
    for i in range(depth):
        last = i == depth - 1
        lam_init = 0.8 - 0.6 * float(np.exp(-0.3 * i))
        if w_in_b is None:
            proj, pm = _inproj(h, hm, norm_g, w_in, i, tm, 1024)
        else:
            proj, pm = _inproj(h, hm, norm_g, w_in_b, i, tm, 2048)
        if not last:
            hm = _meta_layer(pm, hm, w_out, ret_norm_g, diff_norm_g, lams, i, lam_init)
        casts = [(w_out, i)] + ([] if last else [(w_in, i + 1)])
        d, cast = _diffattn(proj, pm, diff_norm_g, lams, i, batch, seq, lam_init, casts)
        w_in_b = None if last else cast[1]
        h = _mix_outproj(proj, pm, d, cast[0], h, ret_norm_g, gf, i, seq, final_norm=last)

    return h.reshape(batch, seq, dm)
```

```python
import functools

import numpy as np
import jax
import jax.numpy as jnp
from jax import lax
from jax.experimental import pallas as pl
from jax.experimental.pallas import tpu as pltpu

F32 = jnp.float32
BF16 = jnp.bfloat16

CHUNK = 64
N_META = 16
EPS = 1e-6
HEADS = 8
HEAD_DIM = 128
DIFF_DH = 64
MIX_HALF = HEADS * HEAD_DIM
BLK = 128
QT = 256
NEG = -1e30
LOG2E = 1.4426950408889634
HEADS_PER_STEP = 4
TILE_UNROLL = 4

VMEM_LIMIT = 56 * 1024 * 1024

NT = (((1,), (1,)), ((), ()))
TN = (((0,), (0,)), ((), ()))


def _log_g(h):
    return float(np.log(1.0 - 2.0 ** (-5.0 - h)))


def _silu(x):
    return x * (1.0 / (1.0 + jnp.exp(-x)))


def _lambda(lq1_ref, lk1_ref, lq2_ref, lk2_ref, lam_init):
    return (jnp.exp(jnp.sum(lq1_ref[...] * lk1_ref[...], axis=-1, keepdims=True))
            - jnp.exp(jnp.sum(lq2_ref[...] * lk2_ref[...], axis=-1, keepdims=True))
            + lam_init)


def _pad_rows(a, rows):
    return jnp.concatenate([a, jnp.zeros((rows - a.shape[0], a.shape[1]), a.dtype)], axis=0)


def _split_components(q):
    lane = lax.broadcasted_iota(jnp.int32, q.shape, 1)
    zero = jnp.zeros((), BF16)
    qscaled = q * jnp.asarray(DIFF_DH ** -0.5, BF16)
    return jnp.concatenate([jnp.where(lane < DIFF_DH, qscaled, zero),
                            jnp.where(lane >= DIFF_DH, qscaled, zero)], axis=0)


def _rms_norm_bf16(x, g):
    ms = jnp.mean(x * x, axis=-1, keepdims=True)
    return (x * lax.rsqrt(ms + EPS) * g).astype(BF16)


def _inproj_kernel(h_ref, hm_ref, g_ref, w_ref, o_ref, pm_ref, u_ref):
    tm = h_ref.shape[0]
    half = tm // 2

    def project(lo, hi):
        return jnp.dot(u_ref[lo:hi], w_ref[...].astype(BF16),
                       preferred_element_type=F32).astype(BF16)

    @pl.when(pl.program_id(1) == 0)
    def _():
        u_ref[:half] = _rms_norm_bf16(h_ref[:half], g_ref[...])
        o_ref[:half] = project(0, half)
        u_ref[half:tm] = _rms_norm_bf16(h_ref[half:], g_ref[...])
        u_ref[tm:] = _rms_norm_bf16(hm_ref[...], g_ref[...])
        out = project(half, tm + N_META)
        o_ref[half:] = out[:tm - half]
        pm_ref[...] = out[tm - half:]

    @pl.when(pl.program_id(1) > 0)
    def _():
        out = project(0, tm + N_META)
        o_ref[...] = out[:tm]
        pm_ref[...] = out[tm:]


def _inproj(h, hm, g, w, layer, tm, tn):
    m, d = h.shape
    n = w.shape[-1]
    if w.ndim == 3:
        w_spec = pl.BlockSpec((None, d, tn), lambda i, j: (layer, 0, j))
    else:
        w_spec = pl.BlockSpec((d, tn), lambda i, j: (0, j))
    return pl.pallas_call(
        _inproj_kernel,
        out_shape=[jax.ShapeDtypeStruct((m, n), BF16),
                   jax.ShapeDtypeStruct((m // tm, N_META, n), BF16)],
        grid=(m // tm, n // tn),
        in_specs=[
            pl.BlockSpec((tm, d), lambda i, j: (i, 0)),
            pl.BlockSpec((N_META, d), lambda i, j: (0, 0)),
            pl.BlockSpec((None, 1, d), lambda i, j: (layer, 0, 0)),
            w_spec,
        ],
        out_specs=[pl.BlockSpec((tm, tn), lambda i, j: (i, j)),
                   pl.BlockSpec((None, N_META, tn), lambda i, j: (i, 0, j))],
        scratch_shapes=[pltpu.VMEM((tm + N_META, d), BF16)],
        compiler_params=pltpu.CompilerParams(
            dimension_semantics=("parallel", "arbitrary"),
            vmem_limit_bytes=VMEM_LIMIT),
        name="inproj",
    )(h, hm, g, w)


def _meta_kernel(pm_ref, hm_ref, wr_ref, wd_ref, gret_ref, gdiff_ref,
                 lq1_ref, lk1_ref, lq2_ref, lk2_ref, hm_out_ref, *, lam_init):
    section = lambda s: pm_ref[:, s * MIX_HALF:(s + 1) * MIX_HALF]
    rq, rk, rv, rgate, dq, dk, dv, dgate = [section(s) for s in range(8)]
    t = lax.broadcasted_iota(jnp.int32, (BLK, BLK), 0)
    s_i = lax.broadcasted_iota(jnp.int32, (BLK, BLK), 1)
    dist = jnp.abs(t - s_i).astype(F32)
    r2 = lax.broadcasted_iota(jnp.int32, (2 * BLK, BLK), 0) % BLK
    c2 = lax.broadcasted_iota(jnp.int32, (2 * BLK, BLK), 1)
    dist2 = jnp.abs(r2 - c2).astype(F32)
    lam = _lambda(lq1_ref, lk1_ref, lq2_ref, lk2_ref, lam_init)
    r_parts, d_parts = [], []
    for h in range(HEADS):
        cols = slice(h * HEAD_DIM, (h + 1) * HEAD_DIM)
        q = _pad_rows(rq[:, cols], BLK)
        k = _pad_rows(rk[:, cols], BLK)
        v = _pad_rows(rv[:, cols], BLK)
        dmat = jnp.exp(_log_g(h) * dist) * (HEAD_DIM ** -0.5)
        s = lax.dot_general(q, k, NT, preferred_element_type=F32) * dmat
        o = jnp.dot(s.astype(BF16), v, preferred_element_type=F32)[:N_META]
        mu = jnp.mean(o, axis=-1, keepdims=True)
        oc = o - mu
        var = jnp.mean(oc * oc, axis=-1, keepdims=True)
        y = oc * lax.rsqrt(var + EPS) * gret_ref[:, cols]
        r_parts.append((y * _silu(rgate[:, cols].astype(F32))).astype(BF16))
        slope = 2.0 ** (-(h + 1.0))
        qs = _split_components(_pad_rows(dq[:, cols], BLK))
        k = _pad_rows(dk[:, cols], BLK)
        v = _pad_rows(dv[:, cols], BLK)
        bias = jnp.where(c2 < N_META, -slope * dist2, NEG)
        s = lax.dot_general(qs, k, NT, preferred_element_type=F32) + bias
        pr = jnp.exp(s - jnp.max(s, axis=-1, keepdims=True))
        a = (jnp.dot(pr.astype(BF16), v, preferred_element_type=F32)
             / jnp.sum(pr, axis=-1, keepdims=True))
        d = a[:N_META] - lam * a[BLK:BLK + N_META]
        y = d * lax.rsqrt(jnp.mean(d * d, axis=-1, keepdims=True) + EPS)
        y = y * gdiff_ref[:, cols] * (1.0 - lam_init)
        d_parts.append((y * _silu(dgate[:, cols].astype(F32))).astype(BF16))
    r = jnp.concatenate(r_parts, axis=1)
    d = jnp.concatenate(d_parts, axis=1)
    y = jnp.dot(r, wr_ref[...].astype(BF16), preferred_element_type=F32)
    y = y + jnp.dot(d, wd_ref[...].astype(BF16), preferred_element_type=F32)
    hm_out_ref[...] = hm_ref[...] + y


def _meta_layer(pm, hm, w_out, gret, gdiff, lams, layer, lam_init):
    dm = hm.shape[1]
    n = pm.shape[-1]
    vec = lambda width: pl.BlockSpec((None, 1, width), lambda i: (layer, 0, 0))
    return pl.pallas_call(
        functools.partial(_meta_kernel, lam_init=lam_init),
        out_shape=jax.ShapeDtypeStruct((N_META, dm), F32),
        grid=(1,),
        in_specs=[
            pl.BlockSpec((None, N_META, n), lambda i: (0, 0, 0)),
            pl.BlockSpec((N_META, dm), lambda i: (0, 0)),
            pl.BlockSpec((None, MIX_HALF, dm), lambda i: (layer, 0, 0),
                         pipeline_mode=pl.Buffered(1)),
            pl.BlockSpec((None, MIX_HALF, dm), lambda i: (layer, 1, 0),
                         pipeline_mode=pl.Buffered(1)),
            vec(MIX_HALF), vec(MIX_HALF),
            vec(DIFF_DH), vec(DIFF_DH), vec(DIFF_DH), vec(DIFF_DH),
        ],
        out_specs=pl.BlockSpec((N_META, dm), lambda i: (0, 0)),
        compiler_params=pltpu.CompilerParams(
            dimension_semantics=("arbitrary",),
            vmem_limit_bytes=VMEM_LIMIT),
        name="meta_layer",
    )(pm, hm, w_out, w_out, gret, gdiff, *lams)


def _diffattn_kernel(*refs, lam_init, ncast):
    (slopes_ref, qa_ref, qb_ref, k_ref, v_ref, km_ref, vm_ref, gatea_ref, gateb_ref, gn_ref,
     lq1_ref, lk1_ref, lq2_ref, lk2_ref) = refs[:14]
    cast_src = refs[14:14 + ncast]
    o_ref = refs[14 + ncast]
    cast_dst = refs[15 + ncast:15 + 2 * ncast]
    (qs_ref, s_ref, sm_ref, mrun_ref, mrow_ref, acc_ref,
     vext_ref, kmext_ref, vmext_ref, dbias_ref) = refs[15 + 2 * ncast:]

    for src, dst in zip(cast_src, cast_dst):
        dst[...] = src[...].astype(BF16)

    grp = pl.program_id(1)
    step = pl.program_id(2)
    hb = HEADS_PER_STEP
    seq = k_ref.shape[0]
    ntile = seq // QT
    head_cols = lambda hh: slice(hh * HEAD_DIM, (hh + 1) * HEAD_DIM)
    slope2 = lambda hh: slopes_ref[grp * hb + hh] * LOG2E
    tiles = ((step, 0, qa_ref, gatea_ref),
             (ntile - 1 - step, step + 1, qb_ref, gateb_ref))

    @pl.when(step == 0)
    def _():
        r = lax.broadcasted_iota(jnp.int32, (QT, QT), 0)
        c = lax.broadcasted_iota(jnp.int32, (QT, QT), 1)
        allowed = (c // CHUNK) <= (r // CHUNK)
        rel = (r - jnp.abs(r - c)).astype(F32)
        for hh in range(hb):
            cols = head_cols(hh)
            dbias_ref[hh] = jnp.where(allowed, slope2(hh) * rel, NEG)
            vext_ref[hh, :, :HEAD_DIM] = v_ref[:, cols]
            vext_ref[hh, :, HEAD_DIM:] = jnp.ones((seq, HEAD_DIM), BF16)
            kmext_ref[hh] = _pad_rows(km_ref[:, cols], BLK)
            vmext_ref[hh, :, :HEAD_DIM] = _pad_rows(vm_ref[:, cols], BLK)
            vmext_ref[hh, :, HEAD_DIM:] = jnp.ones((BLK, HEAD_DIM), BF16)

    def frame_scores(x, hh, j):
        start = pl.multiple_of(j * QT, QT)
        kt = k_ref[pl.ds(start, QT), head_cols(hh)]
        return lax.dot_general(qs_ref[x, hh], kt, NT, preferred_element_type=F32) * LOG2E

    def keep(x, hh, slot, s):
        s_ref[hh, slot] = s
        mrun_ref[x, hh] = jnp.maximum(mrun_ref[x, hh], jnp.maximum(s[:, :BLK], s[:, BLK:]))

    colm = lax.broadcasted_iota(jnp.int32, (1, BLK), 1)

    def first_scores(x, hh):
        t, base, q_ref, _ = tiles[x]
        relm = (colm - N_META - t * QT).astype(F32)
        qs_ref[x, hh] = _split_components(q_ref[:, head_cols(hh)])
        s = lax.dot_general(qs_ref[x, hh], kmext_ref[hh], NT, preferred_element_type=F32)
        s = s * LOG2E + jnp.where(colm < N_META, slope2(hh) * relm, NEG)
        sm_ref[x, hh] = s
        mrun_ref[x, hh] = s
        s = frame_scores(x, hh, t)
        keep(x, hh, base + t,
             jnp.concatenate([s[:QT] + dbias_ref[hh], s[QT:] + dbias_ref[hh]], axis=0))

    col = lax.broadcasted_iota(jnp.int32, (1, QT), 1)

    def for_tile_groups(n, group):
        def trip(i, carry):
            group([TILE_UNROLL * i + u for u in range(TILE_UNROLL)])
            return carry

        lax.fori_loop(0, n // TILE_UNROLL, trip, 0)
        done = (n // TILE_UNROLL) * TILE_UNROLL
        for rest in range(1, TILE_UNROLL):
            @pl.when(n - done == rest)
            def _(rest=rest):
                group([done + u for u in range(rest)])

    def off_diag_loops(x, t, base):
        def off_diag(js):
            for j in js:
                rel = (col + (j - t) * QT).astype(F32)
                for hh in range(hb):
                    keep(x, hh, base + j, frame_scores(x, hh, j) + slope2(hh) * rel)

        for_tile_groups(t, off_diag)

    def row_max(x, hh):
        mrow_ref[x, hh] = jnp.broadcast_to(
            jnp.max(mrun_ref[x, hh], axis=-1, keepdims=True), (2 * QT, BLK))
        p = jnp.exp2(sm_ref[x, hh] - mrow_ref[x, hh])
        acc_ref[x, hh] = jnp.dot(p.astype(BF16), vmext_ref[hh], preferred_element_type=F32)

    def pv_loops(x, t, base):
        def pv(j, hh):
            start = pl.multiple_of(j * QT, QT)
            s = s_ref[hh, base + j]
            m = mrow_ref[x, hh]
            p = jnp.concatenate([jnp.exp2(s[:, :BLK] - m), jnp.exp2(s[:, BLK:] - m)], axis=1)
            return jnp.dot(p.astype(BF16), vext_ref[hh, pl.ds(start, QT), :],
                           preferred_element_type=F32)

        def pv_group(js):
            for hh in range(hb):
                acc_ref[x, hh] = acc_ref[x, hh] + functools.reduce(
                    lambda a, b: a + b, [pv(j, hh) for j in js])

        for_tile_groups(t + 1, pv_group)

    lam = _lambda(lq1_ref, lk1_ref, lq2_ref, lk2_ref, lam_init)

    def finish(x, hh):
        gate_ref = tiles[x][3]
        cols = head_cols(hh)
        acc = acc_ref[x, hh]
        a = acc[:, :HEAD_DIM] / acc[:, HEAD_DIM:]
        d = a[:QT] - lam * a[QT:]
        y = d * lax.rsqrt(jnp.mean(d * d, axis=-1, keepdims=True) + EPS)
        y = y * (gn_ref[:, cols] * (1.0 - lam_init))
        o_ref[x * QT:(x + 1) * QT, cols] = (
            y * _silu(gate_ref[:, cols].astype(F32))).astype(BF16)

    (ta, base_a, _, _), (tb, base_b, _, _) = tiles
    for hh in range(hb):
        first_scores(0, hh)
    off_diag_loops(0, ta, base_a)
    for hh in range(hb):
        first_scores(1, hh)
        row_max(0, hh)
    pv_loops(0, ta, base_a)
    off_diag_loops(1, tb, base_b)
    for hh in range(hb):
        row_max(1, hh)
        finish(0, hh)
    pv_loops(1, tb, base_b)
    for hh in range(hb):
        finish(1, hh)


def _diffattn(proj, pm, gn, lams, layer, batch, seq, lam_init, casts):
    m = proj.shape[0]
    ntile = seq // QT
    hb = HEADS_PER_STEP
    ngrp = HEADS // hb
    width = hb * HEAD_DIM
    slopes = jnp.asarray(2.0 ** (-8.0 * np.arange(1, HEADS + 1) / HEADS), F32)
    qcol, kcol, vcol, gcol = (4 * ngrp, 5 * ngrp, 6 * ngrp, 7 * ngrp)
    vec = pl.BlockSpec((None, 1, DIFF_DH), lambda b, g, t: (layer, 0, 0))
    tile_a = lambda b, t: b * ntile + t
    tile_b = lambda b, t: b * ntile + ntile - 1 - t
    nstep = batch * ngrp * (ntile // 2)
    linear = lambda b, g, t: (b * ngrp + g) * (ntile // 2) + t
    cast_in, cast_out_specs, cast_out_shapes, cast_args = [], [], [], []
    for w, wl in casts:
        rows, ncol = w.shape[1:]
        slab = rows // nstep
        cast_in.append(pl.BlockSpec((None, slab, ncol),
                                    lambda b, g, t, wl=wl: (wl, linear(b, g, t), 0)))
        cast_out_specs.append(pl.BlockSpec((slab, ncol), lambda b, g, t: (linear(b, g, t), 0)))
        cast_out_shapes.append(jax.ShapeDtypeStruct((rows, ncol), BF16))
        cast_args.append(w)
    outs = pl.pallas_call(
        functools.partial(_diffattn_kernel, lam_init=lam_init, ncast=len(casts)),
        out_shape=[jax.ShapeDtypeStruct((m, MIX_HALF), BF16)] + cast_out_shapes,
        grid=(batch, ngrp, ntile // 2),
        in_specs=[
            pl.BlockSpec(memory_space=pltpu.SMEM),
            pl.BlockSpec((QT, width), lambda b, g, t: (tile_a(b, t), qcol + g)),
            pl.BlockSpec((QT, width), lambda b, g, t: (tile_b(b, t), qcol + g)),
            pl.BlockSpec((seq, width), lambda b, g, t: (b, kcol + g)),
            pl.BlockSpec((seq, width), lambda b, g, t: (b, vcol + g)),
            pl.BlockSpec((None, N_META, width), lambda b, g, t: (0, 0, kcol + g)),
            pl.BlockSpec((None, N_META, width), lambda b, g, t: (0, 0, vcol + g)),
            pl.BlockSpec((QT, width), lambda b, g, t: (tile_a(b, t), gcol + g)),
            pl.BlockSpec((QT, width), lambda b, g, t: (tile_b(b, t), gcol + g)),
            pl.BlockSpec((None, 1, width), lambda b, g, t: (layer, 0, g)),
            vec, vec, vec, vec,
        ] + cast_in,
        out_specs=[pl.BlockSpec((2 * QT, width), lambda b, g, t: (b * (ntile // 2) + t, g))]
        + cast_out_specs,
        scratch_shapes=[
            pltpu.VMEM((2, hb, 2 * QT, HEAD_DIM), BF16),
            pltpu.VMEM((hb, ntile + 1, 2 * QT, QT), F32),
            pltpu.VMEM((2, hb, 2 * QT, BLK), F32),
            pltpu.VMEM((2, hb, 2 * QT, BLK), F32),
            pltpu.VMEM((2, hb, 2 * QT, BLK), F32),
            pltpu.VMEM((2, hb, 2 * QT, 2 * HEAD_DIM), F32),
            pltpu.VMEM((hb, seq, 2 * HEAD_DIM), BF16),
            pltpu.VMEM((hb, BLK, HEAD_DIM), BF16),
            pltpu.VMEM((hb, BLK, 2 * HEAD_DIM), BF16),
            pltpu.VMEM((hb, QT, QT), F32),
        ],
        compiler_params=pltpu.CompilerParams(
            dimension_semantics=("parallel", "parallel", "arbitrary"),
            vmem_limit_bytes=VMEM_LIMIT),
        name="diffattn",
    )(slopes, proj, proj, proj, proj, pm, pm, proj, proj, gn, *lams, *cast_args)
    return outs[0], outs[1:]


def _mix_outproj_kernel(q_ref, k_ref, v_ref, gate_ref, km_ref, vm_ref, gret_ref,
                        da_ref, db_ref, wr_ref, wd_ref, h_ref, gf_ref, o_ref,
                        r_ref, state_ref, init_ref, dmat_ref, rq_ref, rk_ref,
                        *, final_norm, tiles_per_seq):
    i = pl.program_id(0)
    scale = HEAD_DIM ** -0.5
    tm = q_ref.shape[0]

    @pl.when(i == 0)
    def _():
        r_ref[...] = jnp.zeros_like(r_ref)
        state_ref[...] = jnp.zeros_like(state_ref)
        t = lax.broadcasted_iota(jnp.int32, (BLK, BLK), 0)
        s = lax.broadcasted_iota(jnp.int32, (BLK, BLK), 1)
        dist = jnp.abs(t - s).astype(F32)
        allowed = (s // CHUNK) <= (t // CHUNK)
        tf = t.astype(F32)
        mf = lax.broadcasted_iota(jnp.int32, (N_META, HEAD_DIM), 0).astype(F32)
        for h in range(HEADS):
            cols = slice(h * HEAD_DIM, (h + 1) * HEAD_DIM)
            log_g = _log_g(h)
            dmat_ref[h] = jnp.where(allowed, jnp.exp(log_g * dist) * scale, 0.0)
            rq_ref[h] = jnp.exp(log_g * (tf + 1.0))
            rk_ref[h] = jnp.exp(log_g * (BLK - 1.0 - tf)) * scale
            zeta = jnp.exp(log_g * (N_META - 1.0 - mf)) * scale
            kz = (km_ref[:, cols].astype(F32) * zeta).astype(BF16)
            init_ref[h] = lax.dot_general(_pad_rows(kz, BLK), _pad_rows(vm_ref[:, cols], BLK),
                                          TN, preferred_element_type=F32)

    r = r_ref[...]
    for half, d_ref in enumerate((da_ref, db_ref)):
        rows = slice(half * QT, (half + 1) * QT)
        y = jnp.dot(r[rows], wr_ref[...], preferred_element_type=F32)
        y = y + jnp.dot(d_ref[...], wd_ref[...], preferred_element_type=F32)
        hn = h_ref[rows] + y
        if final_norm:
            ms = jnp.mean(hn * hn, axis=-1, keepdims=True)
            hn = hn * lax.rsqrt(ms + EPS) * gf_ref[...]
        o_ref[rows] = hn

    first = (i % tiles_per_seq) == 0
    for blk in range(tm // BLK):
        rows = slice(blk * BLK, (blk + 1) * BLK)
        for h in range(HEADS):
            cols = slice(h * HEAD_DIM, (h + 1) * HEAD_DIM)
            decay_blk = float((1.0 - 2.0 ** (-5.0 - h)) ** BLK)
            q = q_ref[rows, cols]
            k = k_ref[rows, cols]
            v = v_ref[rows, cols]
            s = lax.dot_general(q, k, NT, preferred_element_type=F32) * dmat_ref[h]
            intra = jnp.dot(s.astype(BF16), v, preferred_element_type=F32)
            state = state_ref[h]
            if blk == 0:
                state = jnp.where(first, init_ref[h], state)
            qx = (q.astype(F32) * rq_ref[h]).astype(BF16)
            cross = jnp.dot(qx, state.astype(BF16), preferred_element_type=F32)
            kz = (k.astype(F32) * rk_ref[h]).astype(BF16)
            kv = lax.dot_general(kz, v, TN, preferred_element_type=F32)
            state_ref[h] = decay_blk * state + kv

            o = intra + cross
            mu = jnp.mean(o, axis=-1, keepdims=True)
            oc = o - mu
            var = jnp.mean(oc * oc, axis=-1, keepdims=True)
            y = oc * lax.rsqrt(var + EPS) * gret_ref[:, cols]
            r_ref[rows, cols] = (y * _silu(gate_ref[rows, cols].astype(F32))).astype(BF16)


def _mix_outproj(proj, pm, d, w, h, gret, gf, layer, seq, final_norm):
    m, dm = h.shape
    tm = 2 * QT
    ntiles = m // tm
    ntile_q = seq // QT
    cur = lambda i: jnp.minimum(i, ntiles - 1)
    prev = lambda i: jnp.maximum(i - 1, 0)

    def d_block(which):
        def index(i):
            tile = 2 * prev(i) + which
            b, t = tile // ntile_q, tile % ntile_q
            pos = jnp.where(t < ntile_q // 2, 2 * t, 2 * (ntile_q - 1 - t) + 1)
            return b * ntile_q + pos, 0
        return index

    return pl.pallas_call(
        functools.partial(_mix_outproj_kernel, final_norm=final_norm,
                          tiles_per_seq=seq // tm),
        out_shape=jax.ShapeDtypeStruct((m, dm), F32),
        grid=(ntiles + 1,),
        in_specs=[
            pl.BlockSpec((tm, MIX_HALF), lambda i: (cur(i), 0)),
            pl.BlockSpec((tm, MIX_HALF), lambda i: (cur(i), 1)),
            pl.BlockSpec((tm, MIX_HALF), lambda i: (cur(i), 2)),
            pl.BlockSpec((tm, MIX_HALF), lambda i: (cur(i), 3)),
            pl.BlockSpec((None, N_META, MIX_HALF), lambda i: (0, 0, 1)),
            pl.BlockSpec((None, N_META, MIX_HALF), lambda i: (0, 0, 2)),
            pl.BlockSpec((None, 1, MIX_HALF), lambda i: (layer, 0, 0)),
            pl.BlockSpec((QT, MIX_HALF), d_block(0)),
            pl.BlockSpec((QT, MIX_HALF), d_block(1)),
            pl.BlockSpec((MIX_HALF, dm), lambda i: (0, 0), pipeline_mode=pl.Buffered(1)),
            pl.BlockSpec((MIX_HALF, dm), lambda i: (1, 0), pipeline_mode=pl.Buffered(1)),
            pl.BlockSpec((tm, dm), lambda i: (prev(i), 0)),
            pl.BlockSpec((1, dm), lambda i: (0, 0)),
        ],
        out_specs=pl.BlockSpec((tm, dm), lambda i: (prev(i), 0)),
        scratch_shapes=[
            pltpu.VMEM((tm, MIX_HALF), BF16),
            pltpu.VMEM((HEADS, HEAD_DIM, HEAD_DIM), F32),
            pltpu.VMEM((HEADS, HEAD_DIM, HEAD_DIM), F32),
            pltpu.VMEM((HEADS, BLK, BLK), F32),
            pltpu.VMEM((HEADS, BLK, HEAD_DIM), F32),
            pltpu.VMEM((HEADS, BLK, HEAD_DIM), F32),
        ],
        compiler_params=pltpu.CompilerParams(
            dimension_semantics=("arbitrary",),
            vmem_limit_bytes=VMEM_LIMIT),
        name="mix_outproj",
    )(proj, proj, proj, proj, pm, pm, gret, d, d, w, w, h, gf)


def kernel(x, meta_tokens, norm_g, w_in, w_out, ret_norm_g, diff_norm_g,
           lambda_q1, lambda_k1, lambda_q2, lambda_k2, final_norm_g):
    batch, seq, dm = x.shape
    depth = w_in.shape[0]
    m = batch * seq
    assert seq % QT == 0 and meta_tokens.shape[0] == N_META

    h = x.reshape(m, dm)
    hm = meta_tokens.astype(x.dtype)
    gf = final_norm_g.reshape(1, dm)
    per_layer = lambda a: a.reshape(depth, 1, a.shape[-1])
    norm_g, ret_norm_g, diff_norm_g = map(per_layer, (norm_g, ret_norm_g, diff_norm_g))
    lams = tuple(map(per_layer, (lambda_q1, lambda_k1, lambda_q2, lambda_k2)))

    tm = m // 8
    w_in_b = None
    for i in range(depth):
        last = i == depth - 1
        lam_init = 0.8 - 0.6 * float(np.exp(-0.3 * i))
        if w_in_b is None:
            proj, pm = _inproj(h, hm, norm_g, w_in, i, tm, 1024)
        else:
            proj, pm = _inproj(h, hm, norm_g, w_in_b, i, tm, 2048)
        if not last:
            hm = _meta_layer(pm, hm, w_out, ret_norm_g, diff_norm_g, lams, i, lam_init)
        casts = [(w_out, i)] + ([] if last else [(w_in, i + 1)])
        d, cast = _diffattn(proj, pm, diff_norm_g, lams, i, batch, seq, lam_init, casts)
        w_in_b = None if last else cast[1]
        h = _mix_outproj(proj, pm, d, cast[0], h, ret_norm_g, gf, i, seq, final_norm=last)

    return h.reshape(batch, seq, dm)
```

```python
import functools

import numpy as np
import jax
import jax.numpy as jnp
from jax import lax
from jax.experimental import pallas as pl
from jax.experimental.pallas import tpu as pltpu

F32 = jnp.float32
BF16 = jnp.bfloat16

CHUNK = 64
N_META = 16
EPS = 1e-6
HEADS = 8
HEAD_DIM = 128
DIFF_DH = 64
MIX_HALF = HEADS * HEAD_DIM
BLK = 128
QT = 256
NEG = -1e30
LOG2E = 1.4426950408889634
HEADS_PER_STEP = 4
TILE_UNROLL = 4

VMEM_LIMIT = 56 * 1024 * 1024

NT = (((1,), (1,)), ((), ()))
TN = (((0,), (0,)), ((), ()))


def _log_g(h):
    return float(np.log(1.0 - 2.0 ** (-5.0 - h)))


def _silu(x):
    return x * (1.0 / (1.0 + jnp.exp(-x)))


def _lambda(lq1_ref, lk1_ref, lq2_ref, lk2_ref, lam_init):
    return (jnp.exp(jnp.sum(lq1_ref[...] * lk1_ref[...], axis=-1, keepdims=True))
            - jnp.exp(jnp.sum(lq2_ref[...] * lk2_ref[...], axis=-1, keepdims=True))
            + lam_init)


def _pad_rows(a, rows):
    return jnp.concatenate([a, jnp.zeros((rows - a.shape[0], a.shape[1]), a.dtype)], axis=0)


def _split_components(q):
    lane = lax.broadcasted_iota(jnp.int32, q.shape, 1)
    zero = jnp.zeros((), BF16)
    qscaled = q * jnp.asarray(DIFF_DH ** -0.5, BF16)
    return jnp.concatenate([jnp.where(lane < DIFF_DH, qscaled, zero),
                            jnp.where(lane >= DIFF_DH, qscaled, zero)], axis=0)


def _rms_norm_bf16(x, g):
    ms = jnp.mean(x * x, axis=-1, keepdims=True)
    return (x * lax.rsqrt(ms + EPS) * g).astype(BF16)


def _inproj_kernel(h_ref, hm_ref, g_ref, w_ref, o_ref, pm_ref, u_ref):
    tm = h_ref.shape[0]
    half = tm // 2

    def project(lo, hi):
        return jnp.dot(u_ref[lo:hi], w_ref[...].astype(BF16),
                       preferred_element_type=F32).astype(BF16)

    @pl.when(pl.program_id(1) == 0)
    def _():
        u_ref[:half] = _rms_norm_bf16(h_ref[:half], g_ref[...])
        o_ref[:half] = project(0, half)
        u_ref[half:tm] = _rms_norm_bf16(h_ref[half:], g_ref[...])
        u_ref[tm:] = _rms_norm_bf16(hm_ref[...], g_ref[...])
        out = project(half, tm + N_META)
        o_ref[half:] = out[:tm - half]
        pm_ref[...] = out[tm - half:]

    @pl.when(pl.program_id(1) > 0)
    def _():
        out = project(0, tm + N_META)
        o_ref[...] = out[:tm]
        pm_ref[...] = out[tm:]


def _inproj(h, hm, g, w, layer, tm, tn):
    m, d = h.shape
    n = w.shape[-1]
    if w.ndim == 3:
        w_spec = pl.BlockSpec((None, d, tn), lambda i, j: (layer, 0, j))
    else:
        w_spec = pl.BlockSpec((d, tn), lambda i, j: (0, j))
    return pl.pallas_call(
        _inproj_kernel,
        out_shape=[jax.ShapeDtypeStruct((m, n), BF16),
                   jax.ShapeDtypeStruct((m // tm, N_META, n), BF16)],
        grid=(m // tm, n // tn),
        in_specs=[
            pl.BlockSpec((tm, d), lambda i, j: (i, 0)),
            pl.BlockSpec((N_META, d), lambda i, j: (0, 0)),
            pl.BlockSpec((None, 1, d), lambda i, j: (layer, 0, 0)),
            w_spec,
        ],
        out_specs=[pl.BlockSpec((tm, tn), lambda i, j: (i, j)),
                   pl.BlockSpec((None, N_META, tn), lambda i, j: (i, 0, j))],
        scratch_shapes=[pltpu.VMEM((tm + N_META, d), BF16)],
        compiler_params=pltpu.CompilerParams(
            dimension_semantics=("parallel", "arbitrary"),
            vmem_limit_bytes=VMEM_LIMIT),
        name="inproj",
    )(h, hm, g, w)


def _meta_kernel(pm_ref, hm_ref, wr_ref, wd_ref, gret_ref, gdiff_ref,
                 lq1_ref, lk1_ref, lq2_ref, lk2_ref, hm_out_ref, *, lam_init):
    section = lambda s: pm_ref[:, s * MIX_HALF:(s + 1) * MIX_HALF]
    rq, rk, rv, rgate, dq, dk, dv, dgate = [section(s) for s in range(8)]
    t = lax.broadcasted_iota(jnp.int32, (BLK, BLK), 0)
    s_i = lax.broadcasted_iota(jnp.int32, (BLK, BLK), 1)
    dist = jnp.abs(t - s_i).astype(F32)
    r2 = lax.broadcasted_iota(jnp.int32, (2 * BLK, BLK), 0) % BLK
    c2 = lax.broadcasted_iota(jnp.int32, (2 * BLK, BLK), 1)
    dist2 = jnp.abs(r2 - c2).astype(F32)
    lam = _lambda(lq1_ref, lk1_ref, lq2_ref, lk2_ref, lam_init)
    r_parts, d_parts = [], []
    for h in range(HEADS):
        cols = slice(h * HEAD_DIM, (h + 1) * HEAD_DIM)
        q = _pad_rows(rq[:, cols], BLK)
        k = _pad_rows(rk[:, cols], BLK)
        v = _pad_rows(rv[:, cols], BLK)
        dmat = jnp.exp(_log_g(h) * dist) * (HEAD_DIM ** -0.5)
        s = lax.dot_general(q, k, NT, preferred_element_type=F32) * dmat
        o = jnp.dot(s.astype(BF16), v, preferred_element_type=F32)[:N_META]
        mu = jnp.mean(o, axis=-1, keepdims=True)
        oc = o - mu
        var = jnp.mean(oc * oc, axis=-1, keepdims=True)
        y = oc * lax.rsqrt(var + EPS) * gret_ref[:, cols]
        r_parts.append((y * _silu(rgate[:, cols].astype(F32))).astype(BF16))
        slope = 2.0 ** (-(h + 1.0))
        qs = _split_components(_pad_rows(dq[:, cols], BLK))
        k = _pad_rows(dk[:, cols], BLK)
        v = _pad_rows(dv[:, cols], BLK)
        bias = jnp.where(c2 < N_META, -slope * dist2, NEG)
        s = lax.dot_general(qs, k, NT, preferred_element_type=F32) + bias
        pr = jnp.exp(s - jnp.max(s, axis=-1, keepdims=True))
        a = (jnp.dot(pr.astype(BF16), v, preferred_element_type=F32)
             / jnp.sum(pr, axis=-1, keepdims=True))
        d = a[:N_META] - lam * a[BLK:BLK + N_META]
        y = d * lax.rsqrt(jnp.mean(d * d, axis=-1, keepdims=True) + EPS)
        y = y * gdiff_ref[:, cols] * (1.0 - lam_init)
        d_parts.append((y * _silu(dgate[:, cols].astype(F32))).astype(BF16))
    r = jnp.concatenate(r_parts, axis=1)
    d = jnp.concatenate(d_parts, axis=1)
    y = jnp.dot(r, wr_ref[...].astype(BF16), preferred_element_type=F32)
    y = y + jnp.dot(d, wd_ref[...].astype(BF16), preferred_element_type=F32)
    hm_out_ref[...] = hm_ref[...] + y


def _meta_layer(pm, hm, w_out, gret, gdiff, lams, layer, lam_init):
    dm = hm.shape[1]
    n = pm.shape[-1]
    vec = lambda width: pl.BlockSpec((None, 1, width), lambda i: (layer, 0, 0))
    return pl.pallas_call(
        functools.partial(_meta_kernel, lam_init=lam_init),
        out_shape=jax.ShapeDtypeStruct((N_META, dm), F32),
        grid=(1,),
        in_specs=[
            pl.BlockSpec((None, N_META, n), lambda i: (0, 0, 0)),
            pl.BlockSpec((N_META, dm), lambda i: (0, 0)),
            pl.BlockSpec((None, MIX_HALF, dm), lambda i: (layer, 0, 0),
                         pipeline_mode=pl.Buffered(1)),
            pl.BlockSpec((None, MIX_HALF, dm), lambda i: (layer, 1, 0),
                         pipeline_mode=pl.Buffered(1)),
            vec(MIX_HALF), vec(MIX_HALF),
            vec(DIFF_DH), vec(DIFF_DH), vec(DIFF_DH), vec(DIFF_DH),
        ],
        out_specs=pl.BlockSpec((N_META, dm), lambda i: (0, 0)),
        compiler_params=pltpu.CompilerParams(
            dimension_semantics=("arbitrary",),
            vmem_limit_bytes=VMEM_LIMIT),
        name="meta_layer",
    )(pm, hm, w_out, w_out, gret, gdiff, *lams)


def _diffattn_kernel(*refs, lam_init, ncast):
    (slopes_ref, qa_ref, qb_ref, k_ref, v_ref, km_ref, vm_ref, gatea_ref, gateb_ref, gn_ref,
     lq1_ref, lk1_ref, lq2_ref, lk2_ref) = refs[:14]
    cast_src = refs[14:14 + ncast]
    o_ref = refs[14 + ncast]
    cast_dst = refs[15 + ncast:15 + 2 * ncast]
    (qs_ref, s_ref, sm_ref, mrun_ref, mrow_ref, acc_ref,
     vext_ref, kmext_ref, vmext_ref, dbias_ref) = refs[15 + 2 * ncast:]

    for src, dst in zip(cast_src, cast_dst):
        dst[...] = src[...].astype(BF16)

    grp = pl.program_id(1)
    step = pl.program_id(2)
    hb = HEADS_PER_STEP
    seq = k_ref.shape[0]
    ntile = seq // QT
    head_cols = lambda hh: slice(hh * HEAD_DIM, (hh + 1) * HEAD_DIM)
    slope2 = lambda hh: slopes_ref[grp * hb + hh] * LOG2E
    tiles = ((step, 0, qa_ref, gatea_ref),
             (ntile - 1 - step, step + 1, qb_ref, gateb_ref))

    @pl.when(step == 0)
    def _():
        r = lax.broadcasted_iota(jnp.int32, (QT, QT), 0)
        c = lax.broadcasted_iota(jnp.int32, (QT, QT), 1)
        allowed = (c // CHUNK) <= (r // CHUNK)
        rel = (r - jnp.abs(r - c)).astype(F32)
        for hh in range(hb):
            cols = head_cols(hh)
            dbias_ref[hh] = jnp.where(allowed, slope2(hh) * rel, NEG)
            vext_ref[hh, :, :HEAD_DIM] = v_ref[:, cols]
            vext_ref[hh, :, HEAD_DIM:] = jnp.ones((seq, HEAD_DIM), BF16)
            kmext_ref[hh] = _pad_rows(km_ref[:, cols], BLK)
            vmext_ref[hh, :, :HEAD_DIM] = _pad_rows(vm_ref[:, cols], BLK)
            vmext_ref[hh, :, HEAD_DIM:] = jnp.ones((BLK, HEAD_DIM), BF16)

    def frame_scores(x, hh, j):
        start = pl.multiple_of(j * QT, QT)
        kt = k_ref[pl.ds(start, QT), head_cols(hh)]
        return lax.dot_general(qs_ref[x, hh], kt, NT, preferred_element_type=F32) * LOG2E

    def keep(x, hh, slot, s):
        s_ref[hh, slot] = s
        mrun_ref[x, hh] = jnp.maximum(mrun_ref[x, hh], jnp.maximum(s[:, :BLK], s[:, BLK:]))

    colm = lax.broadcasted_iota(jnp.int32, (1, BLK), 1)

    def first_scores(x, hh):
        t, base, q_ref, _ = tiles[x]
        relm = (colm - N_META - t * QT).astype(F32)
        qs_ref[x, hh] = _split_components(q_ref[:, head_cols(hh)])
        s = lax.dot_general(qs_ref[x, hh], kmext_ref[hh], NT, preferred_element_type=F32)
        s = s * LOG2E + jnp.where(colm < N_META, slope2(hh) * relm, NEG)
        sm_ref[x, hh] = s
        mrun_ref[x, hh] = s
        s = frame_scores(x, hh, t)
        keep(x, hh, base + t,
             jnp.concatenate([s[:QT] + dbias_ref[hh], s[QT:] + dbias_ref[hh]], axis=0))

    col = lax.broadcasted_iota(jnp.int32, (1, QT), 1)

    def for_tile_groups(n, group):
        def trip(i, carry):
            group([TILE_UNROLL * i + u for u in range(TILE_UNROLL)])
            return carry

        lax.fori_loop(0, n // TILE_UNROLL, trip, 0)
        done = (n // TILE_UNROLL) * TILE_UNROLL
        for rest in range(1, TILE_UNROLL):
            @pl.when(n - done == rest)
            def _(rest=rest):
                group([done + u for u in range(rest)])

    def off_diag_loops(x, t, base):
        def off_diag(js):
            for j in js:
                rel = (col + (j - t) * QT).astype(F32)
                for hh in range(hb):
                    keep(x, hh, base + j, frame_scores(x, hh, j) + slope2(hh) * rel)

        for_tile_groups(t, off_diag)

    def row_max(x, hh):
        mrow_ref[x, hh] = jnp.broadcast_to(
            jnp.max(mrun_ref[x, hh], axis=-1, keepdims=True), (2 * QT, BLK))
        p = jnp.exp2(sm_ref[x, hh] - mrow_ref[x, hh])
        acc_ref[x, hh] = jnp.dot(p.astype(BF16), vmext_ref[hh], preferred_element_type=F32)

    def pv_loops(x, t, base):
        def pv(j, hh):
            start = pl.multiple_of(j * QT, QT)
            s = s_ref[hh, base + j]
            m = mrow_ref[x, hh]
            p = jnp.concatenate([jnp.exp2(s[:, :BLK] - m), jnp.exp2(s[:, BLK:] - m)], axis=1)
            return jnp.dot(p.astype(BF16), vext_ref[hh, pl.ds(start, QT), :],
                           preferred_element_type=F32)

        def pv_group(js):
            for hh in range(hb):
                acc_ref[x, hh] = acc_ref[x, hh] + functools.reduce(
                    lambda a, b: a + b, [pv(j, hh) for j in js])

        for_tile_groups(t + 1, pv_group)

    lam = _lambda(lq1_ref, lk1_ref, lq2_ref, lk2_ref, lam_init)

    def finish(x, hh):
        gate_ref = tiles[x][3]
        cols = head_cols(hh)
        acc = acc_ref[x, hh]
        a = acc[:, :HEAD_DIM] / acc[:, HEAD_DIM:]
        d = a[:QT] - lam * a[QT:]
        y = d * lax.rsqrt(jnp.mean(d * d, axis=-1, keepdims=True) + EPS)
        y = y * (gn_ref[:, cols] * (1.0 - lam_init))
        o_ref[x * QT:(x + 1) * QT, cols] = (
            y * _silu(gate_ref[:, cols].astype(F32))).astype(BF16)

    (ta, base_a, _, _), (tb, base_b, _, _) = tiles
    for hh in range(hb):
        first_scores(0, hh)
    off_diag_loops(0, ta, base_a)
    for hh in range(hb):
        first_scores(1, hh)
        row_max(0, hh)
    pv_loops(0, ta, base_a)
    off_diag_loops(1, tb, base_b)
    for hh in range(hb):
        row_max(1, hh)
        finish(0, hh)
    pv_loops(1, tb, base_b)
    for hh in range(hb):
        finish(1, hh)


def _diffattn(proj, pm, gn, lams, layer, batch, seq, lam_init, casts):
    m = proj.shape[0]
    ntile = seq // QT
    hb = HEADS_PER_STEP
    ngrp = HEADS // hb
    width = hb * HEAD_DIM
    slopes = jnp.asarray(2.0 ** (-8.0 * np.arange(1, HEADS + 1) / HEADS), F32)
    qcol, kcol, vcol, gcol = (4 * ngrp, 5 * ngrp, 6 * ngrp, 7 * ngrp)
    vec = pl.BlockSpec((None, 1, DIFF_DH), lambda b, g, t: (layer, 0, 0))
    tile_a = lambda b, t: b * ntile + t
    tile_b = lambda b, t: b * ntile + ntile - 1 - t
    nstep = batch * ngrp * (ntile // 2)
    linear = lambda b, g, t: (b * ngrp + g) * (ntile // 2) + t
    cast_in, cast_out_specs, cast_out_shapes, cast_args = [], [], [], []
    for w, wl in casts:
        rows, ncol = w.shape[1:]
        slab = rows // nstep
        cast_in.append(pl.BlockSpec((None, slab, ncol),
                                    lambda b, g, t, wl=wl: (wl, linear(b, g, t), 0)))
        cast_out_specs.append(pl.BlockSpec((slab, ncol), lambda b, g, t: (linear(b, g, t), 0)))
        cast_out_shapes.append(jax.ShapeDtypeStruct((rows, ncol), BF16))
        cast_args.append(w)
    outs = pl.pallas_call(
        functools.partial(_diffattn_kernel, lam_init=lam_init, ncast=len(casts)),
        out_shape=[jax.ShapeDtypeStruct((m, MIX_HALF), BF16)] + cast_out_shapes,
        grid=(batch, ngrp, ntile // 2),
        in_specs=[
            pl.BlockSpec(memory_space=pltpu.SMEM),
            pl.BlockSpec((QT, width), lambda b, g, t: (tile_a(b, t), qcol + g)),
            pl.BlockSpec((QT, width), lambda b, g, t: (tile_b(b, t), qcol + g)),
            pl.BlockSpec((seq, width), lambda b, g, t: (b, kcol + g)),
            pl.BlockSpec((seq, width), lambda b, g, t: (b, vcol + g)),
            pl.BlockSpec((None, N_META, width), lambda b, g, t: (0, 0, kcol + g)),
            pl.BlockSpec((None, N_META, width), lambda b, g, t: (0, 0, vcol + g)),
            pl.BlockSpec((QT, width), lambda b, g, t: (tile_a(b, t), gcol + g)),
            pl.BlockSpec((QT, width), lambda b, g, t: (tile_b(b, t), gcol + g)),
            pl.BlockSpec((None, 1, width), lambda b, g, t: (layer, 0, g)),
            vec, vec, vec, vec,
        ] + cast_in,
        out_specs=[pl.BlockSpec((2 * QT, width), lambda b, g, t: (b * (ntile // 2) + t, g))]
        + cast_out_specs,
        scratch_shapes=[
            pltpu.VMEM((2, hb, 2 * QT, HEAD_DIM), BF16),
            pltpu.VMEM((hb, ntile + 1, 2 * QT, QT), F32),
            pltpu.VMEM((2, hb, 2 * QT, BLK), F32),
            pltpu.VMEM((2, hb, 2 * QT, BLK), F32),
            pltpu.VMEM((2, hb, 2 * QT, BLK), F32),
            pltpu.VMEM((2, hb, 2 * QT, 2 * HEAD_DIM), F32),
            pltpu.VMEM((hb, seq, 2 * HEAD_DIM), BF16),
            pltpu.VMEM((hb, BLK, HEAD_DIM), BF16),
            pltpu.VMEM((hb, BLK, 2 * HEAD_DIM), BF16),
            pltpu.VMEM((hb, QT, QT), F32),
        ],
        compiler_params=pltpu.CompilerParams(
            dimension_semantics=("parallel", "parallel", "arbitrary"),
            vmem_limit_bytes=VMEM_LIMIT),
        name="diffattn",
    )(slopes, proj, proj, proj, proj, pm, pm, proj, proj, gn, *lams, *cast_args)
    return outs[0], outs[1:]


def _mix_outproj_kernel(*refs, final_norm, tiles_per_seq, with_cast):
    (q_ref, k_ref, v_ref, gate_ref, km_ref, vm_ref, gret_ref,
     da_ref, db_ref, wr_ref, wd_ref, h_ref, gf_ref) = refs[:13]
    r_ref, state_ref, init_ref, dmat_ref, rq_ref, rk_ref = refs[len(refs) - 6:]
    if with_cast:
        cast_src, o_ref, cast_dst = refs[13:16]
        cast_dst[...] = cast_src[...].astype(BF16)
    else:
        o_ref = refs[13]
    i = pl.program_id(0)
    scale = HEAD_DIM ** -0.5
    tm = q_ref.shape[0]

    @pl.when(i == 0)
    def _():
        r_ref[...] = jnp.zeros_like(r_ref)
        state_ref[...] = jnp.zeros_like(state_ref)
        t = lax.broadcasted_iota(jnp.int32, (BLK, BLK), 0)
        s = lax.broadcasted_iota(jnp.int32, (BLK, BLK), 1)
        dist = jnp.abs(t - s).astype(F32)
        allowed = (s // CHUNK) <= (t // CHUNK)
        tf = t.astype(F32)
        mf = lax.broadcasted_iota(jnp.int32, (N_META, HEAD_DIM), 0).astype(F32)
        for h in range(HEADS):
            cols = slice(h * HEAD_DIM, (h + 1) * HEAD_DIM)
            log_g = _log_g(h)
            dmat_ref[h] = jnp.where(allowed, jnp.exp(log_g * dist) * scale, 0.0)
            rq_ref[h] = jnp.exp(log_g * (tf + 1.0))
            rk_ref[h] = jnp.exp(log_g * (BLK - 1.0 - tf)) * scale
            zeta = jnp.exp(log_g * (N_META - 1.0 - mf)) * scale
            kz = (km_ref[:, cols].astype(F32) * zeta).astype(BF16)
            init_ref[h] = lax.dot_general(_pad_rows(kz, BLK), _pad_rows(vm_ref[:, cols], BLK),
                                          TN, preferred_element_type=F32)

    r = r_ref[...]
    for half, d_ref in enumerate((da_ref, db_ref)):
        rows = slice(half * QT, (half + 1) * QT)
        y = jnp.dot(r[rows], wr_ref[...], preferred_element_type=F32)
        y = y + jnp.dot(d_ref[...], wd_ref[...], preferred_element_type=F32)
        hn = h_ref[rows] + y
        if final_norm:
            ms = jnp.mean(hn * hn, axis=-1, keepdims=True)
            hn = hn * lax.rsqrt(ms + EPS) * gf_ref[...]
        o_ref[rows] = hn

    first = (i % tiles_per_seq) == 0
    for blk in range(tm // BLK):
        rows = slice(blk * BLK, (blk + 1) * BLK)
        for h in range(HEADS):
            cols = slice(h * HEAD_DIM, (h + 1) * HEAD_DIM)
            decay_blk = float((1.0 - 2.0 ** (-5.0 - h)) ** BLK)
            q = q_ref[rows, cols]
            k = k_ref[rows, cols]
            v = v_ref[rows, cols]
            s = lax.dot_general(q, k, NT, preferred_element_type=F32) * dmat_ref[h]
            intra = jnp.dot(s.astype(BF16), v, preferred_element_type=F32)
            state = state_ref[h]
            if blk == 0:
                state = jnp.where(first, init_ref[h], state)
            qx = (q.astype(F32) * rq_ref[h]).astype(BF16)
            cross = jnp.dot(qx, state.astype(BF16), preferred_element_type=F32)
            kz = (k.astype(F32) * rk_ref[h]).astype(BF16)
            kv = lax.dot_general(kz, v, TN, preferred_element_type=F32)
            state_ref[h] = decay_blk * state + kv

            o = intra + cross
            mu = jnp.mean(o, axis=-1, keepdims=True)
            oc = o - mu
            var = jnp.mean(oc * oc, axis=-1, keepdims=True)
            y = oc * lax.rsqrt(var + EPS) * gret_ref[:, cols]
            r_ref[rows, cols] = (y * _silu(gate_ref[rows, cols].astype(F32))).astype(BF16)


def _mix_outproj(proj, pm, d, w, h, gret, gf, layer, seq, final_norm, cast=None):
    m, dm = h.shape
    tm = 2 * QT
    ntiles = m // tm
    cast_in, cast_out_shape, cast_out_spec, cast_args = [], [], [], []
    if cast is not None:
        wc, wl = cast
        rows, ncol = wc.shape[1:]
        slab = rows // ntiles
        slab_of = lambda i: jnp.minimum(i, ntiles - 1)
        cast_in = [pl.BlockSpec((None, slab, ncol), lambda i: (wl, slab_of(i), 0))]
        cast_out_spec = [pl.BlockSpec((slab, ncol), lambda i: (slab_of(i), 0))]
        cast_out_shape = [jax.ShapeDtypeStruct((rows, ncol), BF16)]
        cast_args = [wc]
    ntile_q = seq // QT
    cur = lambda i: jnp.minimum(i, ntiles - 1)
    prev = lambda i: jnp.maximum(i - 1, 0)

    def d_block(which):
        def index(i):
            tile = 2 * prev(i) + which
            b, t = tile // ntile_q, tile % ntile_q
            pos = jnp.where(t < ntile_q // 2, 2 * t, 2 * (ntile_q - 1 - t) + 1)
            return b * ntile_q + pos, 0
        return index

    return pl.pallas_call(
        functools.partial(_mix_outproj_kernel, final_norm=final_norm,
                          tiles_per_seq=seq // tm, with_cast=cast is not None),
        out_shape=[jax.ShapeDtypeStruct((m, dm), F32)] + cast_out_shape,
        grid=(ntiles + 1,),
        in_specs=[
            pl.BlockSpec((tm, MIX_HALF), lambda i: (cur(i), 0)),
            pl.BlockSpec((tm, MIX_HALF), lambda i: (cur(i), 1)),
            pl.BlockSpec((tm, MIX_HALF), lambda i: (cur(i), 2)),
            pl.BlockSpec((tm, MIX_HALF), lambda i: (cur(i), 3)),
            pl.BlockSpec((None, N_META, MIX_HALF), lambda i: (0, 0, 1)),
            pl.BlockSpec((None, N_META, MIX_HALF), lambda i: (0, 0, 2)),
            pl.BlockSpec((None, 1, MIX_HALF), lambda i: (layer, 0, 0)),
            pl.BlockSpec((QT, MIX_HALF), d_block(0)),
            pl.BlockSpec((QT, MIX_HALF), d_block(1)),
            pl.BlockSpec((MIX_HALF, dm), lambda i: (0, 0), pipeline_mode=pl.Buffered(1)),
            pl.BlockSpec((MIX_HALF, dm), lambda i: (1, 0), pipeline_mode=pl.Buffered(1)),
            pl.BlockSpec((tm, dm), lambda i: (prev(i), 0)),
            pl.BlockSpec((1, dm), lambda i: (0, 0)),
        ] + cast_in,
        out_specs=[pl.BlockSpec((tm, dm), lambda i: (prev(i), 0))] + cast_out_spec,
        scratch_shapes=[
            pltpu.VMEM((tm, MIX_HALF), BF16),
            pltpu.VMEM((HEADS, HEAD_DIM, HEAD_DIM), F32),
            pltpu.VMEM((HEADS, HEAD_DIM, HEAD_DIM), F32),
            pltpu.VMEM((HEADS, BLK, BLK), F32),
            pltpu.VMEM((HEADS, BLK, HEAD_DIM), F32),
            pltpu.VMEM((HEADS, BLK, HEAD_DIM), F32),
        ],
        compiler_params=pltpu.CompilerParams(
            dimension_semantics=("arbitrary",),
            vmem_limit_bytes=VMEM_LIMIT),
        name="mix_outproj",
    )(proj, proj, proj, proj, pm, pm, gret, d, d, w, w, h, gf, *cast_args)


def kernel(x, meta_tokens, norm_g, w_in, w_out, ret_norm_g, diff_norm_g,
           lambda_q1, lambda_k1, lambda_q2, lambda_k2, final_norm_g):
    batch, seq, dm = x.shape
    depth = w_in.shape[0]
    m = batch * seq
    assert seq % QT == 0 and meta_tokens.shape[0] == N_META

    h = x.reshape(m, dm)
    hm = meta_tokens.astype(x.dtype)
    gf = final_norm_g.reshape(1, dm)
    per_layer = lambda a: a.reshape(depth, 1, a.shape[-1])
    norm_g, ret_norm_g, diff_norm_g = map(per_layer, (norm_g, ret_norm_g, diff_norm_g))
    lams = tuple(map(per_layer, (lambda_q1, lambda_k1, lambda_q2, lambda_k2)))

    tm = m // 8
    w_in_b = None
    for i in range(depth):
        last = i == depth - 1
        lam_init = 0.8 - 0.6 * float(np.exp(-0.3 * i))
        if w_in_b is None:
            proj, pm = _inproj(h, hm, norm_g, w_in, i, tm, 1024)
        else:
            proj, pm = _inproj(h, hm, norm_g, w_in_b, i, tm, 2048)
        if not last:
            hm = _meta_layer(pm, hm, w_out, ret_norm_g, diff_norm_g, lams, i, lam_init)
        d, (w_out_b,) = _diffattn(proj, pm, diff_norm_g, lams, i, batch, seq, lam_init,
                                  [(w_out, i)])
        h, *w_in_next = _mix_outproj(proj, pm, d, w_out_b, h, ret_norm_g, gf, i, seq,
                                     final_norm=last, cast=None if last else (w_in, i + 1))
        w_in_b = w_in_next[0] if w_in_next else None

    return h.reshape(batch, seq, dm)
```

```python
import functools

import numpy as np
import jax
import jax.numpy as jnp
from jax import lax
from jax.experimental import pallas as pl
from jax.experimental.pallas import tpu as pltpu

F32 = jnp.float32
BF16 = jnp.bfloat16

CHUNK = 64
N_META = 16
EPS = 1e-6
HEADS = 8
HEAD_DIM = 128
DIFF_DH = 64
MIX_HALF = HEADS * HEAD_DIM
BLK = 128
QT = 256
NEG = -1e30
LOG2E = 1.4426950408889634
HEADS_PER_STEP = 4
TILE_UNROLL = 4

VMEM_LIMIT = 56 * 1024 * 1024

NT = (((1,), (1,)), ((), ()))
TN = (((0,), (0,)), ((), ()))


def _log_g(h):
    return float(np.log(1.0 - 2.0 ** (-5.0 - h)))


def _silu(x):
    return x * (1.0 / (1.0 + jnp.exp(-x)))


def _lambda(lq1_ref, lk1_ref, lq2_ref, lk2_ref, lam_init):
    return (jnp.exp(jnp.sum(lq1_ref[...] * lk1_ref[...], axis=-1, keepdims=True))
            - jnp.exp(jnp.sum(lq2_ref[...] * lk2_ref[...], axis=-1, keepdims=True))
            + lam_init)


def _pad_rows(a, rows):
    return jnp.concatenate([a, jnp.zeros((rows - a.shape[0], a.shape[1]), a.dtype)], axis=0)


def _split_components(q):
    lane = lax.broadcasted_iota(jnp.int32, q.shape, 1)
    zero = jnp.zeros((), BF16)
    qscaled = q * jnp.asarray(DIFF_DH ** -0.5, BF16)
    return jnp.concatenate([jnp.where(lane < DIFF_DH, qscaled, zero),
                            jnp.where(lane >= DIFF_DH, qscaled, zero)], axis=0)


def _rms_norm_bf16(x, g):
    ms = jnp.mean(x * x, axis=-1, keepdims=True)
    return (x * lax.rsqrt(ms + EPS) * g).astype(BF16)


W_SLOTS = 3


def _inproj_kernel(h_ref, hm_ref, g_ref, w_ref, o_ref, pm_ref, u_ref, *ring, layer, tn):
    tm = h_ref.shape[0]
    half = tm // 2

    if ring:
        wbuf, wsem = ring
        nj = pl.num_programs(1)
        total = pl.num_programs(0) * nj
        step = pl.program_id(0) * nj + pl.program_id(1)

        def w_copy(s):
            col = pl.multiple_of((s % nj) * tn, tn)
            slot = s % W_SLOTS
            return pltpu.make_async_copy(w_ref.at[layer, :, pl.ds(col, tn)],
                                         wbuf.at[slot], wsem.at[slot])

        @pl.when(step == 0)
        def _():
            for s in range(W_SLOTS - 1):
                w_copy(s).start()

        @pl.when(step + (W_SLOTS - 1) < total)
        def _():
            w_copy(step + (W_SLOTS - 1)).start()

        w_copy(step).wait()
        weights = lambda: wbuf[step % W_SLOTS]
    else:
        weights = lambda: w_ref[...]

    def project(lo, hi):
        return jnp.dot(u_ref[lo:hi], weights().astype(BF16),
                       preferred_element_type=F32).astype(BF16)

    @pl.when(pl.program_id(1) == 0)
    def _():
        u_ref[:half] = _rms_norm_bf16(h_ref[:half], g_ref[...])
        o_ref[:half] = project(0, half)
        u_ref[half:tm] = _rms_norm_bf16(h_ref[half:], g_ref[...])
        u_ref[tm:] = _rms_norm_bf16(hm_ref[...], g_ref[...])
        out = project(half, tm + N_META)
        o_ref[half:] = out[:tm - half]
        pm_ref[...] = out[tm - half:]

    @pl.when(pl.program_id(1) > 0)
    def _():
        out = project(0, tm + N_META)
        o_ref[...] = out[:tm]
        pm_ref[...] = out[tm:]


def _inproj(h, hm, g, w, layer, tm, tn):
    m, d = h.shape
    n = w.shape[-1]
    if w.ndim == 3:
        w_spec = pl.BlockSpec(memory_space=pl.ANY)
        ring = [pltpu.VMEM((W_SLOTS, d, tn), w.dtype), pltpu.SemaphoreType.DMA((W_SLOTS,))]
    else:
        w_spec = pl.BlockSpec((d, tn), lambda i, j: (0, j))
        ring = []
    return pl.pallas_call(
        functools.partial(_inproj_kernel, layer=layer, tn=tn),
        out_shape=[jax.ShapeDtypeStruct((m, n), BF16),
                   jax.ShapeDtypeStruct((m // tm, N_META, n), BF16)],
        grid=(m // tm, n // tn),
        in_specs=[
            pl.BlockSpec((tm, d), lambda i, j: (i, 0)),
            pl.BlockSpec((N_META, d), lambda i, j: (0, 0)),
            pl.BlockSpec((None, 1, d), lambda i, j: (layer, 0, 0)),
            w_spec,
        ],
        out_specs=[pl.BlockSpec((tm, tn), lambda i, j: (i, j)),
                   pl.BlockSpec((None, N_META, tn), lambda i, j: (i, 0, j))],
        scratch_shapes=[pltpu.VMEM((tm + N_META, d), BF16)] + ring,
        compiler_params=pltpu.CompilerParams(
            dimension_semantics=("arbitrary", "arbitrary"),
            vmem_limit_bytes=VMEM_LIMIT),
        name="inproj",
    )(h, hm, g, w)


def _meta_kernel(pm_ref, hm_ref, wr_ref, wd_ref, gret_ref, gdiff_ref,
                 lq1_ref, lk1_ref, lq2_ref, lk2_ref, hm_out_ref, *, lam_init):
    section = lambda s: pm_ref[:, s * MIX_HALF:(s + 1) * MIX_HALF]
    rq, rk, rv, rgate, dq, dk, dv, dgate = [section(s) for s in range(8)]
    t = lax.broadcasted_iota(jnp.int32, (BLK, BLK), 0)
    s_i = lax.broadcasted_iota(jnp.int32, (BLK, BLK), 1)
    dist = jnp.abs(t - s_i).astype(F32)
    r2 = lax.broadcasted_iota(jnp.int32, (2 * BLK, BLK), 0) % BLK
    c2 = lax.broadcasted_iota(jnp.int32, (2 * BLK, BLK), 1)
    dist2 = jnp.abs(r2 - c2).astype(F32)
    lam = _lambda(lq1_ref, lk1_ref, lq2_ref, lk2_ref, lam_init)
    r_parts, d_parts = [], []
    for h in range(HEADS):
        cols = slice(h * HEAD_DIM, (h + 1) * HEAD_DIM)
        q = _pad_rows(rq[:, cols], BLK)
        k = _pad_rows(rk[:, cols], BLK)
        v = _pad_rows(rv[:, cols], BLK)
        dmat = jnp.exp(_log_g(h) * dist) * (HEAD_DIM ** -0.5)
        s = lax.dot_general(q, k, NT, preferred_element_type=F32) * dmat
        o = jnp.dot(s.astype(BF16), v, preferred_element_type=F32)[:N_META]
        mu = jnp.mean(o, axis=-1, keepdims=True)
        oc = o - mu
        var = jnp.mean(oc * oc, axis=-1, keepdims=True)
        y = oc * lax.rsqrt(var + EPS) * gret_ref[:, cols]
        r_parts.append((y * _silu(rgate[:, cols].astype(F32))).astype(BF16))
        slope = 2.0 ** (-(h + 1.0))
        qs = _split_components(_pad_rows(dq[:, cols], BLK))
        k = _pad_rows(dk[:, cols], BLK)
        v = _pad_rows(dv[:, cols], BLK)
        bias = jnp.where(c2 < N_META, -slope * dist2, NEG)
        s = lax.dot_general(qs, k, NT, preferred_element_type=F32) + bias
        pr = jnp.exp(s - jnp.max(s, axis=-1, keepdims=True))
        a = (jnp.dot(pr.astype(BF16), v, preferred_element_type=F32)
             / jnp.sum(pr, axis=-1, keepdims=True))
        d = a[:N_META] - lam * a[BLK:BLK + N_META]
        y = d * lax.rsqrt(jnp.mean(d * d, axis=-1, keepdims=True) + EPS)
        y = y * gdiff_ref[:, cols] * (1.0 - lam_init)
        d_parts.append((y * _silu(dgate[:, cols].astype(F32))).astype(BF16))
    r = jnp.concatenate(r_parts, axis=1)
    d = jnp.concatenate(d_parts, axis=1)
    y = jnp.dot(r, wr_ref[...].astype(BF16), preferred_element_type=F32)
    y = y + jnp.dot(d, wd_ref[...].astype(BF16), preferred_element_type=F32)
    hm_out_ref[...] = hm_ref[...] + y


def _meta_layer(pm, hm, w_out, gret, gdiff, lams, layer, lam_init):
    dm = hm.shape[1]
    n = pm.shape[-1]
    vec = lambda width: pl.BlockSpec((None, 1, width), lambda i: (layer, 0, 0))
    return pl.pallas_call(
        functools.partial(_meta_kernel, lam_init=lam_init),
        out_shape=jax.ShapeDtypeStruct((N_META, dm), F32),
        grid=(1,),
        in_specs=[
            pl.BlockSpec((None, N_META, n), lambda i: (0, 0, 0)),
            pl.BlockSpec((N_META, dm), lambda i: (0, 0)),
            pl.BlockSpec((None, MIX_HALF, dm), lambda i: (layer, 0, 0),
                         pipeline_mode=pl.Buffered(1)),
            pl.BlockSpec((None, MIX_HALF, dm), lambda i: (layer, 1, 0),
                         pipeline_mode=pl.Buffered(1)),
            vec(MIX_HALF), vec(MIX_HALF),
            vec(DIFF_DH), vec(DIFF_DH), vec(DIFF_DH), vec(DIFF_DH),
        ],
        out_specs=pl.BlockSpec((N_META, dm), lambda i: (0, 0)),
        compiler_params=pltpu.CompilerParams(
            dimension_semantics=("arbitrary",),
            vmem_limit_bytes=VMEM_LIMIT),
        name="meta_layer",
    )(pm, hm, w_out, w_out, gret, gdiff, *lams)


def _diffattn_kernel(*refs, lam_init, ncast):
    (slopes_ref, qa_ref, qb_ref, k_ref, v_ref, km_ref, vm_ref, gatea_ref, gateb_ref, gn_ref,
     lq1_ref, lk1_ref, lq2_ref, lk2_ref) = refs[:14]
    cast_src = refs[14:14 + ncast]
    o_ref = refs[14 + ncast]
    cast_dst = refs[15 + ncast:15 + 2 * ncast]
    (qs_ref, s_ref, sm_ref, mrun_ref, mrow_ref, acc_ref,
     vext_ref, kmext_ref, vmext_ref, dbias_ref) = refs[15 + 2 * ncast:]

    for src, dst in zip(cast_src, cast_dst):
        dst[...] = src[...].astype(BF16)

    grp = pl.program_id(1)
    step = pl.program_id(2)
    hb = HEADS_PER_STEP
    seq = k_ref.shape[0]
    ntile = seq // QT
    head_cols = lambda hh: slice(hh * HEAD_DIM, (hh + 1) * HEAD_DIM)
    slope2 = lambda hh: slopes_ref[grp * hb + hh] * LOG2E
    tiles = ((step, 0, qa_ref, gatea_ref),
             (ntile - 1 - step, step + 1, qb_ref, gateb_ref))

    @pl.when(step == 0)
    def _():
        r = lax.broadcasted_iota(jnp.int32, (QT, QT), 0)
        c = lax.broadcasted_iota(jnp.int32, (QT, QT), 1)
        allowed = (c // CHUNK) <= (r // CHUNK)
        rel = (r - jnp.abs(r - c)).astype(F32)
        for hh in range(hb):
            cols = head_cols(hh)
            dbias_ref[hh] = jnp.where(allowed, slope2(hh) * rel, NEG)
            vext_ref[hh, :, :HEAD_DIM] = v_ref[:, cols]
            vext_ref[hh, :, HEAD_DIM:] = jnp.ones((seq, HEAD_DIM), BF16)
            kmext_ref[hh] = _pad_rows(km_ref[:, cols], BLK)
            vmext_ref[hh, :, :HEAD_DIM] = _pad_rows(vm_ref[:, cols], BLK)
            vmext_ref[hh, :, HEAD_DIM:] = jnp.ones((BLK, HEAD_DIM), BF16)

    def frame_scores(x, hh, j):
        start = pl.multiple_of(j * QT, QT)
        kt = k_ref[pl.ds(start, QT), head_cols(hh)]
        return lax.dot_general(qs_ref[x, hh], kt, NT, preferred_element_type=F32) * LOG2E

    def keep(x, hh, slot, s):
        s_ref[hh, slot] = s
        mrun_ref[x, hh] = jnp.maximum(mrun_ref[x, hh], jnp.maximum(s[:, :BLK], s[:, BLK:]))

    colm = lax.broadcasted_iota(jnp.int32, (1, BLK), 1)

    def first_scores(x, hh):
        t, base, q_ref, _ = tiles[x]
        relm = (colm - N_META - t * QT).astype(F32)
        qs_ref[x, hh] = _split_components(q_ref[:, head_cols(hh)])
        s = lax.dot_general(qs_ref[x, hh], kmext_ref[hh], NT, preferred_element_type=F32)
        s = s * LOG2E + jnp.where(colm < N_META, slope2(hh) * relm, NEG)
        sm_ref[x, hh] = s
        mrun_ref[x, hh] = s
        s = frame_scores(x, hh, t)
        keep(x, hh, base + t,
             jnp.concatenate([s[:QT] + dbias_ref[hh], s[QT:] + dbias_ref[hh]], axis=0))

    col = lax.broadcasted_iota(jnp.int32, (1, QT), 1)

    def for_tile_groups(n, group):
        def trip(i, carry):
            group([TILE_UNROLL * i + u for u in range(TILE_UNROLL)])
            return carry

        lax.fori_loop(0, n // TILE_UNROLL, trip, 0)
        done = (n // TILE_UNROLL) * TILE_UNROLL
        for rest in range(1, TILE_UNROLL):
            @pl.when(n - done == rest)
            def _(rest=rest):
                group([done + u for u in range(rest)])

    def off_diag_loops(x, t, base):
        def off_diag(js):
            for j in js:
                rel = (col + (j - t) * QT).astype(F32)
                for hh in range(hb):
                    keep(x, hh, base + j, frame_scores(x, hh, j) + slope2(hh) * rel)

        for_tile_groups(t, off_diag)

    def row_max(x, hh):
        mrow_ref[x, hh] = jnp.broadcast_to(
            jnp.max(mrun_ref[x, hh], axis=-1, keepdims=True), (2 * QT, BLK))
        p = jnp.exp2(sm_ref[x, hh] - mrow_ref[x, hh])
        acc_ref[x, hh] = jnp.dot(p.astype(BF16), vmext_ref[hh], preferred_element_type=F32)

    def pv_loops(x, t, base):
        def pv(j, hh):
            start = pl.multiple_of(j * QT, QT)
            s = s_ref[hh, base + j]
            m = mrow_ref[x, hh]
            p = jnp.concatenate([jnp.exp2(s[:, :BLK] - m), jnp.exp2(s[:, BLK:] - m)], axis=1)
            return jnp.dot(p.astype(BF16), vext_ref[hh, pl.ds(start, QT), :],
                           preferred_element_type=F32)

        def pv_group(js):
            for hh in range(hb):
                acc_ref[x, hh] = acc_ref[x, hh] + functools.reduce(
                    lambda a, b: a + b, [pv(j, hh) for j in js])

        for_tile_groups(t + 1, pv_group)

    lam = _lambda(lq1_ref, lk1_ref, lq2_ref, lk2_ref, lam_init)

    def finish(x, hh):
        gate_ref = tiles[x][3]
        cols = head_cols(hh)
        acc = acc_ref[x, hh]
        a = acc[:, :HEAD_DIM] / acc[:, HEAD_DIM:]
        d = a[:QT] - lam * a[QT:]
        y = d * lax.rsqrt(jnp.mean(d * d, axis=-1, keepdims=True) + EPS)
        y = y * (gn_ref[:, cols] * (1.0 - lam_init))
        o_ref[x * QT:(x + 1) * QT, cols] = (
            y * _silu(gate_ref[:, cols].astype(F32))).astype(BF16)

    (ta, base_a, _, _), (tb, base_b, _, _) = tiles
    for hh in range(hb):
        first_scores(0, hh)
    off_diag_loops(0, ta, base_a)
    for hh in range(hb):
        first_scores(1, hh)
        row_max(0, hh)
    pv_loops(0, ta, base_a)
    off_diag_loops(1, tb, base_b)
    for hh in range(hb):
        row_max(1, hh)
        finish(0, hh)
    pv_loops(1, tb, base_b)
    for hh in range(hb):
        finish(1, hh)


def _diffattn(proj, pm, gn, lams, layer, batch, seq, lam_init, casts):
    m = proj.shape[0]
    ntile = seq // QT
    hb = HEADS_PER_STEP
    ngrp = HEADS // hb
    width = hb * HEAD_DIM
    slopes = jnp.asarray(2.0 ** (-8.0 * np.arange(1, HEADS + 1) / HEADS), F32)
    qcol, kcol, vcol, gcol = (4 * ngrp, 5 * ngrp, 6 * ngrp, 7 * ngrp)
    vec = pl.BlockSpec((None, 1, DIFF_DH), lambda b, g, t: (layer, 0, 0))
    tile_a = lambda b, t: b * ntile + t
    tile_b = lambda b, t: b * ntile + ntile - 1 - t
    nstep = batch * ngrp * (ntile // 2)
    linear = lambda b, g, t: (b * ngrp + g) * (ntile // 2) + t
    cast_in, cast_out_specs, cast_out_shapes, cast_args = [], [], [], []
    for w, wl in casts:
        rows, ncol = w.shape[1:]
        slab = rows // nstep
        cast_in.append(pl.BlockSpec((None, slab, ncol),
                                    lambda b, g, t, wl=wl: (wl, linear(b, g, t), 0)))
        cast_out_specs.append(pl.BlockSpec((slab, ncol), lambda b, g, t: (linear(b, g, t), 0)))
        cast_out_shapes.append(jax.ShapeDtypeStruct((rows, ncol), BF16))
        cast_args.append(w)
    outs = pl.pallas_call(
        functools.partial(_diffattn_kernel, lam_init=lam_init, ncast=len(casts)),
        out_shape=[jax.ShapeDtypeStruct((m, MIX_HALF), BF16)] + cast_out_shapes,
        grid=(batch, ngrp, ntile // 2),
        in_specs=[
            pl.BlockSpec(memory_space=pltpu.SMEM),
            pl.BlockSpec((QT, width), lambda b, g, t: (tile_a(b, t), qcol + g)),
            pl.BlockSpec((QT, width), lambda b, g, t: (tile_b(b, t), qcol + g)),
            pl.BlockSpec((seq, width), lambda b, g, t: (b, kcol + g)),
            pl.BlockSpec((seq, width), lambda b, g, t: (b, vcol + g)),
            pl.BlockSpec((None, N_META, width), lambda b, g, t: (0, 0, kcol + g)),
            pl.BlockSpec((None, N_META, width), lambda b, g, t: (0, 0, vcol + g)),
            pl.BlockSpec((QT, width), lambda b, g, t: (tile_a(b, t), gcol + g)),
            pl.BlockSpec((QT, width), lambda b, g, t: (tile_b(b, t), gcol + g)),
            pl.BlockSpec((None, 1, width), lambda b, g, t: (layer, 0, g)),
            vec, vec, vec, vec,
        ] + cast_in,
        out_specs=[pl.BlockSpec((2 * QT, width), lambda b, g, t: (b * (ntile // 2) + t, g))]
        + cast_out_specs,
        scratch_shapes=[
            pltpu.VMEM((2, hb, 2 * QT, HEAD_DIM), BF16),
            pltpu.VMEM((hb, ntile + 1, 2 * QT, QT), F32),
            pltpu.VMEM((2, hb, 2 * QT, BLK), F32),
            pltpu.VMEM((2, hb, 2 * QT, BLK), F32),
            pltpu.VMEM((2, hb, 2 * QT, BLK), F32),
            pltpu.VMEM((2, hb, 2 * QT, 2 * HEAD_DIM), F32),
            pltpu.VMEM((hb, seq, 2 * HEAD_DIM), BF16),
            pltpu.VMEM((hb, BLK, HEAD_DIM), BF16),
            pltpu.VMEM((hb, BLK, 2 * HEAD_DIM), BF16),
            pltpu.VMEM((hb, QT, QT), F32),
        ],
        compiler_params=pltpu.CompilerParams(
            dimension_semantics=("parallel", "parallel", "arbitrary"),
            vmem_limit_bytes=VMEM_LIMIT),
        name="diffattn",
    )(slopes, proj, proj, proj, proj, pm, pm, proj, proj, gn, *lams, *cast_args)
    return outs[0], outs[1:]


def _mix_outproj_kernel(q_ref, k_ref, v_ref, gate_ref, km_ref, vm_ref, gret_ref,
                        da_ref, db_ref, wr_ref, wd_ref, h_ref, gf_ref, o_ref,
                        r_ref, state_ref, init_ref, dmat_ref, rq_ref, rk_ref,
                        *, final_norm, tiles_per_seq):
    i = pl.program_id(0)
    scale = HEAD_DIM ** -0.5
    tm = q_ref.shape[0]

    @pl.when(i == 0)
    def _():
        r_ref[...] = jnp.zeros_like(r_ref)
        state_ref[...] = jnp.zeros_like(state_ref)
        t = lax.broadcasted_iota(jnp.int32, (BLK, BLK), 0)
        s = lax.broadcasted_iota(jnp.int32, (BLK, BLK), 1)
        dist = jnp.abs(t - s).astype(F32)
        allowed = (s // CHUNK) <= (t // CHUNK)
        tf = t.astype(F32)
        mf = lax.broadcasted_iota(jnp.int32, (N_META, HEAD_DIM), 0).astype(F32)
        for h in range(HEADS):
            cols = slice(h * HEAD_DIM, (h + 1) * HEAD_DIM)
            log_g = _log_g(h)
            dmat_ref[h] = jnp.where(allowed, jnp.exp(log_g * dist) * scale, 0.0)
            rq_ref[h] = jnp.exp(log_g * (tf + 1.0))
            rk_ref[h] = jnp.exp(log_g * (BLK - 1.0 - tf)) * scale
            zeta = jnp.exp(log_g * (N_META - 1.0 - mf)) * scale
            kz = (km_ref[:, cols].astype(F32) * zeta).astype(BF16)
            init_ref[h] = lax.dot_general(_pad_rows(kz, BLK), _pad_rows(vm_ref[:, cols], BLK),
                                          TN, preferred_element_type=F32)

    r = r_ref[...]
    for half, d_ref in enumerate((da_ref, db_ref)):
        rows = slice(half * QT, (half + 1) * QT)
        y = jnp.dot(r[rows], wr_ref[...], preferred_element_type=F32)
        y = y + jnp.dot(d_ref[...], wd_ref[...], preferred_element_type=F32)
        hn = h_ref[rows] + y
        if final_norm:
            ms = jnp.mean(hn * hn, axis=-1, keepdims=True)
            hn = hn * lax.rsqrt(ms + EPS) * gf_ref[...]
        o_ref[rows] = hn

    first = (i % tiles_per_seq) == 0
    for blk in range(tm // BLK):
        rows = slice(blk * BLK, (blk + 1) * BLK)
        for h in range(HEADS):
            cols = slice(h * HEAD_DIM, (h + 1) * HEAD_DIM)
            decay_blk = float((1.0 - 2.0 ** (-5.0 - h)) ** BLK)
            q = q_ref[rows, cols]
            k = k_ref[rows, cols]
            v = v_ref[rows, cols]
            s = lax.dot_general(q, k, NT, preferred_element_type=F32) * dmat_ref[h]
            intra = jnp.dot(s.astype(BF16), v, preferred_element_type=F32)
            state = state_ref[h]
            if blk == 0:
                state = jnp.where(first, init_ref[h], state)
            qx = (q.astype(F32) * rq_ref[h]).astype(BF16)
            cross = jnp.dot(qx, state.astype(BF16), preferred_element_type=F32)
            kz = (k.astype(F32) * rk_ref[h]).astype(BF16)
            kv = lax.dot_general(kz, v, TN, preferred_element_type=F32)
            state_ref[h] = decay_blk * state + kv

            o = intra + cross
            mu = jnp.mean(o, axis=-1, keepdims=True)
            oc = o - mu
            var = jnp.mean(oc * oc, axis=-1, keepdims=True)
            y = oc * lax.rsqrt(var + EPS) * gret_ref[:, cols]
            r_ref[rows, cols] = (y * _silu(gate_ref[rows, cols].astype(F32))).astype(BF16)


def _mix_outproj(proj, pm, d, w, h, gret, gf, layer, seq, final_norm):
    m, dm = h.shape
    tm = 2 * QT
    ntiles = m // tm
    ntile_q = seq // QT
    cur = lambda i: jnp.minimum(i, ntiles - 1)
    prev = lambda i: jnp.maximum(i - 1, 0)

    def d_block(which):
        def index(i):
            tile = 2 * prev(i) + which
            b, t = tile // ntile_q, tile % ntile_q
            pos = jnp.where(t < ntile_q // 2, 2 * t, 2 * (ntile_q - 1 - t) + 1)
            return b * ntile_q + pos, 0
        return index

    return pl.pallas_call(
        functools.partial(_mix_outproj_kernel, final_norm=final_norm,
                          tiles_per_seq=seq // tm),
        out_shape=jax.ShapeDtypeStruct((m, dm), F32),
        grid=(ntiles + 1,),
        in_specs=[
            pl.BlockSpec((tm, MIX_HALF), lambda i: (cur(i), 0)),
            pl.BlockSpec((tm, MIX_HALF), lambda i: (cur(i), 1)),
            pl.BlockSpec((tm, MIX_HALF), lambda i: (cur(i), 2)),
            pl.BlockSpec((tm, MIX_HALF), lambda i: (cur(i), 3)),
            pl.BlockSpec((None, N_META, MIX_HALF), lambda i: (0, 0, 1)),
            pl.BlockSpec((None, N_META, MIX_HALF), lambda i: (0, 0, 2)),
            pl.BlockSpec((None, 1, MIX_HALF), lambda i: (layer, 0, 0)),
            pl.BlockSpec((QT, MIX_HALF), d_block(0)),
            pl.BlockSpec((QT, MIX_HALF), d_block(1)),
            pl.BlockSpec((MIX_HALF, dm), lambda i: (0, 0), pipeline_mode=pl.Buffered(1)),
            pl.BlockSpec((MIX_HALF, dm), lambda i: (1, 0), pipeline_mode=pl.Buffered(1)),
            pl.BlockSpec((tm, dm), lambda i: (prev(i), 0)),
            pl.BlockSpec((1, dm), lambda i: (0, 0)),
        ],
        out_specs=pl.BlockSpec((tm, dm), lambda i: (prev(i), 0)),
        scratch_shapes=[
            pltpu.VMEM((tm, MIX_HALF), BF16),
            pltpu.VMEM((HEADS, HEAD_DIM, HEAD_DIM), F32),
            pltpu.VMEM((HEADS, HEAD_DIM, HEAD_DIM), F32),
            pltpu.VMEM((HEADS, BLK, BLK), F32),
            pltpu.VMEM((HEADS, BLK, HEAD_DIM), F32),
            pltpu.VMEM((HEADS, BLK, HEAD_DIM), F32),
        ],
        compiler_params=pltpu.CompilerParams(
            dimension_semantics=("arbitrary",),
            vmem_limit_bytes=VMEM_LIMIT),
        name="mix_outproj",
    )(proj, proj, proj, proj, pm, pm, gret, d, d, w, w, h, gf)


def kernel(x, meta_tokens, norm_g, w_in, w_out, ret_norm_g, diff_norm_g,
           lambda_q1, lambda_k1, lambda_q2, lambda_k2, final_norm_g):
    batch, seq, dm = x.shape
    depth = w_in.shape[0]
    m = batch * seq
    assert seq % QT == 0 and meta_tokens.shape[0] == N_META

    h = x.reshape(m, dm)
    hm = meta_tokens.astype(x.dtype)
    gf = final_norm_g.reshape(1, dm)
    per_layer = lambda a: a.reshape(depth, 1, a.shape[-1])
    norm_g, ret_norm_g, diff_norm_g = map(per_layer, (norm_g, ret_norm_g, diff_norm_g))
    lams = tuple(map(per_layer, (lambda_q1, lambda_k1, lambda_q2, lambda_k2)))

    tm = m // 8
    w_in_b = None
    for i in range(depth):
        last = i == depth - 1
        lam_init = 0.8 - 0.6 * float(np.exp(-0.3 * i))
        if w_in_b is None:
            proj, pm = _inproj(h, hm, norm_g, w_in, i, tm, 1024)
        else:
            proj, pm = _inproj(h, hm, norm_g, w_in_b, i, tm, 2048)
        if not last:
            hm = _meta_layer(pm, hm, w_out, ret_norm_g, diff_norm_g, lams, i, lam_init)
        casts = [(w_out, i)] + ([] if last else [(w_in, i + 1)])
        d, cast = _diffattn(proj, pm, diff_norm_g, lams, i, batch, seq, lam_init, casts)
        w_in_b = None if last else cast[1]
        h = _mix_outproj(proj, pm, d, cast[0], h, ret_norm_g, gf, i, seq, final_norm=last)

    return h.reshape(batch, seq, dm)
```

```python
import functools

import numpy as np
import jax
import jax.numpy as jnp
from jax import lax
from jax.experimental import pallas as pl
from jax.experimental.pallas import tpu as pltpu

F32 = jnp.float32
BF16 = jnp.bfloat16

CHUNK = 64
N_META = 16
EPS = 1e-6
HEADS = 8
HEAD_DIM = 128
DIFF_DH = 64
MIX_HALF = HEADS * HEAD_DIM
BLK = 128
QT = 256
NEG = -1e30
LOG2E = 1.4426950408889634
HEADS_PER_STEP = 4
TILE_UNROLL = 4

VMEM_LIMIT = 56 * 1024 * 1024

NT = (((1,), (1,)), ((), ()))
TN = (((0,), (0,)), ((), ()))


def _log_g(h):
    return float(np.log(1.0 - 2.0 ** (-5.0 - h)))


def _silu(x):
    return x * (1.0 / (1.0 + jnp.exp(-x)))


def _lambda(lq1_ref, lk1_ref, lq2_ref, lk2_ref, lam_init):
    return (jnp.exp(jnp.sum(lq1_ref[...] * lk1_ref[...], axis=-1, keepdims=True))
            - jnp.exp(jnp.sum(lq2_ref[...] * lk2_ref[...], axis=-1, keepdims=True))
            + lam_init)


def _pad_rows(a, rows):
    return jnp.concatenate([a, jnp.zeros((rows - a.shape[0], a.shape[1]), a.dtype)], axis=0)


def _split_components(q):
    lane = lax.broadcasted_iota(jnp.int32, q.shape, 1)
    zero = jnp.zeros((), BF16)
    qscaled = q * jnp.asarray(DIFF_DH ** -0.5, BF16)
    return jnp.concatenate([jnp.where(lane < DIFF_DH, qscaled, zero),
                            jnp.where(lane >= DIFF_DH, qscaled, zero)], axis=0)


def _rms_norm_bf16(x, g):
    ms = jnp.mean(x * x, axis=-1, keepdims=True)
    return (x * lax.rsqrt(ms + EPS) * g).astype(BF16)


W_SLOTS = 3


def _inproj_kernel(h_ref, hm_ref, g_ref, w_ref, o_ref, pm_ref, u_ref, *ring, layer, tn):
    tm = h_ref.shape[0]
    half = tm // 2

    if ring:
        wbuf, wsem = ring
        nj = pl.num_programs(1)
        total = pl.num_programs(0) * nj
        step = pl.program_id(0) * nj + pl.program_id(1)

        def w_copy(s):
            col = pl.multiple_of((s % nj) * tn, tn)
            slot = s % W_SLOTS
            return pltpu.make_async_copy(w_ref.at[layer, :, pl.ds(col, tn)],
                                         wbuf.at[slot], wsem.at[slot])

        @pl.when(step == 0)
        def _():
            for s in range(W_SLOTS - 1):
                w_copy(s).start()

        @pl.when(step + (W_SLOTS - 1) < total)
        def _():
            w_copy(step + (W_SLOTS - 1)).start()

        w_copy(step).wait()
        weights = lambda: wbuf[step % W_SLOTS]
    else:
        weights = lambda: w_ref[...]

    def project(lo, hi):
        return jnp.dot(u_ref[lo:hi], weights().astype(BF16),
                       preferred_element_type=F32).astype(BF16)

    @pl.when(pl.program_id(1) == 0)
    def _():
        u_ref[:half] = _rms_norm_bf16(h_ref[:half], g_ref[...])
        o_ref[:half] = project(0, half)
        u_ref[half:tm] = _rms_norm_bf16(h_ref[half:], g_ref[...])
        u_ref[tm:] = _rms_norm_bf16(hm_ref[...], g_ref[...])
        out = project(half, tm + N_META)
        o_ref[half:] = out[:tm - half]
        pm_ref[...] = out[tm - half:]

    @pl.when(pl.program_id(1) > 0)
    def _():
        out = project(0, tm + N_META)
        o_ref[...] = out[:tm]
        pm_ref[...] = out[tm:]


def _inproj(h, hm, g, w, layer, tm, tn):
    m, d = h.shape
    n = w.shape[-1]
    if w.ndim == 3:
        w_spec = pl.BlockSpec(memory_space=pl.ANY)
        ring = [pltpu.VMEM((W_SLOTS, d, tn), w.dtype), pltpu.SemaphoreType.DMA((W_SLOTS,))]
    else:
        w_spec = pl.BlockSpec((d, tn), lambda i, j: (0, j))
        ring = []
    return pl.pallas_call(
        functools.partial(_inproj_kernel, layer=layer, tn=tn),
        out_shape=[jax.ShapeDtypeStruct((m, n), BF16),
                   jax.ShapeDtypeStruct((m // tm, N_META, n), BF16)],
        grid=(m // tm, n // tn),
        in_specs=[
            pl.BlockSpec((tm, d), lambda i, j: (i, 0)),
            pl.BlockSpec((N_META, d), lambda i, j: (0, 0)),
            pl.BlockSpec((None, 1, d), lambda i, j: (layer, 0, 0)),
            w_spec,
        ],
        out_specs=[pl.BlockSpec((tm, tn), lambda i, j: (i, j)),
                   pl.BlockSpec((None, N_META, tn), lambda i, j: (i, 0, j))],
        scratch_shapes=[pltpu.VMEM((tm + N_META, d), BF16)] + ring,
        compiler_params=pltpu.CompilerParams(
            dimension_semantics=("arbitrary", "arbitrary"),
            vmem_limit_bytes=VMEM_LIMIT),
        name="inproj",
    )(h, hm, g, w)


def _meta_kernel(pm_ref, hm_ref, wr_ref, wd_ref, gret_ref, gdiff_ref,
                 lq1_ref, lk1_ref, lq2_ref, lk2_ref, hm_out_ref, *, lam_init):
    section = lambda s: pm_ref[:, s * MIX_HALF:(s + 1) * MIX_HALF]
    rq, rk, rv, rgate, dq, dk, dv, dgate = [section(s) for s in range(8)]
    t = lax.broadcasted_iota(jnp.int32, (BLK, BLK), 0)
    s_i = lax.broadcasted_iota(jnp.int32, (BLK, BLK), 1)
    dist = jnp.abs(t - s_i).astype(F32)
    r2 = lax.broadcasted_iota(jnp.int32, (2 * BLK, BLK), 0) % BLK
    c2 = lax.broadcasted_iota(jnp.int32, (2 * BLK, BLK), 1)
    dist2 = jnp.abs(r2 - c2).astype(F32)
    lam = _lambda(lq1_ref, lk1_ref, lq2_ref, lk2_ref, lam_init)
    r_parts, d_parts = [], []
    for h in range(HEADS):
        cols = slice(h * HEAD_DIM, (h + 1) * HEAD_DIM)
        q = _pad_rows(rq[:, cols], BLK)
        k = _pad_rows(rk[:, cols], BLK)
        v = _pad_rows(rv[:, cols], BLK)
        dmat = jnp.exp(_log_g(h) * dist) * (HEAD_DIM ** -0.5)
        s = lax.dot_general(q, k, NT, preferred_element_type=F32) * dmat
        o = jnp.dot(s.astype(BF16), v, preferred_element_type=F32)[:N_META]
        mu = jnp.mean(o, axis=-1, keepdims=True)
        oc = o - mu
        var = jnp.mean(oc * oc, axis=-1, keepdims=True)
        y = oc * lax.rsqrt(var + EPS) * gret_ref[:, cols]
        r_parts.append((y * _silu(rgate[:, cols].astype(F32))).astype(BF16))
        slope = 2.0 ** (-(h + 1.0))
        qs = _split_components(_pad_rows(dq[:, cols], BLK))
        k = _pad_rows(dk[:, cols], BLK)
        v = _pad_rows(dv[:, cols], BLK)
        bias = jnp.where(c2 < N_META, -slope * dist2, NEG)
        s = lax.dot_general(qs, k, NT, preferred_element_type=F32) + bias
        pr = jnp.exp(s - jnp.max(s, axis=-1, keepdims=True))
        a = (jnp.dot(pr.astype(BF16), v, preferred_element_type=F32)
             / jnp.sum(pr, axis=-1, keepdims=True))
        d = a[:N_META] - lam * a[BLK:BLK + N_META]
        y = d * lax.rsqrt(jnp.mean(d * d, axis=-1, keepdims=True) + EPS)
        y = y * gdiff_ref[:, cols] * (1.0 - lam_init)
        d_parts.append((y * _silu(dgate[:, cols].astype(F32))).astype(BF16))
    r = jnp.concatenate(r_parts, axis=1)
    d = jnp.concatenate(d_parts, axis=1)
    y = jnp.dot(r, wr_ref[...].astype(BF16), preferred_element_type=F32)
    y = y + jnp.dot(d, wd_ref[...].astype(BF16), preferred_element_type=F32)
    hm_out_ref[...] = hm_ref[...] + y


def _meta_layer(pm, hm, w_out, gret, gdiff, lams, layer, lam_init):
    dm = hm.shape[1]
    n = pm.shape[-1]
    vec = lambda width: pl.BlockSpec((None, 1, width), lambda i: (layer, 0, 0))
    return pl.pallas_call(
        functools.partial(_meta_kernel, lam_init=lam_init),
        out_shape=jax.ShapeDtypeStruct((N_META, dm), F32),
        grid=(1,),
        in_specs=[
            pl.BlockSpec((None, N_META, n), lambda i: (0, 0, 0)),
            pl.BlockSpec((N_META, dm), lambda i: (0, 0)),
            pl.BlockSpec((None, MIX_HALF, dm), lambda i: (layer, 0, 0),
                         pipeline_mode=pl.Buffered(1)),
            pl.BlockSpec((None, MIX_HALF, dm), lambda i: (layer, 1, 0),
                         pipeline_mode=pl.Buffered(1)),
            vec(MIX_HALF), vec(MIX_HALF),
            vec(DIFF_DH), vec(DIFF_DH), vec(DIFF_DH), vec(DIFF_DH),
        ],
        out_specs=pl.BlockSpec((N_META, dm), lambda i: (0, 0)),
        compiler_params=pltpu.CompilerParams(
            dimension_semantics=("arbitrary",),
            vmem_limit_bytes=VMEM_LIMIT),
        name="meta_layer",
    )(pm, hm, w_out, w_out, gret, gdiff, *lams)


def _diffattn_kernel(*refs, lam_init, ncast, seq, kcol, vcol):
    (slopes_ref, qa_ref, qb_ref, proj_hbm, km_ref, vm_ref, gatea_ref, gateb_ref, gn_ref,
     lq1_ref, lk1_ref, lq2_ref, lk2_ref) = refs[:13]
    cast_src = refs[13:13 + ncast]
    o_ref = refs[13 + ncast]
    cast_dst = refs[14 + ncast:14 + 2 * ncast]
    (qs_ref, s_ref, sm_ref, mrun_ref, mrow_ref, acc_ref,
     vext_ref, kmext_ref, vmext_ref, dbias_ref, kv_ref, kvsem) = refs[14 + 2 * ncast:]

    for src, dst in zip(cast_src, cast_dst):
        dst[...] = src[...].astype(BF16)

    grp = pl.program_id(1)
    step = pl.program_id(2)
    hb = HEADS_PER_STEP
    width = hb * HEAD_DIM
    ntile = seq // QT

    ngrp = pl.num_programs(1)
    group = pl.program_id(0) * ngrp + grp
    ngroups = pl.num_programs(0) * ngrp
    kv_slot = group % 2

    def kv_copy(gidx, which):
        rows = pl.ds(pl.multiple_of((gidx // ngrp) * seq, seq), seq)
        col = ((kcol, vcol)[which] + gidx % ngrp) * width
        return pltpu.make_async_copy(proj_hbm.at[rows, pl.ds(pl.multiple_of(col, width), width)],
                                     kv_ref.at[gidx % 2, which], kvsem.at[gidx % 2, which])

    @pl.when(step == 0)
    def _():
        @pl.when(group == 0)
        def _():
            kv_copy(group, 0).start()
            kv_copy(group, 1).start()

        @pl.when(group + 1 < ngroups)
        def _():
            kv_copy(group + 1, 0).start()
            kv_copy(group + 1, 1).start()

        kv_copy(group, 0).wait()
        kv_copy(group, 1).wait()

    k_ref = kv_ref.at[kv_slot, 0]
    v_ref = kv_ref.at[kv_slot, 1]
    head_cols = lambda hh: slice(hh * HEAD_DIM, (hh + 1) * HEAD_DIM)
    slope2 = lambda hh: slopes_ref[grp * hb + hh] * LOG2E
    tiles = ((step, 0, qa_ref, gatea_ref),
             (ntile - 1 - step, step + 1, qb_ref, gateb_ref))

    @pl.when(step == 0)
    def _():
        r = lax.broadcasted_iota(jnp.int32, (QT, QT), 0)
        c = lax.broadcasted_iota(jnp.int32, (QT, QT), 1)
        allowed = (c // CHUNK) <= (r // CHUNK)
        rel = (r - jnp.abs(r - c)).astype(F32)
        for hh in range(hb):
            cols = head_cols(hh)
            dbias_ref[hh] = jnp.where(allowed, slope2(hh) * rel, NEG)
            vext_ref[hh, :, :HEAD_DIM] = v_ref[:, cols]
            vext_ref[hh, :, HEAD_DIM:] = jnp.ones((seq, HEAD_DIM), BF16)
            kmext_ref[hh] = _pad_rows(km_ref[:, cols], BLK)
            vmext_ref[hh, :, :HEAD_DIM] = _pad_rows(vm_ref[:, cols], BLK)
            vmext_ref[hh, :, HEAD_DIM:] = jnp.ones((BLK, HEAD_DIM), BF16)

    def frame_scores(x, hh, j):
        start = pl.multiple_of(j * QT, QT)
        kt = k_ref[pl.ds(start, QT), head_cols(hh)]
        return lax.dot_general(qs_ref[x, hh], kt, NT, preferred_element_type=F32) * LOG2E

    def keep(x, hh, slot, s):
        s_ref[hh, slot] = s
        mrun_ref[x, hh] = jnp.maximum(mrun_ref[x, hh], jnp.maximum(s[:, :BLK], s[:, BLK:]))

    colm = lax.broadcasted_iota(jnp.int32, (1, BLK), 1)

    def first_scores(x, hh):
        t, base, q_ref, _ = tiles[x]
        relm = (colm - N_META - t * QT).astype(F32)
        qs_ref[x, hh] = _split_components(q_ref[:, head_cols(hh)])
        s = lax.dot_general(qs_ref[x, hh], kmext_ref[hh], NT, preferred_element_type=F32)
        s = s * LOG2E + jnp.where(colm < N_META, slope2(hh) * relm, NEG)
        sm_ref[x, hh] = s
        mrun_ref[x, hh] = s
        s = frame_scores(x, hh, t)
        keep(x, hh, base + t,
             jnp.concatenate([s[:QT] + dbias_ref[hh], s[QT:] + dbias_ref[hh]], axis=0))

    col = lax.broadcasted_iota(jnp.int32, (1, QT), 1)

    def for_tile_groups(n, group):
        def trip(i, carry):
            group([TILE_UNROLL * i + u for u in range(TILE_UNROLL)])
            return carry

        lax.fori_loop(0, n // TILE_UNROLL, trip, 0)
        done = (n // TILE_UNROLL) * TILE_UNROLL
        for rest in range(1, TILE_UNROLL):
            @pl.when(n - done == rest)
            def _(rest=rest):
                group([done + u for u in range(rest)])

    def off_diag_loops(x, t, base):
        def off_diag(js):
            for j in js:
                rel = (col + (j - t) * QT).astype(F32)
                for hh in range(hb):
                    keep(x, hh, base + j, frame_scores(x, hh, j) + slope2(hh) * rel)

        for_tile_groups(t, off_diag)

    def row_max(x, hh):
        mrow_ref[x, hh] = jnp.broadcast_to(
            jnp.max(mrun_ref[x, hh], axis=-1, keepdims=True), (2 * QT, BLK))
        p = jnp.exp2(sm_ref[x, hh] - mrow_ref[x, hh])
        acc_ref[x, hh] = jnp.dot(p.astype(BF16), vmext_ref[hh], preferred_element_type=F32)

    def pv_loops(x, t, base):
        def pv(j, hh):
            start = pl.multiple_of(j * QT, QT)
            s = s_ref[hh, base + j]
            m = mrow_ref[x, hh]
            p = jnp.concatenate([jnp.exp2(s[:, :BLK] - m), jnp.exp2(s[:, BLK:] - m)], axis=1)
            return jnp.dot(p.astype(BF16), vext_ref[hh, pl.ds(start, QT), :],
                           preferred_element_type=F32)

        def pv_group(js):
            for hh in range(hb):
                acc_ref[x, hh] = acc_ref[x, hh] + functools.reduce(
                    lambda a, b: a + b, [pv(j, hh) for j in js])

        for_tile_groups(t + 1, pv_group)

    lam = _lambda(lq1_ref, lk1_ref, lq2_ref, lk2_ref, lam_init)

    def finish(x, hh):
        gate_ref = tiles[x][3]
        cols = head_cols(hh)
        acc = acc_ref[x, hh]
        a = acc[:, :HEAD_DIM] / acc[:, HEAD_DIM:]
        d = a[:QT] - lam * a[QT:]
        y = d * lax.rsqrt(jnp.mean(d * d, axis=-1, keepdims=True) + EPS)
        y = y * (gn_ref[:, cols] * (1.0 - lam_init))
        o_ref[x * QT:(x + 1) * QT, cols] = (
            y * _silu(gate_ref[:, cols].astype(F32))).astype(BF16)

    (ta, base_a, _, _), (tb, base_b, _, _) = tiles
    for hh in range(hb):
        first_scores(0, hh)
    off_diag_loops(0, ta, base_a)
    for hh in range(hb):
        first_scores(1, hh)
        row_max(0, hh)
    pv_loops(0, ta, base_a)
    off_diag_loops(1, tb, base_b)
    for hh in range(hb):
        row_max(1, hh)
        finish(0, hh)
    pv_loops(1, tb, base_b)
    for hh in range(hb):
        finish(1, hh)


def _diffattn(proj, pm, gn, lams, layer, batch, seq, lam_init, casts):
    m = proj.shape[0]
    ntile = seq // QT
    hb = HEADS_PER_STEP
    ngrp = HEADS // hb
    width = hb * HEAD_DIM
    slopes = jnp.asarray(2.0 ** (-8.0 * np.arange(1, HEADS + 1) / HEADS), F32)
    qcol, kcol, vcol, gcol = (4 * ngrp, 5 * ngrp, 6 * ngrp, 7 * ngrp)
    vec = pl.BlockSpec((None, 1, DIFF_DH), lambda b, g, t: (layer, 0, 0))
    tile_a = lambda b, t: b * ntile + t
    tile_b = lambda b, t: b * ntile + ntile - 1 - t
    nstep = batch * ngrp * (ntile // 2)
    linear = lambda b, g, t: (b * ngrp + g) * (ntile // 2) + t
    cast_in, cast_out_specs, cast_out_shapes, cast_args = [], [], [], []
    for w, wl in casts:
        rows, ncol = w.shape[1:]
        slab = rows // nstep
        cast_in.append(pl.BlockSpec((None, slab, ncol),
                                    lambda b, g, t, wl=wl: (wl, linear(b, g, t), 0)))
        cast_out_specs.append(pl.BlockSpec((slab, ncol), lambda b, g, t: (linear(b, g, t), 0)))
        cast_out_shapes.append(jax.ShapeDtypeStruct((rows, ncol), BF16))
        cast_args.append(w)
    outs = pl.pallas_call(
        functools.partial(_diffattn_kernel, lam_init=lam_init, ncast=len(casts),
                          seq=seq, kcol=kcol, vcol=vcol),
        out_shape=[jax.ShapeDtypeStruct((m, MIX_HALF), BF16)] + cast_out_shapes,
        grid=(batch, ngrp, ntile // 2),
        in_specs=[
            pl.BlockSpec(memory_space=pltpu.SMEM),
            pl.BlockSpec((QT, width), lambda b, g, t: (tile_a(b, t), qcol + g)),
            pl.BlockSpec((QT, width), lambda b, g, t: (tile_b(b, t), qcol + g)),
            pl.BlockSpec(memory_space=pl.ANY),
            pl.BlockSpec((None, N_META, width), lambda b, g, t: (0, 0, kcol + g)),
            pl.BlockSpec((None, N_META, width), lambda b, g, t: (0, 0, vcol + g)),
            pl.BlockSpec((QT, width), lambda b, g, t: (tile_a(b, t), gcol + g)),
            pl.BlockSpec((QT, width), lambda b, g, t: (tile_b(b, t), gcol + g)),
            pl.BlockSpec((None, 1, width), lambda b, g, t: (layer, 0, g)),
            vec, vec, vec, vec,
        ] + cast_in,
        out_specs=[pl.BlockSpec((2 * QT, width), lambda b, g, t: (b * (ntile // 2) + t, g))]
        + cast_out_specs,
        scratch_shapes=[
            pltpu.VMEM((2, hb, 2 * QT, HEAD_DIM), BF16),
            pltpu.VMEM((hb, ntile + 1, 2 * QT, QT), F32),
            pltpu.VMEM((2, hb, 2 * QT, BLK), F32),
            pltpu.VMEM((2, hb, 2 * QT, BLK), F32),
            pltpu.VMEM((2, hb, 2 * QT, BLK), F32),
            pltpu.VMEM((2, hb, 2 * QT, 2 * HEAD_DIM), F32),
            pltpu.VMEM((hb, seq, 2 * HEAD_DIM), BF16),
            pltpu.VMEM((hb, BLK, HEAD_DIM), BF16),
            pltpu.VMEM((hb, BLK, 2 * HEAD_DIM), BF16),
            pltpu.VMEM((hb, QT, QT), F32),
            pltpu.VMEM((2, 2, seq, width), BF16),
            pltpu.SemaphoreType.DMA((2, 2)),
        ],
        compiler_params=pltpu.CompilerParams(
            dimension_semantics=("arbitrary", "arbitrary", "arbitrary"),
            vmem_limit_bytes=VMEM_LIMIT),
        name="diffattn",
    )(slopes, proj, proj, proj, pm, pm, proj, proj, gn, *lams, *cast_args)
    return outs[0], outs[1:]


def _mix_outproj_kernel(q_ref, k_ref, v_ref, gate_ref, km_ref, vm_ref, gret_ref,
                        da_ref, db_ref, wr_ref, wd_ref, h_ref, gf_ref, o_ref,
                        r_ref, state_ref, init_ref, dmat_ref, rq_ref, rk_ref,
                        *, final_norm, tiles_per_seq):
    i = pl.program_id(0)
    scale = HEAD_DIM ** -0.5
    tm = q_ref.shape[0]

    @pl.when(i == 0)
    def _():
        r_ref[...] = jnp.zeros_like(r_ref)
        state_ref[...] = jnp.zeros_like(state_ref)
        t = lax.broadcasted_iota(jnp.int32, (BLK, BLK), 0)
        s = lax.broadcasted_iota(jnp.int32, (BLK, BLK), 1)
        dist = jnp.abs(t - s).astype(F32)
        allowed = (s // CHUNK) <= (t // CHUNK)
        tf = t.astype(F32)
        mf = lax.broadcasted_iota(jnp.int32, (N_META, HEAD_DIM), 0).astype(F32)
        for h in range(HEADS):
            cols = slice(h * HEAD_DIM, (h + 1) * HEAD_DIM)
            log_g = _log_g(h)
            dmat_ref[h] = jnp.where(allowed, jnp.exp(log_g * dist) * scale, 0.0)
            rq_ref[h] = jnp.exp(log_g * (tf + 1.0))
            rk_ref[h] = jnp.exp(log_g * (BLK - 1.0 - tf)) * scale
            zeta = jnp.exp(log_g * (N_META - 1.0 - mf)) * scale
            kz = (km_ref[:, cols].astype(F32) * zeta).astype(BF16)
            init_ref[h] = lax.dot_general(_pad_rows(kz, BLK), _pad_rows(vm_ref[:, cols], BLK),
                                          TN, preferred_element_type=F32)

    r = r_ref[...]
    for half, d_ref in enumerate((da_ref, db_ref)):
        rows = slice(half * QT, (half + 1) * QT)
        y = jnp.dot(r[rows], wr_ref[...], preferred_element_type=F32)
        y = y + jnp.dot(d_ref[...], wd_ref[...], preferred_element_type=F32)
        hn = h_ref[rows] + y
        if final_norm:
            ms = jnp.mean(hn * hn, axis=-1, keepdims=True)
            hn = hn * lax.rsqrt(ms + EPS) * gf_ref[...]
        o_ref[rows] = hn

    first = (i % tiles_per_seq) == 0
    for blk in range(tm // BLK):
        rows = slice(blk * BLK, (blk + 1) * BLK)
        for h in range(HEADS):
            cols = slice(h * HEAD_DIM, (h + 1) * HEAD_DIM)
            decay_blk = float((1.0 - 2.0 ** (-5.0 - h)) ** BLK)
            q = q_ref[rows, cols]
            k = k_ref[rows, cols]
            v = v_ref[rows, cols]
            s = lax.dot_general(q, k, NT, preferred_element_type=F32) * dmat_ref[h]
            intra = jnp.dot(s.astype(BF16), v, preferred_element_type=F32)
            state = state_ref[h]
            if blk == 0:
                state = jnp.where(first, init_ref[h], state)
            qx = (q.astype(F32) * rq_ref[h]).astype(BF16)
            cross = jnp.dot(qx, state.astype(BF16), preferred_element_type=F32)
            kz = (k.astype(F32) * rk_ref[h]).astype(BF16)
            kv = lax.dot_general(kz, v, TN, preferred_element_type=F32)
            state_ref[h] = decay_blk * state + kv

            o = intra + cross
            mu = jnp.mean(o, axis=-1, keepdims=True)
            oc = o - mu
            var = jnp.mean(oc * oc, axis=-1, keepdims=True)
            y = oc * lax.rsqrt(var + EPS) * gret_ref[:, cols]
            r_ref[rows, cols] = (y * _silu(gate_ref[rows, cols].astype(F32))).astype(BF16)


def _mix_outproj(proj, pm, d, w, h, gret, gf, layer, seq, final_norm):
    m, dm = h.shape
    tm = 2 * QT
    ntiles = m // tm
    ntile_q = seq // QT
    cur = lambda i: jnp.minimum(i, ntiles - 1)
    prev = lambda i: jnp.maximum(i - 1, 0)

    def d_block(which):
        def index(i):
            tile = 2 * prev(i) + which
            b, t = tile // ntile_q, tile % ntile_q
            pos = jnp.where(t < ntile_q // 2, 2 * t, 2 * (ntile_q - 1 - t) + 1)
            return b * ntile_q + pos, 0
        return index

    return pl.pallas_call(
        functools.partial(_mix_outproj_kernel, final_norm=final_norm,
                          tiles_per_seq=seq // tm),
        out_shape=jax.ShapeDtypeStruct((m, dm), F32),
        grid=(ntiles + 1,),
        in_specs=[
            pl.BlockSpec((tm, MIX_HALF), lambda i: (cur(i), 0)),
            pl.BlockSpec((tm, MIX_HALF), lambda i: (cur(i), 1)),
            pl.BlockSpec((tm, MIX_HALF), lambda i: (cur(i), 2)),
            pl.BlockSpec((tm, MIX_HALF), lambda i: (cur(i), 3)),
            pl.BlockSpec((None, N_META, MIX_HALF), lambda i: (0, 0, 1)),
            pl.BlockSpec((None, N_META, MIX_HALF), lambda i: (0, 0, 2)),
            pl.BlockSpec((None, 1, MIX_HALF), lambda i: (layer, 0, 0)),
            pl.BlockSpec((QT, MIX_HALF), d_block(0)),
            pl.BlockSpec((QT, MIX_HALF), d_block(1)),
            pl.BlockSpec((MIX_HALF, dm), lambda i: (0, 0), pipeline_mode=pl.Buffered(1)),
            pl.BlockSpec((MIX_HALF, dm), lambda i: (1, 0), pipeline_mode=pl.Buffered(1)),
            pl.BlockSpec((tm, dm), lambda i: (prev(i), 0)),
            pl.BlockSpec((1, dm), lambda i: (0, 0)),
        ],
        out_specs=pl.BlockSpec((tm, dm), lambda i: (prev(i), 0)),
        scratch_shapes=[
            pltpu.VMEM((tm, MIX_HALF), BF16),
            pltpu.VMEM((HEADS, HEAD_DIM, HEAD_DIM), F32),
            pltpu.VMEM((HEADS, HEAD_DIM, HEAD_DIM), F32),
            pltpu.VMEM((HEADS, BLK, BLK), F32),
            pltpu.VMEM((HEADS, BLK, HEAD_DIM), F32),
            pltpu.VMEM((HEADS, BLK, HEAD_DIM), F32),
        ],
        compiler_params=pltpu.CompilerParams(
            dimension_semantics=("arbitrary",),
            vmem_limit_bytes=VMEM_LIMIT),
        name="mix_outproj",
    )(proj, proj, proj, proj, pm, pm, gret, d, d, w, w, h, gf)


def kernel(x, meta_tokens, norm_g, w_in, w_out, ret_norm_g, diff_norm_g,
           lambda_q1, lambda_k1, lambda_q2, lambda_k2, final_norm_g):
    batch, seq, dm = x.shape
    depth = w_in.shape[0]
    m = batch * seq
    assert seq % QT == 0 and meta_tokens.shape[0] == N_META

    h = x.reshape(m, dm)
    hm = meta_tokens.astype(x.dtype)
    gf = final_norm_g.reshape(1, dm)
    per_layer = lambda a: a.reshape(depth, 1, a.shape[-1])
    norm_g, ret_norm_g, diff_norm_g = map(per_layer, (norm_g, ret_norm_g, diff_norm_g))
    lams = tuple(map(per_layer, (lambda_q1, lambda_k1, lambda_q2, lambda_k2)))

    tm = m // 8
    w_in_b = None
    for i in range(depth):
        last = i == depth - 1
        lam_init = 0.8 - 0.6 * float(np.exp(-0.3 * i))
        if w_in_b is None:
            proj, pm = _inproj(h, hm, norm_g, w_in, i, tm, 1024)
        else:
            proj, pm = _inproj(h, hm, norm_g, w_in_b, i, tm, 2048)
        if not last:
            hm = _meta_layer(pm, hm, w_out, ret_norm_g, diff_norm_g, lams, i, lam_init)
        casts = [(w_out, i)] + ([] if last else [(w_in, i + 1)])
        d, cast = _diffattn(proj, pm, diff_norm_g, lams, i, batch, seq, lam_init, casts)
        w_in_b = None if last else cast[1]
        h = _mix_outproj(proj, pm, d, cast[0], h, ret_norm_g, gf, i, seq, final_norm=last)

    return h.reshape(batch, seq, dm)
```

```python
import functools

import numpy as np
import jax
import jax.numpy as jnp
from jax import lax
from jax.experimental import pallas as pl
from jax.experimental.pallas import tpu as pltpu

F32 = jnp.float32
BF16 = jnp.bfloat16

CHUNK = 64
N_META = 16
EPS = 1e-6
HEADS = 8
HEAD_DIM = 128
DIFF_DH = 64
MIX_HALF = HEADS * HEAD_DIM
BLK = 128
QT = 256
NEG = -1e30
LOG2E = 1.4426950408889634
HEADS_PER_STEP = 4
TILE_UNROLL = 4

VMEM_LIMIT = 56 * 1024 * 1024

NT = (((1,), (1,)), ((), ()))
TN = (((0,), (0,)), ((), ()))


def _log_g(h):
    return float(np.log(1.0 - 2.0 ** (-5.0 - h)))


def _silu(x):
    return x * (1.0 / (1.0 + jnp.exp(-x)))


def _lambda(lq1_ref, lk1_ref, lq2_ref, lk2_ref, lam_init):
    return (jnp.exp(jnp.sum(lq1_ref[...] * lk1_ref[...], axis=-1, keepdims=True))
            - jnp.exp(jnp.sum(lq2_ref[...] * lk2_ref[...], axis=-1, keepdims=True))
            + lam_init)


def _pad_rows(a, rows):
    return jnp.concatenate([a, jnp.zeros((rows - a.shape[0], a.shape[1]), a.dtype)], axis=0)


def _split_components(q):
    lane = lax.broadcasted_iota(jnp.int32, q.shape, 1)
    zero = jnp.zeros((), BF16)
    qscaled = q * jnp.asarray(DIFF_DH ** -0.5, BF16)
    return jnp.concatenate([jnp.where(lane < DIFF_DH, qscaled, zero),
                            jnp.where(lane >= DIFF_DH, qscaled, zero)], axis=0)


def _rms_norm_bf16(x, g):
    ms = jnp.mean(x * x, axis=-1, keepdims=True)
    return (x * lax.rsqrt(ms + EPS) * g).astype(BF16)


W_SLOTS = 3


def _inproj_kernel(h_ref, hm_ref, g_ref, w_ref, o_ref, pm_ref, u_ref, xbuf, xsem, *ring,
                   layer, tn):
    tm = o_ref.shape[0]
    half = tm // 2

    def x_copy(tile):
        rows = pl.ds(pl.multiple_of(tile * tm, tm), tm)
        return pltpu.make_async_copy(h_ref.at[rows, :], xbuf, xsem.at[0])

    if ring:
        wbuf, wsem = ring
        nj = pl.num_programs(1)
        total = pl.num_programs(0) * nj
        step = pl.program_id(0) * nj + pl.program_id(1)

        def w_copy(s):
            col = pl.multiple_of((s % nj) * tn, tn)
            slot = s % W_SLOTS
            return pltpu.make_async_copy(w_ref.at[layer, :, pl.ds(col, tn)],
                                         wbuf.at[slot], wsem.at[slot])

        @pl.when(step == 0)
        def _():
            for s in range(W_SLOTS - 1):
                w_copy(s).start()

        @pl.when(step + (W_SLOTS - 1) < total)
        def _():
            w_copy(step + (W_SLOTS - 1)).start()

        w_copy(step).wait()
        weights = lambda: wbuf[step % W_SLOTS]
    else:
        weights = lambda: w_ref[...]

    def project(lo, hi):
        return jnp.dot(u_ref[lo:hi], weights().astype(BF16),
                       preferred_element_type=F32).astype(BF16)

    @pl.when(pl.program_id(1) == 0)
    def _():
        row_tile = pl.program_id(0)

        @pl.when(row_tile == 0)
        def _():
            x_copy(0).start()

        x_copy(row_tile).wait()
        u_ref[:half] = _rms_norm_bf16(xbuf[:half], g_ref[...])
        o_ref[:half] = project(0, half)
        u_ref[half:tm] = _rms_norm_bf16(xbuf[half:], g_ref[...])

        @pl.when(row_tile + 1 < pl.num_programs(0))
        def _():
            x_copy(row_tile + 1).start()

        u_ref[tm:] = _rms_norm_bf16(hm_ref[...], g_ref[...])
        out = project(half, tm + N_META)
        o_ref[half:] = out[:tm - half]
        pm_ref[...] = out[tm - half:]

    @pl.when(pl.program_id(1) > 0)
    def _():
        out = project(0, tm + N_META)
        o_ref[...] = out[:tm]
        pm_ref[...] = out[tm:]


def _inproj(h, hm, g, w, layer, tm, tn):
    m, d = h.shape
    n = w.shape[-1]
    if w.ndim == 3:
        w_spec = pl.BlockSpec(memory_space=pl.ANY)
        ring = [pltpu.VMEM((W_SLOTS, d, tn), w.dtype), pltpu.SemaphoreType.DMA((W_SLOTS,))]
    else:
        w_spec = pl.BlockSpec((d, tn), lambda i, j: (0, j))
        ring = []
    return pl.pallas_call(
        functools.partial(_inproj_kernel, layer=layer, tn=tn),
        out_shape=[jax.ShapeDtypeStruct((m, n), BF16),
                   jax.ShapeDtypeStruct((m // tm, N_META, n), BF16)],
        grid=(m // tm, n // tn),
        in_specs=[
            pl.BlockSpec(memory_space=pl.ANY),
            pl.BlockSpec((N_META, d), lambda i, j: (0, 0)),
            pl.BlockSpec((None, 1, d), lambda i, j: (layer, 0, 0)),
            w_spec,
        ],
        out_specs=[pl.BlockSpec((tm, tn), lambda i, j: (i, j)),
                   pl.BlockSpec((None, N_META, tn), lambda i, j: (i, 0, j))],
        scratch_shapes=[pltpu.VMEM((tm + N_META, d), BF16),
                        pltpu.VMEM((tm, d), h.dtype), pltpu.SemaphoreType.DMA((1,))] + ring,
        compiler_params=pltpu.CompilerParams(
            dimension_semantics=("arbitrary", "arbitrary"),
            vmem_limit_bytes=VMEM_LIMIT),
        name="inproj",
    )(h, hm, g, w)


def _meta_kernel(pm_ref, hm_ref, wr_ref, wd_ref, gret_ref, gdiff_ref,
                 lq1_ref, lk1_ref, lq2_ref, lk2_ref, hm_out_ref, *, lam_init):
    section = lambda s: pm_ref[:, s * MIX_HALF:(s + 1) * MIX_HALF]
    rq, rk, rv, rgate, dq, dk, dv, dgate = [section(s) for s in range(8)]
    t = lax.broadcasted_iota(jnp.int32, (BLK, BLK), 0)
    s_i = lax.broadcasted_iota(jnp.int32, (BLK, BLK), 1)
    dist = jnp.abs(t - s_i).astype(F32)
    r2 = lax.broadcasted_iota(jnp.int32, (2 * BLK, BLK), 0) % BLK
    c2 = lax.broadcasted_iota(jnp.int32, (2 * BLK, BLK), 1)
    dist2 = jnp.abs(r2 - c2).astype(F32)
    lam = _lambda(lq1_ref, lk1_ref, lq2_ref, lk2_ref, lam_init)
    r_parts, d_parts = [], []
    for h in range(HEADS):
        cols = slice(h * HEAD_DIM, (h + 1) * HEAD_DIM)
        q = _pad_rows(rq[:, cols], BLK)
        k = _pad_rows(rk[:, cols], BLK)
        v = _pad_rows(rv[:, cols], BLK)
        dmat = jnp.exp(_log_g(h) * dist) * (HEAD_DIM ** -0.5)
        s = lax.dot_general(q, k, NT, preferred_element_type=F32) * dmat
        o = jnp.dot(s.astype(BF16), v, preferred_element_type=F32)[:N_META]
        mu = jnp.mean(o, axis=-1, keepdims=True)
        oc = o - mu
        var = jnp.mean(oc * oc, axis=-1, keepdims=True)
        y = oc * lax.rsqrt(var + EPS) * gret_ref[:, cols]
        r_parts.append((y * _silu(rgate[:, cols].astype(F32))).astype(BF16))
        slope = 2.0 ** (-(h + 1.0))
        qs = _split_components(_pad_rows(dq[:, cols], BLK))
        k = _pad_rows(dk[:, cols], BLK)
        v = _pad_rows(dv[:, cols], BLK)
        bias = jnp.where(c2 < N_META, -slope * dist2, NEG)
        s = lax.dot_general(qs, k, NT, preferred_element_type=F32) + bias
        pr = jnp.exp(s - jnp.max(s, axis=-1, keepdims=True))
        a = (jnp.dot(pr.astype(BF16), v, preferred_element_type=F32)
             / jnp.sum(pr, axis=-1, keepdims=True))
        d = a[:N_META] - lam * a[BLK:BLK + N_META]
        y = d * lax.rsqrt(jnp.mean(d * d, axis=-1, keepdims=True) + EPS)
        y = y * gdiff_ref[:, cols] * (1.0 - lam_init)
        d_parts.append((y * _silu(dgate[:, cols].astype(F32))).astype(BF16))
    r = jnp.concatenate(r_parts, axis=1)
    d = jnp.concatenate(d_parts, axis=1)
    y = jnp.dot(r, wr_ref[...].astype(BF16), preferred_element_type=F32)
    y = y + jnp.dot(d, wd_ref[...].astype(BF16), preferred_element_type=F32)
    hm_out_ref[...] = hm_ref[...] + y


def _meta_layer(pm, hm, w_out, gret, gdiff, lams, layer, lam_init):
    dm = hm.shape[1]
    n = pm.shape[-1]
    vec = lambda width: pl.BlockSpec((None, 1, width), lambda i: (layer, 0, 0))
    return pl.pallas_call(
        functools.partial(_meta_kernel, lam_init=lam_init),
        out_shape=jax.ShapeDtypeStruct((N_META, dm), F32),
        grid=(1,),
        in_specs=[
            pl.BlockSpec((None, N_META, n), lambda i: (0, 0, 0)),
            pl.BlockSpec((N_META, dm), lambda i: (0, 0)),
            pl.BlockSpec((None, MIX_HALF, dm), lambda i: (layer, 0, 0),
                         pipeline_mode=pl.Buffered(1)),
            pl.BlockSpec((None, MIX_HALF, dm), lambda i: (layer, 1, 0),
                         pipeline_mode=pl.Buffered(1)),
            vec(MIX_HALF), vec(MIX_HALF),
            vec(DIFF_DH), vec(DIFF_DH), vec(DIFF_DH), vec(DIFF_DH),
        ],
        out_specs=pl.BlockSpec((N_META, dm), lambda i: (0, 0)),
        compiler_params=pltpu.CompilerParams(
            dimension_semantics=("arbitrary",),
            vmem_limit_bytes=VMEM_LIMIT),
        name="meta_layer",
    )(pm, hm, w_out, w_out, gret, gdiff, *lams)


def _diffattn_kernel(*refs, lam_init, ncast):
    (slopes_ref, qa_ref, qb_ref, k_ref, v_ref, km_ref, vm_ref, gatea_ref, gateb_ref, gn_ref,
     lq1_ref, lk1_ref, lq2_ref, lk2_ref) = refs[:14]
    cast_src = refs[14:14 + ncast]
    o_ref = refs[14 + ncast]
    cast_dst = refs[15 + ncast:15 + 2 * ncast]
    (qs_ref, s_ref, sm_ref, mrun_ref, mrow_ref, acc_ref,
     vext_ref, kmext_ref, vmext_ref, dbias_ref) = refs[15 + 2 * ncast:]

    for src, dst in zip(cast_src, cast_dst):
        dst[...] = src[...].astype(BF16)

    grp = pl.program_id(1)
    step = pl.program_id(2)
    hb = HEADS_PER_STEP
    seq = k_ref.shape[0]
    ntile = seq // QT
    head_cols = lambda hh: slice(hh * HEAD_DIM, (hh + 1) * HEAD_DIM)
    slope2 = lambda hh: slopes_ref[grp * hb + hh] * LOG2E
    tiles = ((step, 0, qa_ref, gatea_ref),
             (ntile - 1 - step, step + 1, qb_ref, gateb_ref))

    @pl.when(step == 0)
    def _():
        r = lax.broadcasted_iota(jnp.int32, (QT, QT), 0)
        c = lax.broadcasted_iota(jnp.int32, (QT, QT), 1)
        allowed = (c // CHUNK) <= (r // CHUNK)
        rel = (r - jnp.abs(r - c)).astype(F32)
        for hh in range(hb):
            cols = head_cols(hh)
            dbias_ref[hh] = jnp.where(allowed, slope2(hh) * rel, NEG)
            vext_ref[hh, :, :HEAD_DIM] = v_ref[:, cols]
            vext_ref[hh, :, HEAD_DIM:] = jnp.ones((seq, HEAD_DIM), BF16)
            kmext_ref[hh] = _pad_rows(km_ref[:, cols], BLK)
            vmext_ref[hh, :, :HEAD_DIM] = _pad_rows(vm_ref[:, cols], BLK)
            vmext_ref[hh, :, HEAD_DIM:] = jnp.ones((BLK, HEAD_DIM), BF16)

    def frame_scores(x, hh, j):
        start = pl.multiple_of(j * QT, QT)
        kt = k_ref[pl.ds(start, QT), head_cols(hh)]
        return lax.dot_general(qs_ref[x, hh], kt, NT, preferred_element_type=F32) * LOG2E

    def keep(x, hh, slot, s):
        s_ref[hh, slot] = s
        mrun_ref[x, hh] = jnp.maximum(mrun_ref[x, hh], jnp.maximum(s[:, :BLK], s[:, BLK:]))

    colm = lax.broadcasted_iota(jnp.int32, (1, BLK), 1)

    def first_scores(x, hh):
        t, base, q_ref, _ = tiles[x]
        relm = (colm - N_META - t * QT).astype(F32)
        qs_ref[x, hh] = _split_components(q_ref[:, head_cols(hh)])
        s = lax.dot_general(qs_ref[x, hh], kmext_ref[hh], NT, preferred_element_type=F32)
        s = s * LOG2E + jnp.where(colm < N_META, slope2(hh) * relm, NEG)
        sm_ref[x, hh] = s
        mrun_ref[x, hh] = s
        s = frame_scores(x, hh, t)
        keep(x, hh, base + t,
             jnp.concatenate([s[:QT] + dbias_ref[hh], s[QT:] + dbias_ref[hh]], axis=0))

    col = lax.broadcasted_iota(jnp.int32, (1, QT), 1)

    def for_tile_groups(n, group):
        def trip(i, carry):
            group([TILE_UNROLL * i + u for u in range(TILE_UNROLL)])
            return carry

        lax.fori_loop(0, n // TILE_UNROLL, trip, 0)
        done = (n // TILE_UNROLL) * TILE_UNROLL
        for rest in range(1, TILE_UNROLL):
            @pl.when(n - done == rest)
            def _(rest=rest):
                group([done + u for u in range(rest)])

    def off_diag_loops(x, t, base):
        def off_diag(js):
            for j in js:
                rel = (col + (j - t) * QT).astype(F32)
                for hh in range(hb):
                    keep(x, hh, base + j, frame_scores(x, hh, j) + slope2(hh) * rel)

        for_tile_groups(t, off_diag)

    def row_max(x, hh):
        mrow_ref[x, hh] = jnp.broadcast_to(
            jnp.max(mrun_ref[x, hh], axis=-1, keepdims=True), (2 * QT, BLK))
        p = jnp.exp2(sm_ref[x, hh] - mrow_ref[x, hh])
        acc_ref[x, hh] = jnp.dot(p.astype(BF16), vmext_ref[hh], preferred_element_type=F32)

    def pv_loops(x, t, base):
        def pv(j, hh):
            start = pl.multiple_of(j * QT, QT)
            s = s_ref[hh, base + j]
            m = mrow_ref[x, hh]
            p = jnp.concatenate([jnp.exp2(s[:, :BLK] - m), jnp.exp2(s[:, BLK:] - m)], axis=1)
            return jnp.dot(p.astype(BF16), vext_ref[hh, pl.ds(start, QT), :],
                           preferred_element_type=F32)

        def pv_group(js):
            for hh in range(hb):
                acc_ref[x, hh] = acc_ref[x, hh] + functools.reduce(
                    lambda a, b: a + b, [pv(j, hh) for j in js])

        for_tile_groups(t + 1, pv_group)

    lam = _lambda(lq1_ref, lk1_ref, lq2_ref, lk2_ref, lam_init)

    def finish(x, hh):
        gate_ref = tiles[x][3]
        cols = head_cols(hh)
        acc = acc_ref[x, hh]
        a = acc[:, :HEAD_DIM] / acc[:, HEAD_DIM:]
        d = a[:QT] - lam * a[QT:]
        y = d * lax.rsqrt(jnp.mean(d * d, axis=-1, keepdims=True) + EPS)
        y = y * (gn_ref[:, cols] * (1.0 - lam_init))
        o_ref[x * QT:(x + 1) * QT, cols] = (
            y * _silu(gate_ref[:, cols].astype(F32))).astype(BF16)

    (ta, base_a, _, _), (tb, base_b, _, _) = tiles
    for hh in range(hb):
        first_scores(0, hh)
    off_diag_loops(0, ta, base_a)
    for hh in range(hb):
        first_scores(1, hh)
        row_max(0, hh)
    pv_loops(0, ta, base_a)
    off_diag_loops(1, tb, base_b)
    for hh in range(hb):
        row_max(1, hh)
        finish(0, hh)
    pv_loops(1, tb, base_b)
    for hh in range(hb):
        finish(1, hh)


def _diffattn(proj, pm, gn, lams, layer, batch, seq, lam_init, casts):
    m = proj.shape[0]
    ntile = seq // QT
    hb = HEADS_PER_STEP
    ngrp = HEADS // hb
    width = hb * HEAD_DIM
    slopes = jnp.asarray(2.0 ** (-8.0 * np.arange(1, HEADS + 1) / HEADS), F32)
    qcol, kcol, vcol, gcol = (4 * ngrp, 5 * ngrp, 6 * ngrp, 7 * ngrp)
    vec = pl.BlockSpec((None, 1, DIFF_DH), lambda b, g, t: (layer, 0, 0))
    tile_a = lambda b, t: b * ntile + t
    tile_b = lambda b, t: b * ntile + ntile - 1 - t
    nstep = batch * ngrp * (ntile // 2)
    linear = lambda b, g, t: (b * ngrp + g) * (ntile // 2) + t
    cast_in, cast_out_specs, cast_out_shapes, cast_args = [], [], [], []
    for w, wl in casts:
        rows, ncol = w.shape[1:]
        slab = rows // nstep
        cast_in.append(pl.BlockSpec((None, slab, ncol),
                                    lambda b, g, t, wl=wl: (wl, linear(b, g, t), 0)))
        cast_out_specs.append(pl.BlockSpec((slab, ncol), lambda b, g, t: (linear(b, g, t), 0)))
        cast_out_shapes.append(jax.ShapeDtypeStruct((rows, ncol), BF16))
        cast_args.append(w)
    outs = pl.pallas_call(
        functools.partial(_diffattn_kernel, lam_init=lam_init, ncast=len(casts)),
        out_shape=[jax.ShapeDtypeStruct((m, MIX_HALF), BF16)] + cast_out_shapes,
        grid=(batch, ngrp, ntile // 2),
        in_specs=[
            pl.BlockSpec(memory_space=pltpu.SMEM),
            pl.BlockSpec((QT, width), lambda b, g, t: (tile_a(b, t), qcol + g)),
            pl.BlockSpec((QT, width), lambda b, g, t: (tile_b(b, t), qcol + g)),
            pl.BlockSpec((seq, width), lambda b, g, t: (b, kcol + g)),
            pl.BlockSpec((seq, width), lambda b, g, t: (b, vcol + g)),
            pl.BlockSpec((None, N_META, width), lambda b, g, t: (0, 0, kcol + g)),
            pl.BlockSpec((None, N_META, width), lambda b, g, t: (0, 0, vcol + g)),
            pl.BlockSpec((QT, width), lambda b, g, t: (tile_a(b, t), gcol + g)),
            pl.BlockSpec((QT, width), lambda b, g, t: (tile_b(b, t), gcol + g)),
            pl.BlockSpec((None, 1, width), lambda b, g, t: (layer, 0, g)),
            vec, vec, vec, vec,
        ] + cast_in,
        out_specs=[pl.BlockSpec((2 * QT, width), lambda b, g, t: (b * (ntile // 2) + t, g))]
        + cast_out_specs,
        scratch_shapes=[
            pltpu.VMEM((2, hb, 2 * QT, HEAD_DIM), BF16),
            pltpu.VMEM((hb, ntile + 1, 2 * QT, QT), F32),
            pltpu.VMEM((2, hb, 2 * QT, BLK), F32),
            pltpu.VMEM((2, hb, 2 * QT, BLK), F32),
            pltpu.VMEM((2, hb, 2 * QT, BLK), F32),
            pltpu.VMEM((2, hb, 2 * QT, 2 * HEAD_DIM), F32),
            pltpu.VMEM((hb, seq, 2 * HEAD_DIM), BF16),
            pltpu.VMEM((hb, BLK, HEAD_DIM), BF16),
            pltpu.VMEM((hb, BLK, 2 * HEAD_DIM), BF16),
            pltpu.VMEM((hb, QT, QT), F32),
        ],
        compiler_params=pltpu.CompilerParams(
            dimension_semantics=("parallel", "parallel", "arbitrary"),
            vmem_limit_bytes=VMEM_LIMIT),
        name="diffattn",
    )(slopes, proj, proj, proj, proj, pm, pm, proj, proj, gn, *lams, *cast_args)
    return outs[0], outs[1:]


def _mix_outproj_kernel(q_ref, k_ref, v_ref, gate_ref, km_ref, vm_ref, gret_ref,
                        da_ref, db_ref, wr_ref, wd_ref, h_ref, gf_ref, o_ref,
                        r_ref, state_ref, init_ref, dmat_ref, rq_ref, rk_ref,
                        *, final_norm, tiles_per_seq):
    i = pl.program_id(0)
    scale = HEAD_DIM ** -0.5
    tm = q_ref.shape[0]

    @pl.when(i == 0)
    def _():
        r_ref[...] = jnp.zeros_like(r_ref)
        state_ref[...] = jnp.zeros_like(state_ref)
        t = lax.broadcasted_iota(jnp.int32, (BLK, BLK), 0)
        s = lax.broadcasted_iota(jnp.int32, (BLK, BLK), 1)
        dist = jnp.abs(t - s).astype(F32)
        allowed = (s // CHUNK) <= (t // CHUNK)
        tf = t.astype(F32)
        mf = lax.broadcasted_iota(jnp.int32, (N_META, HEAD_DIM), 0).astype(F32)
        for h in range(HEADS):
            cols = slice(h * HEAD_DIM, (h + 1) * HEAD_DIM)
            log_g = _log_g(h)
            dmat_ref[h] = jnp.where(allowed, jnp.exp(log_g * dist) * scale, 0.0)
            rq_ref[h] = jnp.exp(log_g * (tf + 1.0))
            rk_ref[h] = jnp.exp(log_g * (BLK - 1.0 - tf)) * scale
            zeta = jnp.exp(log_g * (N_META - 1.0 - mf)) * scale
            kz = (km_ref[:, cols].astype(F32) * zeta).astype(BF16)
            init_ref[h] = lax.dot_general(_pad_rows(kz, BLK), _pad_rows(vm_ref[:, cols], BLK),
                                          TN, preferred_element_type=F32)

    r = r_ref[...]
    for half, d_ref in enumerate((da_ref, db_ref)):
        rows = slice(half * QT, (half + 1) * QT)
        y = jnp.dot(r[rows], wr_ref[...], preferred_element_type=F32)
        y = y + jnp.dot(d_ref[...], wd_ref[...], preferred_element_type=F32)
        hn = h_ref[rows] + y
        if final_norm:
            ms = jnp.mean(hn * hn, axis=-1, keepdims=True)
            hn = hn * lax.rsqrt(ms + EPS) * gf_ref[...]
        o_ref[rows] = hn

    first = (i % tiles_per_seq) == 0
    for blk in range(tm // BLK):
        rows = slice(blk * BLK, (blk + 1) * BLK)
        for h in range(HEADS):
            cols = slice(h * HEAD_DIM, (h + 1) * HEAD_DIM)
            decay_blk = float((1.0 - 2.0 ** (-5.0 - h)) ** BLK)
            q = q_ref[rows, cols]
            k = k_ref[rows, cols]
            v = v_ref[rows, cols]
            s = lax.dot_general(q, k, NT, preferred_element_type=F32) * dmat_ref[h]
            intra = jnp.dot(s.astype(BF16), v, preferred_element_type=F32)
            state = state_ref[h]
            if blk == 0:
                state = jnp.where(first, init_ref[h], state)
            qx = (q.astype(F32) * rq_ref[h]).astype(BF16)
            cross = jnp.dot(qx, state.astype(BF16), preferred_element_type=F32)
            kz = (k.astype(F32) * rk_ref[h]).astype(BF16)
            kv = lax.dot_general(kz, v, TN, preferred_element_type=F32)
            state_ref[h] = decay_blk * state + kv

            o = intra + cross
            mu = jnp.mean(o, axis=-1, keepdims=True)
            oc = o - mu
            var = jnp.mean(oc * oc, axis=-1, keepdims=True)
            y = oc * lax.rsqrt(var + EPS) * gret_ref[:, cols]
            r_ref[rows, cols] = (y * _silu(gate_ref[rows, cols].astype(F32))).astype(BF16)


def _mix_outproj(proj, pm, d, w, h, gret, gf, layer, seq, final_norm):
    m, dm = h.shape
    tm = 2 * QT
    ntiles = m // tm
    ntile_q = seq // QT
    cur = lambda i: jnp.minimum(i, ntiles - 1)
    prev = lambda i: jnp.maximum(i - 1, 0)

    def d_block(which):
        def index(i):
            tile = 2 * prev(i) + which
            b, t = tile // ntile_q, tile % ntile_q
            pos = jnp.where(t < ntile_q // 2, 2 * t, 2 * (ntile_q - 1 - t) + 1)
            return b * ntile_q + pos, 0
        return index

    return pl.pallas_call(
        functools.partial(_mix_outproj_kernel, final_norm=final_norm,
                          tiles_per_seq=seq // tm),
        out_shape=jax.ShapeDtypeStruct((m, dm), F32),
        grid=(ntiles + 1,),
        in_specs=[
            pl.BlockSpec((tm, MIX_HALF), lambda i: (cur(i), 0)),
            pl.BlockSpec((tm, MIX_HALF), lambda i: (cur(i), 1)),
            pl.BlockSpec((tm, MIX_HALF), lambda i: (cur(i), 2)),
            pl.BlockSpec((tm, MIX_HALF), lambda i: (cur(i), 3)),
            pl.BlockSpec((None, N_META, MIX_HALF), lambda i: (0, 0, 1)),
            pl.BlockSpec((None, N_META, MIX_HALF), lambda i: (0, 0, 2)),
            pl.BlockSpec((None, 1, MIX_HALF), lambda i: (layer, 0, 0)),
            pl.BlockSpec((QT, MIX_HALF), d_block(0)),
            pl.BlockSpec((QT, MIX_HALF), d_block(1)),
            pl.BlockSpec((MIX_HALF, dm), lambda i: (0, 0), pipeline_mode=pl.Buffered(1)),
            pl.BlockSpec((MIX_HALF, dm), lambda i: (1, 0), pipeline_mode=pl.Buffered(1)),
            pl.BlockSpec((tm, dm), lambda i: (prev(i), 0)),
            pl.BlockSpec((1, dm), lambda i: (0, 0)),
        ],
        out_specs=pl.BlockSpec((tm, dm), lambda i: (prev(i), 0)),
        scratch_shapes=[
            pltpu.VMEM((tm, MIX_HALF), BF16),
            pltpu.VMEM((HEADS, HEAD_DIM, HEAD_DIM), F32),
            pltpu.VMEM((HEADS, HEAD_DIM, HEAD_DIM), F32),
            pltpu.VMEM((HEADS, BLK, BLK), F32),
            pltpu.VMEM((HEADS, BLK, HEAD_DIM), F32),
            pltpu.VMEM((HEADS, BLK, HEAD_DIM), F32),
        ],
        compiler_params=pltpu.CompilerParams(
            dimension_semantics=("arbitrary",),
            vmem_limit_bytes=VMEM_LIMIT),
        name="mix_outproj",
    )(proj, proj, proj, proj, pm, pm, gret, d, d, w, w, h, gf)


def kernel(x, meta_tokens, norm_g, w_in, w_out, ret_norm_g, diff_norm_g,
           lambda_q1, lambda_k1, lambda_q2, lambda_k2, final_norm_g):
    batch, seq, dm = x.shape
    depth = w_in.shape[0]
    m = batch * seq
    assert seq % QT == 0 and meta_tokens.shape[0] == N_META

    h = x.reshape(m, dm)
    hm = meta_tokens.astype(x.dtype)
    gf = final_norm_g.reshape(1, dm)
    per_layer = lambda a: a.reshape(depth, 1, a.shape[-1])
    norm_g, ret_norm_g, diff_norm_g = map(per_layer, (norm_g, ret_norm_g, diff_norm_g))
    lams = tuple(map(per_layer, (lambda_q1, lambda_k1, lambda_q2, lambda_k2)))

    tm = m // 8
    w_in_b = None
    for i in range(depth):
        last = i == depth - 1
        lam_init = 0.8 - 0.6 * float(np.exp(-0.3 * i))
        if w_in_b is None:
            proj, pm = _inproj(h, hm, norm_g, w_in, i, tm, 1024)
        else:
            proj, pm = _inproj(h, hm, norm_g, w_in_b, i, tm, 2048)
        if not last:
            hm = _meta_layer(pm, hm, w_out, ret_norm_g, diff_norm_g, lams, i, lam_init)
        casts = [(w_out, i)] + ([] if last else [(w_in, i + 1)])
        d, cast = _diffattn(proj, pm, diff_norm_g, lams, i, batch, seq, lam_init, casts)
        w_in_b = None if last else cast[1]
        h = _mix_outproj(proj, pm, d, cast[0], h, ret_norm_g, gf, i, seq, final_norm=last)

    return h.reshape(batch, seq, dm)
```
